```python
import jax, jax.numpy as jnp
from jax import lax
import numpy as np

D_MODEL = 1024
BATCH = 16
SEQ = 256
DEPTH = 4
DEC_BATCH = 4
DEC_SEQ = 4096
PAST_LEN = 256

GRID_W = 64
HEAD_DIM = 64
GLA_HEADS = 4
GLA_DK = 64
GLA_DV = 64
GLA_GATE_RANK = 16
GLA_TAU = 16.0
GLA_CHUNK = 64
SWA_Q_HEADS = 4
SWA_KV_HEADS = 2
SWA_REP = SWA_Q_HEADS // SWA_KV_HEADS
SWA_WINDOW = 128
SWA_BLOCK = 128
FNET_GROUPS = 4
FNET_GROUP_CH = 64
MLA_HEADS = 4
MLA_Q_RANK = 256
MLA_KV_RANK = 128
MLA_NOPE = 64
MLA_ROPE = 32
MLA_V = 64
D_FF = 2816
FFN_RES = 0.5
N_MOD = 9
ROPE_BASE = 10000.0
EPS = 1e-6
Q_BLOCK = 128
NEG_INF = -1e30

IN_SIZES = (
    GLA_HEADS * GLA_DK, GLA_HEADS * GLA_DK, GLA_HEADS * GLA_DV, GLA_HEADS * GLA_DV,
    GLA_GATE_RANK, GLA_GATE_RANK,
    SWA_Q_HEADS * HEAD_DIM, SWA_KV_HEADS * HEAD_DIM, SWA_KV_HEADS * HEAD_DIM,
    FNET_GROUPS * FNET_GROUP_CH,
    MLA_Q_RANK, MLA_KV_RANK, MLA_ROPE,
)
D_IN = sum(IN_SIZES)
MIX_WIDTH = GLA_HEADS * GLA_DV + SWA_Q_HEADS * HEAD_DIM + FNET_GROUPS * FNET_GROUP_CH + MLA_HEADS * MLA_V

kernel_name = "hybrid_flow_trunk_step"


def _rmsnorm(x, g):
    xf = x.astype(jnp.float32)
    y = xf * lax.rsqrt(jnp.mean(xf * xf, axis=-1, keepdims=True) + EPS)
    return (y * g.astype(jnp.float32)).astype(x.dtype)


def _split_cols(z):
    offs = np.cumsum(IN_SIZES)[:-1].tolist()
    return jnp.split(z, offs, axis=-1)


def _split_heads(z, n_heads):
    b, s, _ = z.shape
    return z.reshape(b, s, n_heads, -1).transpose(0, 2, 1, 3)


def _merge_heads(z):
    b, h, s, d = z.shape
    return z.transpose(0, 2, 1, 3).reshape(b, s, h * d)


def _rope_1d(x, pos):
    dim = x.shape[-1]
    inv = ROPE_BASE ** (-jnp.arange(0, dim, 2, dtype=jnp.float32) / dim)
    ang = pos[:, None] * inv[None, :]
    cos, sin = jnp.cos(ang), jnp.sin(ang)
    x1, x2 = jnp.split(x.astype(jnp.float32), 2, axis=-1)
    return jnp.concatenate([x1 * cos - x2 * sin, x2 * cos + x1 * sin], axis=-1).astype(x.dtype)


def _rope_axial(x):
    t = jnp.arange(x.shape[-2])
    rows = (t // GRID_W).astype(jnp.float32)
    cols = (t % GRID_W).astype(jnp.float32)
    half = x.shape[-1] // 2
    return jnp.concatenate([_rope_1d(x[..., :half], rows), _rope_1d(x[..., half:], cols)], axis=-1)


def _softmax_with_sink(s, sink):
    if sink is None:
        return jax.nn.softmax(s, axis=-1)
    sk = sink.astype(jnp.float32).reshape(sink.shape + (1,) * (s.ndim - 3))[None]
    m = jnp.maximum(jnp.max(s, axis=-1, keepdims=True), sk)
    e = jnp.exp(s - m)
    return e / (jnp.sum(e, axis=-1, keepdims=True) + jnp.exp(sk - m))


def _dense_attention(q, k, v, sink=None):
    b, g, r, sq, d = q.shape
    nq = sq // Q_BLOCK
    scale = d ** -0.5
    qb = jnp.moveaxis(q.reshape(b, g, r, nq, Q_BLOCK, d), 3, 0)

    def one_block(qblk):
        s = jnp.einsum("bgrqd,bgkd->bgrqk", qblk, k).astype(jnp.float32) * scale
        p = _softmax_with_sink(s, sink)
        return jnp.einsum("bgrqk,bgkd->bgrqd", p.astype(v.dtype), v)

    o = lax.map(one_block, qb)
    return jnp.moveaxis(o, 0, 3).reshape(b, g, r, sq, v.shape[-1])


def _window_attention(q, k, v, k_ctx, v_ctx, sink):
    b, g, r, s, d = q.shape
    nb = s // SWA_BLOCK
    kw_len = 3 * SWA_BLOCK

    def bands(t):
        tb = jnp.pad(t, ((0, 0), (0, 0), (SWA_BLOCK, SWA_BLOCK), (0, 0))).reshape(b, g, nb + 2, SWA_BLOCK, t.shape[-1])
        return jnp.concatenate([tb[:, :, :-2], tb[:, :, 1:-1], tb[:, :, 2:]], axis=3)

    kw, vw = bands(k), bands(v)
    qb = q.reshape(b, g, r, nb, SWA_BLOCK, d)
    scale = d ** -0.5
    s_loc = jnp.einsum("bgrnqd,bgnkd->bgrnqk", qb, kw).astype(jnp.float32) * scale
    s_ctx = jnp.einsum("bgrnqd,bgcd->bgrnqc", qb, k_ctx).astype(jnp.float32) * scale
    qpos = jnp.arange(nb)[:, None, None] * SWA_BLOCK + jnp.arange(SWA_BLOCK)[None, :, None]
    kpos = jnp.arange(nb)[:, None, None] * SWA_BLOCK - SWA_BLOCK + jnp.arange(kw_len)[None, None, :]
    valid = (jnp.abs(qpos - kpos) <= SWA_WINDOW) & (kpos >= 0) & (kpos < s)
    s_loc = jnp.where(valid, s_loc, NEG_INF)
    p = _softmax_with_sink(jnp.concatenate([s_loc, s_ctx], axis=-1), sink).astype(v.dtype)
    o = (jnp.einsum("bgrnqk,bgnkd->bgrnqd", p[..., :kw_len], vw)
         + jnp.einsum("bgrnqc,bgcd->bgrnqd", p[..., kw_len:], v_ctx))
    return o.reshape(b, g, r, s, d)


def _gla_scan(q, k, v, log_a, s0):
    b, h, s, _ = q.shape
    dv = v.shape[-1]
    n = s // GLA_CHUNK

    def chunks(t):
        return jnp.moveaxis(t.astype(jnp.float32).reshape(b, h, n, GLA_CHUNK, t.shape[-1]), 2, 0)

    causal = jnp.tril(jnp.ones((GLA_CHUNK, GLA_CHUNK), dtype=bool))

    def step(state, inp):
        qc, kc, vc, ac = inp
        cum = jnp.cumsum(ac, axis=-2)
        last = cum[..., -1:, :]
        q_dec = qc * jnp.exp(cum)
        k_inv = kc * jnp.exp(-cum)
        att = jnp.where(causal, jnp.einsum("bhid,bhjd->bhij", q_dec, k_inv), 0.0)
        o = jnp.einsum("bhij,bhjv->bhiv", att, vc) + jnp.einsum("bhid,bhdv->bhiv", q_dec, state)
        state = (jnp.exp(last)[..., 0, :, None] * state
                 + jnp.einsum("bhjd,bhjv->bhdv", kc * jnp.exp(last - cum), vc))
        return state, o

    state, o = lax.scan(step, s0.astype(jnp.float32), (chunks(q), chunks(k), chunks(v), chunks(log_a)))
    return jnp.moveaxis(o, 0, 2).reshape(b, h, s, dv), state


def _gla_mixer(q, k, v, r, a_f, a_b, lp, s0_f, s0_b):
    qh = _split_heads(q, GLA_HEADS) * (GLA_DK ** -0.5)
    kh = _split_heads(k, GLA_HEADS)
    vh = _split_heads(v, GLA_HEADS)

    def log_decay(a, i):
        logit = (a @ lp["gla_w_gate"][i] + lp["gla_b_gate"][i]).astype(jnp.float32)
        return _split_heads(jax.nn.log_sigmoid(logit) / GLA_TAU, GLA_HEADS)

    def flip(t):
        return jnp.flip(t, axis=2)

    o_f, st_f = _gla_scan(qh, kh, vh, log_decay(a_f, 0), s0_f)
    o_b, st_b = _gla_scan(flip(qh), flip(kh), flip(vh), flip(log_decay(a_b, 1)), s0_b)
    o = _rmsnorm(o_f + flip(o_b), lp["gla_g_out"]).astype(r.dtype)
    return _merge_heads(o) * jax.nn.silu(r), st_f, st_b


def _fourier_mix(z):
    b, s, _ = z.shape
    zg = z.astype(jnp.float32).reshape(b, s, FNET_GROUPS, FNET_GROUP_CH)
    y = jnp.fft.fft2(zg, axes=(1, 3), norm="ortho").real
    return y.reshape(b, s, FNET_GROUPS * FNET_GROUP_CH).astype(z.dtype)


def _mla_expand(c_kv, w_kv_b):
    b, s, _ = c_kv.shape
    kv = (c_kv @ w_kv_b).reshape(b, s, MLA_HEADS, MLA_NOPE + MLA_V).transpose(0, 2, 1, 3)
    return kv[..., :MLA_NOPE], kv[..., MLA_NOPE:]


def _token_mix(h, lp, ctx):
    b, s, _ = h.shape
    latent = ctx is not None
    (q_g, k_g, v_g, r_g, a_f, a_b, q_s, k_s, v_s, z_f, q_a, kv_a, k_r) = _split_cols(h @ lp["w_in"])

    if latent:
        s0_f, s0_b = ctx["gla"][:, 0], ctx["gla"][:, 1]
    else:
        s0_f = jnp.zeros((b, GLA_HEADS, GLA_DK, GLA_DV), jnp.float32)
        s0_b = s0_f
    o_gla, st_f, st_b = _gla_mixer(q_g, k_g, v_g, r_g, a_f, a_b, lp, s0_f, s0_b)

    qs = _split_heads(q_s, SWA_Q_HEADS)
    ks = _split_heads(k_s, SWA_KV_HEADS)
    vs = _split_heads(v_s, SWA_KV_HEADS)
    sink = lp["swa_sink"].reshape(SWA_KV_HEADS, SWA_REP)
    if latent:
        qs = _rope_axial(qs).reshape(b, SWA_KV_HEADS, SWA_REP, s, HEAD_DIM)
        o_swa = _window_attention(qs, _rope_axial(ks), vs, ctx["swa_k"], ctx["swa_v"], sink)
    else:
        o_swa = _dense_attention(qs.reshape(b, SWA_KV_HEADS, SWA_REP, s, HEAD_DIM), ks, vs, sink)
    o_swa = _merge_heads(o_swa.reshape(b, SWA_Q_HEADS, s, HEAD_DIM))

    o_fft = _fourier_mix(z_f)

    c_kv = _rmsnorm(kv_a, lp["mla_g_kv"])
    q_m = (_rmsnorm(q_a, lp["mla_g_q"]) @ lp["mla_w_q_b"]).reshape(b, s, MLA_HEADS, MLA_NOPE + MLA_ROPE).transpose(0, 2, 1, 3)
    k_nope, v_m = _mla_expand(c_kv, lp["mla_w_kv_b"])
    k_rope = k_r[:, None]
    if latent:
        q_m = jnp.concatenate([q_m[..., :MLA_NOPE], _rope_axial(q_m[..., MLA_NOPE:])], axis=-1)
        k_rope = _rope_axial(k_rope)
    k_m = jnp.concatenate([k_nope, jnp.broadcast_to(k_rope, (b, MLA_HEADS, s, MLA_ROPE))], axis=-1)
    if latent:
        c_len = ctx["mla_ckv"].shape[1]
        kc_nope, v_c = _mla_expand(ctx["mla_ckv"], lp["mla_w_kv_b"])
        k_c = jnp.concatenate([kc_nope, jnp.broadcast_to(ctx["mla_kr"][:, None], (b, MLA_HEADS, c_len, MLA_ROPE))], axis=-1)
        k_m = jnp.concatenate([k_c, k_m], axis=2)
        v_m = jnp.concatenate([v_c, v_m], axis=2)
    o_mla = _merge_heads(_dense_attention(q_m[:, :, None], k_m, v_m)[:, :, 0])

    out = jnp.concatenate([o_gla, o_swa, o_fft, o_mla], axis=-1) @ lp["w_out"]
    if latent:
        return out, None
    new_ctx = (jnp.stack([st_f, st_b], axis=1),
               k_s.reshape(b, s, SWA_KV_HEADS, HEAD_DIM),
               v_s.reshape(b, s, SWA_KV_HEADS, HEAD_DIM),
               c_kv, k_r)
    return out, new_ctx


def _swiglu(h, lp, i):
    return (jax.nn.silu(h @ lp["ffn_gate"][i]) * (h @ lp["ffn_up"][i])) @ lp["ffn_down"][i]


def _layer(x, mod, lp, ctx):
    sh1, sc1, gt1, sh2, sc2, gt2, sh3, sc3, gt3 = jnp.split(mod[:, None, :].astype(x.dtype), N_MOD, axis=-1)
    g = lp["g_norm"]
    h = _rmsnorm(x, g[0]) * (1 + sc1) + sh1
    x = x + FFN_RES * gt1 * _rmsnorm(_swiglu(h, lp, 0), g[1])
    h = _rmsnorm(x, g[2]) * (1 + sc2) + sh2
    o, new_ctx = _token_mix(h, lp, ctx)
    x = x + gt2 * _rmsnorm(o, g[3])
    h = _rmsnorm(x, g[4]) * (1 + sc3) + sh3
    x = x + FFN_RES * gt3 * _rmsnorm(_swiglu(h, lp, 1), g[5])
    return x, new_ctx


def setup_inputs(seed: int = 0) -> dict:
    key = jax.random.key(seed)
    ks = jax.random.split(key, 26)
    D = D_MODEL

    def nrm(k, shape, scale):
        return jax.random.normal(k, shape, jnp.float32) * scale

    return {
        "x_prompt": nrm(ks[0], (BATCH, SEQ, D), 1.0),
        "x_sample": nrm(ks[1], (DEC_BATCH, DEC_SEQ, D), 1.0),
        "c": nrm(ks[2], (DEC_BATCH, D), 1.0),
        "state_gla": nrm(ks[3], (DEC_BATCH, DEPTH, 2, GLA_HEADS, GLA_DK, GLA_DV), 1.0),
        "cache_swa_k": nrm(ks[4], (DEC_BATCH, DEPTH, PAST_LEN, SWA_KV_HEADS, HEAD_DIM), 1.0),
        "cache_swa_v": nrm(ks[5], (DEC_BATCH, DEPTH, PAST_LEN, SWA_KV_HEADS, HEAD_DIM), 1.0),
        "cache_mla_ckv": nrm(ks[6], (DEC_BATCH, DEPTH, PAST_LEN, MLA_KV_RANK), 1.0),
        "cache_mla_krope": nrm(ks[7], (DEC_BATCH, DEPTH, PAST_LEN, MLA_ROPE), 1.0),
        "c_ctx": nrm(ks[8], (D,), 1.0),
        "w_mod": nrm(ks[9], (DEPTH, D, N_MOD * D), 0.5 * D ** -0.5),
        "b_mod": nrm(ks[10], (DEPTH, N_MOD * D), 0.02),
        "g_norm": 1.0 + nrm(ks[11], (DEPTH, 6, D), 0.02),
        "w_ffn_gate": nrm(ks[12], (DEPTH, 2, D, D_FF), D ** -0.5),
        "w_ffn_up": nrm(ks[13], (DEPTH, 2, D, D_FF), D ** -0.5),
        "w_ffn_down": nrm(ks[14], (DEPTH, 2, D_FF, D), D_FF ** -0.5),
        "w_in": nrm(ks[15], (DEPTH, D, D_IN), D ** -0.5),
        "gla_w_gate": nrm(ks[16], (DEPTH, 2, GLA_GATE_RANK, GLA_HEADS * GLA_DK), GLA_GATE_RANK ** -0.5),
        "gla_b_gate": nrm(ks[17], (DEPTH, 2, GLA_HEADS * GLA_DK), 0.1),
        "gla_g_out": 1.0 + nrm(ks[18], (DEPTH, GLA_DV), 0.02),
        "swa_sink": nrm(ks[19], (DEPTH, SWA_Q_HEADS), 0.1),
        "mla_g_q": 1.0 + nrm(ks[20], (DEPTH, MLA_Q_RANK), 0.02),
        "mla_g_kv": 1.0 + nrm(ks[21], (DEPTH, MLA_KV_RANK), 0.02),
        "mla_w_q_b": nrm(ks[22], (DEPTH, MLA_Q_RANK, MLA_HEADS * (MLA_NOPE + MLA_ROPE)), MLA_Q_RANK ** -0.5),
        "mla_w_kv_b": nrm(ks[23], (DEPTH, MLA_KV_RANK, MLA_HEADS * (MLA_NOPE + MLA_V)), MLA_KV_RANK ** -0.5),
        "w_out": nrm(ks[24], (DEPTH, MIX_WIDTH, D), MIX_WIDTH ** -0.5),
    }


def reference(x_prompt, x_sample, c, state_gla, cache_swa_k, cache_swa_v, cache_mla_ckv, cache_mla_krope,
              c_ctx, w_mod, b_mod, g_norm, w_ffn_gate, w_ffn_up, w_ffn_down, w_in, gla_w_gate, gla_b_gate,
              gla_g_out, swa_sink, mla_g_q, mla_g_kv, mla_w_q_b, mla_w_kv_b, w_out):
    y_prompt, y_sample = x_prompt, x_sample
    st_gla, st_k, st_v, st_ckv, st_kr = [], [], [], [], []
    for l in range(DEPTH):
        lp = {
            "w_in": w_in[l], "w_out": w_out[l], "g_norm": g_norm[l],
            "ffn_gate": w_ffn_gate[l], "ffn_up": w_ffn_up[l], "ffn_down": w_ffn_down[l],
            "gla_w_gate": gla_w_gate[l], "gla_b_gate": gla_b_gate[l], "gla_g_out": gla_g_out[l],
            "swa_sink": swa_sink[l], "mla_g_q": mla_g_q[l], "mla_g_kv": mla_g_kv[l],
            "mla_w_q_b": mla_w_q_b[l], "mla_w_kv_b": mla_w_kv_b[l],
        }
        mod_ctx = jax.nn.silu(c_ctx)[None, :] @ w_mod[l] + b_mod[l]
        mod_lat = jax.nn.silu(c) @ w_mod[l] + b_mod[l]

        y_prompt, (g_st, k_c, v_c, ckv_c, kr_c) = _layer(y_prompt, mod_ctx, lp, None)
        st_gla.append(g_st)
        st_k.append(k_c)
        st_v.append(v_c)
        st_ckv.append(ckv_c)
        st_kr.append(kr_c)

        ctx = {
            "gla": state_gla[:, l],
            "swa_k": cache_swa_k[:, l].transpose(0, 2, 1, 3),
            "swa_v": cache_swa_v[:, l].transpose(0, 2, 1, 3),
            "mla_ckv": cache_mla_ckv[:, l],
            "mla_kr": cache_mla_krope[:, l],
        }
        y_sample, _ = _layer(y_sample, mod_lat, lp, ctx)

    return (y_prompt, y_sample, jnp.stack(st_gla, axis=1), jnp.stack(st_k, axis=1), jnp.stack(st_v, axis=1),
            jnp.stack(st_ckv, axis=1), jnp.stack(st_kr, axis=1))
```

```python
import functools

import jax
import jax.numpy as jnp
from jax import lax
from jax.experimental import pallas as pl
from jax.experimental.pallas import tpu as pltpu

D_MODEL = 1024
DEPTH = 4
GRID_W = 64
HEAD_DIM = 64
GLA_HEADS = 4
GLA_DK = 64
GLA_DV = 64
GLA_GATE_RANK = 16
GLA_TAU = 16.0
GLA_CHUNK = 64
SWA_Q_HEADS = 4
SWA_KV_HEADS = 2
SWA_BLOCK = 128
FNET_GROUPS = 4
FNET_GROUP_CH = 64
MLA_HEADS = 4
MLA_Q_RANK = 256
MLA_KV_RANK = 128
MLA_NOPE = 64
MLA_ROPE = 32
MLA_V = 64
D_FF = 2816
FFN_RES = 0.5
N_MOD = 9
ROPE_BASE = 10000.0
EPS = 1e-6
NEG_INF = -1e30

MIX = 256
LANES = 128
MXU_DEPTH = 256
FF_CHUNK = 256
MOD_ROWS = 8
VMEM_LIMIT = 56 * 1024 * 1024

BF = jnp.bfloat16
F32 = jnp.float32

C_GQ, C_GK, C_GV, C_GR, C_A = 0, 256, 512, 768, 1024
C_SQ, C_SK, C_SV, C_ZF, C_QA, C_KVA, C_KR = 1152, 1664, 1792, 1920, 2176, 2432, 2560
C_SQR, C_SKR, C_KRR = 2688, 3200, 3328
NC_CTX, NC_LAT = 2688, 3456


def _dot(a, b):
    return jnp.dot(a, b, preferred_element_type=F32)


def _dot_nt(a, b):
    return lax.dot_general(a, b, (((1,), (1,)), ((), ())), preferred_element_type=F32)


def _dot_tn(a, b):
    return lax.dot_general(a, b, (((0,), (0,)), ((), ())), preferred_element_type=F32)


def _rms(x, g):
    return x * lax.rsqrt(jnp.mean(x * x, axis=-1, keepdims=True) + EPS) * g


def _silu(x):
    return x * jax.nn.sigmoid(x)


def _params(*sem):
    return pltpu.CompilerParams(dimension_semantics=sem, vmem_limit_bytes=VMEM_LIMIT)


def _const_spec(shape):
    nd = len(shape)
    return pl.BlockSpec(shape, lambda *_: (0,) * nd, pipeline_mode=pl.Buffered(1))


def _mod_kernel(c_ref, w_ref, b_ref, o_ref):
    s = _silu(c_ref[...]).astype(BF)
    o_ref[...] = _dot(s, w_ref[...].astype(BF)) + b_ref[...]


def _mod_call(cvec, w_mod, b_mod):
    nl = w_mod.shape[0]
    tn = D_MODEL
    return pl.pallas_call(
        _mod_kernel,
        grid=(nl, N_MOD * D_MODEL // tn),
        in_specs=[
            pl.BlockSpec((MOD_ROWS, D_MODEL), lambda l, j: (0, 0)),
            pl.BlockSpec((None, D_MODEL, tn), lambda l, j: (l, 0, j)),
            pl.BlockSpec((None, 1, tn), lambda l, j: (l, 0, j)),
        ],
        out_specs=pl.BlockSpec((None, MOD_ROWS, tn), lambda l, j: (l, 0, j)),
        out_shape=jax.ShapeDtypeStruct((nl, MOD_ROWS, N_MOD * D_MODEL), F32),
        compiler_params=_params("parallel", "parallel"),
        name="mod",
    )(cvec, w_mod, b_mod.reshape(nl, 1, N_MOD * D_MODEL))


def _mod_spec(sub, tm, rows_per_batch, first_row):
    per = rows_per_batch // tm
    return pl.BlockSpec((None, 3, 1, D_MODEL), lambda i: (first_row + i // per, sub, 0, 0))


def _ffn_body(x, mod_ref, g_pre, g_post, wg_ref, wu_ref, wd_ref):
    sh, sc, gt = mod_ref[0], mod_ref[1], mod_ref[2]
    hb = (_rms(x, g_pre) * (1.0 + sc) + sh).astype(BF)
    acc = None
    for j in range(D_FF // FF_CHUNK):
        sl = slice(j * FF_CHUNK, (j + 1) * FF_CHUNK)
        g = _dot(hb, wg_ref[:, sl])
        u = _dot(hb, wu_ref[:, sl])
        d = _dot((_silu(g) * u).astype(BF), wd_ref[sl, :])
        acc = d if acc is None else acc + d
    return x + FFN_RES * gt * _rms(acc, g_post)


def _ffn_kernel(x_ref, mod_ref, gn_ref, wg_ref, wu_ref, wd_ref, o_ref):
    o_ref[...] = _ffn_body(x_ref[...], mod_ref, gn_ref[0], gn_ref[1], wg_ref, wu_ref, wd_ref)


def _ffn_call(x, mod_l, gn, wg, wu, wd, rows_per_batch, first_row, tm=512):
    r = x.shape[0]
    return pl.pallas_call(
        _ffn_kernel,
        grid=(r // tm,),
        in_specs=[
            pl.BlockSpec((tm, D_MODEL), lambda i: (i, 0)),
            _mod_spec(0, tm, rows_per_batch, first_row),
            pl.BlockSpec((2, 1, D_MODEL), lambda i: (0, 0, 0)),
            _const_spec(wg.shape), _const_spec(wu.shape), _const_spec(wd.shape),
        ],
        out_specs=pl.BlockSpec((tm, D_MODEL), lambda i: (i, 0)),
        out_shape=jax.ShapeDtypeStruct(x.shape, F32),
        compiler_params=_params("parallel"),
        name="ffn",
    )(x, mod_l, gn, wg, wu, wd)


def _outffn_kernel(og_ref, os_ref, of_ref, om_ref, x_ref, mod2_ref, mod3_ref, gn_ref, wo_ref,
                   wg_ref, wu_ref, wd_ref, o_ref):
    mix = jnp.concatenate([og_ref[...], os_ref[...], of_ref[...], om_ref[...]], axis=1)
    x = x_ref[...] + mod2_ref[2] * _rms(_dot(mix, wo_ref[...]), gn_ref[0])
    o_ref[...] = _ffn_body(x, mod3_ref, gn_ref[1], gn_ref[2], wg_ref, wu_ref, wd_ref)


def _outffn_call(og, osw, of, om, x, mod_l, gn, wo, wg, wu, wd, seq, rows_per_batch, first_row, tm):
    r = x.shape[0]
    per = seq // tm
    mspec = pl.BlockSpec((tm, MIX), lambda i: (i, 0))
    return pl.pallas_call(
        _outffn_kernel,
        grid=(r // tm,),
        in_specs=[
            mspec, mspec,
            pl.BlockSpec((tm, MIX), lambda i: (i % per, i // per)),
            mspec,
            pl.BlockSpec((tm, D_MODEL), lambda i: (i, 0)),
            _mod_spec(1, tm, rows_per_batch, first_row),
            _mod_spec(2, tm, rows_per_batch, first_row),
            pl.BlockSpec((3, 1, D_MODEL), lambda i: (0, 0, 0)),
            _const_spec(wo.shape), _const_spec(wg.shape), _const_spec(wu.shape), _const_spec(wd.shape),
        ],
        out_specs=pl.BlockSpec((tm, D_MODEL), lambda i: (i, 0)),
        out_shape=jax.ShapeDtypeStruct(x.shape, F32),
        compiler_params=_params("parallel"),
        name="outffn",
    )(og, osw, of, om, x, mod_l, mod_l, gn, wo, wg, wu, wd)


def _log_sigmoid(x):
    return -(jnp.maximum(-x, 0.0) + jnp.log(1.0 + jnp.exp(-jnp.abs(x))))


def _inproj_kernel(*refs, latent):
    (x_ref, mod_ref, gn_ref, w_ref, wgate_ref, bgate_ref, gq_ref, gkv_ref, wqb_ref, wkbd_ref,
     cdft_ref, sdft_ref) = refs[:12]
    pos = 12
    if latent:
        cs_ref, ss_ref, cm_ref, sm_ref = refs[pos:pos + 4]
        pos += 4
    (gq_o, gk_o, gv_o, gr_o, la_o, sq_o, sk_o, sv_o, zc_o, zs_o, qc_o, qr_o, kcat_o) = refs[pos:pos + 13]
    pos += 13
    if not latent:
        ckv_o, kr_o = refs[pos:pos + 2]

    x = x_ref[...]
    hb = (_rms(x, gn_ref[0]) * (1.0 + mod_ref[1]) + mod_ref[0]).astype(BF)

    def col(off, n):
        return _dot(hb, w_ref[:, off:off + n])

    gq_o[...] = col(C_GQ, MIX) * (GLA_DK ** -0.5)
    gk_o[...] = col(C_GK, MIX)
    gv_o[...] = col(C_GV, MIX).astype(BF)
    gr_o[...] = _silu(col(C_GR, MIX))
    logit = _dot(col(C_A, LANES).astype(BF), wgate_ref[...]) + bgate_ref[...]
    la_o[...] = _log_sigmoid(logit) * (1.0 / GLA_TAU)

    sq = col(C_SQ, 2 * MIX)
    sk = col(C_SK, LANES)
    sv = col(C_SV, LANES)
    if latent:
        cs, ss = cs_ref[...], ss_ref[...]
        cs4 = jnp.concatenate([cs] * 4, axis=1)
        ss4 = jnp.concatenate([ss] * 4, axis=1)
        sq = sq * cs4 + col(C_SQR, 2 * MIX) * ss4
        sk = sk * cs + col(C_SKR, LANES) * ss
    sq_o[...] = (sq * (HEAD_DIM ** -0.5)).astype(BF)
    sk_o[...] = sk.astype(sk_o.dtype)
    sv_o[...] = sv.astype(sv_o.dtype)

    zf = col(C_ZF, MIX).astype(BF)
    zc_o[...] = _dot(zf, cdft_ref[...]).astype(BF)
    zs_o[...] = _dot(zf, sdft_ref[...]).astype(BF)

    qn = _rms(col(C_QA, MLA_Q_RANK), gq_ref[...]).astype(BF)
    qm = _dot(qn, wqb_ref[...])
    q_rope = qm[:, MIX:MIX + LANES]
    ckv = _rms(col(C_KVA, MLA_KV_RANK), gkv_ref[...])
    kr = col(C_KR, LANES)
    if not latent:
        ckv_o[...] = ckv
        kr_o[...] = kr[:, :MLA_ROPE]
    else:
        cm, sm = cm_ref[...], sm_ref[...]
        q_rope = q_rope * cm + qm[:, MIX + LANES:MIX + 2 * LANES] * sm
        kr = kr * cm + col(C_KRR, LANES) * sm
    scale = (MLA_NOPE + MLA_ROPE) ** -0.5
    qc_o[...] = (_dot(qm[:, :MIX].astype(BF), wkbd_ref[...]) * scale).astype(BF)
    qr_o[...] = (q_rope * scale).astype(BF)
    kcat_o[:, :LANES] = ckv.astype(BF)
    kcat_o[:, LANES:] = kr.astype(BF)


def _inproj_call(x, mod_l, gn, w, wgate, bgate, gq, gkv, wqb, wkbd, cdft, sdft, tables,
                 batch, seq, rows_per_batch, first_row, latent, tm=256):
    r = x.shape[0]
    per = seq // tm
    row = lambda n: pl.BlockSpec((tm, n), lambda i: (i, 0))
    in_specs = [
        row(D_MODEL),
        _mod_spec(1, tm, rows_per_batch, first_row),
        pl.BlockSpec((1, 1, D_MODEL), lambda i: (0, 0, 0)),
        _const_spec(w.shape), _const_spec(wgate.shape), _const_spec(bgate.shape),
        _const_spec(gq.shape), _const_spec(gkv.shape), _const_spec(wqb.shape), _const_spec(wkbd.shape),
        _const_spec(cdft.shape), _const_spec(sdft.shape),
    ]
    args = [x, mod_l, gn, w, wgate, bgate, gq, gkv, wqb, wkbd, cdft, sdft]
    if latent:
        in_specs += [pl.BlockSpec((tm, LANES), lambda i: (i % per, 0))] * 4
        args += list(tables)
    kvdt = BF if latent else F32
    fft_spec = pl.BlockSpec((tm, MIX), lambda i: (i % per, i // per))
    out_specs = [row(MIX), row(MIX), row(MIX), row(MIX), row(2 * MIX), row(2 * MIX), row(LANES), row(LANES),
                 fft_spec, fft_spec, row(2 * MIX), row(LANES), row(MIX)]
    sd = jax.ShapeDtypeStruct
    out_shape = [sd((r, MIX), F32), sd((r, MIX), F32), sd((r, MIX), BF), sd((r, MIX), F32),
                 sd((r, 2 * MIX), F32), sd((r, 2 * MIX), BF), sd((r, LANES), kvdt), sd((r, LANES), kvdt),
                 sd((seq, batch * MIX), BF), sd((seq, batch * MIX), BF),
                 sd((r, 2 * MIX), BF), sd((r, LANES), BF), sd((r, MIX), BF)]
    if not latent:
        out_specs += [row(MLA_KV_RANK), row(MLA_ROPE)]
        out_shape += [sd((r, MLA_KV_RANK), F32), sd((r, MLA_ROPE), F32)]
    return pl.pallas_call(
        functools.partial(_inproj_kernel, latent=latent),
        grid=(r // tm,),
        in_specs=in_specs,
        out_specs=out_specs,
        out_shape=out_shape,
        compiler_params=_params("parallel"),
        name="inproj_lat" if latent else "inproj_ctx",
    )(*args)


def _gla_kernel(*refs, reverse, final, nchunk):
    q_ref, k_ref, v_ref, la_ref, s0_ref = refs[:5]
    pos = 5
    if final:
        of_ref, r_ref, gout_ref = refs[pos:pos + 3]
        pos += 3
    o_ref, sfin_ref, st_ref = refs[pos:pos + 3]
    i = pl.program_id(1)
    c = GLA_CHUNK

    @pl.when(i == 0)
    def _():
        st_ref[...] = s0_ref[...]

    row = lax.broadcasted_iota(jnp.int32, (c, MIX), 0)
    lane = lax.broadcasted_iota(jnp.int32, (c, MIX), 1) % c
    keep = (lane >= row) if reverse else (lane <= row)
    r64 = lax.broadcasted_iota(jnp.int32, (c, c), 0)
    c64 = lax.broadcasted_iota(jnp.int32, (c, c), 1)
    tri = ((c64 >= r64) if reverse else (c64 <= r64)).astype(BF)
    lane1 = lax.broadcasted_iota(jnp.int32, (1, MIX), 1) // c
    hmask = [(lane1 == h).astype(F32) for h in range(GLA_HEADS)]
    hmask_b = [m.astype(BF) for m in hmask]
    bdiag = (lax.broadcasted_iota(jnp.int32, (MIX, MIX), 0) // c
             == lax.broadcasted_iota(jnp.int32, (MIX, MIX), 1) // c)
    if final:
        avg = jnp.where(bdiag, 1.0 / GLA_DV, 0.0).astype(BF)
        gout = gout_ref[...]

    def chunk(t, carry):
        cc = (nchunk - 1 - t) if reverse else t
        rs = pl.ds(pl.multiple_of(cc * c, c), c)
        la = la_ref[rs, :]
        la_hi = la.astype(BF)
        la_lo = (la - la_hi.astype(F32)).astype(BF)
        cum = _dot(tri, la_hi) + _dot(tri, la_lo)
        tot = cum[0:1] if reverse else cum[c - 1:c]
        q, k, v = q_ref[rs, :], k_ref[rs, :], v_ref[rs, :]
        qd = (q * jnp.exp(cum)).astype(BF)
        ki = k * jnp.exp(-cum)
        kd = (k * jnp.exp(tot - cum)).astype(BF)
        kstack = jnp.concatenate([(ki * m).astype(BF) for m in hmask], axis=0)
        vstack = jnp.concatenate([v * m for m in hmask_b], axis=0)
        att = jnp.where(keep, _dot_nt(qd, kstack), 0.0).astype(BF)
        st = st_ref[...]
        o = _dot(att, vstack) + _dot_nt(qd, st.astype(BF))
        st_ref[...] = st * jnp.exp(tot) + jnp.where(bdiag, _dot_tn(v, kd), 0.0)
        if final:
            o = o + of_ref[rs, :]
            sq = o * o
            sq_hi = sq.astype(BF)
            sq_lo = (sq - sq_hi.astype(F32)).astype(BF)
            ms = _dot(sq_hi, avg) + _dot(sq_lo, avg)
            o_ref[rs, :] = (o * lax.rsqrt(ms + EPS) * gout * r_ref[rs, :]).astype(BF)
        else:
            o_ref[rs, :] = o
        return carry

    lax.fori_loop(0, nchunk, chunk, 0)

    @pl.when(i == pl.num_programs(1) - 1)
    def _():
        sfin_ref[...] = st_ref[...]


def _gla_call(q, k, v, la, s0, extra, batch, seq, reverse, tb):
    nblk = seq // tb
    final = extra is not None
    if reverse:
        rmap = lambda b, i: (b * nblk + nblk - 1 - i, 0)
        lmap = lambda b, i: (b * nblk + nblk - 1 - i, 1)
    else:
        rmap = lambda b, i: (b * nblk + i, 0)
        lmap = rmap
    blk = pl.BlockSpec((tb, MIX), rmap)
    in_specs = [blk, blk, blk, pl.BlockSpec((tb, MIX), lmap),
                pl.BlockSpec((None, MIX, MIX), lambda b, i: (b, 0, 0))]
    args = [q, k, v, la, s0]
    if final:
        in_specs += [blk, blk, pl.BlockSpec((1, MIX), lambda b, i: (0, 0))]
        args += list(extra)
    return pl.pallas_call(
        functools.partial(_gla_kernel, reverse=reverse, final=final, nchunk=tb // GLA_CHUNK),
        grid=(batch, nblk),
        in_specs=in_specs,
        out_specs=[blk, pl.BlockSpec((None, MIX, MIX), lambda b, i: (b, 0, 0))],
        out_shape=[jax.ShapeDtypeStruct((batch * seq, MIX), BF if final else F32),
                   jax.ShapeDtypeStruct((batch, MIX, MIX), F32)],
        scratch_shapes=[pltpu.VMEM((MIX, MIX), F32)],
        compiler_params=_params("parallel", "arbitrary"),
        name="gla_bwd" if reverse else "gla_fwd",
    )(*args)


def _swa_kernel(*refs, local, nb):
    sink_ref, q_ref = refs[:2]
    pos = 2
    if local:
        kp_ref, kc_ref, kn_ref, vp_ref, vc_ref, vn_ref = refs[pos:pos + 6]
        pos += 6
    kx_ref, vx_ref, o_ref = refs[pos:pos + 3]
    i = pl.program_id(1)
    blk = SWA_BLOCK
    row = lax.broadcasted_iota(jnp.int32, (blk, blk), 0)
    lane = lax.broadcasted_iota(jnp.int32, (blk, blk), 1)
    low = lane < HEAD_DIM
    kx = kx_ref[...].astype(BF)
    vx = vx_ref[...].astype(BF)
    if local:
        keep_p = jnp.logical_and(lane >= row, i > 0)
        keep_n = jnp.logical_and(lane <= row, i < nb - 1)
        kp, kc, kn = kp_ref[...], kc_ref[...], kn_ref[...]
        vp, vc, vn = vp_ref[...], vc_ref[...], vn_ref[...]
    outs = []
    for hq in range(SWA_Q_HEADS):
        qh = q_ref[:, hq * LANES:(hq + 1) * LANES]
        sink = sink_ref[hq]
        sx = _dot_nt(qh, kx)
        m = jnp.maximum(jnp.max(sx, axis=-1, keepdims=True), sink)
        if local:
            sp = jnp.where(keep_p, _dot_nt(qh, kp), NEG_INF)
            sc = _dot_nt(qh, kc)
            sn = jnp.where(keep_n, _dot_nt(qh, kn), NEG_INF)
            for s in (sp, sc, sn):
                m = jnp.maximum(m, jnp.max(s, axis=-1, keepdims=True))
        ex = jnp.exp(sx - m)
        den = jnp.sum(ex, axis=-1, keepdims=True) + jnp.exp(sink - m)
        acc = _dot(ex.astype(BF), vx)
        if local:
            for s, vv in ((sp, vp), (sc, vc), (sn, vn)):
                e = jnp.exp(s - m)
                den = den + jnp.sum(e, axis=-1, keepdims=True)
                acc = acc + _dot(e.astype(BF), vv)
        outs.append(acc / den)
    t0 = jnp.where(low, outs[0], pltpu.roll(outs[1], HEAD_DIM, 1))
    t1 = jnp.where(low, pltpu.roll(outs[2], HEAD_DIM, 1), outs[3])
    o_ref[:, :LANES] = t0.astype(BF)
    o_ref[:, LANES:] = t1.astype(BF)


def _swa_call(sink, q, k, v, kx, vx, batch, seq, local):
    nb = seq // SWA_BLOCK
    cx = kx.shape[1]
    qspec = pl.BlockSpec((SWA_BLOCK, 2 * MIX), lambda b, i: (b * nb + i, 0))
    in_specs = [pl.BlockSpec(memory_space=pltpu.SMEM), qspec]
    args = [sink, q]
    if local:
        prv = pl.BlockSpec((SWA_BLOCK, LANES), lambda b, i: (b * nb + jnp.maximum(i - 1, 0), 0))
        cur = pl.BlockSpec((SWA_BLOCK, LANES), lambda b, i: (b * nb + i, 0))
        nxt = pl.BlockSpec((SWA_BLOCK, LANES), lambda b, i: (b * nb + jnp.minimum(i + 1, nb - 1), 0))
        in_specs += [prv, cur, nxt, prv, cur, nxt]
        args += [k, k, k, v, v, v]
    xspec = pl.BlockSpec((None, cx, LANES), lambda b, i: (b, 0, 0))
    in_specs += [xspec, xspec]
    args += [kx, vx]
    return pl.pallas_call(
        functools.partial(_swa_kernel, local=local, nb=nb),
        grid=(batch, nb),
        in_specs=in_specs,
        out_specs=pl.BlockSpec((SWA_BLOCK, MIX), lambda b, i: (b * nb + i, 0)),
        out_shape=jax.ShapeDtypeStruct((batch * seq, MIX), BF),
        compiler_params=_params("parallel", "parallel"),
        name="swa_lat" if local else "swa_ctx",
    )(*args)


def _fft_kernel(xc_ref, xs_ref, yc_ref, ys_ref, zc_ref, zs_ref, o_ref, *, nj, scale):
    yc, ys = yc_ref[...], ys_ref[...]
    xc, xs = xc_ref[...], xs_ref[...]
    acc = None
    for jp in range(nj // 2):
        cos_t, sin_t = [], []
        for j in (2 * jp, 2 * jp + 1):
            a, b = xc[:, j:j + 1], xs[:, j:j + 1]
            cos_t.append(a * yc - b * ys)
            sin_t.append(b * yc + a * ys)
        cos_t = jnp.concatenate(cos_t, axis=1).astype(BF)
        sin_t = jnp.concatenate(sin_t, axis=1).astype(BF)
        rows = slice(jp * MXU_DEPTH, (jp + 1) * MXU_DEPTH)
        d = _dot(cos_t, zc_ref[rows, :]) + _dot(sin_t, zs_ref[rows, :])
        acc = d if acc is None else acc + d
    o_ref[...] = (acc * scale).astype(BF)


def _fft_tables(seq):
    sp = jnp.arange(seq, dtype=jnp.int32)[:, None]
    w = 2.0 * jnp.pi / seq
    ax = ((sp * (jnp.arange(seq // LANES, dtype=jnp.int32) * LANES)[None, :]) % seq).astype(F32) * w
    ay = ((sp * jnp.arange(LANES, dtype=jnp.int32)[None, :]) % seq).astype(F32) * w
    return jnp.cos(ax), jnp.sin(ax), jnp.cos(ay), jnp.sin(ay)


def _fft_call(tables, zc, zs, seq, tm=256):
    xc, xs, yc, ys = tables
    nj = seq // LANES
    width = zc.shape[1]
    return pl.pallas_call(
        functools.partial(_fft_kernel, nj=nj, scale=(seq * FNET_GROUP_CH) ** -0.5),
        grid=(seq // tm,),
        in_specs=[
            pl.BlockSpec((tm, nj), lambda i: (i, 0)), pl.BlockSpec((tm, nj), lambda i: (i, 0)),
            pl.BlockSpec((tm, LANES), lambda i: (i, 0)), pl.BlockSpec((tm, LANES), lambda i: (i, 0)),
            _const_spec(zc.shape), _const_spec(zs.shape),
        ],
        out_specs=pl.BlockSpec((tm, width), lambda i: (i, 0)),
        out_shape=jax.ShapeDtypeStruct((seq, width), BF),
        compiler_params=_params("parallel"),
        name="fft",
    )(xc, xs, yc, ys, zc, zs)


def _mla_kernel(*refs, has_ctx):
    qc_ref, qr_ref, ks_ref = refs[:3]
    pos = 3
    if has_ctx:
        kx_ref = refs[pos]
        pos += 1
    wv_ref, o_ref = refs[pos:pos + 2]
    qr = qr_ref[...]
    lane = lax.broadcasted_iota(jnp.int32, qr.shape, 1) // MLA_ROPE
    tq = qr.shape[0]
    qs = []
    for h in range(MLA_HEADS):
        qrh = jnp.where(lane == h, qr, jnp.zeros_like(qr))
        qs.append(jnp.concatenate([qc_ref[:, h * LANES:(h + 1) * LANES], qrh], axis=1))
    qall = jnp.concatenate(qs, axis=0)
    s1 = _dot_nt(qall, ks_ref[...])
    m = jnp.max(s1, axis=-1, keepdims=True)
    if has_ctx:
        s0 = _dot_nt(qall, kx_ref[...])
        m = jnp.maximum(m, jnp.max(s0, axis=-1, keepdims=True))
    e1 = jnp.exp(s1 - m)
    den = jnp.sum(e1, axis=-1, keepdims=True)
    acc = _dot(e1.astype(BF), ks_ref[:, :LANES])
    if has_ctx:
        e0 = jnp.exp(s0 - m)
        den = den + jnp.sum(e0, axis=-1, keepdims=True)
        acc = acc + _dot(e0.astype(BF), kx_ref[:, :LANES])
    ot = acc / den
    ocat = jnp.concatenate([ot[h * tq:(h + 1) * tq] for h in range(MLA_HEADS)], axis=1).astype(BF)
    o_ref[...] = _dot(ocat, wv_ref[...]).astype(BF)


def _mla_call(qc, qr, kcat, kx, wvbd, batch, seq, tq=128):
    nq = seq // tq
    has_ctx = kx is not None
    in_specs = [pl.BlockSpec((tq, 2 * MIX), lambda b, i: (b * nq + i, 0)),
                pl.BlockSpec((tq, LANES), lambda b, i: (b * nq + i, 0)),
                pl.BlockSpec((None, seq, MIX), lambda b, i: (b, 0, 0))]
    args = [qc, qr, kcat.reshape(batch, seq, MIX)]
    if has_ctx:
        in_specs.append(pl.BlockSpec((None, kx.shape[1], MIX), lambda b, i: (b, 0, 0)))
        args.append(kx)
    in_specs.append(_const_spec(wvbd.shape))
    args.append(wvbd)
    return pl.pallas_call(
        functools.partial(_mla_kernel, has_ctx=has_ctx),
        grid=(batch, nq),
        in_specs=in_specs,
        out_specs=pl.BlockSpec((tq, MIX), lambda b, i: (b * nq + i, 0)),
        out_shape=jax.ShapeDtypeStruct((batch * seq, MIX), BF),
        compiler_params=_params("parallel", "parallel"),
        name="mla_lat" if has_ctx else "mla_ctx",
    )(*args)


def _rot_cols(w, width):
    half = width // 2
    n = w.shape[1]
    idx = jnp.arange(n)
    first = (idx % width) < half
    src = jnp.where(first, idx + half, idx - half)
    return jnp.where(first[None, :], -w[:, src], w[:, src])


def _block_diag(blocks):
    n, r, c = blocks.shape
    eye = jnp.eye(n, dtype=blocks.dtype)
    return jnp.einsum("hrc,hg->hrgc", blocks, eye).reshape(n * r, n * c)


def _layer_weights(w_in, gla_w_gate, gla_b_gate, mla_w_q_b, mla_w_kv_b, latent):
    offs = [0]
    for n in (256, 256, 256, 256, 16, 16, 256, 128, 128, 256, 256, 128, 32):
        offs.append(offs[-1] + n)
    seg = lambda k: w_in[:, offs[k]:offs[k + 1]]
    zeros = lambda n: jnp.zeros((D_MODEL, n), w_in.dtype)
    sq = seg(6).reshape(D_MODEL, SWA_Q_HEADS, HEAD_DIM)
    sq_tiles = []
    for hq in range(SWA_Q_HEADS):
        parts = [zeros(HEAD_DIM), zeros(HEAD_DIM)]
        parts[hq // 2] = sq[:, hq]
        sq_tiles.append(jnp.concatenate(parts, axis=1))
    sq_w = jnp.concatenate(sq_tiles, axis=1)
    kr4 = jnp.tile(seg(12), (1, MLA_HEADS))
    cols = [seg(0), seg(1), seg(2), seg(3), seg(4), seg(5), zeros(LANES - 2 * GLA_GATE_RANK),
            sq_w, seg(7), seg(8), seg(9), seg(10), seg(11), kr4]
    if latent:
        cols += [_rot_cols(sq_w, HEAD_DIM // 2), _rot_cols(seg(7), HEAD_DIM // 2), _rot_cols(kr4, MLA_ROPE // 2)]
    w = jnp.concatenate(cols, axis=1).astype(BF)

    wgate = jnp.zeros((LANES, 2 * MIX), F32)
    wgate = wgate.at[:GLA_GATE_RANK, :MIX].set(gla_w_gate[0])
    wgate = wgate.at[GLA_GATE_RANK:2 * GLA_GATE_RANK, MIX:].set(gla_w_gate[1])
    bgate = gla_b_gate.reshape(1, 2 * MIX)

    wq = mla_w_q_b.reshape(MLA_Q_RANK, MLA_HEADS, MLA_NOPE + MLA_ROPE)
    q_nope = wq[:, :, :MLA_NOPE].reshape(MLA_Q_RANK, MLA_HEADS * MLA_NOPE)
    q_rope = wq[:, :, MLA_NOPE:].reshape(MLA_Q_RANK, MLA_HEADS * MLA_ROPE)
    qcols = [q_nope, q_rope] + ([_rot_cols(q_rope, MLA_ROPE // 2)] if latent else [])
    wqb = jnp.concatenate(qcols, axis=1).astype(BF)
    wkv = mla_w_kv_b.reshape(MLA_KV_RANK, MLA_HEADS, MLA_NOPE + MLA_V)
    wkbd = _block_diag(jnp.transpose(wkv[:, :, :MLA_NOPE], (1, 2, 0))).astype(BF)
    wvbd = _block_diag(jnp.transpose(wkv[:, :, MLA_NOPE:], (1, 0, 2))).astype(BF)
    return w, wgate.astype(BF), bgate, wqb, wkbd, wvbd


def _rope_tables(seq):
    t = jnp.arange(seq)
    rows = (t // GRID_W).astype(F32)[:, None]
    cols = (t % GRID_W).astype(F32)[:, None]

    def table(width):
        half = width // 2
        lane = jnp.arange(LANES)
        inv = ROPE_BASE ** (-(2 * (lane % half)).astype(F32) / width)
        ang = jnp.where(((lane // width) % 2 == 0)[None, :], rows, cols) * inv[None, :]
        return jnp.cos(ang), jnp.sin(ang)

    cs, ss = table(HEAD_DIM // 2)
    cm, sm = table(MLA_ROPE // 2)
    return cs, ss, cm, sm


def _dft_channel():
    c = jnp.arange(FNET_GROUP_CH, dtype=jnp.int32)
    ang = ((c[:, None] * c[None, :]) % FNET_GROUP_CH).astype(F32) * (2.0 * jnp.pi / FNET_GROUP_CH)
    cos_b = jnp.broadcast_to(jnp.cos(ang)[None], (FNET_GROUPS, FNET_GROUP_CH, FNET_GROUP_CH))
    sin_b = jnp.broadcast_to(jnp.sin(ang)[None], (FNET_GROUPS, FNET_GROUP_CH, FNET_GROUP_CH))
    return _block_diag(cos_b).astype(BF), (-_block_diag(sin_b)).astype(BF)


def _state_in(s):
    return jnp.einsum("bhdv,hg->bhvgd", s, jnp.eye(GLA_HEADS, dtype=s.dtype)).reshape(s.shape[0], MIX, MIX)


def _state_out(st):
    blocks = [st[:, h * GLA_DV:(h + 1) * GLA_DV, h * GLA_DK:(h + 1) * GLA_DK] for h in range(GLA_HEADS)]
    return jnp.swapaxes(jnp.stack(blocks, axis=1), 2, 3)


def _token_mix(x, mod_l, gn_l, wts, misc, st0, kv_ctx, batch, seq, rows_per_batch, first_row, latent):
    w, wgate, bgate, wqb, wkbd, wvbd = wts
    gq, gkv, gout, sink, cdft, sdft, rope_t, fft_t = misc
    outs = _inproj_call(x, mod_l, gn_l[2:3], w, wgate, bgate, gq, gkv, wqb, wkbd, cdft, sdft, rope_t,
                        batch, seq, rows_per_batch, first_row, latent)
    gqv, gkv_, gv, gr, la, sq, sk, sv, zc, zs, qc, qr, kcat = outs[:13]
    tb = min(seq, 512)
    o_f, st_f = _gla_call(gqv, gkv_, gv, la, st0[0], None, batch, seq, False, tb)
    o_gla, st_b = _gla_call(gqv, gkv_, gv, la, st0[1], (o_f, gr, gout), batch, seq, True, tb)
    if latent:
        o_swa = _swa_call(sink, sq, sk, sv, kv_ctx[0], kv_ctx[1], batch, seq, True)
        o_mla = _mla_call(qc, qr, kcat, kv_ctx[2], wvbd, batch, seq)
        new_ctx = None
    else:
        k3 = sk.reshape(batch, seq, LANES)
        v3 = sv.reshape(batch, seq, LANES)
        o_swa = _swa_call(sink, sq, None, None, k3, v3, batch, seq, False)
        o_mla = _mla_call(qc, qr, kcat, None, wvbd, batch, seq)
        new_ctx = (st_f, st_b, sk, sv, outs[13], outs[14])
    o_fft = _fft_call(fft_t, zc, zs, seq)
    return (o_gla, o_swa, o_fft, o_mla), new_ctx


def kernel(x_prompt, x_sample, c, state_gla, cache_swa_k, cache_swa_v, cache_mla_ckv, cache_mla_krope,
           c_ctx, w_mod, b_mod, g_norm, w_ffn_gate, w_ffn_up, w_ffn_down, w_in, gla_w_gate, gla_b_gate,
           gla_g_out, swa_sink, mla_g_q, mla_g_kv, mla_w_q_b, mla_w_kv_b, w_out):
    nb, ns, _ = x_prompt.shape
    db, dsq, _ = x_sample.shape
    past = cache_swa_k.shape[2]

    cvec = jnp.zeros((MOD_ROWS, D_MODEL), F32).at[0].set(c_ctx).at[1:1 + db].set(c)
    mod = _mod_call(cvec, w_mod, b_mod)

    wg_all = w_ffn_gate.astype(BF)
    wu_all = w_ffn_up.astype(BF)
    wd_all = w_ffn_down.astype(BF)
    wo_all = w_out.astype(BF)
    cdft, sdft = _dft_channel()
    rope_t = _rope_tables(dsq)
    fft_ctx, fft_lat = _fft_tables(ns), _fft_tables(dsq)

    xc = x_prompt.reshape(nb * ns, D_MODEL)
    xl = x_sample.reshape(db * dsq, D_MODEL)
    st_gla, st_k, st_v, st_ckv, st_kr = [], [], [], [], []
    zero_state = jnp.zeros((nb, MIX, MIX), F32)
    for l in range(DEPTH):
        mod_l = mod[l].reshape(MOD_ROWS, N_MOD, 1, D_MODEL)
        gn_l = g_norm[l].reshape(6, 1, D_MODEL)
        ffn = [(wg_all[l, s], wu_all[l, s], wd_all[l, s]) for s in range(2)]
        gq = mla_g_q[l].reshape(1, MLA_Q_RANK)
        gkv = mla_g_kv[l].reshape(1, MLA_KV_RANK)
        gout = jnp.tile(gla_g_out[l], GLA_HEADS).reshape(1, MIX)
        sink = swa_sink[l]

        wts = _layer_weights(w_in[l], gla_w_gate[l], gla_b_gate[l], mla_w_q_b[l], mla_w_kv_b[l], False)
        misc = (gq, gkv, gout, sink, cdft, sdft, None, fft_ctx)
        xc = _ffn_call(xc, mod_l, gn_l[0:2], *ffn[0], rows_per_batch=nb * ns, first_row=0)
        mixed, new_ctx = _token_mix(xc, mod_l, gn_l, wts, misc, (zero_state, zero_state), None,
                                    nb, ns, nb * ns, 0, False)
        xc = _outffn_call(*mixed, xc, mod_l, gn_l[3:6], wo_all[l], *ffn[1],
                          seq=ns, rows_per_batch=nb * ns, first_row=0, tm=256)
        st_f, st_b, k_c, v_c, ckv_c, kr_c = new_ctx
        st_gla.append(jnp.stack([_state_out(st_f), _state_out(st_b)], axis=1))
        st_k.append(k_c.reshape(nb, ns, SWA_KV_HEADS, HEAD_DIM))
        st_v.append(v_c.reshape(nb, ns, SWA_KV_HEADS, HEAD_DIM))
        st_ckv.append(ckv_c.reshape(nb, ns, MLA_KV_RANK))
        st_kr.append(kr_c.reshape(nb, ns, MLA_ROPE))

        wts = _layer_weights(w_in[l], gla_w_gate[l], gla_b_gate[l], mla_w_q_b[l], mla_w_kv_b[l], True)
        misc = (gq, gkv, gout, sink, cdft, sdft, rope_t, fft_lat)
        st0 = (_state_in(state_gla[:, l, 0]), _state_in(state_gla[:, l, 1]))
        kx = cache_swa_k[:, l].reshape(db, past, LANES)
        vx = cache_swa_v[:, l].reshape(db, past, LANES)
        mx = jnp.concatenate([cache_mla_ckv[:, l], jnp.tile(cache_mla_krope[:, l], (1, 1, MLA_HEADS))],
                             axis=-1).astype(BF)
        xl = _ffn_call(xl, mod_l, gn_l[0:2], *ffn[0], rows_per_batch=dsq, first_row=1)
        mixed, _ = _token_mix(xl, mod_l, gn_l, wts, misc, st0, (kx, vx, mx), db, dsq, dsq, 1, True)
        xl = _outffn_call(*mixed, xl, mod_l, gn_l[3:6], wo_all[l], *ffn[1],
                          seq=dsq, rows_per_batch=dsq, first_row=1, tm=512)

    return (xc.reshape(nb, ns, D_MODEL), xl.reshape(db, dsq, D_MODEL), jnp.stack(st_gla, axis=1),
            jnp.stack(st_k, axis=1), jnp.stack(st_v, axis=1), jnp.stack(st_ckv, axis=1),
            jnp.stack(st_kr, axis=1))
```

```python
import functools

import jax
import jax.numpy as jnp
from jax import lax
from jax.experimental import pallas as pl
from jax.experimental.pallas import tpu as pltpu

D_MODEL = 1024
DEPTH = 4
GRID_W = 64
HEAD_DIM = 64
GLA_HEADS = 4
GLA_DK = 64
GLA_DV = 64
GLA_GATE_RANK = 16
GLA_TAU = 16.0
GLA_CHUNK = 64
SWA_Q_HEADS = 4
SWA_KV_HEADS = 2
SWA_BLOCK = 128
FNET_GROUPS = 4
FNET_GROUP_CH = 64
MLA_HEADS = 4
MLA_Q_RANK = 256
MLA_KV_RANK = 128
MLA_NOPE = 64
MLA_ROPE = 32
MLA_V = 64
D_FF = 2816
FFN_RES = 0.5
N_MOD = 9
ROPE_BASE = 10000.0
EPS = 1e-6
NEG_INF = -1e30

MIX = 256
LANES = 128
MXU_DEPTH = 256
FF_CHUNK = 256
MLA_KEY_TILE = 1024
GLA_GROUP = 4
MOD_ROWS = 8
VMEM_LIMIT = 56 * 1024 * 1024

BF = jnp.bfloat16
F32 = jnp.float32

C_GQ, C_GK, C_GV, C_GR, C_A = 0, 256, 512, 768, 1024
C_SQ, C_SK, C_SV, C_ZF, C_QA, C_KVA, C_KR = 1152, 1664, 1792, 1920, 2176, 2432, 2560
C_SQR, C_SKR, C_KRR = 2688, 3200, 3328
NC_CTX, NC_LAT = 2688, 3456


def _dot(a, b):
    return jnp.dot(a, b, preferred_element_type=F32)


def _dot_nt(a, b):
    return lax.dot_general(a, b, (((1,), (1,)), ((), ())), preferred_element_type=F32)


def _dot_tn(a, b):
    return lax.dot_general(a, b, (((0,), (0,)), ((), ())), preferred_element_type=F32)


def _rms(x, g):
    return x * lax.rsqrt(jnp.mean(x * x, axis=-1, keepdims=True) + EPS) * g


def _silu(x):
    return x * jax.nn.sigmoid(x)


def _params(*sem):
    return pltpu.CompilerParams(dimension_semantics=sem, vmem_limit_bytes=VMEM_LIMIT)


def _pick(arr, *lead, block=None):
    tail = tuple(arr.shape[len(lead):]) if block is None else tuple(block)
    zeros = (0,) * len(tail)
    return pl.BlockSpec((None,) * len(lead) + tail, lambda *_: tuple(lead) + zeros,
                        pipeline_mode=pl.Buffered(1))


def _mod_kernel(c_ref, w_ref, b_ref, o_ref):
    s = _silu(c_ref[...]).astype(BF)
    o_ref[...] = _dot(s, w_ref[...].astype(BF)) + b_ref[...]


def _mod_call(cvec, w_mod, b_mod):
    nl = w_mod.shape[0]
    tn = D_MODEL
    return pl.pallas_call(
        _mod_kernel,
        grid=(nl, N_MOD * D_MODEL // tn),
        in_specs=[
            pl.BlockSpec((MOD_ROWS, D_MODEL), lambda l, j: (0, 0)),
            pl.BlockSpec((None, D_MODEL, tn), lambda l, j: (l, 0, j)),
            pl.BlockSpec((None, 1, tn), lambda l, j: (l, 0, j)),
        ],
        out_specs=pl.BlockSpec((None, MOD_ROWS, tn), lambda l, j: (l, 0, j)),
        out_shape=jax.ShapeDtypeStruct((nl, MOD_ROWS, N_MOD * D_MODEL), F32),
        compiler_params=_params("parallel", "parallel"),
        name="mod",
    )(cvec, w_mod, b_mod.reshape(nl, 1, N_MOD * D_MODEL))


def _mod_spec(l, sub, tm, rows_per_batch, first_row):
    per = rows_per_batch // tm
    return pl.BlockSpec((None, None, 3, 1, D_MODEL), lambda i: (l, first_row + i // per, sub, 0, 0))


def _gn_spec(l, first, n):
    return pl.BlockSpec((None, n, 1, D_MODEL), lambda i: (l, first // n, 0, 0))


def _ffn_body(x, mod_ref, g_pre, g_post, wg_ref, wu_ref, wd_ref):
    sh, sc, gt = mod_ref[0], mod_ref[1], mod_ref[2]
    hb = (_rms(x, g_pre) * (1.0 + sc) + sh).astype(BF)
    acc = None
    for j in range(D_FF // FF_CHUNK):
        sl = slice(j * FF_CHUNK, (j + 1) * FF_CHUNK)
        g = _dot(hb, wg_ref[:, sl])
        u = _dot(hb, wu_ref[:, sl])
        d = _dot((_silu(g) * u).astype(BF), wd_ref[sl, :])
        acc = d if acc is None else acc + d
    return x + FFN_RES * gt * _rms(acc, g_post)


def _ffn_kernel(x_ref, mod_ref, gn_ref, wg_ref, wu_ref, wd_ref, o_ref):
    o_ref[...] = _ffn_body(x_ref[...], mod_ref, gn_ref[0], gn_ref[1], wg_ref, wu_ref, wd_ref)


def _ffn_call(x, mod, gn, wg, wu, wd, l, rows_per_batch, first_row, tm=512):
    r = x.shape[0]
    return pl.pallas_call(
        _ffn_kernel,
        grid=(r // tm,),
        in_specs=[
            pl.BlockSpec((tm, D_MODEL), lambda i: (i, 0)),
            _mod_spec(l, 0, tm, rows_per_batch, first_row),
            _gn_spec(l, 0, 2),
            _pick(wg, l, 0), _pick(wu, l, 0), _pick(wd, l, 0),
        ],
        out_specs=pl.BlockSpec((tm, D_MODEL), lambda i: (i, 0)),
        out_shape=jax.ShapeDtypeStruct(x.shape, F32),
        compiler_params=_params("parallel"),
        name="ffn",
    )(x, mod, gn, wg, wu, wd)


def _outffn_kernel(og_ref, os_ref, of_ref, om_ref, x_ref, mod2_ref, mod3_ref, gn_ref, wo_ref,
                   wg_ref, wu_ref, wd_ref, o_ref):
    mix = jnp.concatenate([og_ref[...], os_ref[...], of_ref[...], om_ref[...]], axis=1)
    x = x_ref[...] + mod2_ref[2] * _rms(_dot(mix, wo_ref[...]), gn_ref[0])
    o_ref[...] = _ffn_body(x, mod3_ref, gn_ref[1], gn_ref[2], wg_ref, wu_ref, wd_ref)


def _outffn_call(og, osw, of, om, x, mod, gn, wo, wg, wu, wd, l, seq, rows_per_batch, first_row, tm):
    r = x.shape[0]
    per = seq // tm
    mspec = pl.BlockSpec((tm, MIX), lambda i: (i, 0))
    return pl.pallas_call(
        _outffn_kernel,
        grid=(r // tm,),
        in_specs=[
            mspec, mspec,
            pl.BlockSpec((tm, MIX), lambda i: (i % per, i // per)),
            mspec,
            pl.BlockSpec((tm, D_MODEL), lambda i: (i, 0)),
            _mod_spec(l, 1, tm, rows_per_batch, first_row),
            _mod_spec(l, 2, tm, rows_per_batch, first_row),
            _gn_spec(l, 3, 3),
            _pick(wo, l), _pick(wg, l, 1), _pick(wu, l, 1), _pick(wd, l, 1),
        ],
        out_specs=pl.BlockSpec((tm, D_MODEL), lambda i: (i, 0)),
        out_shape=jax.ShapeDtypeStruct(x.shape, F32),
        compiler_params=_params("parallel"),
        name="outffn",
    )(og, osw, of, om, x, mod, mod, gn, wo, wg, wu, wd)


def _log_sigmoid(x):
    return -(jnp.maximum(-x, 0.0) + jnp.log(1.0 + jnp.exp(-jnp.abs(x))))


def _inproj_kernel(*refs, latent):
    (x_ref, mod_ref, gn_ref, w_ref, wgate_ref, bgate_ref, gq_ref, gkv_ref, wqb_ref, wkbd_ref,
     cdft_ref, sdft_ref) = refs[:12]
    pos = 12
    if latent:
        cs_ref, ss_ref, cm_ref, sm_ref = refs[pos:pos + 4]
        pos += 4
    (gq_o, gk_o, gv_o, gr_o, la_o, sq_o, sk_o, sv_o, zc_o, zs_o, qc_o, qr_o, kcat_o) = refs[pos:pos + 13]
    pos += 13
    if not latent:
        ckv_o, kr_o = refs[pos:pos + 2]

    x = x_ref[...]
    hb = (_rms(x, gn_ref[0]) * (1.0 + mod_ref[1]) + mod_ref[0]).astype(BF)

    def col(off, n):
        return _dot(hb, w_ref[:, off:off + n])

    gq_o[...] = col(C_GQ, MIX) * (GLA_DK ** -0.5)
    gk_o[...] = col(C_GK, MIX)
    gv_o[...] = col(C_GV, MIX).astype(BF)
    gr_o[...] = _silu(col(C_GR, MIX))
    logit = _dot(col(C_A, LANES).astype(BF), wgate_ref[...]) + bgate_ref[...]
    la_o[...] = _log_sigmoid(logit) * (1.0 / GLA_TAU)

    sq = col(C_SQ, 2 * MIX)
    sk = col(C_SK, LANES)
    sv = col(C_SV, LANES)
    if latent:
        cs, ss = cs_ref[...], ss_ref[...]
        cs4 = jnp.concatenate([cs] * 4, axis=1)
        ss4 = jnp.concatenate([ss] * 4, axis=1)
        sq = sq * cs4 + col(C_SQR, 2 * MIX) * ss4
        sk = sk * cs + col(C_SKR, LANES) * ss
    sq_o[...] = (sq * (HEAD_DIM ** -0.5)).astype(BF)
    sk_o[...] = sk.astype(sk_o.dtype)
    sv_o[...] = sv.astype(sv_o.dtype)

    zf = col(C_ZF, MIX).astype(BF)
    zc_o[...] = _dot(zf, cdft_ref[...]).astype(BF)
    zs_o[...] = _dot(zf, sdft_ref[...]).astype(BF)

    qn = _rms(col(C_QA, MLA_Q_RANK), gq_ref[...]).astype(BF)
    qm = _dot(qn, wqb_ref[...])
    q_rope = qm[:, MIX:MIX + LANES]
    ckv = _rms(col(C_KVA, MLA_KV_RANK), gkv_ref[...])
    kr = col(C_KR, LANES)
    if not latent:
        ckv_o[...] = ckv
        kr_o[...] = kr[:, :MLA_ROPE]
    else:
        cm, sm = cm_ref[...], sm_ref[...]
        q_rope = q_rope * cm + qm[:, MIX + LANES:MIX + 2 * LANES] * sm
        kr = kr * cm + col(C_KRR, LANES) * sm
    scale = (MLA_NOPE + MLA_ROPE) ** -0.5
    qc_o[...] = (_dot(qm[:, :MIX].astype(BF), wkbd_ref[...]) * scale).astype(BF)
    qr_o[...] = (q_rope * scale).astype(BF)
    kcat_o[:, :LANES] = ckv.astype(BF)
    kcat_o[:, LANES:] = kr.astype(BF)


def _inproj_call(x, mod, gn, wts, gq, gkv, cdft, sdft, tables, l, batch, seq, rows_per_batch, first_row,
                 latent, tm):
    w, wgate, bgate, wqb, wkbd = wts
    r = x.shape[0]
    per = seq // tm
    row = lambda n: pl.BlockSpec((tm, n), lambda i: (i, 0))
    ncol = NC_LAT if latent else NC_CTX
    nq = wqb.shape[-1] if latent else MIX + LANES
    in_specs = [
        row(D_MODEL),
        _mod_spec(l, 1, tm, rows_per_batch, first_row),
        _gn_spec(l, 2, 1),
        _pick(w, l, block=(D_MODEL, ncol)), _pick(wgate, l), _pick(bgate, l), _pick(gq, l), _pick(gkv, l),
        _pick(wqb, l, block=(MLA_Q_RANK, nq)), _pick(wkbd, l), _pick(cdft), _pick(sdft),
    ]
    args = [x, mod, gn, w, wgate, bgate, gq, gkv, wqb, wkbd, cdft, sdft]
    if latent:
        in_specs += [pl.BlockSpec((tm, LANES), lambda i: (i % per, 0))] * 4
        args += list(tables)
    kvdt = BF if latent else F32
    fft_spec = pl.BlockSpec((tm, MIX), lambda i: (i % per, i // per))
    out_specs = [row(MIX), row(MIX), row(MIX), row(MIX), row(2 * MIX), row(2 * MIX), row(LANES), row(LANES),
                 fft_spec, fft_spec, row(2 * MIX), row(LANES), row(MIX)]
    sd = jax.ShapeDtypeStruct
    out_shape = [sd((r, MIX), F32), sd((r, MIX), F32), sd((r, MIX), BF), sd((r, MIX), F32),
                 sd((r, 2 * MIX), F32), sd((r, 2 * MIX), BF), sd((r, LANES), kvdt), sd((r, LANES), kvdt),
                 sd((seq, batch * MIX), BF), sd((seq, batch * MIX), BF),
                 sd((r, 2 * MIX), BF), sd((r, LANES), BF), sd((r, MIX), BF)]
    if not latent:
        out_specs += [row(MLA_KV_RANK), row(MLA_ROPE)]
        out_shape += [sd((r, MLA_KV_RANK), F32), sd((r, MLA_ROPE), F32)]
    return pl.pallas_call(
        functools.partial(_inproj_kernel, latent=latent),
        grid=(r // tm,),
        in_specs=in_specs,
        out_specs=out_specs,
        out_shape=out_shape,
        compiler_params=_params("parallel"),
        name="inproj_lat" if latent else "inproj_ctx",
    )(*args)


def _gla_kernel(*refs, reverse, final, nchunk, group):
    q_ref, k_ref, v_ref, la_ref, s0_ref = refs[:5]
    pos = 5
    if final:
        of_ref, r_ref, gout_ref = refs[pos:pos + 3]
        pos += 3
    o_ref, sfin_ref, st_ref = refs[pos:pos + 3]
    i = pl.program_id(1)
    c = GLA_CHUNK

    @pl.when(i == 0)
    def _():
        st_ref[...] = s0_ref[...]

    row = lax.broadcasted_iota(jnp.int32, (c, MIX), 0)
    lane = lax.broadcasted_iota(jnp.int32, (c, MIX), 1) % c
    keep = (lane >= row) if reverse else (lane <= row)
    r64 = lax.broadcasted_iota(jnp.int32, (c, c), 0)
    c64 = lax.broadcasted_iota(jnp.int32, (c, c), 1)
    tri = ((c64 >= r64) if reverse else (c64 <= r64)).astype(BF)
    lane1 = lax.broadcasted_iota(jnp.int32, (1, MIX), 1) // c
    hmask = [(lane1 == h).astype(F32) for h in range(GLA_HEADS)]
    hmask_b = [m.astype(BF) for m in hmask]
    bdiag = (lax.broadcasted_iota(jnp.int32, (MIX, MIX), 0) // c
             == lax.broadcasted_iota(jnp.int32, (MIX, MIX), 1) // c)
    if final:
        avg = jnp.where(bdiag, 1.0 / GLA_DV, 0.0).astype(BF)
        gout = gout_ref[...]

    def chunk(t, carry):
        cc = (nchunk - 1 - t) if reverse else t
        rs = pl.ds(pl.multiple_of(cc * c, c), c)
        for g in range(group):
            la = la_ref[g, rs, :]
            la_hi = la.astype(BF)
            la_lo = (la - la_hi.astype(F32)).astype(BF)
            cum = _dot(tri, la_hi) + _dot(tri, la_lo)
            tot = cum[0:1] if reverse else cum[c - 1:c]
            q, k, v = q_ref[g, rs, :], k_ref[g, rs, :], v_ref[g, rs, :]
            qd = (q * jnp.exp(cum)).astype(BF)
            ki = k * jnp.exp(-cum)
            kd = (k * jnp.exp(tot - cum)).astype(BF)
            kstack = jnp.concatenate([(ki * m).astype(BF) for m in hmask], axis=0)
            vstack = jnp.concatenate([v * m for m in hmask_b], axis=0)
            att = jnp.where(keep, _dot_nt(qd, kstack), 0.0).astype(BF)
            st = st_ref[g]
            o = _dot(att, vstack) + _dot_nt(qd, st.astype(BF))
            st_ref[g] = st * jnp.exp(tot) + jnp.where(bdiag, _dot_tn(v, kd), 0.0)
            if final:
                o = o + of_ref[g, rs, :]
                sq = o * o
                sq_hi = sq.astype(BF)
                sq_lo = (sq - sq_hi.astype(F32)).astype(BF)
                ms = _dot(sq_hi, avg) + _dot(sq_lo, avg)
                o_ref[g, rs, :] = (o * lax.rsqrt(ms + EPS) * gout * r_ref[g, rs, :]).astype(BF)
            else:
                o_ref[g, rs, :] = o
        return carry

    lax.fori_loop(0, nchunk, chunk, 0)

    @pl.when(i == pl.num_programs(1) - 1)
    def _():
        sfin_ref[...] = st_ref[...]


def _gla_call(q, k, v, la, s0, s0_idx, extra, batch, seq, reverse, tb, group=GLA_GROUP):
    nblk = seq // tb
    final = extra is not None
    pos = (lambda i: nblk - 1 - i) if reverse else (lambda i: i)
    blk = pl.BlockSpec((group, tb, MIX), lambda b, i: (b, pos(i), 0))
    lblk = pl.BlockSpec((group, tb, MIX), lambda b, i: (b, pos(i), 1 if reverse else 0))
    sblk = pl.BlockSpec((group, MIX, MIX), lambda b, i: (b, 0, 0))
    s0blk = pl.BlockSpec((group, None, None, MIX, MIX), lambda b, i: (b,) + tuple(s0_idx) + (0, 0))
    r3 = lambda a: a.reshape(batch, seq, a.shape[-1])
    in_specs = [blk, blk, blk, lblk, s0blk]
    args = [r3(q), r3(k), r3(v), r3(la), s0]
    if final:
        in_specs += [blk, blk, _pick(extra[2], extra[3])]
        args += [r3(extra[0]), r3(extra[1]), extra[2]]
    o, st = pl.pallas_call(
        functools.partial(_gla_kernel, reverse=reverse, final=final, nchunk=tb // GLA_CHUNK, group=group),
        grid=(batch // group, nblk),
        in_specs=in_specs,
        out_specs=[blk, sblk],
        out_shape=[jax.ShapeDtypeStruct((batch, seq, MIX), BF if final else F32),
                   jax.ShapeDtypeStruct((batch, MIX, MIX), F32)],
        scratch_shapes=[pltpu.VMEM((group, MIX, MIX), F32)],
        compiler_params=_params("parallel", "arbitrary"),
        name="gla_bwd" if reverse else "gla_fwd",
    )(*args)
    return o.reshape(batch * seq, MIX), st


def _swa_kernel(*refs, local, layer):
    sink_ref, q_ref = refs[:2]
    pos = 2
    if local:
        kp_ref, kc_ref, kn_ref, vp_ref, vc_ref, vn_ref = refs[pos:pos + 6]
        pos += 6
    kx_ref, vx_ref, o_ref = refs[pos:pos + 3]
    i = pl.program_id(1)
    last = pl.num_programs(1) - 1
    blk = SWA_BLOCK
    tq = q_ref.shape[0]
    cx = kx_ref.shape[0]
    kx = kx_ref[...].astype(BF)
    vx = vx_ref[...].astype(BF)
    if local:
        nsub = tq // blk
        kblocks = [kp_ref[...]] + [kc_ref[j * blk:(j + 1) * blk, :] for j in range(nsub)] + [kn_ref[...]]
        vblocks = [vp_ref[...]] + [vc_ref[j * blk:(j + 1) * blk, :] for j in range(nsub)] + [vn_ref[...]]
        row = lax.broadcasted_iota(jnp.int32, (blk, blk), 0)
        col = lax.broadcasted_iota(jnp.int32, (blk, blk), 1)
        full = jnp.full((blk, blk), True)
        fullx = jnp.full((blk, cx), True)
    else:
        nsub = 1
    rq = tq // nsub
    rows = lax.broadcasted_iota(jnp.int32, (2 * rq, 1), 0)
    low = lax.broadcasted_iota(jnp.int32, (rq, LANES), 1) < HEAD_DIM
    for j in range(nsub):
        if local:
            kcat = jnp.concatenate([kx, kblocks[j], kblocks[j + 1], kblocks[j + 2]], axis=0)
            vcat = jnp.concatenate([vx, vblocks[j], vblocks[j + 1], vblocks[j + 2]], axis=0)
            prev_ok = (col >= row) if j > 0 else jnp.logical_and(col >= row, i > 0)
            next_ok = (col <= row) if j < nsub - 1 else jnp.logical_and(col <= row, i < last)
            keep = jnp.concatenate([fullx, prev_ok, full, next_ok], axis=1)
            keep = jnp.concatenate([keep, keep], axis=0)
        else:
            kcat, vcat = kx, vx
        rs = slice(j * rq, (j + 1) * rq)
        tiles = []
        for g in range(SWA_KV_HEADS):
            q2 = jnp.concatenate([q_ref[rs, (2 * g) * LANES:(2 * g + 1) * LANES],
                                  q_ref[rs, (2 * g + 1) * LANES:(2 * g + 2) * LANES]], axis=0)
            sink = jnp.where(rows < rq, sink_ref[layer, 2 * g], sink_ref[layer, 2 * g + 1])
            s = _dot_nt(q2, kcat)
            if local:
                s = jnp.where(keep, s, NEG_INF)
            m = jnp.maximum(jnp.max(s, axis=-1, keepdims=True), sink)
            e = jnp.exp(s - m)
            den = jnp.sum(e, axis=-1, keepdims=True) + jnp.exp(sink - m)
            o2 = _dot(e.astype(BF), vcat) / den
            oa, ob = o2[:rq], o2[rq:]
            if g == 0:
                tiles.append(jnp.where(low, oa, pltpu.roll(ob, HEAD_DIM, 1)))
            else:
                tiles.append(jnp.where(low, pltpu.roll(oa, HEAD_DIM, 1), ob))
        o_ref[rs, :LANES] = tiles[0].astype(BF)
        o_ref[rs, LANES:] = tiles[1].astype(BF)


def _swa_call(sinks, layer, q, k, v, kx, vx, x_layer, batch, seq, local, tq=256):
    nt = seq // tq
    per = tq // SWA_BLOCK
    nb = seq // SWA_BLOCK
    cx = kx.shape[2]
    qspec = pl.BlockSpec((tq, 2 * MIX), lambda b, i: (b * nt + i, 0))
    in_specs = [pl.BlockSpec(memory_space=pltpu.SMEM), qspec]
    args = [sinks, q]
    if local:
        prv = pl.BlockSpec((SWA_BLOCK, LANES), lambda b, i: (b * nb + jnp.maximum(i * per - 1, 0), 0))
        cur = pl.BlockSpec((tq, LANES), lambda b, i: (b * nt + i, 0))
        nxt = pl.BlockSpec((SWA_BLOCK, LANES), lambda b, i: (b * nb + jnp.minimum((i + 1) * per, nb - 1), 0))
        in_specs += [prv, cur, nxt, prv, cur, nxt]
        args += [k, k, k, v, v, v]
    xspec = pl.BlockSpec((None, None, cx, LANES), lambda b, i: (b, x_layer, 0, 0))
    in_specs += [xspec, xspec]
    args += [kx, vx]
    return pl.pallas_call(
        functools.partial(_swa_kernel, local=local, layer=layer),
        grid=(batch, nt),
        in_specs=in_specs,
        out_specs=pl.BlockSpec((tq, MIX), lambda b, i: (b * nt + i, 0)),
        out_shape=jax.ShapeDtypeStruct((batch * seq, MIX), BF),
        compiler_params=_params("parallel", "parallel"),
        name="swa_lat" if local else "swa_ctx",
    )(*args)


def _fft_kernel(xc_ref, xs_ref, yc_ref, ys_ref, zc_ref, zs_ref, o_ref, *, nj, scale):
    yc, ys = yc_ref[...], ys_ref[...]
    xc, xs = xc_ref[...], xs_ref[...]
    acc = None
    for jp in range(nj // 2):
        cos_t, sin_t = [], []
        for j in (2 * jp, 2 * jp + 1):
            a, b = xc[:, j:j + 1], xs[:, j:j + 1]
            cos_t.append(a * yc - b * ys)
            sin_t.append(b * yc + a * ys)
        cos_t = jnp.concatenate(cos_t, axis=1).astype(BF)
        sin_t = jnp.concatenate(sin_t, axis=1).astype(BF)
        rows = slice(jp * MXU_DEPTH, (jp + 1) * MXU_DEPTH)
        d = _dot(cos_t, zc_ref[rows, :]) + _dot(sin_t, zs_ref[rows, :])
        acc = d if acc is None else acc + d
    o_ref[...] = (acc * scale).astype(BF)


def _fft_tables(seq):
    sp = jnp.arange(seq, dtype=jnp.int32)[:, None]
    w = 2.0 * jnp.pi / seq
    ax = ((sp * (jnp.arange(seq // LANES, dtype=jnp.int32) * LANES)[None, :]) % seq).astype(F32) * w
    ay = ((sp * jnp.arange(LANES, dtype=jnp.int32)[None, :]) % seq).astype(F32) * w
    return jnp.cos(ax), jnp.sin(ax), jnp.cos(ay), jnp.sin(ay)


def _fft_call(tables, zc, zs, seq, tm=256):
    xc, xs, yc, ys = tables
    nj = seq // LANES
    width = zc.shape[1]
    return pl.pallas_call(
        functools.partial(_fft_kernel, nj=nj, scale=(seq * FNET_GROUP_CH) ** -0.5),
        grid=(seq // tm,),
        in_specs=[
            pl.BlockSpec((tm, nj), lambda i: (i, 0)), pl.BlockSpec((tm, nj), lambda i: (i, 0)),
            pl.BlockSpec((tm, LANES), lambda i: (i, 0)), pl.BlockSpec((tm, LANES), lambda i: (i, 0)),
            _pick(zc), _pick(zs),
        ],
        out_specs=pl.BlockSpec((tm, width), lambda i: (i, 0)),
        out_shape=jax.ShapeDtypeStruct((seq, width), BF),
        compiler_params=_params("parallel"),
        name="fft",
    )(xc, xs, yc, ys, zc, zs)


def _mla_kernel(*refs, has_ctx):
    qc_ref, qr_ref, ks_ref = refs[:3]
    pos = 3
    if has_ctx:
        kx_ref = refs[pos]
        pos += 1
    wv_ref, o_ref = refs[pos:pos + 2]
    qr = qr_ref[...]
    lane = lax.broadcasted_iota(jnp.int32, qr.shape, 1) // MLA_ROPE
    tq = qr.shape[0]
    qs = []
    for h in range(MLA_HEADS):
        qrh = jnp.where(lane == h, qr, jnp.zeros_like(qr))
        qs.append(jnp.concatenate([qc_ref[:, h * LANES:(h + 1) * LANES], qrh], axis=1))
    qall = jnp.concatenate(qs, axis=0)
    tiles = [(kx_ref, 0, kx_ref.shape[0])] if has_ctx else []
    nk = ks_ref.shape[0]
    tk = min(nk, MLA_KEY_TILE)
    tiles += [(ks_ref, j * tk, tk) for j in range(nk // tk)]
    m = den = acc = None
    for ref, lo, n in tiles:
        kt = ref[lo:lo + n, :]
        s = _dot_nt(qall, kt)
        mt = jnp.max(s, axis=-1, keepdims=True)
        if m is None:
            m_new = mt
        else:
            m_new = jnp.maximum(m, mt)
            alpha = jnp.exp(m - m_new)
        p = jnp.exp(s - m_new)
        pv = _dot(p.astype(BF), kt[:, :LANES])
        ps = jnp.sum(p, axis=-1, keepdims=True)
        if m is None:
            den, acc = ps, pv
        else:
            den = alpha * den + ps
            acc = alpha * acc + pv
        m = m_new
    ot = acc / den
    ocat = jnp.concatenate([ot[h * tq:(h + 1) * tq] for h in range(MLA_HEADS)], axis=1).astype(BF)
    o_ref[...] = _dot(ocat, wv_ref[...]).astype(BF)


def _mla_call(qc, qr, kcat, kx, wvbd, l, batch, seq, tq=128):
    nq = seq // tq
    has_ctx = kx is not None
    in_specs = [pl.BlockSpec((tq, 2 * MIX), lambda b, i: (b * nq + i, 0)),
                pl.BlockSpec((tq, LANES), lambda b, i: (b * nq + i, 0)),
                pl.BlockSpec((None, seq, MIX), lambda b, i: (b, 0, 0))]
    args = [qc, qr, kcat.reshape(batch, seq, MIX)]
    if has_ctx:
        in_specs.append(pl.BlockSpec((None, None, kx.shape[2], MIX), lambda b, i: (b, l, 0, 0)))
        args.append(kx)
    in_specs.append(_pick(wvbd, l))
    args.append(wvbd)
    return pl.pallas_call(
        functools.partial(_mla_kernel, has_ctx=has_ctx),
        grid=(batch, nq),
        in_specs=in_specs,
        out_specs=pl.BlockSpec((tq, MIX), lambda b, i: (b * nq + i, 0)),
        out_shape=jax.ShapeDtypeStruct((batch * seq, MIX), BF),
        compiler_params=_params("parallel", "parallel"),
        name="mla_lat" if has_ctx else "mla_ctx",
    )(*args)


def _rot_cols(w, width):
    half = width // 2
    parts = w.reshape(w.shape[:-1] + (w.shape[-1] // width, 2, half))
    return jnp.stack([-parts[..., 1, :], parts[..., 0, :]], axis=-2).reshape(w.shape)


def _block_diag(blocks):
    n, r, c = blocks.shape[-3:]
    eye = jnp.eye(n, dtype=blocks.dtype)
    out = jnp.einsum("...hrc,hg->...hrgc", blocks, eye)
    return out.reshape(blocks.shape[:-3] + (n * r, n * c))


def _layer_weights(w_in, gla_w_gate, gla_b_gate, mla_w_q_b, mla_w_kv_b):
    nl = w_in.shape[0]
    offs = [0]
    for n in (256, 256, 256, 256, 16, 16, 256, 128, 128, 256, 256, 128, 32):
        offs.append(offs[-1] + n)
    seg = lambda k: w_in[..., offs[k]:offs[k + 1]]
    zeros = lambda n: jnp.zeros((nl, D_MODEL, n), w_in.dtype)
    sq = seg(6).reshape(nl, D_MODEL, SWA_Q_HEADS, HEAD_DIM)
    sq_tiles = []
    for hq in range(SWA_Q_HEADS):
        parts = [zeros(HEAD_DIM), zeros(HEAD_DIM)]
        parts[hq // 2] = sq[:, :, hq]
        sq_tiles += parts
    sq_w = jnp.concatenate(sq_tiles, axis=-1)
    kr4 = jnp.tile(seg(12), (1, 1, MLA_HEADS))
    cols = [seg(0), seg(1), seg(2), seg(3), seg(4), seg(5), zeros(LANES - 2 * GLA_GATE_RANK),
            sq_w, seg(7), seg(8), seg(9), seg(10), seg(11), kr4,
            _rot_cols(sq_w, HEAD_DIM // 2), _rot_cols(seg(7), HEAD_DIM // 2), _rot_cols(kr4, MLA_ROPE // 2)]
    w = jnp.concatenate(cols, axis=-1).astype(BF)

    wgate = jnp.zeros((nl, LANES, 2 * MIX), F32)
    wgate = wgate.at[:, :GLA_GATE_RANK, :MIX].set(gla_w_gate[:, 0])
    wgate = wgate.at[:, GLA_GATE_RANK:2 * GLA_GATE_RANK, MIX:].set(gla_w_gate[:, 1])
    bgate = gla_b_gate.reshape(nl, 1, 2 * MIX)

    wq = mla_w_q_b.reshape(nl, MLA_Q_RANK, MLA_HEADS, MLA_NOPE + MLA_ROPE)
    q_nope = wq[..., :MLA_NOPE].reshape(nl, MLA_Q_RANK, MLA_HEADS * MLA_NOPE)
    q_rope = wq[..., MLA_NOPE:].reshape(nl, MLA_Q_RANK, MLA_HEADS * MLA_ROPE)
    wqb = jnp.concatenate([q_nope, q_rope, _rot_cols(q_rope, MLA_ROPE // 2)], axis=-1).astype(BF)
    wkv = mla_w_kv_b.reshape(nl, MLA_KV_RANK, MLA_HEADS, MLA_NOPE + MLA_V)
    wkbd = _block_diag(jnp.transpose(wkv[..., :MLA_NOPE], (0, 2, 3, 1))).astype(BF)
    wvbd = _block_diag(jnp.transpose(wkv[..., MLA_NOPE:], (0, 2, 1, 3))).astype(BF)
    return (w, wgate.astype(BF), bgate, wqb, wkbd), wvbd


def _rope_tables(seq):
    t = jnp.arange(seq)
    rows = (t // GRID_W).astype(F32)[:, None]
    cols = (t % GRID_W).astype(F32)[:, None]

    def table(width):
        half = width // 2
        lane = jnp.arange(LANES)
        inv = ROPE_BASE ** (-(2 * (lane % half)).astype(F32) / width)
        ang = jnp.where(((lane // width) % 2 == 0)[None, :], rows, cols) * inv[None, :]
        return jnp.cos(ang), jnp.sin(ang)

    cs, ss = table(HEAD_DIM // 2)
    cm, sm = table(MLA_ROPE // 2)
    return cs, ss, cm, sm


def _dft_channel():
    c = jnp.arange(FNET_GROUP_CH, dtype=jnp.int32)
    ang = ((c[:, None] * c[None, :]) % FNET_GROUP_CH).astype(F32) * (2.0 * jnp.pi / FNET_GROUP_CH)
    cos_b = jnp.broadcast_to(jnp.cos(ang)[None], (FNET_GROUPS, FNET_GROUP_CH, FNET_GROUP_CH))
    sin_b = jnp.broadcast_to(jnp.sin(ang)[None], (FNET_GROUPS, FNET_GROUP_CH, FNET_GROUP_CH))
    return _block_diag(cos_b).astype(BF), (-_block_diag(sin_b)).astype(BF)


def _state_in(s):
    return _block_diag(jnp.swapaxes(s, -1, -2))


def _state_out(st):
    blocks = [st[:, h * GLA_DV:(h + 1) * GLA_DV, h * GLA_DK:(h + 1) * GLA_DK] for h in range(GLA_HEADS)]
    return jnp.swapaxes(jnp.stack(blocks, axis=1), 2, 3)


def _token_mix(x, l, shared, st0, st0_idx, cache, batch, seq, rows_per_batch, first_row, latent):
    mod, gn, wts, wvbd, gq, gkv, gout, sinks, cdft, sdft, rope_t, fft_t = shared
    outs = _inproj_call(x, mod, gn, wts, gq, gkv, cdft, sdft, rope_t, l, batch, seq, rows_per_batch,
                        first_row, latent, tm=512 if latent else 256)
    gqv, gkv_, gv, gr, la, sq, sk, sv, zc, zs, qc, qr, kcat = outs[:13]
    tb = min(seq, 512)
    o_f, st_f = _gla_call(gqv, gkv_, gv, la, st0, st0_idx[0], None, batch, seq, False, tb)
    o_gla, st_b = _gla_call(gqv, gkv_, gv, la, st0, st0_idx[1], (o_f, gr, gout, l), batch, seq, True, tb)
    if latent:
        kx, vx, mx = cache
        o_swa = _swa_call(sinks, l, sq, sk, sv, kx, vx, l, batch, seq, True)
        o_mla = _mla_call(qc, qr, kcat, mx, wvbd, l, batch, seq)
        new_ctx = None
    else:
        k4 = sk.reshape(batch, 1, seq, LANES)
        v4 = sv.reshape(batch, 1, seq, LANES)
        o_swa = _swa_call(sinks, l, sq, None, None, k4, v4, 0, batch, seq, False)
        o_mla = _mla_call(qc, qr, kcat, None, wvbd, l, batch, seq)
        new_ctx = (st_f, st_b, sk, sv, outs[13], outs[14])
    o_fft = _fft_call(fft_t, zc, zs, seq)
    return (o_gla, o_swa, o_fft, o_mla), new_ctx


def kernel(x_prompt, x_sample, c, state_gla, cache_swa_k, cache_swa_v, cache_mla_ckv, cache_mla_krope,
           c_ctx, w_mod, b_mod, g_norm, w_ffn_gate, w_ffn_up, w_ffn_down, w_in, gla_w_gate, gla_b_gate,
           gla_g_out, swa_sink, mla_g_q, mla_g_kv, mla_w_q_b, mla_w_kv_b, w_out):
    nb, ns, _ = x_prompt.shape
    db, dsq, _ = x_sample.shape
    past = cache_swa_k.shape[2]

    cvec = jnp.zeros((MOD_ROWS, D_MODEL), F32).at[0].set(c_ctx).at[1:1 + db].set(c)
    mod = _mod_call(cvec, w_mod, b_mod).reshape(DEPTH, MOD_ROWS, N_MOD, 1, D_MODEL)
    gn = g_norm.reshape(DEPTH, 6, 1, D_MODEL)

    wg, wu, wd, wo = (a.astype(BF) for a in (w_ffn_gate, w_ffn_up, w_ffn_down, w_out))
    wts, wvbd = _layer_weights(w_in, gla_w_gate, gla_b_gate, mla_w_q_b, mla_w_kv_b)
    gq = mla_g_q.reshape(DEPTH, 1, MLA_Q_RANK)
    gkv = mla_g_kv.reshape(DEPTH, 1, MLA_KV_RANK)
    gout = jnp.tile(gla_g_out, (1, GLA_HEADS)).reshape(DEPTH, 1, MIX)
    cdft, sdft = _dft_channel()
    shared = (mod, gn, wts, wvbd, gq, gkv, gout, swa_sink, cdft, sdft)
    shared_ctx = shared + (None, _fft_tables(ns))
    shared_lat = shared + (_rope_tables(dsq), _fft_tables(dsq))

    st_lat = _state_in(state_gla)
    st_zero = jnp.zeros((nb, 1, 1, MIX, MIX), F32)
    kx = cache_swa_k.reshape(db, DEPTH, past, LANES)
    vx = cache_swa_v.reshape(db, DEPTH, past, LANES)
    mx = jnp.concatenate([cache_mla_ckv, jnp.tile(cache_mla_krope, (1, 1, 1, MLA_HEADS))], axis=-1).astype(BF)

    xc = x_prompt.reshape(nb * ns, D_MODEL)
    xl = x_sample.reshape(db * dsq, D_MODEL)
    st_gla, st_k, st_v, st_ckv, st_kr = [], [], [], [], []
    for l in range(DEPTH):
        xc = _ffn_call(xc, mod, gn, wg, wu, wd, l, rows_per_batch=nb * ns, first_row=0)
        mixed, new_ctx = _token_mix(xc, l, shared_ctx, st_zero, ((0, 0), (0, 0)), None, nb, ns, nb * ns, 0, False)
        xc = _outffn_call(*mixed, xc, mod, gn, wo, wg, wu, wd, l, seq=ns, rows_per_batch=nb * ns,
                          first_row=0, tm=256)
        st_f, st_b, k_c, v_c, ckv_c, kr_c = new_ctx
        st_gla.append(jnp.stack([_state_out(st_f), _state_out(st_b)], axis=1))
        st_k.append(k_c.reshape(nb, ns, SWA_KV_HEADS, HEAD_DIM))
        st_v.append(v_c.reshape(nb, ns, SWA_KV_HEADS, HEAD_DIM))
        st_ckv.append(ckv_c.reshape(nb, ns, MLA_KV_RANK))
        st_kr.append(kr_c.reshape(nb, ns, MLA_ROPE))

        xl = _ffn_call(xl, mod, gn, wg, wu, wd, l, rows_per_batch=dsq, first_row=1)
        mixed, _ = _token_mix(xl, l, shared_lat, st_lat, ((l, 0), (l, 1)), (kx, vx, mx), db, dsq, dsq, 1, True)
        xl = _outffn_call(*mixed, xl, mod, gn, wo, wg, wu, wd, l, seq=dsq, rows_per_batch=dsq,
                          first_row=1, tm=512)

    return (xc.reshape(nb, ns, D_MODEL), xl.reshape(db, dsq, D_MODEL), jnp.stack(st_gla, axis=1),
            jnp.stack(st_k, axis=1), jnp.stack(st_v, axis=1), jnp.stack(st_ckv, axis=1),
            jnp.stack(st_kr, axis=1))
```

```python
import functools

import jax
import jax.numpy as jnp
from jax import lax
from jax.experimental import pallas as pl
from jax.experimental.pallas import tpu as pltpu

D_MODEL = 1024
DEPTH = 4
GRID_W = 64
HEAD_DIM = 64
GLA_HEADS = 4
GLA_DK = 64
GLA_DV = 64
GLA_GATE_RANK = 16
GLA_TAU = 16.0
GLA_CHUNK = 64
SWA_Q_HEADS = 4
SWA_KV_HEADS = 2
SWA_BLOCK = 128
FNET_GROUPS = 4
FNET_GROUP_CH = 64
MLA_HEADS = 4
MLA_Q_RANK = 256
MLA_KV_RANK = 128
MLA_NOPE = 64
MLA_ROPE = 32
MLA_V = 64
D_FF = 2816
FFN_RES = 0.5
N_MOD = 9
ROPE_BASE = 10000.0
EPS = 1e-6
NEG_INF = -1e30

MIX = 256
LANES = 128
MXU_DEPTH = 256
FF_CHUNK = 256
MLA_KEY_TILE = 1024
GLA_GROUP = 4
LOG2E = 1.4426950408889634
MOD_ROWS = 8
VMEM_LIMIT = 56 * 1024 * 1024

BF = jnp.bfloat16
F32 = jnp.float32

C_GQ, C_GK, C_GV, C_GR, C_A = 0, 256, 512, 768, 1024
C_SQ, C_SK, C_SV, C_ZF, C_QA, C_KVA, C_KR = 1152, 1664, 1792, 1920, 2176, 2432, 2560
C_SQR, C_SKR, C_KRR = 2688, 3200, 3328
NC_CTX, NC_LAT = 2688, 3456


def _dot(a, b):
    return jnp.dot(a, b, preferred_element_type=F32)


def _dot_nt(a, b):
    return lax.dot_general(a, b, (((1,), (1,)), ((), ())), preferred_element_type=F32)


def _dot_tn(a, b):
    return lax.dot_general(a, b, (((0,), (0,)), ((), ())), preferred_element_type=F32)


def _rms(x, g):
    return x * lax.rsqrt(jnp.mean(x * x, axis=-1, keepdims=True) + EPS) * g


def _silu(x):
    return x * jax.nn.sigmoid(x)


def _params(*sem):
    return pltpu.CompilerParams(dimension_semantics=sem, vmem_limit_bytes=VMEM_LIMIT)


def _pick(arr, *lead, block=None):
    tail = tuple(arr.shape[len(lead):]) if block is None else tuple(block)
    zeros = (0,) * len(tail)
    return pl.BlockSpec((None,) * len(lead) + tail, lambda *_: tuple(lead) + zeros,
                        pipeline_mode=pl.Buffered(1))


def _mod_kernel(c_ref, w_ref, b_ref, o_ref):
    s = _silu(c_ref[...]).astype(BF)
    o_ref[...] = _dot(s, w_ref[...].astype(BF)) + b_ref[...]


def _mod_call(cvec, w_mod, b_mod):
    nl = w_mod.shape[0]
    tn = D_MODEL
    return pl.pallas_call(
        _mod_kernel,
        grid=(nl, N_MOD * D_MODEL // tn),
        in_specs=[
            pl.BlockSpec((MOD_ROWS, D_MODEL), lambda l, j: (0, 0)),
            pl.BlockSpec((None, D_MODEL, tn), lambda l, j: (l, 0, j)),
            pl.BlockSpec((None, 1, tn), lambda l, j: (l, 0, j)),
        ],
        out_specs=pl.BlockSpec((None, MOD_ROWS, tn), lambda l, j: (l, 0, j)),
        out_shape=jax.ShapeDtypeStruct((nl, MOD_ROWS, N_MOD * D_MODEL), F32),
        compiler_params=_params("parallel", "parallel"),
        name="mod",
    )(cvec, w_mod, b_mod.reshape(nl, 1, N_MOD * D_MODEL))


def _mod_spec(l, sub, tm, rows_per_batch, first_row):
    per = rows_per_batch // tm
    return pl.BlockSpec((None, None, 3, 1, D_MODEL), lambda i: (l, first_row + i // per, sub, 0, 0))


def _fft_layout_spec(tm, seq):
    per = max(seq // tm, 1)
    nbat = max(tm // seq, 1)
    return pl.BlockSpec((tm // nbat, nbat * MIX), lambda i: (i % per, i // per))


def _gn_spec(l, first, n):
    return pl.BlockSpec((None, n, 1, D_MODEL), lambda i: (l, first // n, 0, 0))


def _ffn_body(x, mod_ref, g_pre, g_post, wg_ref, wu_ref, wd_ref):
    sh, sc, gt = mod_ref[0], mod_ref[1], mod_ref[2]
    hb = (_rms(x, g_pre) * (1.0 + sc) + sh).astype(BF)
    acc = None
    for j in range(D_FF // FF_CHUNK):
        sl = slice(j * FF_CHUNK, (j + 1) * FF_CHUNK)
        g = _dot(hb, wg_ref[:, sl])
        u = _dot(hb, wu_ref[:, sl])
        d = _dot((_silu(g) * u).astype(BF), wd_ref[sl, :])
        acc = d if acc is None else acc + d
    return x + FFN_RES * gt * _rms(acc, g_post)


def _ffn_kernel(x_ref, mod_ref, gn_ref, wg_ref, wu_ref, wd_ref, o_ref):
    o_ref[...] = _ffn_body(x_ref[...], mod_ref, gn_ref[0], gn_ref[1], wg_ref, wu_ref, wd_ref)


def _ffn_call(x, mod, gn, wg, wu, wd, l, rows_per_batch, first_row, tm=512):
    r = x.shape[0]
    return pl.pallas_call(
        _ffn_kernel,
        grid=(r // tm,),
        in_specs=[
            pl.BlockSpec((tm, D_MODEL), lambda i: (i, 0)),
            _mod_spec(l, 0, tm, rows_per_batch, first_row),
            _gn_spec(l, 0, 2),
            _pick(wg, l, 0), _pick(wu, l, 0), _pick(wd, l, 0),
        ],
        out_specs=pl.BlockSpec((tm, D_MODEL), lambda i: (i, 0)),
        out_shape=jax.ShapeDtypeStruct(x.shape, F32),
        compiler_params=_params("parallel"),
        name="ffn",
    )(x, mod, gn, wg, wu, wd)


def _outffn_kernel(og_ref, gr_ref, os_ref, of_ref, om_ref, x_ref, mod2_ref, mod3_ref, gn_ref, gout_ref,
                   avg_ref, wo_ref, wg_ref, wu_ref, wd_ref, o_ref):
    og = og_ref[...]
    ms = _dot((og * og).astype(BF), avg_ref[...])
    og = (og * lax.rsqrt(ms + EPS) * gout_ref[...] * gr_ref[...]).astype(BF)
    nbat = of_ref.shape[1] // MIX
    of = jnp.concatenate([of_ref[:, j * MIX:(j + 1) * MIX] for j in range(nbat)], axis=0)
    mix = jnp.concatenate([og, os_ref[...], of, om_ref[...]], axis=1)
    x = x_ref[...] + mod2_ref[2] * _rms(_dot(mix, wo_ref[...]), gn_ref[0])
    o_ref[...] = _ffn_body(x, mod3_ref, gn_ref[1], gn_ref[2], wg_ref, wu_ref, wd_ref)


def _outffn_call(og, gr, osw, of, om, x, mod, gn, gout, avg, wo, wg, wu, wd, l, seq, rows_per_batch, first_row, tm):
    r = x.shape[0]
    mspec = pl.BlockSpec((tm, MIX), lambda i: (i, 0))
    return pl.pallas_call(
        _outffn_kernel,
        grid=(r // tm,),
        in_specs=[
            mspec, mspec, mspec,
            _fft_layout_spec(tm, seq),
            mspec,
            pl.BlockSpec((tm, D_MODEL), lambda i: (i, 0)),
            _mod_spec(l, 1, tm, rows_per_batch, first_row),
            _mod_spec(l, 2, tm, rows_per_batch, first_row),
            _gn_spec(l, 3, 3),
            _pick(gout, l), _pick(avg),
            _pick(wo, l), _pick(wg, l, 1), _pick(wu, l, 1), _pick(wd, l, 1),
        ],
        out_specs=pl.BlockSpec((tm, D_MODEL), lambda i: (i, 0)),
        out_shape=jax.ShapeDtypeStruct(x.shape, F32),
        compiler_params=_params("parallel"),
        name="outffn",
    )(og, gr, osw, of, om, x, mod, mod, gn, gout, avg, wo, wg, wu, wd)


def _log_sigmoid(x):
    return -(jnp.maximum(-x, 0.0) + jnp.log(1.0 + jnp.exp(-jnp.abs(x))))


def _inproj_kernel(*refs, latent, nbat):
    (x_ref, mod_ref, gn_ref, w_ref, wgate_ref, bgate_ref, gq_ref, gkv_ref, wqb_ref, wkbd_ref,
     cdft_ref, sdft_ref) = refs[:12]
    pos = 12
    if latent:
        cs_ref, ss_ref, cm_ref, sm_ref = refs[pos:pos + 4]
        pos += 4
    (gq_o, gk_o, gv_o, gr_o, la_o, sq_o, sk_o, sv_o, zc_o, zs_o, qc_o, qr_o, kcat_o) = refs[pos:pos + 13]
    pos += 13
    if not latent:
        ckv_o, kr_o = refs[pos:pos + 2]

    x = x_ref[...]
    hb = (_rms(x, gn_ref[0]) * (1.0 + mod_ref[1]) + mod_ref[0]).astype(BF)

    def col(off, n):
        return _dot(hb, w_ref[:, off:off + n])

    gq_o[...] = col(C_GQ, MIX) * (GLA_DK ** -0.5)
    gk_o[...] = col(C_GK, MIX)
    gv_o[...] = col(C_GV, MIX).astype(BF)
    gr_o[...] = _silu(col(C_GR, MIX))
    logit = _dot(col(C_A, LANES).astype(BF), wgate_ref[...]) + bgate_ref[...]
    la_o[...] = _log_sigmoid(logit) * (1.0 / GLA_TAU)

    sq = col(C_SQ, 2 * MIX)
    sk = col(C_SK, LANES)
    sv = col(C_SV, LANES)
    if latent:
        cs, ss = cs_ref[...], ss_ref[...]
        cs4 = jnp.concatenate([cs] * 4, axis=1)
        ss4 = jnp.concatenate([ss] * 4, axis=1)
        sq = sq * cs4 + col(C_SQR, 2 * MIX) * ss4
        sk = sk * cs + col(C_SKR, LANES) * ss
    sq_o[...] = (sq * (HEAD_DIM ** -0.5 * LOG2E)).astype(BF)
    sk_o[...] = sk.astype(sk_o.dtype)
    sv_o[...] = sv.astype(sv_o.dtype)

    zf = col(C_ZF, MIX).astype(BF)
    zc = _dot(zf, cdft_ref[...]).astype(BF)
    zs = _dot(zf, sdft_ref[...]).astype(BF)
    rows_b = zf.shape[0] // nbat
    for j in range(nbat):
        zc_o[:, j * MIX:(j + 1) * MIX] = zc[j * rows_b:(j + 1) * rows_b]
        zs_o[:, j * MIX:(j + 1) * MIX] = zs[j * rows_b:(j + 1) * rows_b]

    qn = _rms(col(C_QA, MLA_Q_RANK), gq_ref[...]).astype(BF)
    qm = _dot(qn, wqb_ref[...])
    q_rope = qm[:, MIX:MIX + LANES]
    ckv = _rms(col(C_KVA, MLA_KV_RANK), gkv_ref[...])
    kr = col(C_KR, LANES)
    if not latent:
        ckv_o[...] = ckv
        kr_o[...] = kr[:, :MLA_ROPE]
    else:
        cm, sm = cm_ref[...], sm_ref[...]
        q_rope = q_rope * cm + qm[:, MIX + LANES:MIX + 2 * LANES] * sm
        kr = kr * cm + col(C_KRR, LANES) * sm
    scale = (MLA_NOPE + MLA_ROPE) ** -0.5 * LOG2E
    qc_o[...] = (_dot(qm[:, :MIX].astype(BF), wkbd_ref[...]) * scale).astype(BF)
    qr_o[...] = (q_rope * scale).astype(BF)
    kcat_o[:, :LANES] = ckv.astype(BF)
    kcat_o[:, LANES:] = kr.astype(BF)


def _inproj_call(x, mod, gn, wts, gq, gkv, cdft, sdft, tables, l, batch, seq, rows_per_batch, first_row,
                 latent, tm):
    w, wgate, bgate, wqb, wkbd = wts
    r = x.shape[0]
    per = max(seq // tm, 1)
    nbat = max(tm // seq, 1)
    row = lambda n: pl.BlockSpec((tm, n), lambda i: (i, 0))
    ncol = NC_LAT if latent else NC_CTX
    nq = wqb.shape[-1] if latent else MIX + LANES
    in_specs = [
        row(D_MODEL),
        _mod_spec(l, 1, tm, rows_per_batch, first_row),
        _gn_spec(l, 2, 1),
        _pick(w, l, block=(D_MODEL, ncol)), _pick(wgate, l), _pick(bgate, l), _pick(gq, l), _pick(gkv, l),
        _pick(wqb, l, block=(MLA_Q_RANK, nq)), _pick(wkbd, l), _pick(cdft), _pick(sdft),
    ]
    args = [x, mod, gn, w, wgate, bgate, gq, gkv, wqb, wkbd, cdft, sdft]
    if latent:
        in_specs += [pl.BlockSpec((tm, LANES), lambda i: (i % per, 0))] * 4
        args += list(tables)
    kvdt = BF if latent else F32
    fft_spec = _fft_layout_spec(tm, seq)
    out_specs = [row(MIX), row(MIX), row(MIX), row(MIX), row(2 * MIX), row(2 * MIX), row(LANES), row(LANES),
                 fft_spec, fft_spec, row(2 * MIX), row(LANES), row(MIX)]
    sd = jax.ShapeDtypeStruct
    out_shape = [sd((r, MIX), F32), sd((r, MIX), F32), sd((r, MIX), BF), sd((r, MIX), F32),
                 sd((r, 2 * MIX), F32), sd((r, 2 * MIX), BF), sd((r, LANES), kvdt), sd((r, LANES), kvdt),
                 sd((seq, batch * MIX), BF), sd((seq, batch * MIX), BF),
                 sd((r, 2 * MIX), BF), sd((r, LANES), BF), sd((r, MIX), BF)]
    if not latent:
        out_specs += [row(MLA_KV_RANK), row(MLA_ROPE)]
        out_shape += [sd((r, MLA_KV_RANK), F32), sd((r, MLA_ROPE), F32)]
    return pl.pallas_call(
        functools.partial(_inproj_kernel, latent=latent, nbat=nbat),
        grid=(r // tm,),
        in_specs=in_specs,
        out_specs=out_specs,
        out_shape=out_shape,
        compiler_params=_params("parallel"),
        name="inproj_lat" if latent else "inproj_ctx",
    )(*args)


def _chunk_cumsum(x, row, reverse):
    n = x.shape[0]
    k = 1
    while k < n:
        if reverse:
            x = x + jnp.where(row < n - k, pltpu.roll(x, n - k, 0), 0.0)
        else:
            x = x + jnp.where(row >= k, pltpu.roll(x, k, 0), 0.0)
        k *= 2
    return x


def _gla_kernel(q_ref, k_ref, v_ref, la_ref, s0_ref, *rest, reverse, final, nchunk, group):
    if final:
        of_ref, o_ref, sfin_ref, st_ref = rest
    else:
        o_ref, sfin_ref, st_ref = rest
    i = pl.program_id(1)
    c = GLA_CHUNK

    @pl.when(i == 0)
    def _():
        st_ref[...] = s0_ref[...]

    row = lax.broadcasted_iota(jnp.int32, (c, MIX), 0)
    lane = lax.broadcasted_iota(jnp.int32, (c, MIX), 1) % c
    keep = (lane >= row) if reverse else (lane <= row)
    lane1 = lax.broadcasted_iota(jnp.int32, (1, MIX), 1) // c
    hmask = [(lane1 == h).astype(F32) for h in range(GLA_HEADS)]
    hmask_b = [m.astype(BF) for m in hmask]
    bdiag = (lax.broadcasted_iota(jnp.int32, (MIX, MIX), 0) // c
             == lax.broadcasted_iota(jnp.int32, (MIX, MIX), 1) // c)

    def chunk(t, carry):
        cc = (nchunk - 1 - t) if reverse else t
        rs = pl.ds(pl.multiple_of(cc * c, c), c)
        for g in range(group):
            cum = _chunk_cumsum(la_ref[g, rs, :], row, reverse)
            tot = cum[0:1] if reverse else cum[c - 1:c]
            q, k, v = q_ref[g, rs, :], k_ref[g, rs, :], v_ref[g, rs, :]
            qd = (q * jnp.exp(cum)).astype(BF)
            ki = k * jnp.exp(-cum)
            kd = (k * jnp.exp(tot - cum)).astype(BF)
            kstack = jnp.concatenate([(ki * m).astype(BF) for m in hmask], axis=0)
            vstack = jnp.concatenate([v * m for m in hmask_b], axis=0)
            att = jnp.where(keep, _dot_nt(qd, kstack), 0.0).astype(BF)
            st = st_ref[g]
            o = _dot(att, vstack) + _dot_nt(qd, st.astype(BF))
            st_ref[g] = st * jnp.exp(tot) + jnp.where(bdiag, _dot_tn(v, kd), 0.0)
            if final:
                o = o + of_ref[g, rs, :]
            o_ref[g, rs, :] = o
        return carry

    lax.fori_loop(0, nchunk, chunk, 0, unroll=2)

    @pl.when(i == pl.num_programs(1) - 1)
    def _():
        sfin_ref[...] = st_ref[...]


def _gla_call(q, k, v, la, s0, s0_idx, extra, batch, seq, reverse, tb, group=GLA_GROUP):
    nblk = seq // tb
    final = extra is not None
    pos = (lambda i: nblk - 1 - i) if reverse else (lambda i: i)
    blk = pl.BlockSpec((group, tb, MIX), lambda b, i: (b, pos(i), 0))
    lblk = pl.BlockSpec((group, tb, MIX), lambda b, i: (b, pos(i), 1 if reverse else 0))
    sblk = pl.BlockSpec((group, MIX, MIX), lambda b, i: (b, 0, 0))
    s0blk = pl.BlockSpec((group, None, None, MIX, MIX), lambda b, i: (b,) + tuple(s0_idx) + (0, 0))
    r3 = lambda a: a.reshape(batch, seq, a.shape[-1])
    in_specs = [blk, blk, blk, lblk, s0blk]
    args = [r3(q), r3(k), r3(v), r3(la), s0]
    if final:
        in_specs.append(blk)
        args.append(r3(extra))
    o, st = pl.pallas_call(
        functools.partial(_gla_kernel, reverse=reverse, final=final, nchunk=tb // GLA_CHUNK, group=group),
        grid=(batch // group, nblk),
        in_specs=in_specs,
        out_specs=[blk, sblk],
        out_shape=[jax.ShapeDtypeStruct((batch, seq, MIX), F32),
                   jax.ShapeDtypeStruct((batch, MIX, MIX), F32)],
        scratch_shapes=[pltpu.VMEM((group, MIX, MIX), F32)],
        compiler_params=_params("parallel", "arbitrary"),
        name="gla_bwd" if reverse else "gla_fwd",
    )(*args)
    return o.reshape(batch * seq, MIX), st


def _swa_kernel(*refs, local, layer):
    sink_ref, q_ref = refs[:2]
    pos = 2
    if local:
        kp_ref, kc_ref, kn_ref, vp_ref, vc_ref, vn_ref = refs[pos:pos + 6]
        pos += 6
    kx_ref, vx_ref, o_ref = refs[pos:pos + 3]
    i = pl.program_id(1)
    last = pl.num_programs(1) - 1
    blk = SWA_BLOCK
    tq = q_ref.shape[0]
    cx = kx_ref.shape[0]
    kx = kx_ref[...].astype(BF)
    vx = vx_ref[...].astype(BF)
    if local:
        nsub = tq // blk
        kblocks = [kp_ref[...]] + [kc_ref[j * blk:(j + 1) * blk, :] for j in range(nsub)] + [kn_ref[...]]
        vblocks = [vp_ref[...]] + [vc_ref[j * blk:(j + 1) * blk, :] for j in range(nsub)] + [vn_ref[...]]
        row = lax.broadcasted_iota(jnp.int32, (blk, blk), 0)
        col = lax.broadcasted_iota(jnp.int32, (blk, blk), 1)
        full = jnp.full((blk, blk), True)
        fullx = jnp.full((blk, cx), True)
    else:
        nsub = 1
    rq = tq // nsub
    rows = lax.broadcasted_iota(jnp.int32, (2 * rq, 1), 0)
    low = lax.broadcasted_iota(jnp.int32, (rq, LANES), 1) < HEAD_DIM
    for j in range(nsub):
        if local:
            kcat = jnp.concatenate([kx, kblocks[j], kblocks[j + 1], kblocks[j + 2]], axis=0)
            vcat = jnp.concatenate([vx, vblocks[j], vblocks[j + 1], vblocks[j + 2]], axis=0)
            prev_ok = (col >= row) if j > 0 else jnp.logical_and(col >= row, i > 0)
            next_ok = (col <= row) if j < nsub - 1 else jnp.logical_and(col <= row, i < last)
            keep = jnp.concatenate([fullx, prev_ok, full, next_ok], axis=1)
            keep = jnp.concatenate([keep, keep], axis=0)
        else:
            kcat, vcat = kx, vx
        rs = slice(j * rq, (j + 1) * rq)
        tiles = []
        for g in range(SWA_KV_HEADS):
            q2 = jnp.concatenate([q_ref[rs, (2 * g) * LANES:(2 * g + 1) * LANES],
                                  q_ref[rs, (2 * g + 1) * LANES:(2 * g + 2) * LANES]], axis=0)
            sink = jnp.where(rows < rq, sink_ref[layer, 2 * g], sink_ref[layer, 2 * g + 1]) * LOG2E
            s = _dot_nt(q2, kcat)
            if local:
                s = jnp.where(keep, s, NEG_INF)
            m = jnp.maximum(jnp.max(s, axis=-1, keepdims=True), sink)
            e = jnp.exp2(s - m)
            den = jnp.sum(e, axis=-1, keepdims=True) + jnp.exp2(sink - m)
            o2 = _dot(e.astype(BF), vcat) / den
            oa, ob = o2[:rq], o2[rq:]
            if g == 0:
                tiles.append(jnp.where(low, oa, pltpu.roll(ob, HEAD_DIM, 1)))
            else:
                tiles.append(jnp.where(low, pltpu.roll(oa, HEAD_DIM, 1), ob))
        o_ref[rs, :LANES] = tiles[0].astype(BF)
        o_ref[rs, LANES:] = tiles[1].astype(BF)


def _swa_call(sinks, layer, q, k, v, kx, vx, x_layer, batch, seq, local, tq=256):
    nt = seq // tq
    per = tq // SWA_BLOCK
    nb = seq // SWA_BLOCK
    cx = kx.shape[2]
    qspec = pl.BlockSpec((tq, 2 * MIX), lambda b, i: (b * nt + i, 0))
    in_specs = [pl.BlockSpec(memory_space=pltpu.SMEM), qspec]
    args = [sinks, q]
    if local:
        prv = pl.BlockSpec((SWA_BLOCK, LANES), lambda b, i: (b * nb + jnp.maximum(i * per - 1, 0), 0))
        cur = pl.BlockSpec((tq, LANES), lambda b, i: (b * nt + i, 0))
        nxt = pl.BlockSpec((SWA_BLOCK, LANES), lambda b, i: (b * nb + jnp.minimum((i + 1) * per, nb - 1), 0))
        in_specs += [prv, cur, nxt, prv, cur, nxt]
        args += [k, k, k, v, v, v]
    xspec = pl.BlockSpec((None, None, cx, LANES), lambda b, i: (b, x_layer, 0, 0))
    in_specs += [xspec, xspec]
    args += [kx, vx]
    return pl.pallas_call(
        functools.partial(_swa_kernel, local=local, layer=layer),
        grid=(batch, nt),
        in_specs=in_specs,
        out_specs=pl.BlockSpec((tq, MIX), lambda b, i: (b * nt + i, 0)),
        out_shape=jax.ShapeDtypeStruct((batch * seq, MIX), BF),
        compiler_params=_params("parallel", "parallel"),
        name="swa_lat" if local else "swa_ctx",
    )(*args)


def _fft_kernel(xc_ref, xs_ref, yc_ref, ys_ref, zc_ref, zs_ref, o_ref, *, nj, scale):
    yc, ys = yc_ref[...], ys_ref[...]
    xc, xs = xc_ref[...], xs_ref[...]
    acc = None
    for jp in range(nj // 2):
        cos_t, sin_t = [], []
        for j in (2 * jp, 2 * jp + 1):
            a, b = xc[:, j:j + 1], xs[:, j:j + 1]
            cos_t.append(a * yc - b * ys)
            sin_t.append(b * yc + a * ys)
        cos_t = jnp.concatenate(cos_t, axis=1).astype(BF)
        sin_t = jnp.concatenate(sin_t, axis=1).astype(BF)
        rows = slice(jp * MXU_DEPTH, (jp + 1) * MXU_DEPTH)
        d = _dot(cos_t, zc_ref[rows, :]) + _dot(sin_t, zs_ref[rows, :])
        acc = d if acc is None else acc + d
    o_ref[...] = (acc * scale).astype(BF)


def _fft_tables(seq):
    sp = jnp.arange(seq, dtype=jnp.int32)[:, None]
    w = 2.0 * jnp.pi / seq
    ax = ((sp * (jnp.arange(seq // LANES, dtype=jnp.int32) * LANES)[None, :]) % seq).astype(F32) * w
    ay = ((sp * jnp.arange(LANES, dtype=jnp.int32)[None, :]) % seq).astype(F32) * w
    return jnp.cos(ax), jnp.sin(ax), jnp.cos(ay), jnp.sin(ay)


def _fft_call(tables, zc, zs, seq, tm=256):
    xc, xs, yc, ys = tables
    nj = seq // LANES
    width = zc.shape[1]
    return pl.pallas_call(
        functools.partial(_fft_kernel, nj=nj, scale=(seq * FNET_GROUP_CH) ** -0.5),
        grid=(seq // tm,),
        in_specs=[
            pl.BlockSpec((tm, nj), lambda i: (i, 0)), pl.BlockSpec((tm, nj), lambda i: (i, 0)),
            pl.BlockSpec((tm, LANES), lambda i: (i, 0)), pl.BlockSpec((tm, LANES), lambda i: (i, 0)),
            _pick(zc), _pick(zs),
        ],
        out_specs=pl.BlockSpec((tm, width), lambda i: (i, 0)),
        out_shape=jax.ShapeDtypeStruct((seq, width), BF),
        compiler_params=_params("parallel"),
        name="fft",
    )(xc, xs, yc, ys, zc, zs)


def _mla_kernel(*refs, has_ctx):
    qc_ref, qr_ref, ks_ref = refs[:3]
    pos = 3
    if has_ctx:
        kx_ref = refs[pos]
        pos += 1
    wv_ref, o_ref = refs[pos:pos + 2]
    qr = qr_ref[...]
    lane = lax.broadcasted_iota(jnp.int32, qr.shape, 1) // MLA_ROPE
    tq = qr.shape[0]
    qs = []
    for h in range(MLA_HEADS):
        qrh = jnp.where(lane == h, qr, jnp.zeros_like(qr))
        qs.append(jnp.concatenate([qc_ref[:, h * LANES:(h + 1) * LANES], qrh], axis=1))
    qall = jnp.concatenate(qs, axis=0)
    tiles = [(kx_ref, 0, kx_ref.shape[0])] if has_ctx else []
    nk = ks_ref.shape[0]
    tk = min(nk, MLA_KEY_TILE)
    tiles += [(ks_ref, j * tk, tk) for j in range(nk // tk)]
    m = den = acc = None
    for ref, lo, n in tiles:
        kt = ref[lo:lo + n, :]
        s = _dot_nt(qall, kt)
        mt = jnp.max(s, axis=-1, keepdims=True)
        if m is None:
            m_new = mt
        else:
            m_new = jnp.maximum(m, mt)
            alpha = jnp.exp2(m - m_new)
        p = jnp.exp2(s - m_new)
        pv = _dot(p.astype(BF), kt[:, :LANES])
        ps = jnp.sum(p, axis=-1, keepdims=True)
        if m is None:
            den, acc = ps, pv
        else:
            den = alpha * den + ps
            acc = alpha * acc + pv
        m = m_new
    ot = acc / den
    ocat = jnp.concatenate([ot[h * tq:(h + 1) * tq] for h in range(MLA_HEADS)], axis=1).astype(BF)
    o_ref[...] = _dot(ocat, wv_ref[...]).astype(BF)


def _mla_call(qc, qr, kcat, kx, wvbd, l, batch, seq, tq=256):
    nq = seq // tq
    has_ctx = kx is not None
    in_specs = [pl.BlockSpec((tq, 2 * MIX), lambda b, i: (b * nq + i, 0)),
                pl.BlockSpec((tq, LANES), lambda b, i: (b * nq + i, 0)),
                pl.BlockSpec((None, seq, MIX), lambda b, i: (b, 0, 0))]
    args = [qc, qr, kcat.reshape(batch, seq, MIX)]
    if has_ctx:
        in_specs.append(pl.BlockSpec((None, None, kx.shape[2], MIX), lambda b, i: (b, l, 0, 0)))
        args.append(kx)
    in_specs.append(_pick(wvbd, l))
    args.append(wvbd)
    return pl.pallas_call(
        functools.partial(_mla_kernel, has_ctx=has_ctx),
        grid=(batch, nq),
        in_specs=in_specs,
        out_specs=pl.BlockSpec((tq, MIX), lambda b, i: (b * nq + i, 0)),
        out_shape=jax.ShapeDtypeStruct((batch * seq, MIX), BF),
        compiler_params=_params("parallel", "parallel"),
        name="mla_lat" if has_ctx else "mla_ctx",
    )(*args)


def _rot_cols(w, width):
    half = width // 2
    parts = w.reshape(w.shape[:-1] + (w.shape[-1] // width, 2, half))
    return jnp.stack([-parts[..., 1, :], parts[..., 0, :]], axis=-2).reshape(w.shape)


def _block_diag(blocks):
    n, r, c = blocks.shape[-3:]
    eye = jnp.eye(n, dtype=blocks.dtype)
    out = jnp.einsum("...hrc,hg->...hrgc", blocks, eye)
    return out.reshape(blocks.shape[:-3] + (n * r, n * c))


def _layer_weights(w_in, gla_w_gate, gla_b_gate, mla_w_q_b, mla_w_kv_b):
    nl = w_in.shape[0]
    w_in = w_in.astype(BF)
    offs = [0]
    for n in (256, 256, 256, 256, 16, 16, 256, 128, 128, 256, 256, 128, 32):
        offs.append(offs[-1] + n)
    seg = lambda k: w_in[..., offs[k]:offs[k + 1]]
    zeros = lambda n: jnp.zeros((nl, D_MODEL, n), w_in.dtype)
    sq = seg(6).reshape(nl, D_MODEL, SWA_Q_HEADS, HEAD_DIM)
    sq_tiles = []
    for hq in range(SWA_Q_HEADS):
        parts = [zeros(HEAD_DIM), zeros(HEAD_DIM)]
        parts[hq // 2] = sq[:, :, hq]
        sq_tiles += parts
    sq_w = jnp.concatenate(sq_tiles, axis=-1)
    kr4 = jnp.tile(seg(12), (1, 1, MLA_HEADS))
    cols = [seg(0), seg(1), seg(2), seg(3), seg(4), seg(5), zeros(LANES - 2 * GLA_GATE_RANK),
            sq_w, seg(7), seg(8), seg(9), seg(10), seg(11), kr4,
            _rot_cols(sq_w, HEAD_DIM // 2), _rot_cols(seg(7), HEAD_DIM // 2), _rot_cols(kr4, MLA_ROPE // 2)]
    w = jnp.concatenate(cols, axis=-1)

    wgate = jnp.zeros((nl, LANES, 2 * MIX), F32)
    wgate = wgate.at[:, :GLA_GATE_RANK, :MIX].set(gla_w_gate[:, 0])
    wgate = wgate.at[:, GLA_GATE_RANK:2 * GLA_GATE_RANK, MIX:].set(gla_w_gate[:, 1])
    bgate = gla_b_gate.reshape(nl, 1, 2 * MIX)

    wq = mla_w_q_b.reshape(nl, MLA_Q_RANK, MLA_HEADS, MLA_NOPE + MLA_ROPE)
    q_nope = wq[..., :MLA_NOPE].reshape(nl, MLA_Q_RANK, MLA_HEADS * MLA_NOPE)
    q_rope = wq[..., MLA_NOPE:].reshape(nl, MLA_Q_RANK, MLA_HEADS * MLA_ROPE)
    wqb = jnp.concatenate([q_nope, q_rope, _rot_cols(q_rope, MLA_ROPE // 2)], axis=-1).astype(BF)
    wkv = mla_w_kv_b.reshape(nl, MLA_KV_RANK, MLA_HEADS, MLA_NOPE + MLA_V)
    wkbd = _block_diag(jnp.transpose(wkv[..., :MLA_NOPE], (0, 2, 3, 1))).astype(BF)
    wvbd = _block_diag(jnp.transpose(wkv[..., MLA_NOPE:], (0, 2, 1, 3))).astype(BF)
    return (w, wgate.astype(BF), bgate, wqb, wkbd), wvbd


def _rope_tables(seq):
    t = jnp.arange(seq)
    rows = (t // GRID_W).astype(F32)[:, None]
    cols = (t % GRID_W).astype(F32)[:, None]

    def table(width):
        half = width // 2
        lane = jnp.arange(LANES)
        inv = ROPE_BASE ** (-(2 * (lane % half)).astype(F32) / width)
        ang = jnp.where(((lane // width) % 2 == 0)[None, :], rows, cols) * inv[None, :]
        return jnp.cos(ang), jnp.sin(ang)

    cs, ss = table(HEAD_DIM // 2)
    cm, sm = table(MLA_ROPE // 2)
    return cs, ss, cm, sm


def _dft_channel():
    c = jnp.arange(FNET_GROUP_CH, dtype=jnp.int32)
    ang = ((c[:, None] * c[None, :]) % FNET_GROUP_CH).astype(F32) * (2.0 * jnp.pi / FNET_GROUP_CH)
    cos_b = jnp.broadcast_to(jnp.cos(ang)[None], (FNET_GROUPS, FNET_GROUP_CH, FNET_GROUP_CH))
    sin_b = jnp.broadcast_to(jnp.sin(ang)[None], (FNET_GROUPS, FNET_GROUP_CH, FNET_GROUP_CH))
    return _block_diag(cos_b).astype(BF), (-_block_diag(sin_b)).astype(BF)


def _state_in(s):
    return _block_diag(jnp.swapaxes(s, -1, -2))


def _state_out(st):
    blocks = [st[:, h * GLA_DV:(h + 1) * GLA_DV, h * GLA_DK:(h + 1) * GLA_DK] for h in range(GLA_HEADS)]
    return jnp.swapaxes(jnp.stack(blocks, axis=1), 2, 3)


def _token_mix(x, l, shared, st0, st0_idx, cache, batch, seq, rows_per_batch, first_row, latent):
    mod, gn, wts, wvbd, gq, gkv, sinks, cdft, sdft, rope_t, fft_t = shared
    outs = _inproj_call(x, mod, gn, wts, gq, gkv, cdft, sdft, rope_t, l, batch, seq, rows_per_batch,
                        first_row, latent, tm=512)
    gqv, gkv_, gv, gr, la, sq, sk, sv, zc, zs, qc, qr, kcat = outs[:13]
    tb = min(seq, 512)
    o_f, st_f = _gla_call(gqv, gkv_, gv, la, st0, st0_idx[0], None, batch, seq, False, tb)
    o_gla, st_b = _gla_call(gqv, gkv_, gv, la, st0, st0_idx[1], o_f, batch, seq, True, tb)
    if latent:
        kx, vx, mx = cache
        o_swa = _swa_call(sinks, l, sq, sk, sv, kx, vx, l, batch, seq, True)
        o_mla = _mla_call(qc, qr, kcat, mx, wvbd, l, batch, seq)
        new_ctx = None
    else:
        k4 = sk.reshape(batch, 1, seq, LANES)
        v4 = sv.reshape(batch, 1, seq, LANES)
        o_swa = _swa_call(sinks, l, sq, None, None, k4, v4, 0, batch, seq, False)
        o_mla = _mla_call(qc, qr, kcat, None, wvbd, l, batch, seq)
        new_ctx = (st_f, st_b, sk, sv, outs[13], outs[14])
    o_fft = _fft_call(fft_t, zc, zs, seq)
    return (o_gla, gr, o_swa, o_fft, o_mla), new_ctx


def kernel(x_prompt, x_sample, c, state_gla, cache_swa_k, cache_swa_v, cache_mla_ckv, cache_mla_krope,
           c_ctx, w_mod, b_mod, g_norm, w_ffn_gate, w_ffn_up, w_ffn_down, w_in, gla_w_gate, gla_b_gate,
           gla_g_out, swa_sink, mla_g_q, mla_g_kv, mla_w_q_b, mla_w_kv_b, w_out):
    nb, ns, _ = x_prompt.shape
    db, dsq, _ = x_sample.shape
    past = cache_swa_k.shape[2]

    cvec = jnp.zeros((MOD_ROWS, D_MODEL), F32).at[0].set(c_ctx).at[1:1 + db].set(c)
    mod = _mod_call(cvec, w_mod, b_mod).reshape(DEPTH, MOD_ROWS, N_MOD, 1, D_MODEL)
    gn = g_norm.reshape(DEPTH, 6, 1, D_MODEL)

    wg, wu, wd, wo = (a.astype(BF) for a in (w_ffn_gate, w_ffn_up, w_ffn_down, w_out))
    wts, wvbd = _layer_weights(w_in, gla_w_gate, gla_b_gate, mla_w_q_b, mla_w_kv_b)
    gq = mla_g_q.reshape(DEPTH, 1, MLA_Q_RANK)
    gkv = mla_g_kv.reshape(DEPTH, 1, MLA_KV_RANK)
    gout = jnp.tile(gla_g_out, (1, GLA_HEADS)).reshape(DEPTH, 1, MIX)
    cdft, sdft = _dft_channel()
    shared = (mod, gn, wts, wvbd, gq, gkv, swa_sink, cdft, sdft)
    head = jnp.arange(MIX) // GLA_DV
    avg = jnp.where(head[:, None] == head[None, :], 1.0 / GLA_DV, 0.0).astype(BF)
    shared_ctx = shared + (None, _fft_tables(ns))
    shared_lat = shared + (_rope_tables(dsq), _fft_tables(dsq))

    st_lat = _state_in(state_gla)
    st_zero = jnp.zeros((nb, 1, 1, MIX, MIX), F32)
    kx = cache_swa_k.reshape(db, DEPTH, past, LANES)
    vx = cache_swa_v.reshape(db, DEPTH, past, LANES)
    mx = jnp.concatenate([cache_mla_ckv, jnp.tile(cache_mla_krope, (1, 1, 1, MLA_HEADS))], axis=-1).astype(BF)

    xc = x_prompt.reshape(nb * ns, D_MODEL)
    xl = x_sample.reshape(db * dsq, D_MODEL)
    st_gla, st_k, st_v, st_ckv, st_kr = [], [], [], [], []
    for l in range(DEPTH):
        xc = _ffn_call(xc, mod, gn, wg, wu, wd, l, rows_per_batch=nb * ns, first_row=0)
        mixed, new_ctx = _token_mix(xc, l, shared_ctx, st_zero, ((0, 0), (0, 0)), None, nb, ns, nb * ns, 0, False)
        xc = _outffn_call(*mixed, xc, mod, gn, gout, avg, wo, wg, wu, wd, l, seq=ns, rows_per_batch=nb * ns,
                          first_row=0, tm=512)
        st_f, st_b, k_c, v_c, ckv_c, kr_c = new_ctx
        st_gla.append(jnp.stack([_state_out(st_f), _state_out(st_b)], axis=1))
        st_k.append(k_c.reshape(nb, ns, SWA_KV_HEADS, HEAD_DIM))
        st_v.append(v_c.reshape(nb, ns, SWA_KV_HEADS, HEAD_DIM))
        st_ckv.append(ckv_c.reshape(nb, ns, MLA_KV_RANK))
        st_kr.append(kr_c.reshape(nb, ns, MLA_ROPE))

        xl = _ffn_call(xl, mod, gn, wg, wu, wd, l, rows_per_batch=dsq, first_row=1)
        mixed, _ = _token_mix(xl, l, shared_lat, st_lat, ((l, 0), (l, 1)), (kx, vx, mx), db, dsq, dsq, 1, True)
        xl = _outffn_call(*mixed, xl, mod, gn, gout, avg, wo, wg, wu, wd, l, seq=dsq, rows_per_batch=dsq,
                          first_row=1, tm=512)

    return (xc.reshape(nb, ns, D_MODEL), xl.reshape(db, dsq, D_MODEL), jnp.stack(st_gla, axis=1),
            jnp.stack(st_k, axis=1), jnp.stack(st_v, axis=1), jnp.stack(st_ckv, axis=1),
            jnp.stack(st_kr, axis=1))
```

```python
import functools

import jax
import jax.numpy as jnp
from jax import lax
from jax.experimental import pallas as pl
from jax.experimental.pallas import tpu as pltpu

D_MODEL = 1024
DEPTH = 4
GRID_W = 64
HEAD_DIM = 64
GLA_HEADS = 4
GLA_DK = 64
GLA_DV = 64
GLA_GATE_RANK = 16
GLA_TAU = 16.0
GLA_CHUNK = 64
SWA_Q_HEADS = 4
SWA_KV_HEADS = 2
SWA_BLOCK = 128
FNET_GROUPS = 4
FNET_GROUP_CH = 64
MLA_HEADS = 4
MLA_Q_RANK = 256
MLA_KV_RANK = 128
MLA_NOPE = 64
MLA_ROPE = 32
MLA_V = 64
D_FF = 2816
FFN_RES = 0.5
N_MOD = 9
ROPE_BASE = 10000.0
EPS = 1e-6
NEG_INF = -1e30

MIX = 256
LANES = 128
MXU_DEPTH = 256
FF_CHUNK = 256
MLA_KEY_TILE = 2048
GLA_GROUP = 4
LOG2E = 1.4426950408889634
MOD_ROWS = 8
VMEM_LIMIT = 56 * 1024 * 1024

BF = jnp.bfloat16
F32 = jnp.float32

C_GQ, C_GK, C_GV, C_GR, C_A = 0, 256, 512, 768, 1024
C_SQ, C_SK, C_SV, C_ZF, C_QA, C_KVA, C_KR = 1152, 1664, 1792, 1920, 2176, 2432, 2560
C_SQR, C_SKR, C_KRR = 2688, 3200, 3328
NC_CTX, NC_LAT = 2688, 3456


def _dot(a, b):
    return jnp.dot(a, b, preferred_element_type=F32)


def _dot_nt(a, b):
    return lax.dot_general(a, b, (((1,), (1,)), ((), ())), preferred_element_type=F32)


def _dot_tn(a, b):
    return lax.dot_general(a, b, (((0,), (0,)), ((), ())), preferred_element_type=F32)


def _rms(x, g):
    return x * lax.rsqrt(jnp.mean(x * x, axis=-1, keepdims=True) + EPS) * g


def _silu(x):
    return x * jax.nn.sigmoid(x)


def _params(*sem):
    return pltpu.CompilerParams(dimension_semantics=sem, vmem_limit_bytes=VMEM_LIMIT)


def _pick(arr, *lead, block=None):
    tail = tuple(arr.shape[len(lead):]) if block is None else tuple(block)
    zeros = (0,) * len(tail)
    return pl.BlockSpec((None,) * len(lead) + tail, lambda *_: tuple(lead) + zeros,
                        pipeline_mode=pl.Buffered(1))


def _mod_kernel(c_ref, w_ref, b_ref, o_ref):
    s = _silu(c_ref[...]).astype(BF)
    o_ref[...] = _dot(s, w_ref[...].astype(BF)) + b_ref[...]


def _mod_call(cvec, w_mod, b_mod):
    nl = w_mod.shape[0]
    tn = D_MODEL
    return pl.pallas_call(
        _mod_kernel,
        grid=(nl, N_MOD * D_MODEL // tn),
        in_specs=[
            pl.BlockSpec((MOD_ROWS, D_MODEL), lambda l, j: (0, 0)),
            pl.BlockSpec((None, D_MODEL, tn), lambda l, j: (l, 0, j)),
            pl.BlockSpec((None, 1, tn), lambda l, j: (l, 0, j)),
        ],
        out_specs=pl.BlockSpec((None, MOD_ROWS, tn), lambda l, j: (l, 0, j)),
        out_shape=jax.ShapeDtypeStruct((nl, MOD_ROWS, N_MOD * D_MODEL), F32),
        compiler_params=_params("parallel", "parallel"),
        name="mod",
    )(cvec, w_mod, b_mod.reshape(nl, 1, N_MOD * D_MODEL))


def _mod_spec(l, sub, tm, rows_per_batch, first_row):
    per = rows_per_batch // tm
    return pl.BlockSpec((None, None, 3, 1, D_MODEL), lambda i: (l, first_row + i // per, sub, 0, 0))


def _fft_layout_spec(tm, seq):
    per = max(seq // tm, 1)
    nbat = max(tm // seq, 1)
    return pl.BlockSpec((tm // nbat, nbat * MIX), lambda i: (i % per, i // per))


def _gn_spec(l, first, n):
    return pl.BlockSpec((None, n, 1, D_MODEL), lambda i: (l, first // n, 0, 0))


def _ffn_body(x, mod_ref, g_pre, g_post, wg_ref, wu_ref, wd_ref):
    sh, sc, gt = mod_ref[0], mod_ref[1], mod_ref[2]
    hb = (_rms(x, g_pre) * (1.0 + sc) + sh).astype(BF)
    acc = None
    for j in range(D_FF // FF_CHUNK):
        sl = slice(j * FF_CHUNK, (j + 1) * FF_CHUNK)
        g = _dot(hb, wg_ref[:, sl])
        u = _dot(hb, wu_ref[:, sl])
        d = _dot((_silu(g) * u).astype(BF), wd_ref[sl, :])
        acc = d if acc is None else acc + d
    return x + FFN_RES * gt * _rms(acc, g_post)


def _ffn_kernel(x_ref, mod_ref, gn_ref, wg_ref, wu_ref, wd_ref, o_ref):
    o_ref[...] = _ffn_body(x_ref[...], mod_ref, gn_ref[0], gn_ref[1], wg_ref, wu_ref, wd_ref)


def _ffn_call(x, mod, gn, wg, wu, wd, l, rows_per_batch, first_row, tm=512):
    r = x.shape[0]
    return pl.pallas_call(
        _ffn_kernel,
        grid=(r // tm,),
        in_specs=[
            pl.BlockSpec((tm, D_MODEL), lambda i: (i, 0)),
            _mod_spec(l, 0, tm, rows_per_batch, first_row),
            _gn_spec(l, 0, 2),
            _pick(wg, l, 0), _pick(wu, l, 0), _pick(wd, l, 0),
        ],
        out_specs=pl.BlockSpec((tm, D_MODEL), lambda i: (i, 0)),
        out_shape=jax.ShapeDtypeStruct(x.shape, F32),
        compiler_params=_params("parallel"),
        name="ffn",
    )(x, mod, gn, wg, wu, wd)


def _outffn_kernel(og_ref, gr_ref, os_ref, of_ref, om_ref, x_ref, mod2_ref, mod3_ref, gn_ref, gout_ref,
                   avg_ref, wo_ref, wg_ref, wu_ref, wd_ref, o_ref):
    og = og_ref[...]
    ms = _dot((og * og).astype(BF), avg_ref[...])
    og = (og * lax.rsqrt(ms + EPS) * gout_ref[...] * gr_ref[...]).astype(BF)
    nbat = of_ref.shape[1] // MIX
    of = jnp.concatenate([of_ref[:, j * MIX:(j + 1) * MIX] for j in range(nbat)], axis=0).astype(BF)
    mix = jnp.concatenate([og, os_ref[...], of, om_ref[...]], axis=1)
    x = x_ref[...] + mod2_ref[2] * _rms(_dot(mix, wo_ref[...]), gn_ref[0])
    o_ref[...] = _ffn_body(x, mod3_ref, gn_ref[1], gn_ref[2], wg_ref, wu_ref, wd_ref)


def _outffn_call(og, gr, osw, of, om, x, mod, gn, gout, avg, wo, wg, wu, wd, l, seq, rows_per_batch, first_row, tm):
    r = x.shape[0]
    mspec = pl.BlockSpec((tm, MIX), lambda i: (i, 0))
    return pl.pallas_call(
        _outffn_kernel,
        grid=(r // tm,),
        in_specs=[
            mspec, mspec, mspec,
            _fft_layout_spec(tm, seq),
            mspec,
            pl.BlockSpec((tm, D_MODEL), lambda i: (i, 0)),
            _mod_spec(l, 1, tm, rows_per_batch, first_row),
            _mod_spec(l, 2, tm, rows_per_batch, first_row),
            _gn_spec(l, 3, 3),
            _pick(gout, l), _pick(avg),
            _pick(wo, l), _pick(wg, l, 1), _pick(wu, l, 1), _pick(wd, l, 1),
        ],
        out_specs=pl.BlockSpec((tm, D_MODEL), lambda i: (i, 0)),
        out_shape=jax.ShapeDtypeStruct(x.shape, F32),
        compiler_params=_params("parallel"),
        name="outffn",
    )(og, gr, osw, of, om, x, mod, mod, gn, gout, avg, wo, wg, wu, wd)


def _log_sigmoid(x):
    return -(jnp.maximum(-x, 0.0) + jnp.log(1.0 + jnp.exp(-jnp.abs(x))))


def _inproj_kernel(*refs, latent, nbat):
    (x_ref, mod_ref, gn_ref, w_ref, wgate_ref, bgate_ref, gq_ref, gkv_ref, wqb_ref, wkbd_ref,
     cdft_ref, sdft_ref) = refs[:12]
    pos = 12
    if latent:
        cs_ref, ss_ref, cm_ref, sm_ref = refs[pos:pos + 4]
        pos += 4
    (gq_o, gk_o, gv_o, gr_o, la_o, sq_o, sk_o, sv_o, zc_o, zs_o, qc_o, qr_o, kcat_o) = refs[pos:pos + 13]
    pos += 13
    if not latent:
        ckv_o, kr_o = refs[pos:pos + 2]

    x = x_ref[...]
    hb = (_rms(x, gn_ref[0]) * (1.0 + mod_ref[1]) + mod_ref[0]).astype(BF)

    def col(off, n):
        return _dot(hb, w_ref[:, off:off + n])

    gq_o[...] = col(C_GQ, MIX) * (GLA_DK ** -0.5)
    gk_o[...] = col(C_GK, MIX)
    gv_o[...] = col(C_GV, MIX).astype(BF)
    gr_o[...] = _silu(col(C_GR, MIX))
    logit = _dot(col(C_A, LANES).astype(BF), wgate_ref[...]) + bgate_ref[...]
    la_o[...] = _log_sigmoid(logit) * (1.0 / GLA_TAU)

    sq = col(C_SQ, 2 * MIX)
    sk = col(C_SK, LANES)
    sv = col(C_SV, LANES)
    if latent:
        cs, ss = cs_ref[...], ss_ref[...]
        cs4 = jnp.concatenate([cs] * 4, axis=1)
        ss4 = jnp.concatenate([ss] * 4, axis=1)
        sq = sq * cs4 + col(C_SQR, 2 * MIX) * ss4
        sk = sk * cs + col(C_SKR, LANES) * ss
    sq_o[...] = (sq * (HEAD_DIM ** -0.5 * LOG2E)).astype(BF)
    sk_o[...] = sk.astype(sk_o.dtype)
    sv_o[...] = sv.astype(sv_o.dtype)

    zf = col(C_ZF, MIX).astype(BF)
    zc = _dot(zf, cdft_ref[...]).astype(zc_o.dtype)
    zs = _dot(zf, sdft_ref[...]).astype(zs_o.dtype)
    rows_b = zf.shape[0] // nbat
    for j in range(nbat):
        zc_o[:, j * MIX:(j + 1) * MIX] = zc[j * rows_b:(j + 1) * rows_b]
        zs_o[:, j * MIX:(j + 1) * MIX] = zs[j * rows_b:(j + 1) * rows_b]

    qn = _rms(col(C_QA, MLA_Q_RANK), gq_ref[...]).astype(BF)
    qm = _dot(qn, wqb_ref[...])
    q_rope = qm[:, MIX:MIX + LANES]
    ckv = _rms(col(C_KVA, MLA_KV_RANK), gkv_ref[...])
    kr = col(C_KR, LANES)
    if not latent:
        ckv_o[...] = ckv
        kr_o[...] = kr[:, :MLA_ROPE]
    else:
        cm, sm = cm_ref[...], sm_ref[...]
        q_rope = q_rope * cm + qm[:, MIX + LANES:MIX + 2 * LANES] * sm
        kr = kr * cm + col(C_KRR, LANES) * sm
    scale = (MLA_NOPE + MLA_ROPE) ** -0.5 * LOG2E
    qc_o[...] = (_dot(qm[:, :MIX].astype(BF), wkbd_ref[...]) * scale).astype(BF)
    qr_o[...] = (q_rope * scale).astype(BF)
    kcat_o[:, :LANES] = ckv.astype(BF)
    kcat_o[:, LANES:] = kr.astype(BF)


def _inproj_call(x, mod, gn, wts, gq, gkv, cdft, sdft, tables, l, batch, seq, rows_per_batch, first_row,
                 latent, tm):
    w, wgate, bgate, wqb, wkbd = wts
    r = x.shape[0]
    per = max(seq // tm, 1)
    nbat = max(tm // seq, 1)
    row = lambda n: pl.BlockSpec((tm, n), lambda i: (i, 0))
    ncol = NC_LAT if latent else NC_CTX
    nq = wqb.shape[-1] if latent else MIX + LANES
    in_specs = [
        row(D_MODEL),
        _mod_spec(l, 1, tm, rows_per_batch, first_row),
        _gn_spec(l, 2, 1),
        _pick(w, l, block=(D_MODEL, ncol)), _pick(wgate, l), _pick(bgate, l), _pick(gq, l), _pick(gkv, l),
        _pick(wqb, l, block=(MLA_Q_RANK, nq)), _pick(wkbd, l), _pick(cdft), _pick(sdft),
    ]
    args = [x, mod, gn, w, wgate, bgate, gq, gkv, wqb, wkbd, cdft, sdft]
    if latent:
        in_specs += [pl.BlockSpec((tm, LANES), lambda i: (i % per, 0))] * 4
        args += list(tables)
    kvdt = BF if latent else F32
    zdt = F32 if latent else BF
    fft_spec = _fft_layout_spec(tm, seq)
    out_specs = [row(MIX), row(MIX), row(MIX), row(MIX), row(2 * MIX), row(2 * MIX), row(LANES), row(LANES),
                 fft_spec, fft_spec, row(2 * MIX), row(LANES), row(MIX)]
    sd = jax.ShapeDtypeStruct
    out_shape = [sd((r, MIX), F32), sd((r, MIX), F32), sd((r, MIX), BF), sd((r, MIX), F32),
                 sd((r, 2 * MIX), F32), sd((r, 2 * MIX), BF), sd((r, LANES), kvdt), sd((r, LANES), kvdt),
                 sd((seq, batch * MIX), zdt), sd((seq, batch * MIX), zdt),
                 sd((r, 2 * MIX), BF), sd((r, LANES), BF), sd((r, MIX), BF)]
    if not latent:
        out_specs += [row(MLA_KV_RANK), row(MLA_ROPE)]
        out_shape += [sd((r, MLA_KV_RANK), F32), sd((r, MLA_ROPE), F32)]
    return pl.pallas_call(
        functools.partial(_inproj_kernel, latent=latent, nbat=nbat),
        grid=(r // tm,),
        in_specs=in_specs,
        out_specs=out_specs,
        out_shape=out_shape,
        compiler_params=_params("parallel"),
        name="inproj_lat" if latent else "inproj_ctx",
    )(*args)


def _chunk_cumsum(x, row, reverse):
    n = x.shape[0]
    k = 1
    while k < n:
        if reverse:
            x = x + jnp.where(row < n - k, pltpu.roll(x, n - k, 0), 0.0)
        else:
            x = x + jnp.where(row >= k, pltpu.roll(x, k, 0), 0.0)
        k *= 2
    return x


def _gla_kernel(q_ref, k_ref, v_ref, la_ref, s0_ref, *rest, reverse, final, nchunk, group):
    if final:
        of_ref, o_ref, sfin_ref, st_ref = rest
    else:
        o_ref, sfin_ref, st_ref = rest
    i = pl.program_id(1)
    c = GLA_CHUNK

    @pl.when(i == 0)
    def _():
        st_ref[...] = s0_ref[...]

    row = lax.broadcasted_iota(jnp.int32, (c, MIX), 0)
    lane = lax.broadcasted_iota(jnp.int32, (c, MIX), 1) % c
    keep = (lane >= row) if reverse else (lane <= row)
    lane1 = lax.broadcasted_iota(jnp.int32, (1, MIX), 1) // c
    hmask = [(lane1 == h).astype(F32) for h in range(GLA_HEADS)]
    hmask_b = [m.astype(BF) for m in hmask]
    bdiag = (lax.broadcasted_iota(jnp.int32, (MIX, MIX), 0) // c
             == lax.broadcasted_iota(jnp.int32, (MIX, MIX), 1) // c)

    def chunk(t, carry):
        cc = (nchunk - 1 - t) if reverse else t
        rs = pl.ds(pl.multiple_of(cc * c, c), c)
        for g in range(group):
            cum = _chunk_cumsum(la_ref[g, rs, :], row, reverse)
            tot = cum[0:1] if reverse else cum[c - 1:c]
            q, k, v = q_ref[g, rs, :], k_ref[g, rs, :], v_ref[g, rs, :]
            qd = (q * jnp.exp(cum)).astype(BF)
            ki = k * jnp.exp(-cum)
            kd = (k * jnp.exp(tot - cum)).astype(BF)
            kstack = jnp.concatenate([(ki * m).astype(BF) for m in hmask], axis=0)
            vstack = jnp.concatenate([v * m for m in hmask_b], axis=0)
            att = jnp.where(keep, _dot_nt(qd, kstack), 0.0).astype(BF)
            st = st_ref[g]
            o = _dot(att, vstack) + _dot_nt(qd, st.astype(BF))
            st_ref[g] = st * jnp.exp(tot) + jnp.where(bdiag, _dot_tn(v, kd), 0.0)
            if final:
                o = o + of_ref[g, rs, :]
            o_ref[g, rs, :] = o
        return carry

    lax.fori_loop(0, nchunk, chunk, 0, unroll=2)

    @pl.when(i == pl.num_programs(1) - 1)
    def _():
        sfin_ref[...] = st_ref[...]


def _gla_call(q, k, v, la, s0, s0_idx, extra, batch, seq, reverse, tb, group=GLA_GROUP):
    nblk = seq // tb
    final = extra is not None
    pos = (lambda i: nblk - 1 - i) if reverse else (lambda i: i)
    blk = pl.BlockSpec((group, tb, MIX), lambda b, i: (b, pos(i), 0))
    lblk = pl.BlockSpec((group, tb, MIX), lambda b, i: (b, pos(i), 1 if reverse else 0))
    sblk = pl.BlockSpec((group, MIX, MIX), lambda b, i: (b, 0, 0))
    s0blk = pl.BlockSpec((group, None, None, MIX, MIX), lambda b, i: (b,) + tuple(s0_idx) + (0, 0))
    r3 = lambda a: a.reshape(batch, seq, a.shape[-1])
    in_specs = [blk, blk, blk, lblk, s0blk]
    args = [r3(q), r3(k), r3(v), r3(la), s0]
    if final:
        in_specs.append(blk)
        args.append(r3(extra))
    o, st = pl.pallas_call(
        functools.partial(_gla_kernel, reverse=reverse, final=final, nchunk=tb // GLA_CHUNK, group=group),
        grid=(batch // group, nblk),
        in_specs=in_specs,
        out_specs=[blk, sblk],
        out_shape=[jax.ShapeDtypeStruct((batch, seq, MIX), F32),
                   jax.ShapeDtypeStruct((batch, MIX, MIX), F32)],
        scratch_shapes=[pltpu.VMEM((group, MIX, MIX), F32)],
        compiler_params=_params("parallel", "arbitrary"),
        name="gla_bwd" if reverse else "gla_fwd",
    )(*args)
    return o.reshape(batch * seq, MIX), st


def _swa_kernel(*refs, local, layer):
    sink_ref, q_ref = refs[:2]
    pos = 2
    if local:
        kp_ref, kc_ref, kn_ref, vp_ref, vc_ref, vn_ref = refs[pos:pos + 6]
        pos += 6
    kx_ref, vx_ref, o_ref = refs[pos:pos + 3]
    i = pl.program_id(1)
    last = pl.num_programs(1) - 1
    blk = SWA_BLOCK
    tq = q_ref.shape[0]
    cx = kx_ref.shape[0]
    kx = kx_ref[...].astype(BF)
    vx = vx_ref[...].astype(BF)
    if local:
        nsub = tq // blk
        kblocks = [kp_ref[...]] + [kc_ref[j * blk:(j + 1) * blk, :] for j in range(nsub)] + [kn_ref[...]]
        vblocks = [vp_ref[...]] + [vc_ref[j * blk:(j + 1) * blk, :] for j in range(nsub)] + [vn_ref[...]]
        row = lax.broadcasted_iota(jnp.int32, (blk, blk), 0)
        col = lax.broadcasted_iota(jnp.int32, (blk, blk), 1)
        full = jnp.full((blk, blk), True)
        fullx = jnp.full((blk, cx), True)
    else:
        nsub = 1
    rq = tq // nsub
    rows = lax.broadcasted_iota(jnp.int32, (2 * rq, 1), 0)
    low = lax.broadcasted_iota(jnp.int32, (rq, LANES), 1) < HEAD_DIM
    for j in range(nsub):
        if local:
            kcat = jnp.concatenate([kx, kblocks[j], kblocks[j + 1], kblocks[j + 2]], axis=0)
            vcat = jnp.concatenate([vx, vblocks[j], vblocks[j + 1], vblocks[j + 2]], axis=0)
            prev_ok = (col >= row) if j > 0 else jnp.logical_and(col >= row, i > 0)
            next_ok = (col <= row) if j < nsub - 1 else jnp.logical_and(col <= row, i < last)
            keep = jnp.concatenate([fullx, prev_ok, full, next_ok], axis=1)
            keep = jnp.concatenate([keep, keep], axis=0)
        else:
            kcat, vcat = kx, vx
        rs = slice(j * rq, (j + 1) * rq)
        tiles = []
        for g in range(SWA_KV_HEADS):
            q2 = jnp.concatenate([q_ref[rs, (2 * g) * LANES:(2 * g + 1) * LANES],
                                  q_ref[rs, (2 * g + 1) * LANES:(2 * g + 2) * LANES]], axis=0)
            sink = jnp.where(rows < rq, sink_ref[layer, 2 * g], sink_ref[layer, 2 * g + 1]) * LOG2E
            s = _dot_nt(q2, kcat)
            if local:
                s = jnp.where(keep, s, NEG_INF)
            m = jnp.maximum(jnp.max(s, axis=-1, keepdims=True), sink)
            e = jnp.exp2(s - m)
            den = jnp.sum(e, axis=-1, keepdims=True) + jnp.exp2(sink - m)
            o2 = _dot(e.astype(BF), vcat) / den
            oa, ob = o2[:rq], o2[rq:]
            if g == 0:
                tiles.append(jnp.where(low, oa, pltpu.roll(ob, HEAD_DIM, 1)))
            else:
                tiles.append(jnp.where(low, pltpu.roll(oa, HEAD_DIM, 1), ob))
        o_ref[rs, :LANES] = tiles[0].astype(BF)
        o_ref[rs, LANES:] = tiles[1].astype(BF)


def _swa_call(sinks, layer, q, k, v, kx, vx, x_layer, batch, seq, local, tq=256):
    nt = seq // tq
    per = tq // SWA_BLOCK
    nb = seq // SWA_BLOCK
    cx = kx.shape[2]
    qspec = pl.BlockSpec((tq, 2 * MIX), lambda b, i: (b * nt + i, 0))
    in_specs = [pl.BlockSpec(memory_space=pltpu.SMEM), qspec]
    args = [sinks, q]
    if local:
        prv = pl.BlockSpec((SWA_BLOCK, LANES), lambda b, i: (b * nb + jnp.maximum(i * per - 1, 0), 0))
        cur = pl.BlockSpec((tq, LANES), lambda b, i: (b * nt + i, 0))
        nxt = pl.BlockSpec((SWA_BLOCK, LANES), lambda b, i: (b * nb + jnp.minimum((i + 1) * per, nb - 1), 0))
        in_specs += [prv, cur, nxt, prv, cur, nxt]
        args += [k, k, k, v, v, v]
    xspec = pl.BlockSpec((None, None, cx, LANES), lambda b, i: (b, x_layer, 0, 0))
    in_specs += [xspec, xspec]
    args += [kx, vx]
    return pl.pallas_call(
        functools.partial(_swa_kernel, local=local, layer=layer),
        grid=(batch, nt),
        in_specs=in_specs,
        out_specs=pl.BlockSpec((tq, MIX), lambda b, i: (b * nt + i, 0)),
        out_shape=jax.ShapeDtypeStruct((batch * seq, MIX), BF),
        compiler_params=_params("parallel", "parallel"),
        name="swa_lat" if local else "swa_ctx",
    )(*args)


def _fft_kernel(xc_ref, xs_ref, yc_ref, ys_ref, zc_ref, zs_ref, o_ref, *, nj, scale):
    yc, ys = yc_ref[...], ys_ref[...]
    xc, xs = xc_ref[...], xs_ref[...]
    acc = None
    for jp in range(nj // 2):
        cos_t, sin_t = [], []
        for j in (2 * jp, 2 * jp + 1):
            a, b = xc[:, j:j + 1], xs[:, j:j + 1]
            cos_t.append(a * yc - b * ys)
            sin_t.append(b * yc + a * ys)
        cos_t = jnp.concatenate(cos_t, axis=1).astype(BF)
        sin_t = jnp.concatenate(sin_t, axis=1).astype(BF)
        rows = slice(jp * MXU_DEPTH, (jp + 1) * MXU_DEPTH)
        d = _dot(cos_t, zc_ref[rows, :]) + _dot(sin_t, zs_ref[rows, :])
        acc = d if acc is None else acc + d
    o_ref[...] = (acc * scale).astype(BF)


def _fft_tables(seq):
    sp = jnp.arange(seq, dtype=jnp.int32)[:, None]
    w = 2.0 * jnp.pi / seq
    ax = ((sp * (jnp.arange(seq // LANES, dtype=jnp.int32) * LANES)[None, :]) % seq).astype(F32) * w
    ay = ((sp * jnp.arange(LANES, dtype=jnp.int32)[None, :]) % seq).astype(F32) * w
    return jnp.cos(ax), jnp.sin(ax), jnp.cos(ay), jnp.sin(ay)


def _fft_call(tables, zc, zs, seq, tm=256):
    xc, xs, yc, ys = tables
    nj = seq // LANES
    width = zc.shape[1]
    return pl.pallas_call(
        functools.partial(_fft_kernel, nj=nj, scale=(seq * FNET_GROUP_CH) ** -0.5),
        grid=(seq // tm,),
        in_specs=[
            pl.BlockSpec((tm, nj), lambda i: (i, 0)), pl.BlockSpec((tm, nj), lambda i: (i, 0)),
            pl.BlockSpec((tm, LANES), lambda i: (i, 0)), pl.BlockSpec((tm, LANES), lambda i: (i, 0)),
            _pick(zc), _pick(zs),
        ],
        out_specs=pl.BlockSpec((tm, width), lambda i: (i, 0)),
        out_shape=jax.ShapeDtypeStruct((seq, width), BF),
        compiler_params=_params("parallel"),
        name="fft",
    )(xc, xs, yc, ys, zc, zs)


def _fft2_kernel(zc_ref, zs_ref, m1_ref, m2_ref, tr_ref, ti_ref, o_ref, x_ref, *, nb1, scale):
    i = pl.program_id(1)
    n = m2_ref.shape[0]

    @pl.when(i < nb1)
    def _():
        m1 = m1_ref[...]
        for j in range(8):
            z = jnp.concatenate([zc_ref[:, j, :], zs_ref[:, j, :]], axis=0).astype(BF)
            x1 = _dot(m1, z)
            xr, xi = x1[:n], x1[n:]
            tr, ti = tr_ref[:, j:j + 1], ti_ref[:, j:j + 1]
            x2 = jnp.concatenate([xr * tr - xi * ti, xr * ti + xi * tr], axis=0)
            x_ref[i * 8 + j] = x2.reshape(2 * n // 8, 8, x2.shape[-1])

    @pl.when(i >= nb1)
    def _():
        m2 = m2_ref[...]
        blk = i - nb1
        for j in range(8):
            z = jnp.concatenate([x_ref[:, blk, j, :], x_ref[:, n // 8 + blk, j, :]], axis=0).astype(BF)
            o_ref[:, j, :] = _dot(m2, z) * scale


def _fft2_call(zc, zs, seq, wl=512):
    n = int(round(seq ** 0.5))
    assert seq == n * n and n % 8 == 0
    width = zc.shape[1]
    nb = n // 8
    k = jnp.arange(n, dtype=jnp.int32)
    ang = ((k[:, None] * k[None, :]) % n).astype(F32) * (2.0 * jnp.pi / n)
    c, sn = jnp.cos(ang), jnp.sin(ang)
    m1 = jnp.concatenate([jnp.concatenate([c, sn], axis=1), jnp.concatenate([-sn, c], axis=1)], axis=0).astype(BF)
    m2 = jnp.concatenate([c, sn], axis=1).astype(BF)
    tw = (k[:, None] * k[None, :]).astype(F32) * (2.0 * jnp.pi / seq)
    tr = jnp.cos(tw).reshape(n, nb, 8).transpose(1, 0, 2)
    ti = (-jnp.sin(tw)).reshape(n, nb, 8).transpose(1, 0, 2)
    zspec = pl.BlockSpec((n, 8, wl), lambda h, i: (0, jnp.minimum(i, nb - 1), h))
    tspec = pl.BlockSpec((None, n, 8), lambda h, i: (jnp.minimum(i, nb - 1), 0, 0))
    out = pl.pallas_call(
        functools.partial(_fft2_kernel, nb1=nb, scale=(seq * FNET_GROUP_CH) ** -0.5),
        grid=(width // wl, 2 * nb),
        in_specs=[zspec, zspec, _pick(m1), _pick(m2), tspec, tspec],
        out_specs=pl.BlockSpec((n, 8, wl), lambda h, i: (0, jnp.maximum(i - nb, 0), h)),
        out_shape=jax.ShapeDtypeStruct((n, n, width), F32),
        scratch_shapes=[pltpu.VMEM((n, 2 * n // 8, 8, wl), F32)],
        compiler_params=_params("parallel", "arbitrary"),
        name="fft2",
    )(zc.reshape(n, n, width), zs.reshape(n, n, width), m1, m2, tr, ti)
    return out.reshape(seq, width)


def _mla_kernel(*refs, has_ctx):
    qc_ref, qr_ref, ks_ref = refs[:3]
    pos = 3
    if has_ctx:
        kx_ref = refs[pos]
        pos += 1
    wv_ref, o_ref = refs[pos:pos + 2]
    qr = qr_ref[...]
    lane = lax.broadcasted_iota(jnp.int32, qr.shape, 1) // MLA_ROPE
    tq = qr.shape[0]
    qs = []
    for h in range(MLA_HEADS):
        qrh = jnp.where(lane == h, qr, jnp.zeros_like(qr))
        qs.append(jnp.concatenate([qc_ref[:, h * LANES:(h + 1) * LANES], qrh], axis=1))
    qall = jnp.concatenate(qs, axis=0)
    tiles = [(kx_ref, 0, kx_ref.shape[0])] if has_ctx else []
    nk = ks_ref.shape[0]
    tk = min(nk, MLA_KEY_TILE)
    tiles += [(ks_ref, j * tk, tk) for j in range(nk // tk)]
    m = den = acc = None
    for ref, lo, n in tiles:
        kt = ref[lo:lo + n, :]
        s = _dot_nt(qall, kt)
        mt = jnp.max(s, axis=-1, keepdims=True)
        if m is None:
            m_new = mt
        else:
            m_new = jnp.maximum(m, mt)
            alpha = jnp.exp2(m - m_new)
        p = jnp.exp2(s - m_new)
        pv = _dot(p.astype(BF), kt[:, :LANES])
        ps = jnp.sum(p, axis=-1, keepdims=True)
        if m is None:
            den, acc = ps, pv
        else:
            den = alpha * den + ps
            acc = alpha * acc + pv
        m = m_new
    ot = acc / den
    ocat = jnp.concatenate([ot[h * tq:(h + 1) * tq] for h in range(MLA_HEADS)], axis=1).astype(BF)
    o_ref[...] = _dot(ocat, wv_ref[...]).astype(BF)


def _mla_call(qc, qr, kcat, kx, wvbd, l, batch, seq, tq=256):
    nq = seq // tq
    has_ctx = kx is not None
    in_specs = [pl.BlockSpec((tq, 2 * MIX), lambda b, i: (b * nq + i, 0)),
                pl.BlockSpec((tq, LANES), lambda b, i: (b * nq + i, 0)),
                pl.BlockSpec((None, seq, MIX), lambda b, i: (b, 0, 0))]
    args = [qc, qr, kcat.reshape(batch, seq, MIX)]
    if has_ctx:
        in_specs.append(pl.BlockSpec((None, None, kx.shape[2], MIX), lambda b, i: (b, l, 0, 0)))
        args.append(kx)
    in_specs.append(_pick(wvbd, l))
    args.append(wvbd)
    return pl.pallas_call(
        functools.partial(_mla_kernel, has_ctx=has_ctx),
        grid=(batch, nq),
        in_specs=in_specs,
        out_specs=pl.BlockSpec((tq, MIX), lambda b, i: (b * nq + i, 0)),
        out_shape=jax.ShapeDtypeStruct((batch * seq, MIX), BF),
        compiler_params=_params("parallel", "parallel"),
        name="mla_lat" if has_ctx else "mla_ctx",
    )(*args)


def _rot_cols(w, width):
    half = width // 2
    parts = w.reshape(w.shape[:-1] + (w.shape[-1] // width, 2, half))
    return jnp.stack([-parts[..., 1, :], parts[..., 0, :]], axis=-2).reshape(w.shape)


def _block_diag(blocks):
    n, r, c = blocks.shape[-3:]
    eye = jnp.eye(n, dtype=blocks.dtype)
    out = jnp.einsum("...hrc,hg->...hrgc", blocks, eye)
    return out.reshape(blocks.shape[:-3] + (n * r, n * c))


def _layer_weights(w_in, gla_w_gate, gla_b_gate, mla_w_q_b, mla_w_kv_b):
    nl = w_in.shape[0]
    w_in = w_in.astype(BF)
    offs = [0]
    for n in (256, 256, 256, 256, 16, 16, 256, 128, 128, 256, 256, 128, 32):
        offs.append(offs[-1] + n)
    seg = lambda k: w_in[..., offs[k]:offs[k + 1]]
    zeros = lambda n: jnp.zeros((nl, D_MODEL, n), w_in.dtype)
    sq = seg(6).reshape(nl, D_MODEL, SWA_Q_HEADS, HEAD_DIM)
    sq_tiles = []
    for hq in range(SWA_Q_HEADS):
        parts = [zeros(HEAD_DIM), zeros(HEAD_DIM)]
        parts[hq // 2] = sq[:, :, hq]
        sq_tiles += parts
    sq_w = jnp.concatenate(sq_tiles, axis=-1)
    kr4 = jnp.tile(seg(12), (1, 1, MLA_HEADS))
    cols = [seg(0), seg(1), seg(2), seg(3), seg(4), seg(5), zeros(LANES - 2 * GLA_GATE_RANK),
            sq_w, seg(7), seg(8), seg(9), seg(10), seg(11), kr4,
            _rot_cols(sq_w, HEAD_DIM // 2), _rot_cols(seg(7), HEAD_DIM // 2), _rot_cols(kr4, MLA_ROPE // 2)]
    w = jnp.concatenate(cols, axis=-1)

    wgate = jnp.zeros((nl, LANES, 2 * MIX), F32)
    wgate = wgate.at[:, :GLA_GATE_RANK, :MIX].set(gla_w_gate[:, 0])
    wgate = wgate.at[:, GLA_GATE_RANK:2 * GLA_GATE_RANK, MIX:].set(gla_w_gate[:, 1])
    bgate = gla_b_gate.reshape(nl, 1, 2 * MIX)

    wq = mla_w_q_b.reshape(nl, MLA_Q_RANK, MLA_HEADS, MLA_NOPE + MLA_ROPE)
    q_nope = wq[..., :MLA_NOPE].reshape(nl, MLA_Q_RANK, MLA_HEADS * MLA_NOPE)
    q_rope = wq[..., MLA_NOPE:].reshape(nl, MLA_Q_RANK, MLA_HEADS * MLA_ROPE)
    wqb = jnp.concatenate([q_nope, q_rope, _rot_cols(q_rope, MLA_ROPE // 2)], axis=-1).astype(BF)
    wkv = mla_w_kv_b.reshape(nl, MLA_KV_RANK, MLA_HEADS, MLA_NOPE + MLA_V)
    wkbd = _block_diag(jnp.transpose(wkv[..., :MLA_NOPE], (0, 2, 3, 1))).astype(BF)
    wvbd = _block_diag(jnp.transpose(wkv[..., MLA_NOPE:], (0, 2, 1, 3))).astype(BF)
    return (w, wgate.astype(BF), bgate, wqb, wkbd), wvbd


def _rope_tables(seq):
    t = jnp.arange(seq)
    rows = (t // GRID_W).astype(F32)[:, None]
    cols = (t % GRID_W).astype(F32)[:, None]

    def table(width):
        half = width // 2
        lane = jnp.arange(LANES)
        inv = ROPE_BASE ** (-(2 * (lane % half)).astype(F32) / width)
        ang = jnp.where(((lane // width) % 2 == 0)[None, :], rows, cols) * inv[None, :]
        return jnp.cos(ang), jnp.sin(ang)

    cs, ss = table(HEAD_DIM // 2)
    cm, sm = table(MLA_ROPE // 2)
    return cs, ss, cm, sm


def _dft_channel():
    c = jnp.arange(FNET_GROUP_CH, dtype=jnp.int32)
    ang = ((c[:, None] * c[None, :]) % FNET_GROUP_CH).astype(F32) * (2.0 * jnp.pi / FNET_GROUP_CH)
    cos_b = jnp.broadcast_to(jnp.cos(ang)[None], (FNET_GROUPS, FNET_GROUP_CH, FNET_GROUP_CH))
    sin_b = jnp.broadcast_to(jnp.sin(ang)[None], (FNET_GROUPS, FNET_GROUP_CH, FNET_GROUP_CH))
    return _block_diag(cos_b).astype(BF), (-_block_diag(sin_b)).astype(BF)


def _state_in(s):
    return _block_diag(jnp.swapaxes(s, -1, -2))


def _state_out(st):
    blocks = [st[:, h * GLA_DV:(h + 1) * GLA_DV, h * GLA_DK:(h + 1) * GLA_DK] for h in range(GLA_HEADS)]
    return jnp.swapaxes(jnp.stack(blocks, axis=1), 2, 3)


def _token_mix(x, l, shared, st0, st0_idx, cache, batch, seq, rows_per_batch, first_row, latent):
    mod, gn, wts, wvbd, gq, gkv, sinks, cdft, sdft, rope_t, fft_t = shared
    outs = _inproj_call(x, mod, gn, wts, gq, gkv, cdft, sdft, rope_t, l, batch, seq, rows_per_batch,
                        first_row, latent, tm=512)
    gqv, gkv_, gv, gr, la, sq, sk, sv, zc, zs, qc, qr, kcat = outs[:13]
    tb = min(seq, 512)
    o_f, st_f = _gla_call(gqv, gkv_, gv, la, st0, st0_idx[0], None, batch, seq, False, tb)
    o_gla, st_b = _gla_call(gqv, gkv_, gv, la, st0, st0_idx[1], o_f, batch, seq, True, tb)
    if latent:
        kx, vx, mx = cache
        o_swa = _swa_call(sinks, l, sq, sk, sv, kx, vx, l, batch, seq, True, tq=512)
        o_mla = _mla_call(qc, qr, kcat, mx, wvbd, l, batch, seq)
        new_ctx = None
    else:
        k4 = sk.reshape(batch, 1, seq, LANES)
        v4 = sv.reshape(batch, 1, seq, LANES)
        o_swa = _swa_call(sinks, l, sq, None, None, k4, v4, 0, batch, seq, False)
        o_mla = _mla_call(qc, qr, kcat, None, wvbd, l, batch, seq)
        new_ctx = (st_f, st_b, sk, sv, outs[13], outs[14])
    o_fft = _fft2_call(zc, zs, seq) if latent else _fft_call(fft_t, zc, zs, seq)
    return (o_gla, gr, o_swa, o_fft, o_mla), new_ctx


def kernel(x_prompt, x_sample, c, state_gla, cache_swa_k, cache_swa_v, cache_mla_ckv, cache_mla_krope,
           c_ctx, w_mod, b_mod, g_norm, w_ffn_gate, w_ffn_up, w_ffn_down, w_in, gla_w_gate, gla_b_gate,
           gla_g_out, swa_sink, mla_g_q, mla_g_kv, mla_w_q_b, mla_w_kv_b, w_out):
    nb, ns, _ = x_prompt.shape
    db, dsq, _ = x_sample.shape
    past = cache_swa_k.shape[2]

    cvec = jnp.zeros((MOD_ROWS, D_MODEL), F32).at[0].set(c_ctx).at[1:1 + db].set(c)
    mod = _mod_call(cvec, w_mod, b_mod).reshape(DEPTH, MOD_ROWS, N_MOD, 1, D_MODEL)
    gn = g_norm.reshape(DEPTH, 6, 1, D_MODEL)

    wg, wu, wd, wo = (a.astype(BF) for a in (w_ffn_gate, w_ffn_up, w_ffn_down, w_out))
    wts, wvbd = _layer_weights(w_in, gla_w_gate, gla_b_gate, mla_w_q_b, mla_w_kv_b)
    gq = mla_g_q.reshape(DEPTH, 1, MLA_Q_RANK)
    gkv = mla_g_kv.reshape(DEPTH, 1, MLA_KV_RANK)
    gout = jnp.tile(gla_g_out, (1, GLA_HEADS)).reshape(DEPTH, 1, MIX)
    cdft, sdft = _dft_channel()
    shared = (mod, gn, wts, wvbd, gq, gkv, swa_sink, cdft, sdft)
    head = jnp.arange(MIX) // GLA_DV
    avg = jnp.where(head[:, None] == head[None, :], 1.0 / GLA_DV, 0.0).astype(BF)
    shared_ctx = shared + (None, _fft_tables(ns))
    shared_lat = shared + (_rope_tables(dsq), None)

    st_lat = _state_in(state_gla)
    st_zero = jnp.zeros((nb, 1, 1, MIX, MIX), F32)
    kx = cache_swa_k.reshape(db, DEPTH, past, LANES)
    vx = cache_swa_v.reshape(db, DEPTH, past, LANES)
    mx = jnp.concatenate([cache_mla_ckv, jnp.tile(cache_mla_krope, (1, 1, 1, MLA_HEADS))], axis=-1).astype(BF)

    xc = x_prompt.reshape(nb * ns, D_MODEL)
    xl = x_sample.reshape(db * dsq, D_MODEL)
    st_gla, st_k, st_v, st_ckv, st_kr = [], [], [], [], []
    for l in range(DEPTH):
        xc = _ffn_call(xc, mod, gn, wg, wu, wd, l, rows_per_batch=nb * ns, first_row=0)
        mixed, new_ctx = _token_mix(xc, l, shared_ctx, st_zero, ((0, 0), (0, 0)), None, nb, ns, nb * ns, 0, False)
        xc = _outffn_call(*mixed, xc, mod, gn, gout, avg, wo, wg, wu, wd, l, seq=ns, rows_per_batch=nb * ns,
                          first_row=0, tm=512)
        st_f, st_b, k_c, v_c, ckv_c, kr_c = new_ctx
        st_gla.append(jnp.stack([_state_out(st_f), _state_out(st_b)], axis=1))
        st_k.append(k_c.reshape(nb, ns, SWA_KV_HEADS, HEAD_DIM))
        st_v.append(v_c.reshape(nb, ns, SWA_KV_HEADS, HEAD_DIM))
        st_ckv.append(ckv_c.reshape(nb, ns, MLA_KV_RANK))
        st_kr.append(kr_c.reshape(nb, ns, MLA_ROPE))

        xl = _ffn_call(xl, mod, gn, wg, wu, wd, l, rows_per_batch=dsq, first_row=1)
        mixed, _ = _token_mix(xl, l, shared_lat, st_lat, ((l, 0), (l, 1)), (kx, vx, mx), db, dsq, dsq, 1, True)
        xl = _outffn_call(*mixed, xl, mod, gn, gout, avg, wo, wg, wu, wd, l, seq=dsq, rows_per_batch=dsq,
                          first_row=1, tm=512)

    return (xc.reshape(nb, ns, D_MODEL), xl.reshape(db, dsq, D_MODEL), jnp.stack(st_gla, axis=1),
            jnp.stack(st_k, axis=1), jnp.stack(st_v, axis=1), jnp.stack(st_ckv, axis=1),
            jnp.stack(st_kr, axis=1))
```

```python
import functools

import jax
import jax.numpy as jnp
from jax import lax
from jax.experimental import pallas as pl
from jax.experimental.pallas import tpu as pltpu

D_MODEL = 1024
DEPTH = 4
GRID_W = 64
HEAD_DIM = 64
GLA_HEADS = 4
GLA_DK = 64
GLA_DV = 64
GLA_GATE_RANK = 16
GLA_TAU = 16.0
GLA_CHUNK = 64
SWA_Q_HEADS = 4
SWA_KV_HEADS = 2
SWA_BLOCK = 128
FNET_GROUPS = 4
FNET_GROUP_CH = 64
MLA_HEADS = 4
MLA_Q_RANK = 256
MLA_KV_RANK = 128
MLA_NOPE = 64
MLA_ROPE = 32
MLA_V = 64
D_FF = 2816
FFN_RES = 0.5
N_MOD = 9
ROPE_BASE = 10000.0
EPS = 1e-6
NEG_INF = -1e30

MIX = 256
LANES = 128
MXU_DEPTH = 256
FF_CHUNK = 256
MLA_KEY_TILE = 2048
GLA_GROUP = 4
LOG2E = 1.4426950408889634
MOD_ROWS = 8
VMEM_LIMIT = 56 * 1024 * 1024

BF = jnp.bfloat16
F32 = jnp.float32

C_GQ, C_GK, C_GV, C_GR = 0, 256, 512, 768
C_ASK, C_SVKVA, C_SQ, C_ZF, C_QA, C_KR = 1024, 1280, 1536, 2048, 2304, 2560
C_SKR, C_SQR, C_KRR = 2688, 2816, 3328
NC_CTX, NC_LAT = 2688, 3456


def _dot(a, b):
    return jnp.dot(a, b, preferred_element_type=F32)


def _dot_nt(a, b):
    return lax.dot_general(a, b, (((1,), (1,)), ((), ())), preferred_element_type=F32)


def _dot_tn(a, b):
    return lax.dot_general(a, b, (((0,), (0,)), ((), ())), preferred_element_type=F32)


def _rms(x, g):
    return x * lax.rsqrt(jnp.mean(x * x, axis=-1, keepdims=True) + EPS) * g


def _silu(x):
    return x * jax.nn.sigmoid(x)


def _params(*sem):
    return pltpu.CompilerParams(dimension_semantics=sem, vmem_limit_bytes=VMEM_LIMIT)


def _pick(arr, *lead, block=None):
    tail = tuple(arr.shape[len(lead):]) if block is None else tuple(block)
    zeros = (0,) * len(tail)
    return pl.BlockSpec((None,) * len(lead) + tail, lambda *_: tuple(lead) + zeros,
                        pipeline_mode=pl.Buffered(1))


def _mod_kernel(c_ref, w_ref, b_ref, o_ref):
    s = _silu(c_ref[...]).astype(BF)
    o_ref[...] = _dot(s, w_ref[...].astype(BF)) + b_ref[...]


def _mod_call(cvec, w_mod, b_mod):
    nl = w_mod.shape[0]
    tn = D_MODEL
    return pl.pallas_call(
        _mod_kernel,
        grid=(nl, N_MOD * D_MODEL // tn),
        in_specs=[
            pl.BlockSpec((MOD_ROWS, D_MODEL), lambda l, j: (0, 0)),
            pl.BlockSpec((None, D_MODEL, tn), lambda l, j: (l, 0, j)),
            pl.BlockSpec((None, 1, tn), lambda l, j: (l, 0, j)),
        ],
        out_specs=pl.BlockSpec((None, MOD_ROWS, tn), lambda l, j: (l, 0, j)),
        out_shape=jax.ShapeDtypeStruct((nl, MOD_ROWS, N_MOD * D_MODEL), F32),
        compiler_params=_params("parallel", "parallel"),
        name="mod",
    )(cvec, w_mod, b_mod.reshape(nl, 1, N_MOD * D_MODEL))


def _mod_spec(l, sub, tm, rows_per_batch, first_row):
    per = rows_per_batch // tm
    return pl.BlockSpec((None, None, 3, 1, D_MODEL), lambda i: (l, first_row + i // per, sub, 0, 0))


def _fft_layout_spec(tm, seq):
    per = max(seq // tm, 1)
    nbat = max(tm // seq, 1)
    return pl.BlockSpec((tm // nbat, nbat * MIX), lambda i: (i % per, i // per))


def _gn_spec(l, first, n):
    return pl.BlockSpec((None, n, 1, D_MODEL), lambda i: (l, first // n, 0, 0))


def _ffn_body(x, mod_ref, g_pre, g_post, wg_ref, wu_ref, wd_ref):
    sh, sc, gt = mod_ref[0], mod_ref[1], mod_ref[2]
    hb = (_rms(x, g_pre) * (1.0 + sc) + sh).astype(BF)
    acc = None
    for j in range(D_FF // FF_CHUNK):
        sl = slice(j * FF_CHUNK, (j + 1) * FF_CHUNK)
        g = _dot(hb, wg_ref[:, sl])
        u = _dot(hb, wu_ref[:, sl])
        d = _dot((_silu(g) * u).astype(BF), wd_ref[sl, :])
        acc = d if acc is None else acc + d
    return x + FFN_RES * gt * _rms(acc, g_post)


def _ffn_kernel(x_ref, mod_ref, gn_ref, wg_ref, wu_ref, wd_ref, o_ref):
    o_ref[...] = _ffn_body(x_ref[...], mod_ref, gn_ref[0], gn_ref[1], wg_ref, wu_ref, wd_ref)


def _ffn_call(x, mod, gn, wg, wu, wd, l, rows_per_batch, first_row, tm=512):
    r = x.shape[0]
    return pl.pallas_call(
        _ffn_kernel,
        grid=(r // tm,),
        in_specs=[
            pl.BlockSpec((tm, D_MODEL), lambda i: (i, 0)),
            _mod_spec(l, 0, tm, rows_per_batch, first_row),
            _gn_spec(l, 0, 2),
            _pick(wg, l, 0), _pick(wu, l, 0), _pick(wd, l, 0),
        ],
        out_specs=pl.BlockSpec((tm, D_MODEL), lambda i: (i, 0)),
        out_shape=jax.ShapeDtypeStruct(x.shape, F32),
        compiler_params=_params("parallel"),
        name="ffn",
    )(x, mod, gn, wg, wu, wd)


def _outffn_kernel(og_ref, gr_ref, os_ref, of_ref, om_ref, x_ref, mod2_ref, mod3_ref, gn_ref, gout_ref,
                   avg_ref, wo_ref, wg_ref, wu_ref, wd_ref, o_ref):
    og = og_ref[...]
    ms = _dot((og * og).astype(BF), avg_ref[...])
    og = (og * lax.rsqrt(ms + EPS) * gout_ref[...] * gr_ref[...]).astype(BF)
    nbat = of_ref.shape[1] // MIX
    of = jnp.concatenate([of_ref[:, j * MIX:(j + 1) * MIX] for j in range(nbat)], axis=0).astype(BF)
    mix = jnp.concatenate([og, os_ref[...], of, om_ref[...]], axis=1)
    x = x_ref[...] + mod2_ref[2] * _rms(_dot(mix, wo_ref[...]), gn_ref[0])
    o_ref[...] = _ffn_body(x, mod3_ref, gn_ref[1], gn_ref[2], wg_ref, wu_ref, wd_ref)


def _outffn_call(og, gr, osw, of, om, x, mod, gn, gout, avg, wo, wg, wu, wd, l, seq, rows_per_batch, first_row, tm):
    r = x.shape[0]
    mspec = pl.BlockSpec((tm, MIX), lambda i: (i, 0))
    return pl.pallas_call(
        _outffn_kernel,
        grid=(r // tm,),
        in_specs=[
            mspec, mspec, mspec,
            _fft_layout_spec(tm, seq),
            mspec,
            pl.BlockSpec((tm, D_MODEL), lambda i: (i, 0)),
            _mod_spec(l, 1, tm, rows_per_batch, first_row),
            _mod_spec(l, 2, tm, rows_per_batch, first_row),
            _gn_spec(l, 3, 3),
            _pick(gout, l), _pick(avg),
            _pick(wo, l), _pick(wg, l, 1), _pick(wu, l, 1), _pick(wd, l, 1),
        ],
        out_specs=pl.BlockSpec((tm, D_MODEL), lambda i: (i, 0)),
        out_shape=jax.ShapeDtypeStruct(x.shape, F32),
        compiler_params=_params("parallel"),
        name="outffn",
    )(og, gr, osw, of, om, x, mod, mod, gn, gout, avg, wo, wg, wu, wd)


def _log_sigmoid(x):
    return -(jnp.maximum(-x, 0.0) + jnp.log(1.0 + jnp.exp(-jnp.abs(x))))


def _inproj_kernel(*refs, latent, nbat):
    (x_ref, mod_ref, gn_ref, w_ref, wgate_ref, bgate_ref, gq_ref, gkv_ref, wqb_ref, wkbd_ref,
     cdft_ref, sdft_ref) = refs[:12]
    pos = 12
    if latent:
        cs_ref, ss_ref, cm_ref, sm_ref = refs[pos:pos + 4]
        pos += 4
    (gq_o, gk_o, gv_o, gr_o, la_o, sq_o, sk_o, sv_o, zc_o, zs_o, qc_o, qr_o, kcat_o) = refs[pos:pos + 13]
    pos += 13
    if not latent:
        ckv_o, kr_o = refs[pos:pos + 2]

    x = x_ref[...]
    hb = (_rms(x, gn_ref[0]) * (1.0 + mod_ref[1]) + mod_ref[0]).astype(BF)

    def col(off, n):
        return _dot(hb, w_ref[:, off:off + n])

    qa = col(C_QA, MLA_Q_RANK)
    ask = col(C_ASK, 2 * LANES)
    a_in = ask[:, :LANES].astype(BF)
    zf = col(C_ZF, MIX).astype(BF)

    gq_o[...] = col(C_GQ, MIX) * (GLA_DK ** -0.5)
    gk_o[...] = col(C_GK, MIX)

    qm = _dot(_rms(qa, gq_ref[...]).astype(BF), wqb_ref[...])
    logit = _dot(a_in, wgate_ref[...]) + bgate_ref[...]
    la_o[...] = _log_sigmoid(logit) * (1.0 / GLA_TAU)

    zc = _dot(zf, cdft_ref[...]).astype(zc_o.dtype)
    zs = _dot(zf, sdft_ref[...]).astype(zs_o.dtype)
    rows_b = zf.shape[0] // nbat
    for j in range(nbat):
        zc_o[:, j * MIX:(j + 1) * MIX] = zc[j * rows_b:(j + 1) * rows_b]
        zs_o[:, j * MIX:(j + 1) * MIX] = zs[j * rows_b:(j + 1) * rows_b]

    gv_o[...] = col(C_GV, MIX).astype(BF)
    gr_o[...] = _silu(col(C_GR, MIX))

    scale = (MLA_NOPE + MLA_ROPE) ** -0.5 * LOG2E
    qc_o[...] = (_dot(qm[:, :MIX].astype(BF), wkbd_ref[...]) * scale).astype(BF)
    q_rope = qm[:, MIX:MIX + LANES]
    if latent:
        cm, sm = cm_ref[...], sm_ref[...]
        q_rope = q_rope * cm + qm[:, MIX + LANES:MIX + 2 * LANES] * sm
    qr_o[...] = (q_rope * scale).astype(BF)

    sq = col(C_SQ, 2 * MIX)
    sk = ask[:, LANES:]
    svk = col(C_SVKVA, 2 * LANES)
    sv = svk[:, :LANES]
    if latent:
        krp = col(C_KR, 2 * LANES)
        kr = krp[:, :LANES]
        cs, ss = cs_ref[...], ss_ref[...]
        cs4 = jnp.concatenate([cs] * 4, axis=1)
        ss4 = jnp.concatenate([ss] * 4, axis=1)
        sq = sq * cs4 + col(C_SQR, 2 * MIX) * ss4
        sk = sk * cs + krp[:, LANES:] * ss
    else:
        kr = col(C_KR, LANES)
    sq_o[...] = (sq * (HEAD_DIM ** -0.5 * LOG2E)).astype(BF)
    sk_o[...] = sk.astype(sk_o.dtype)
    sv_o[...] = sv.astype(sv_o.dtype)

    ckv = _rms(svk[:, LANES:], gkv_ref[...])
    if not latent:
        ckv_o[...] = ckv
        kr_o[...] = kr[:, :MLA_ROPE]
    else:
        kr = kr * cm + col(C_KRR, LANES) * sm
    kcat_o[:, :LANES] = ckv.astype(BF)
    kcat_o[:, LANES:] = kr.astype(BF)


def _inproj_call(x, mod, gn, wts, gq, gkv, cdft, sdft, tables, l, batch, seq, rows_per_batch, first_row,
                 latent, tm):
    w, wgate, bgate, wqb, wkbd = wts
    r = x.shape[0]
    per = max(seq // tm, 1)
    nbat = max(tm // seq, 1)
    row = lambda n: pl.BlockSpec((tm, n), lambda i: (i, 0))
    ncol = NC_LAT if latent else NC_CTX
    nq = wqb.shape[-1] if latent else MIX + LANES
    in_specs = [
        row(D_MODEL),
        _mod_spec(l, 1, tm, rows_per_batch, first_row),
        _gn_spec(l, 2, 1),
        _pick(w, l, block=(D_MODEL, ncol)), _pick(wgate, l), _pick(bgate, l), _pick(gq, l), _pick(gkv, l),
        _pick(wqb, l, block=(MLA_Q_RANK, nq)), _pick(wkbd, l), _pick(cdft), _pick(sdft),
    ]
    args = [x, mod, gn, w, wgate, bgate, gq, gkv, wqb, wkbd, cdft, sdft]
    if latent:
        in_specs += [pl.BlockSpec((tm, LANES), lambda i: (i % per, 0))] * 4
        args += list(tables)
    kvdt = BF if latent else F32
    zdt = F32 if latent else BF
    fft_spec = _fft_layout_spec(tm, seq)
    out_specs = [row(MIX), row(MIX), row(MIX), row(MIX), row(2 * MIX), row(2 * MIX), row(LANES), row(LANES),
                 fft_spec, fft_spec, row(2 * MIX), row(LANES), row(MIX)]
    sd = jax.ShapeDtypeStruct
    out_shape = [sd((r, MIX), F32), sd((r, MIX), F32), sd((r, MIX), BF), sd((r, MIX), F32),
                 sd((r, 2 * MIX), F32), sd((r, 2 * MIX), BF), sd((r, LANES), kvdt), sd((r, LANES), kvdt),
                 sd((seq, batch * MIX), zdt), sd((seq, batch * MIX), zdt),
                 sd((r, 2 * MIX), BF), sd((r, LANES), BF), sd((r, MIX), BF)]
    if not latent:
        out_specs += [row(MLA_KV_RANK), row(MLA_ROPE)]
        out_shape += [sd((r, MLA_KV_RANK), F32), sd((r, MLA_ROPE), F32)]
    return pl.pallas_call(
        functools.partial(_inproj_kernel, latent=latent, nbat=nbat),
        grid=(r // tm,),
        in_specs=in_specs,
        out_specs=out_specs,
        out_shape=out_shape,
        compiler_params=_params("parallel"),
        name="inproj_lat" if latent else "inproj_ctx",
    )(*args)


def _chunk_cumsum(x, row, reverse):
    n = x.shape[0]
    k = 1
    while k < n:
        if reverse:
            x = x + jnp.where(row < n - k, pltpu.roll(x, n - k, 0), 0.0)
        else:
            x = x + jnp.where(row >= k, pltpu.roll(x, k, 0), 0.0)
        k *= 2
    return x


def _gla_kernel(q_ref, k_ref, v_ref, la_ref, s0_ref, *rest, reverse, final, nchunk, group):
    if final:
        of_ref, o_ref, sfin_ref, st_ref = rest
    else:
        o_ref, sfin_ref, st_ref = rest
    i = pl.program_id(1)
    c = GLA_CHUNK

    @pl.when(i == 0)
    def _():
        st_ref[...] = s0_ref[...]

    row = lax.broadcasted_iota(jnp.int32, (c, MIX), 0)
    lane = lax.broadcasted_iota(jnp.int32, (c, MIX), 1) % c
    keep = (lane >= row) if reverse else (lane <= row)
    lane1 = lax.broadcasted_iota(jnp.int32, (1, MIX), 1) // c
    hmask = [(lane1 == h).astype(F32) for h in range(GLA_HEADS)]
    hmask_b = [m.astype(BF) for m in hmask]
    bdiag = (lax.broadcasted_iota(jnp.int32, (MIX, MIX), 0) // c
             == lax.broadcasted_iota(jnp.int32, (MIX, MIX), 1) // c)

    def chunk(t, carry):
        cc = (nchunk - 1 - t) if reverse else t
        rs = pl.ds(pl.multiple_of(cc * c, c), c)
        def front(g):
            cum = _chunk_cumsum(la_ref[g, rs, :], row, reverse)
            tot = cum[0:1] if reverse else cum[c - 1:c]
            q, k, v = q_ref[g, rs, :], k_ref[g, rs, :], v_ref[g, rs, :]
            qd = (q * jnp.exp(cum)).astype(BF)
            ki = k * jnp.exp(-cum)
            kd = (k * jnp.exp(tot - cum)).astype(BF)
            kstack = jnp.concatenate([(ki * m).astype(BF) for m in hmask], axis=0)
            vstack = jnp.concatenate([v * m for m in hmask_b], axis=0)
            return qd, vstack, _dot_nt(qd, kstack), _dot_tn(v, kd), jnp.exp(tot)

        def back(g, qd, vstack, scores, upd, decay):
            att = jnp.where(keep, scores, 0.0).astype(BF)
            st = st_ref[g]
            o = _dot(att, vstack) + _dot_nt(qd, st.astype(BF))
            st_ref[g] = st * decay + jnp.where(bdiag, upd, 0.0)
            if final:
                o = o + of_ref[g, rs, :]
            o_ref[g, rs, :] = o

        pending = front(0)
        for g in range(group):
            cur = pending
            if g + 1 < group:
                pending = front(g + 1)
            back(g, *cur)
        return carry

    lax.fori_loop(0, nchunk, chunk, 0, unroll=2)

    @pl.when(i == pl.num_programs(1) - 1)
    def _():
        sfin_ref[...] = st_ref[...]


def _gla_call(q, k, v, la, s0, s0_idx, extra, batch, seq, reverse, tb, group=GLA_GROUP):
    nblk = seq // tb
    final = extra is not None
    pos = (lambda i: nblk - 1 - i) if reverse else (lambda i: i)
    blk = pl.BlockSpec((group, tb, MIX), lambda b, i: (b, pos(i), 0))
    lblk = pl.BlockSpec((group, tb, MIX), lambda b, i: (b, pos(i), 1 if reverse else 0))
    sblk = pl.BlockSpec((group, MIX, MIX), lambda b, i: (b, 0, 0))
    s0blk = pl.BlockSpec((group, None, None, MIX, MIX), lambda b, i: (b,) + tuple(s0_idx) + (0, 0))
    r3 = lambda a: a.reshape(batch, seq, a.shape[-1])
    in_specs = [blk, blk, blk, lblk, s0blk]
    args = [r3(q), r3(k), r3(v), r3(la), s0]
    if final:
        in_specs.append(blk)
        args.append(r3(extra))
    o, st = pl.pallas_call(
        functools.partial(_gla_kernel, reverse=reverse, final=final, nchunk=tb // GLA_CHUNK, group=group),
        grid=(batch // group, nblk),
        in_specs=in_specs,
        out_specs=[blk, sblk],
        out_shape=[jax.ShapeDtypeStruct((batch, seq, MIX), F32),
                   jax.ShapeDtypeStruct((batch, MIX, MIX), F32)],
        scratch_shapes=[pltpu.VMEM((group, MIX, MIX), F32)],
        compiler_params=_params("parallel", "arbitrary"),
        name="gla_bwd" if reverse else "gla_fwd",
    )(*args)
    return o.reshape(batch * seq, MIX), st


def _swa_kernel(*refs, local, layer):
    sink_ref, q_ref = refs[:2]
    pos = 2
    if local:
        kp_ref, kc_ref, kn_ref, vp_ref, vc_ref, vn_ref = refs[pos:pos + 6]
        pos += 6
    kx_ref, vx_ref, o_ref = refs[pos:pos + 3]
    i = pl.program_id(1)
    last = pl.num_programs(1) - 1
    blk = SWA_BLOCK
    tq = q_ref.shape[0]
    cx = kx_ref.shape[0]
    kx = kx_ref[...].astype(BF)
    vx = vx_ref[...].astype(BF)
    if local:
        nsub = tq // blk
        kblocks = [kp_ref[...]] + [kc_ref[j * blk:(j + 1) * blk, :] for j in range(nsub)] + [kn_ref[...]]
        vblocks = [vp_ref[...]] + [vc_ref[j * blk:(j + 1) * blk, :] for j in range(nsub)] + [vn_ref[...]]
        row = lax.broadcasted_iota(jnp.int32, (blk, blk), 0)
        col = lax.broadcasted_iota(jnp.int32, (blk, blk), 1)
        full = jnp.full((blk, blk), True)
        fullx = jnp.full((blk, cx), True)
    else:
        nsub = 1
    rq = tq // nsub
    rows = lax.broadcasted_iota(jnp.int32, (2 * rq, 1), 0)
    low = lax.broadcasted_iota(jnp.int32, (rq, LANES), 1) < HEAD_DIM
    units = []
    for j in range(nsub):
        if local:
            kcat = jnp.concatenate([kx, kblocks[j], kblocks[j + 1], kblocks[j + 2]], axis=0)
            vcat = jnp.concatenate([vx, vblocks[j], vblocks[j + 1], vblocks[j + 2]], axis=0)
            prev_ok = (col >= row) if j > 0 else jnp.logical_and(col >= row, i > 0)
            next_ok = (col <= row) if j < nsub - 1 else jnp.logical_and(col <= row, i < last)
            keep = jnp.concatenate([fullx, prev_ok, full, next_ok], axis=1)
            keep = jnp.concatenate([keep, keep], axis=0)
        else:
            kcat, vcat, keep = kx, vx, None
        for g in range(SWA_KV_HEADS):
            units.append((j, g, kcat, vcat, keep))

    def scores(u):
        j, g, kcat, _, _ = units[u]
        rs = slice(j * rq, (j + 1) * rq)
        q2 = jnp.concatenate([q_ref[rs, (2 * g) * LANES:(2 * g + 1) * LANES],
                              q_ref[rs, (2 * g + 1) * LANES:(2 * g + 2) * LANES]], axis=0)
        return _dot_nt(q2, kcat)

    tiles = {}
    s_next = scores(0)
    for u, (j, g, _, vcat, keep) in enumerate(units):
        s = s_next
        if u + 1 < len(units):
            s_next = scores(u + 1)
        sink = jnp.where(rows < rq, sink_ref[layer, 2 * g], sink_ref[layer, 2 * g + 1]) * LOG2E
        if local:
            s = jnp.where(keep, s, NEG_INF)
        m = jnp.maximum(jnp.max(s, axis=-1, keepdims=True), sink)
        e = jnp.exp2(s - m)
        den = jnp.sum(e, axis=-1, keepdims=True) + jnp.exp2(sink - m)
        o2 = _dot(e.astype(BF), vcat) / den
        oa, ob = o2[:rq], o2[rq:]
        if g == 0:
            tiles[(j, g)] = jnp.where(low, oa, pltpu.roll(ob, HEAD_DIM, 1))
        else:
            tiles[(j, g)] = jnp.where(low, pltpu.roll(oa, HEAD_DIM, 1), ob)
            rs = slice(j * rq, (j + 1) * rq)
            o_ref[rs, :LANES] = tiles[(j, 0)].astype(BF)
            o_ref[rs, LANES:] = tiles[(j, 1)].astype(BF)


def _swa_call(sinks, layer, q, k, v, kx, vx, x_layer, batch, seq, local, tq=256):
    nt = seq // tq
    per = tq // SWA_BLOCK
    nb = seq // SWA_BLOCK
    cx = kx.shape[2]
    qspec = pl.BlockSpec((tq, 2 * MIX), lambda b, i: (b * nt + i, 0))
    in_specs = [pl.BlockSpec(memory_space=pltpu.SMEM), qspec]
    args = [sinks, q]
    if local:
        prv = pl.BlockSpec((SWA_BLOCK, LANES), lambda b, i: (b * nb + jnp.maximum(i * per - 1, 0), 0))
        cur = pl.BlockSpec((tq, LANES), lambda b, i: (b * nt + i, 0))
        nxt = pl.BlockSpec((SWA_BLOCK, LANES), lambda b, i: (b * nb + jnp.minimum((i + 1) * per, nb - 1), 0))
        in_specs += [prv, cur, nxt, prv, cur, nxt]
        args += [k, k, k, v, v, v]
    xspec = pl.BlockSpec((None, None, cx, LANES), lambda b, i: (b, x_layer, 0, 0))
    in_specs += [xspec, xspec]
    args += [kx, vx]
    return pl.pallas_call(
        functools.partial(_swa_kernel, local=local, layer=layer),
        grid=(batch, nt),
        in_specs=in_specs,
        out_specs=pl.BlockSpec((tq, MIX), lambda b, i: (b * nt + i, 0)),
        out_shape=jax.ShapeDtypeStruct((batch * seq, MIX), BF),
        compiler_params=_params("parallel", "parallel"),
        name="swa_lat" if local else "swa_ctx",
    )(*args)


def _fft_kernel(xc_ref, xs_ref, yc_ref, ys_ref, zc_ref, zs_ref, o_ref, *, nj, scale):
    yc, ys = yc_ref[...], ys_ref[...]
    xc, xs = xc_ref[...], xs_ref[...]
    acc = None
    for jp in range(nj // 2):
        cos_t, sin_t = [], []
        for j in (2 * jp, 2 * jp + 1):
            a, b = xc[:, j:j + 1], xs[:, j:j + 1]
            cos_t.append(a * yc - b * ys)
            sin_t.append(b * yc + a * ys)
        cos_t = jnp.concatenate(cos_t, axis=1).astype(BF)
        sin_t = jnp.concatenate(sin_t, axis=1).astype(BF)
        rows = slice(jp * MXU_DEPTH, (jp + 1) * MXU_DEPTH)
        d = _dot(cos_t, zc_ref[rows, :]) + _dot(sin_t, zs_ref[rows, :])
        acc = d if acc is None else acc + d
    o_ref[...] = (acc * scale).astype(BF)


def _fft_tables(seq):
    sp = jnp.arange(seq, dtype=jnp.int32)[:, None]
    w = 2.0 * jnp.pi / seq
    ax = ((sp * (jnp.arange(seq // LANES, dtype=jnp.int32) * LANES)[None, :]) % seq).astype(F32) * w
    ay = ((sp * jnp.arange(LANES, dtype=jnp.int32)[None, :]) % seq).astype(F32) * w
    return jnp.cos(ax), jnp.sin(ax), jnp.cos(ay), jnp.sin(ay)


def _fft_call(tables, zc, zs, seq, tm=256):
    xc, xs, yc, ys = tables
    nj = seq // LANES
    width = zc.shape[1]
    return pl.pallas_call(
        functools.partial(_fft_kernel, nj=nj, scale=(seq * FNET_GROUP_CH) ** -0.5),
        grid=(seq // tm,),
        in_specs=[
            pl.BlockSpec((tm, nj), lambda i: (i, 0)), pl.BlockSpec((tm, nj), lambda i: (i, 0)),
            pl.BlockSpec((tm, LANES), lambda i: (i, 0)), pl.BlockSpec((tm, LANES), lambda i: (i, 0)),
            _pick(zc), _pick(zs),
        ],
        out_specs=pl.BlockSpec((tm, width), lambda i: (i, 0)),
        out_shape=jax.ShapeDtypeStruct((seq, width), BF),
        compiler_params=_params("parallel"),
        name="fft",
    )(xc, xs, yc, ys, zc, zs)


def _fft2_kernel(zc_ref, zs_ref, m1_ref, m2_ref, tr_ref, ti_ref, o_ref, x_ref, *, nb1, scale):
    i = pl.program_id(1)
    n = m2_ref.shape[0]

    @pl.when(i < nb1)
    def _():
        m1 = m1_ref[...]
        for j in range(8):
            z = jnp.concatenate([zc_ref[:, j, :], zs_ref[:, j, :]], axis=0).astype(BF)
            x1 = _dot(m1, z)
            xr, xi = x1[:n], x1[n:]
            tr, ti = tr_ref[:, j:j + 1], ti_ref[:, j:j + 1]
            x2 = jnp.concatenate([xr * tr - xi * ti, xr * ti + xi * tr], axis=0)
            x_ref[i * 8 + j] = x2.reshape(2 * n // 8, 8, x2.shape[-1])

    @pl.when(i >= nb1)
    def _():
        m2 = m2_ref[...]
        blk = i - nb1
        for j in range(8):
            z = jnp.concatenate([x_ref[:, blk, j, :], x_ref[:, n // 8 + blk, j, :]], axis=0).astype(BF)
            o_ref[:, j, :] = _dot(m2, z) * scale


def _fft2_call(zc, zs, seq, wl=512):
    n = int(round(seq ** 0.5))
    assert seq == n * n and n % 8 == 0
    width = zc.shape[1]
    nb = n // 8
    k = jnp.arange(n, dtype=jnp.int32)
    ang = ((k[:, None] * k[None, :]) % n).astype(F32) * (2.0 * jnp.pi / n)
    c, sn = jnp.cos(ang), jnp.sin(ang)
    m1 = jnp.concatenate([jnp.concatenate([c, sn], axis=1), jnp.concatenate([-sn, c], axis=1)], axis=0).astype(BF)
    m2 = jnp.concatenate([c, sn], axis=1).astype(BF)
    tw = (k[:, None] * k[None, :]).astype(F32) * (2.0 * jnp.pi / seq)
    tr = jnp.cos(tw).reshape(n, nb, 8).transpose(1, 0, 2)
    ti = (-jnp.sin(tw)).reshape(n, nb, 8).transpose(1, 0, 2)
    zspec = pl.BlockSpec((n, 8, wl), lambda h, i: (0, jnp.minimum(i, nb - 1), h))
    tspec = pl.BlockSpec((None, n, 8), lambda h, i: (jnp.minimum(i, nb - 1), 0, 0))
    out = pl.pallas_call(
        functools.partial(_fft2_kernel, nb1=nb, scale=(seq * FNET_GROUP_CH) ** -0.5),
        grid=(width // wl, 2 * nb),
        in_specs=[zspec, zspec, _pick(m1), _pick(m2), tspec, tspec],
        out_specs=pl.BlockSpec((n, 8, wl), lambda h, i: (0, jnp.maximum(i - nb, 0), h)),
        out_shape=jax.ShapeDtypeStruct((n, n, width), F32),
        scratch_shapes=[pltpu.VMEM((n, 2 * n // 8, 8, wl), F32)],
        compiler_params=_params("parallel", "arbitrary"),
        name="fft2",
    )(zc.reshape(n, n, width), zs.reshape(n, n, width), m1, m2, tr, ti)
    return out.reshape(seq, width)


def _mla_kernel(*refs, has_ctx):
    qc_ref, qr_ref, ks_ref = refs[:3]
    pos = 3
    if has_ctx:
        kx_ref = refs[pos]
        pos += 1
    wv_ref, o_ref = refs[pos:pos + 2]
    qr = qr_ref[...]
    lane = lax.broadcasted_iota(jnp.int32, qr.shape, 1) // MLA_ROPE
    tq = qr.shape[0]
    qs = []
    for h in range(MLA_HEADS):
        qrh = jnp.where(lane == h, qr, jnp.zeros_like(qr))
        qs.append(jnp.concatenate([qc_ref[:, h * LANES:(h + 1) * LANES], qrh], axis=1))
    qall = jnp.concatenate(qs, axis=0)
    tiles = [(kx_ref, 0, kx_ref.shape[0])] if has_ctx else []
    nk = ks_ref.shape[0]
    tk = min(nk, MLA_KEY_TILE)
    tiles += [(ks_ref, j * tk, tk) for j in range(nk // tk)]
    m = den = acc = None
    for ref, lo, n in tiles:
        kt = ref[lo:lo + n, :]
        s = _dot_nt(qall, kt)
        mt = jnp.max(s, axis=-1, keepdims=True)
        if m is None:
            m_new = mt
        else:
            m_new = jnp.maximum(m, mt)
            alpha = jnp.exp2(m - m_new)
        p = jnp.exp2(s - m_new)
        pv = _dot(p.astype(BF), kt[:, :LANES])
        ps = jnp.sum(p, axis=-1, keepdims=True)
        if m is None:
            den, acc = ps, pv
        else:
            den = alpha * den + ps
            acc = alpha * acc + pv
        m = m_new
    ot = acc / den
    ocat = jnp.concatenate([ot[h * tq:(h + 1) * tq] for h in range(MLA_HEADS)], axis=1).astype(BF)
    o_ref[...] = _dot(ocat, wv_ref[...]).astype(BF)


def _mla_call(qc, qr, kcat, kx, wvbd, l, batch, seq, tq=256):
    nq = seq // tq
    has_ctx = kx is not None
    in_specs = [pl.BlockSpec((tq, 2 * MIX), lambda b, i: (b * nq + i, 0)),
                pl.BlockSpec((tq, LANES), lambda b, i: (b * nq + i, 0)),
                pl.BlockSpec((None, seq, MIX), lambda b, i: (b, 0, 0))]
    args = [qc, qr, kcat.reshape(batch, seq, MIX)]
    if has_ctx:
        in_specs.append(pl.BlockSpec((None, None, kx.shape[2], MIX), lambda b, i: (b, l, 0, 0)))
        args.append(kx)
    in_specs.append(_pick(wvbd, l))
    args.append(wvbd)
    return pl.pallas_call(
        functools.partial(_mla_kernel, has_ctx=has_ctx),
        grid=(batch, nq),
        in_specs=in_specs,
        out_specs=pl.BlockSpec((tq, MIX), lambda b, i: (b * nq + i, 0)),
        out_shape=jax.ShapeDtypeStruct((batch * seq, MIX), BF),
        compiler_params=_params("parallel", "parallel"),
        name="mla_lat" if has_ctx else "mla_ctx",
    )(*args)


def _rot_cols(w, width):
    half = width // 2
    parts = w.reshape(w.shape[:-1] + (w.shape[-1] // width, 2, half))
    return jnp.stack([-parts[..., 1, :], parts[..., 0, :]], axis=-2).reshape(w.shape)


def _block_diag(blocks):
    n, r, c = blocks.shape[-3:]
    eye = jnp.eye(n, dtype=blocks.dtype)
    out = jnp.einsum("...hrc,hg->...hrgc", blocks, eye)
    return out.reshape(blocks.shape[:-3] + (n * r, n * c))


def _layer_weights(w_in, gla_w_gate, gla_b_gate, mla_w_q_b, mla_w_kv_b):
    nl = w_in.shape[0]
    w_in = w_in.astype(BF)
    offs = [0]
    for n in (256, 256, 256, 256, 16, 16, 256, 128, 128, 256, 256, 128, 32):
        offs.append(offs[-1] + n)
    seg = lambda k: w_in[..., offs[k]:offs[k + 1]]
    zeros = lambda n: jnp.zeros((nl, D_MODEL, n), w_in.dtype)
    sq = seg(6).reshape(nl, D_MODEL, SWA_Q_HEADS, HEAD_DIM)
    sq_tiles = []
    for hq in range(SWA_Q_HEADS):
        parts = [zeros(HEAD_DIM), zeros(HEAD_DIM)]
        parts[hq // 2] = sq[:, :, hq]
        sq_tiles += parts
    sq_w = jnp.concatenate(sq_tiles, axis=-1)
    kr4 = jnp.tile(seg(12), (1, 1, MLA_HEADS))
    cols = [seg(0), seg(1), seg(2), seg(3), seg(4), seg(5), zeros(LANES - 2 * GLA_GATE_RANK), seg(7),
            seg(8), seg(11), sq_w, seg(9), seg(10), kr4,
            _rot_cols(seg(7), HEAD_DIM // 2), _rot_cols(sq_w, HEAD_DIM // 2), _rot_cols(kr4, MLA_ROPE // 2)]
    w = jnp.concatenate(cols, axis=-1)

    wgate = jnp.zeros((nl, LANES, 2 * MIX), F32)
    wgate = wgate.at[:, :GLA_GATE_RANK, :MIX].set(gla_w_gate[:, 0])
    wgate = wgate.at[:, GLA_GATE_RANK:2 * GLA_GATE_RANK, MIX:].set(gla_w_gate[:, 1])
    bgate = gla_b_gate.reshape(nl, 1, 2 * MIX)

    wq = mla_w_q_b.reshape(nl, MLA_Q_RANK, MLA_HEADS, MLA_NOPE + MLA_ROPE)
    q_nope = wq[..., :MLA_NOPE].reshape(nl, MLA_Q_RANK, MLA_HEADS * MLA_NOPE)
    q_rope = wq[..., MLA_NOPE:].reshape(nl, MLA_Q_RANK, MLA_HEADS * MLA_ROPE)
    wqb = jnp.concatenate([q_nope, q_rope, _rot_cols(q_rope, MLA_ROPE // 2)], axis=-1).astype(BF)
    wkv = mla_w_kv_b.reshape(nl, MLA_KV_RANK, MLA_HEADS, MLA_NOPE + MLA_V)
    wkbd = _block_diag(jnp.transpose(wkv[..., :MLA_NOPE], (0, 2, 3, 1))).astype(BF)
    wvbd = _block_diag(jnp.transpose(wkv[..., MLA_NOPE:], (0, 2, 1, 3))).astype(BF)
    return (w, wgate.astype(BF), bgate, wqb, wkbd), wvbd


def _rope_tables(seq):
    t = jnp.arange(seq)
    rows = (t // GRID_W).astype(F32)[:, None]
    cols = (t % GRID_W).astype(F32)[:, None]

    def table(width):
        half = width // 2
        lane = jnp.arange(LANES)
        inv = ROPE_BASE ** (-(2 * (lane % half)).astype(F32) / width)
        ang = jnp.where(((lane // width) % 2 == 0)[None, :], rows, cols) * inv[None, :]
        return jnp.cos(ang), jnp.sin(ang)

    cs, ss = table(HEAD_DIM // 2)
    cm, sm = table(MLA_ROPE // 2)
    return cs, ss, cm, sm


def _dft_channel():
    c = jnp.arange(FNET_GROUP_CH, dtype=jnp.int32)
    ang = ((c[:, None] * c[None, :]) % FNET_GROUP_CH).astype(F32) * (2.0 * jnp.pi / FNET_GROUP_CH)
    cos_b = jnp.broadcast_to(jnp.cos(ang)[None], (FNET_GROUPS, FNET_GROUP_CH, FNET_GROUP_CH))
    sin_b = jnp.broadcast_to(jnp.sin(ang)[None], (FNET_GROUPS, FNET_GROUP_CH, FNET_GROUP_CH))
    return _block_diag(cos_b).astype(BF), (-_block_diag(sin_b)).astype(BF)


def _state_in(s):
    return _block_diag(jnp.swapaxes(s, -1, -2))


def _state_out(st):
    blocks = [st[:, h * GLA_DV:(h + 1) * GLA_DV, h * GLA_DK:(h + 1) * GLA_DK] for h in range(GLA_HEADS)]
    return jnp.swapaxes(jnp.stack(blocks, axis=1), 2, 3)


def _token_mix(x, l, shared, st0, st0_idx, cache, batch, seq, rows_per_batch, first_row, latent):
    mod, gn, wts, wvbd, gq, gkv, sinks, cdft, sdft, rope_t, fft_t = shared
    outs = _inproj_call(x, mod, gn, wts, gq, gkv, cdft, sdft, rope_t, l, batch, seq, rows_per_batch,
                        first_row, latent, tm=512)
    gqv, gkv_, gv, gr, la, sq, sk, sv, zc, zs, qc, qr, kcat = outs[:13]
    tb = min(seq, 512)
    o_f, st_f = _gla_call(gqv, gkv_, gv, la, st0, st0_idx[0], None, batch, seq, False, tb)
    o_gla, st_b = _gla_call(gqv, gkv_, gv, la, st0, st0_idx[1], o_f, batch, seq, True, tb)
    if latent:
        kx, vx, mx = cache
        o_swa = _swa_call(sinks, l, sq, sk, sv, kx, vx, l, batch, seq, True, tq=512)
        o_mla = _mla_call(qc, qr, kcat, mx, wvbd, l, batch, seq)
        new_ctx = None
    else:
        k4 = sk.reshape(batch, 1, seq, LANES)
        v4 = sv.reshape(batch, 1, seq, LANES)
        o_swa = _swa_call(sinks, l, sq, None, None, k4, v4, 0, batch, seq, False)
        o_mla = _mla_call(qc, qr, kcat, None, wvbd, l, batch, seq)
        new_ctx = (st_f, st_b, sk, sv, outs[13], outs[14])
    o_fft = _fft2_call(zc, zs, seq) if latent else _fft_call(fft_t, zc, zs, seq)
    return (o_gla, gr, o_swa, o_fft, o_mla), new_ctx


def kernel(x_prompt, x_sample, c, state_gla, cache_swa_k, cache_swa_v, cache_mla_ckv, cache_mla_krope,
           c_ctx, w_mod, b_mod, g_norm, w_ffn_gate, w_ffn_up, w_ffn_down, w_in, gla_w_gate, gla_b_gate,
           gla_g_out, swa_sink, mla_g_q, mla_g_kv, mla_w_q_b, mla_w_kv_b, w_out):
    nb, ns, _ = x_prompt.shape
    db, dsq, _ = x_sample.shape
    past = cache_swa_k.shape[2]

    cvec = jnp.zeros((MOD_ROWS, D_MODEL), F32).at[0].set(c_ctx).at[1:1 + db].set(c)
    mod = _mod_call(cvec, w_mod, b_mod).reshape(DEPTH, MOD_ROWS, N_MOD, 1, D_MODEL)
    gn = g_norm.reshape(DEPTH, 6, 1, D_MODEL)

    wg, wu, wd, wo = (a.astype(BF) for a in (w_ffn_gate, w_ffn_up, w_ffn_down, w_out))
    wts, wvbd = _layer_weights(w_in, gla_w_gate, gla_b_gate, mla_w_q_b, mla_w_kv_b)
    gq = mla_g_q.reshape(DEPTH, 1, MLA_Q_RANK)
    gkv = mla_g_kv.reshape(DEPTH, 1, MLA_KV_RANK)
    gout = jnp.tile(gla_g_out, (1, GLA_HEADS)).reshape(DEPTH, 1, MIX)
    cdft, sdft = _dft_channel()
    shared = (mod, gn, wts, wvbd, gq, gkv, swa_sink, cdft, sdft)
    head = jnp.arange(MIX) // GLA_DV
    avg = jnp.where(head[:, None] == head[None, :], 1.0 / GLA_DV, 0.0).astype(BF)
    shared_ctx = shared + (None, _fft_tables(ns))
    shared_lat = shared + (_rope_tables(dsq), None)

    st_lat = _state_in(state_gla)
    st_zero = jnp.zeros((nb, 1, 1, MIX, MIX), F32)
    kx = cache_swa_k.reshape(db, DEPTH, past, LANES)
    vx = cache_swa_v.reshape(db, DEPTH, past, LANES)
    mx = jnp.concatenate([cache_mla_ckv, jnp.tile(cache_mla_krope, (1, 1, 1, MLA_HEADS))], axis=-1).astype(BF)

    xc = x_prompt.reshape(nb * ns, D_MODEL)
    xl = x_sample.reshape(db * dsq, D_MODEL)
    st_gla, st_k, st_v, st_ckv, st_kr = [], [], [], [], []
    for l in range(DEPTH):
        xc = _ffn_call(xc, mod, gn, wg, wu, wd, l, rows_per_batch=nb * ns, first_row=0)
        mixed, new_ctx = _token_mix(xc, l, shared_ctx, st_zero, ((0, 0), (0, 0)), None, nb, ns, nb * ns, 0, False)
        xc = _outffn_call(*mixed, xc, mod, gn, gout, avg, wo, wg, wu, wd, l, seq=ns, rows_per_batch=nb * ns,
                          first_row=0, tm=512)
        st_f, st_b, k_c, v_c, ckv_c, kr_c = new_ctx
        st_gla.append(jnp.stack([_state_out(st_f), _state_out(st_b)], axis=1))
        st_k.append(k_c.reshape(nb, ns, SWA_KV_HEADS, HEAD_DIM))
        st_v.append(v_c.reshape(nb, ns, SWA_KV_HEADS, HEAD_DIM))
        st_ckv.append(ckv_c.reshape(nb, ns, MLA_KV_RANK))
        st_kr.append(kr_c.reshape(nb, ns, MLA_ROPE))

        xl = _ffn_call(xl, mod, gn, wg, wu, wd, l, rows_per_batch=dsq, first_row=1)
        mixed, _ = _token_mix(xl, l, shared_lat, st_lat, ((l, 0), (l, 1)), (kx, vx, mx), db, dsq, dsq, 1, True)
        xl = _outffn_call(*mixed, xl, mod, gn, gout, avg, wo, wg, wu, wd, l, seq=dsq, rows_per_batch=dsq,
                          first_row=1, tm=512)

    return (xc.reshape(nb, ns, D_MODEL), xl.reshape(db, dsq, D_MODEL), jnp.stack(st_gla, axis=1),
            jnp.stack(st_k, axis=1), jnp.stack(st_v, axis=1), jnp.stack(st_ckv, axis=1),
            jnp.stack(st_kr, axis=1))
```

```python
import functools

import jax
import jax.numpy as jnp
from jax import lax
from jax.experimental import pallas as pl
from jax.experimental.pallas import tpu as pltpu

D_MODEL = 1024
DEPTH = 4
GRID_W = 64
HEAD_DIM = 64
GLA_HEADS = 4
GLA_DK = 64
GLA_DV = 64
GLA_GATE_RANK = 16
GLA_TAU = 16.0
GLA_CHUNK = 64
SWA_Q_HEADS = 4
SWA_KV_HEADS = 2
SWA_BLOCK = 128
FNET_GROUPS = 4
FNET_GROUP_CH = 64
MLA_HEADS = 4
MLA_Q_RANK = 256
MLA_KV_RANK = 128
MLA_NOPE = 64
MLA_ROPE = 32
MLA_V = 64
D_FF = 2816
FFN_RES = 0.5
N_MOD = 9
ROPE_BASE = 10000.0
EPS = 1e-6
NEG_INF = -1e30

MIX = 256
LANES = 128
MXU_DEPTH = 256
FF_CHUNK = 256
MLA_KEY_TILE = 2048
MLA_SUB_TILE = 256
GLA_GROUP = 4
LOG2E = 1.4426950408889634
MOD_ROWS = 8
VMEM_LIMIT = 56 * 1024 * 1024

BF = jnp.bfloat16
F32 = jnp.float32

C_GQ, C_GK, C_GV, C_GR = 0, 256, 512, 768
C_ASK, C_SVKVA, C_SQ, C_ZF, C_QA, C_KR = 1024, 1280, 1536, 2048, 2304, 2560
C_SKR, C_SQR, C_KRR = 2688, 2816, 3328
NC_CTX, NC_LAT = 2688, 3456


def _dot(a, b):
    return jnp.dot(a, b, preferred_element_type=F32)


def _dot_nt(a, b):
    return lax.dot_general(a, b, (((1,), (1,)), ((), ())), preferred_element_type=F32)


def _dot_tn(a, b):
    return lax.dot_general(a, b, (((0,), (0,)), ((), ())), preferred_element_type=F32)


def _rms(x, g):
    return x * lax.rsqrt(jnp.mean(x * x, axis=-1, keepdims=True) + EPS) * g


def _silu(x):
    return x * jax.nn.sigmoid(x)


def _params(*sem):
    return pltpu.CompilerParams(dimension_semantics=sem, vmem_limit_bytes=VMEM_LIMIT)


def _pick(arr, *lead, block=None):
    tail = tuple(arr.shape[len(lead):]) if block is None else tuple(block)
    zeros = (0,) * len(tail)
    return pl.BlockSpec((None,) * len(lead) + tail, lambda *_: tuple(lead) + zeros,
                        pipeline_mode=pl.Buffered(1))


def _mod_kernel(c_ref, w_ref, b_ref, o_ref):
    s = _silu(c_ref[...]).astype(BF)
    o_ref[...] = _dot(s, w_ref[...].astype(BF)) + b_ref[...]


def _mod_call(cvec, w_mod, b_mod):
    nl = w_mod.shape[0]
    tn = D_MODEL
    return pl.pallas_call(
        _mod_kernel,
        grid=(nl, N_MOD * D_MODEL // tn),
        in_specs=[
            pl.BlockSpec((MOD_ROWS, D_MODEL), lambda l, j: (0, 0)),
            pl.BlockSpec((None, D_MODEL, tn), lambda l, j: (l, 0, j)),
            pl.BlockSpec((None, 1, tn), lambda l, j: (l, 0, j)),
        ],
        out_specs=pl.BlockSpec((None, MOD_ROWS, tn), lambda l, j: (l, 0, j)),
        out_shape=jax.ShapeDtypeStruct((nl, MOD_ROWS, N_MOD * D_MODEL), F32),
        compiler_params=_params("parallel", "parallel"),
        name="mod",
    )(cvec, w_mod, b_mod.reshape(nl, 1, N_MOD * D_MODEL))


def _mod_spec(l, sub, tm, rows_per_batch, first_row):
    per = rows_per_batch // tm
    return pl.BlockSpec((None, None, 3, 1, D_MODEL), lambda i: (l, first_row + i // per, sub, 0, 0))


def _fft_layout_spec(tm, seq):
    per = max(seq // tm, 1)
    nbat = max(tm // seq, 1)
    return pl.BlockSpec((tm // nbat, nbat * MIX), lambda i: (i % per, i // per))


def _gn_spec(l, first, n):
    return pl.BlockSpec((None, n, 1, D_MODEL), lambda i: (l, first // n, 0, 0))


def _ffn_body(x, mod_ref, g_pre, g_post, wg_ref, wu_ref, wd_ref):
    sh, sc, gt = mod_ref[0], mod_ref[1], mod_ref[2]
    hb = (_rms(x, g_pre) * (1.0 + sc) + sh).astype(BF)
    acc = None
    for j in range(D_FF // FF_CHUNK):
        sl = slice(j * FF_CHUNK, (j + 1) * FF_CHUNK)
        g = _dot(hb, wg_ref[:, sl])
        u = _dot(hb, wu_ref[:, sl])
        d = _dot((_silu(g) * u).astype(BF), wd_ref[sl, :])
        acc = d if acc is None else acc + d
    return x + FFN_RES * gt * _rms(acc, g_post)


def _ffn_kernel(x_ref, mod_ref, gn_ref, wg_ref, wu_ref, wd_ref, o_ref):
    o_ref[...] = _ffn_body(x_ref[...], mod_ref, gn_ref[0], gn_ref[1], wg_ref, wu_ref, wd_ref)


def _ffn_call(x, mod, gn, wg, wu, wd, l, rows_per_batch, first_row, tm=512):
    r = x.shape[0]
    return pl.pallas_call(
        _ffn_kernel,
        grid=(r // tm,),
        in_specs=[
            pl.BlockSpec((tm, D_MODEL), lambda i: (i, 0)),
            _mod_spec(l, 0, tm, rows_per_batch, first_row),
            _gn_spec(l, 0, 2),
            _pick(wg, l, 0), _pick(wu, l, 0), _pick(wd, l, 0),
        ],
        out_specs=pl.BlockSpec((tm, D_MODEL), lambda i: (i, 0)),
        out_shape=jax.ShapeDtypeStruct(x.shape, F32),
        compiler_params=_params("parallel"),
        name="ffn",
    )(x, mod, gn, wg, wu, wd)


def _outffn_kernel(og_ref, gr_ref, os_ref, of_ref, om_ref, x_ref, mod2_ref, mod3_ref, gn_ref, gout_ref,
                   avg_ref, wo_ref, wg_ref, wu_ref, wd_ref, o_ref):
    og = og_ref[...]
    ms = _dot((og * og).astype(BF), avg_ref[...])
    og = (og * lax.rsqrt(ms + EPS) * gout_ref[...] * gr_ref[...]).astype(BF)
    nbat = of_ref.shape[1] // MIX
    of = jnp.concatenate([of_ref[:, j * MIX:(j + 1) * MIX] for j in range(nbat)], axis=0).astype(BF)
    mix = jnp.concatenate([og, os_ref[...], of, om_ref[...]], axis=1)
    x = x_ref[...] + mod2_ref[2] * _rms(_dot(mix, wo_ref[...]), gn_ref[0])
    o_ref[...] = _ffn_body(x, mod3_ref, gn_ref[1], gn_ref[2], wg_ref, wu_ref, wd_ref)


def _outffn_call(og, gr, osw, of, om, x, mod, gn, gout, avg, wo, wg, wu, wd, l, seq, rows_per_batch, first_row, tm):
    r = x.shape[0]
    mspec = pl.BlockSpec((tm, MIX), lambda i: (i, 0))
    return pl.pallas_call(
        _outffn_kernel,
        grid=(r // tm,),
        in_specs=[
            mspec, mspec, mspec,
            _fft_layout_spec(tm, seq),
            mspec,
            pl.BlockSpec((tm, D_MODEL), lambda i: (i, 0)),
            _mod_spec(l, 1, tm, rows_per_batch, first_row),
            _mod_spec(l, 2, tm, rows_per_batch, first_row),
            _gn_spec(l, 3, 3),
            _pick(gout, l), _pick(avg),
            _pick(wo, l), _pick(wg, l, 1), _pick(wu, l, 1), _pick(wd, l, 1),
        ],
        out_specs=pl.BlockSpec((tm, D_MODEL), lambda i: (i, 0)),
        out_shape=jax.ShapeDtypeStruct(x.shape, F32),
        compiler_params=_params("parallel"),
        name="outffn",
    )(og, gr, osw, of, om, x, mod, mod, gn, gout, avg, wo, wg, wu, wd)


def _log_sigmoid(x):
    return -(jnp.maximum(-x, 0.0) + jnp.log(1.0 + jnp.exp(-jnp.abs(x))))


def _inproj_kernel(*refs, latent, nbat):
    (x_ref, mod_ref, gn_ref, w_ref, wgate_ref, bgate_ref, gq_ref, gkv_ref, wqb_ref, wkbd_ref,
     cdft_ref, sdft_ref) = refs[:12]
    pos = 12
    if latent:
        cs_ref, ss_ref, cm_ref, sm_ref = refs[pos:pos + 4]
        pos += 4
    (gq_o, gk_o, gv_o, gr_o, la_o, sq_o, sk_o, sv_o, zc_o, zs_o, qc_o, qr_o, kcat_o, vt_o) = refs[pos:pos + 14]
    pos += 14
    if not latent:
        ckv_o, kr_o = refs[pos:pos + 2]

    x = x_ref[...]
    hb = (_rms(x, gn_ref[0]) * (1.0 + mod_ref[1]) + mod_ref[0]).astype(BF)

    def col(off, n):
        return _dot(hb, w_ref[:, off:off + n])

    qa = col(C_QA, MLA_Q_RANK)
    ask = col(C_ASK, 2 * LANES)
    a_in = ask[:, :LANES].astype(BF)
    zf = col(C_ZF, MIX).astype(BF)

    gq_o[...] = col(C_GQ, MIX) * (GLA_DK ** -0.5)
    gk_o[...] = col(C_GK, MIX)

    qm = _dot(_rms(qa, gq_ref[...]).astype(BF), wqb_ref[...])
    logit = _dot(a_in, wgate_ref[...]) + bgate_ref[...]
    la_o[...] = _log_sigmoid(logit) * (1.0 / GLA_TAU)

    zc = _dot(zf, cdft_ref[...]).astype(zc_o.dtype)
    zs = _dot(zf, sdft_ref[...]).astype(zs_o.dtype)
    rows_b = zf.shape[0] // nbat
    for j in range(nbat):
        zc_o[:, j * MIX:(j + 1) * MIX] = zc[j * rows_b:(j + 1) * rows_b]
        zs_o[:, j * MIX:(j + 1) * MIX] = zs[j * rows_b:(j + 1) * rows_b]

    gv_o[...] = col(C_GV, MIX).astype(BF)
    gr_o[...] = _silu(col(C_GR, MIX))

    scale = (MLA_NOPE + MLA_ROPE) ** -0.5 * LOG2E
    qc_o[...] = (_dot(qm[:, :MIX].astype(BF), wkbd_ref[...]) * scale).astype(BF)
    q_rope = qm[:, MIX:MIX + LANES]
    if latent:
        cm, sm = cm_ref[...], sm_ref[...]
        q_rope = q_rope * cm + qm[:, MIX + LANES:MIX + 2 * LANES] * sm
    qr_o[...] = (q_rope * scale).astype(BF)

    sq = col(C_SQ, 2 * MIX)
    sk = ask[:, LANES:]
    svk = col(C_SVKVA, 2 * LANES)
    sv = svk[:, :LANES]
    if latent:
        krp = col(C_KR, 2 * LANES)
        kr = krp[:, :LANES]
        cs, ss = cs_ref[...], ss_ref[...]
        cs4 = jnp.concatenate([cs] * 4, axis=1)
        ss4 = jnp.concatenate([ss] * 4, axis=1)
        sq = sq * cs4 + col(C_SQR, 2 * MIX) * ss4
        sk = sk * cs + krp[:, LANES:] * ss
    else:
        kr = col(C_KR, LANES)
    sq_o[...] = (sq * (HEAD_DIM ** -0.5 * LOG2E)).astype(BF)
    sk_o[...] = sk.astype(sk_o.dtype)
    sv_o[...] = sv.astype(sv_o.dtype)

    ckv = _rms(svk[:, LANES:], gkv_ref[...])
    if not latent:
        ckv_o[...] = ckv
        kr_o[...] = kr[:, :MLA_ROPE]
    else:
        kr = kr * cm + col(C_KRR, LANES) * sm
    kcat_o[:, :LANES] = ckv.astype(BF)
    kcat_o[:, LANES:] = kr.astype(BF)
    ckv_t = ckv.T.astype(BF)
    cols_b = ckv_t.shape[1] // nbat
    for j in range(nbat):
        vt_o[j] = ckv_t[:, j * cols_b:(j + 1) * cols_b]


def _inproj_call(x, mod, gn, wts, gq, gkv, cdft, sdft, tables, l, batch, seq, rows_per_batch, first_row,
                 latent, tm):
    w, wgate, bgate, wqb, wkbd = wts
    r = x.shape[0]
    per = max(seq // tm, 1)
    nbat = max(tm // seq, 1)
    row = lambda n: pl.BlockSpec((tm, n), lambda i: (i, 0))
    ncol = NC_LAT if latent else NC_CTX
    nq = wqb.shape[-1] if latent else MIX + LANES
    in_specs = [
        row(D_MODEL),
        _mod_spec(l, 1, tm, rows_per_batch, first_row),
        _gn_spec(l, 2, 1),
        _pick(w, l, block=(D_MODEL, ncol)), _pick(wgate, l), _pick(bgate, l), _pick(gq, l), _pick(gkv, l),
        _pick(wqb, l, block=(MLA_Q_RANK, nq)), _pick(wkbd, l), _pick(cdft), _pick(sdft),
    ]
    args = [x, mod, gn, w, wgate, bgate, gq, gkv, wqb, wkbd, cdft, sdft]
    if latent:
        in_specs += [pl.BlockSpec((tm, LANES), lambda i: (i % per, 0))] * 4
        args += list(tables)
    kvdt = BF if latent else F32
    zdt = F32 if latent else BF
    fft_spec = _fft_layout_spec(tm, seq)
    vt_spec = pl.BlockSpec((nbat, MLA_KV_RANK, tm // nbat), lambda i: (i // per, 0, i % per))
    out_specs = [row(MIX), row(MIX), row(MIX), row(MIX), row(2 * MIX), row(2 * MIX), row(LANES), row(LANES),
                 fft_spec, fft_spec, row(2 * MIX), row(LANES), row(MIX), vt_spec]
    sd = jax.ShapeDtypeStruct
    out_shape = [sd((r, MIX), F32), sd((r, MIX), F32), sd((r, MIX), BF), sd((r, MIX), F32),
                 sd((r, 2 * MIX), F32), sd((r, 2 * MIX), BF), sd((r, LANES), kvdt), sd((r, LANES), kvdt),
                 sd((seq, batch * MIX), zdt), sd((seq, batch * MIX), zdt),
                 sd((r, 2 * MIX), BF), sd((r, LANES), BF), sd((r, MIX), BF), sd((batch, MLA_KV_RANK, seq), BF)]
    if not latent:
        out_specs += [row(MLA_KV_RANK), row(MLA_ROPE)]
        out_shape += [sd((r, MLA_KV_RANK), F32), sd((r, MLA_ROPE), F32)]
    return pl.pallas_call(
        functools.partial(_inproj_kernel, latent=latent, nbat=nbat),
        grid=(r // tm,),
        in_specs=in_specs,
        out_specs=out_specs,
        out_shape=out_shape,
        compiler_params=_params("parallel"),
        name="inproj_lat" if latent else "inproj_ctx",
    )(*args)


def _chunk_cumsum(x, row, reverse):
    n = x.shape[0]
    k = 1
    while k < n:
        if reverse:
            x = x + jnp.where(row < n - k, pltpu.roll(x, n - k, 0), 0.0)
        else:
            x = x + jnp.where(row >= k, pltpu.roll(x, k, 0), 0.0)
        k *= 2
    return x


def _gla_kernel(q_ref, k_ref, v_ref, la_ref, s0_ref, *rest, reverse, final, nchunk, group):
    if final:
        of_ref, o_ref, sfin_ref, st_ref = rest
    else:
        o_ref, sfin_ref, st_ref = rest
    i = pl.program_id(1)
    c = GLA_CHUNK

    @pl.when(i == 0)
    def _():
        st_ref[...] = s0_ref[...]

    row = lax.broadcasted_iota(jnp.int32, (c, MIX), 0)
    lane = lax.broadcasted_iota(jnp.int32, (c, MIX), 1) % c
    keep = (lane >= row) if reverse else (lane <= row)
    lane1 = lax.broadcasted_iota(jnp.int32, (1, MIX), 1) // c
    hmask = [(lane1 == h).astype(F32) for h in range(GLA_HEADS)]
    hmask_b = [m.astype(BF) for m in hmask]
    bdiag = (lax.broadcasted_iota(jnp.int32, (MIX, MIX), 0) // c
             == lax.broadcasted_iota(jnp.int32, (MIX, MIX), 1) // c)

    def chunk(t, carry):
        cc = (nchunk - 1 - t) if reverse else t
        rs = pl.ds(pl.multiple_of(cc * c, c), c)
        def front(g):
            cum = _chunk_cumsum(la_ref[g, rs, :], row, reverse)
            tot = cum[0:1] if reverse else cum[c - 1:c]
            q, k, v = q_ref[g, rs, :], k_ref[g, rs, :], v_ref[g, rs, :]
            qd = (q * jnp.exp(cum)).astype(BF)
            ki = k * jnp.exp(-cum)
            kd = (k * jnp.exp(tot - cum)).astype(BF)
            kstack = jnp.concatenate([(ki * m).astype(BF) for m in hmask], axis=0)
            vstack = jnp.concatenate([v * m for m in hmask_b], axis=0)
            return qd, vstack, _dot_nt(qd, kstack), _dot_tn(v, kd), jnp.exp(tot)

        def back(g, qd, vstack, scores, upd, decay):
            att = jnp.where(keep, scores, 0.0).astype(BF)
            st = st_ref[g]
            o = _dot(att, vstack) + _dot_nt(qd, st.astype(BF))
            st_ref[g] = st * decay + jnp.where(bdiag, upd, 0.0)
            if final:
                o = o + of_ref[g, rs, :]
            o_ref[g, rs, :] = o

        pending = front(0)
        for g in range(group):
            cur = pending
            if g + 1 < group:
                pending = front(g + 1)
            back(g, *cur)
        return carry

    lax.fori_loop(0, nchunk, chunk, 0, unroll=2)

    @pl.when(i == pl.num_programs(1) - 1)
    def _():
        sfin_ref[...] = st_ref[...]


def _gla_call(q, k, v, la, s0, s0_idx, extra, batch, seq, reverse, tb, group=GLA_GROUP):
    nblk = seq // tb
    final = extra is not None
    pos = (lambda i: nblk - 1 - i) if reverse else (lambda i: i)
    blk = pl.BlockSpec((group, tb, MIX), lambda b, i: (b, pos(i), 0))
    lblk = pl.BlockSpec((group, tb, MIX), lambda b, i: (b, pos(i), 1 if reverse else 0))
    sblk = pl.BlockSpec((group, MIX, MIX), lambda b, i: (b, 0, 0))
    s0blk = pl.BlockSpec((group, None, None, MIX, MIX), lambda b, i: (b,) + tuple(s0_idx) + (0, 0))
    r3 = lambda a: a.reshape(batch, seq, a.shape[-1])
    in_specs = [blk, blk, blk, lblk, s0blk]
    args = [r3(q), r3(k), r3(v), r3(la), s0]
    if final:
        in_specs.append(blk)
        args.append(r3(extra))
    o, st = pl.pallas_call(
        functools.partial(_gla_kernel, reverse=reverse, final=final, nchunk=tb // GLA_CHUNK, group=group),
        grid=(batch // group, nblk),
        in_specs=in_specs,
        out_specs=[blk, sblk],
        out_shape=[jax.ShapeDtypeStruct((batch, seq, MIX), F32),
                   jax.ShapeDtypeStruct((batch, MIX, MIX), F32)],
        scratch_shapes=[pltpu.VMEM((group, MIX, MIX), F32)],
        compiler_params=_params("parallel", "arbitrary"),
        name="gla_bwd" if reverse else "gla_fwd",
    )(*args)
    return o.reshape(batch * seq, MIX), st


def _swa_kernel(*refs, local, layer):
    sink_ref, q_ref = refs[:2]
    pos = 2
    if local:
        kp_ref, kc_ref, kn_ref, vp_ref, vc_ref, vn_ref = refs[pos:pos + 6]
        pos += 6
    kx_ref, vx_ref, o_ref = refs[pos:pos + 3]
    i = pl.program_id(1)
    last = pl.num_programs(1) - 1
    blk = SWA_BLOCK
    tq = q_ref.shape[0]
    cx = kx_ref.shape[0]
    kx = kx_ref[...].astype(BF)
    vx = vx_ref[...].astype(BF)
    if local:
        nsub = tq // blk
        kblocks = [kp_ref[...]] + [kc_ref[j * blk:(j + 1) * blk, :] for j in range(nsub)] + [kn_ref[...]]
        vblocks = [vp_ref[...]] + [vc_ref[j * blk:(j + 1) * blk, :] for j in range(nsub)] + [vn_ref[...]]
        row = lax.broadcasted_iota(jnp.int32, (blk, blk), 0)
        col = lax.broadcasted_iota(jnp.int32, (blk, blk), 1)
        full = jnp.full((blk, blk), True)
        fullx = jnp.full((blk, cx), True)
    else:
        nsub = 1
    rq = tq // nsub
    rows = lax.broadcasted_iota(jnp.int32, (2 * rq, 1), 0)
    low = lax.broadcasted_iota(jnp.int32, (rq, LANES), 1) < HEAD_DIM
    units = []
    for j in range(nsub):
        if local:
            kcat = jnp.concatenate([kx, kblocks[j], kblocks[j + 1], kblocks[j + 2]], axis=0)
            vcat = jnp.concatenate([vx, vblocks[j], vblocks[j + 1], vblocks[j + 2]], axis=0)
            prev_ok = (col >= row) if j > 0 else jnp.logical_and(col >= row, i > 0)
            next_ok = (col <= row) if j < nsub - 1 else jnp.logical_and(col <= row, i < last)
            keep = jnp.concatenate([fullx, prev_ok, full, next_ok], axis=1)
            keep = jnp.concatenate([keep, keep], axis=0)
        else:
            kcat, vcat, keep = kx, vx, None
        for g in range(SWA_KV_HEADS):
            units.append((j, g, kcat, vcat, keep))

    def scores(u):
        j, g, kcat, _, _ = units[u]
        rs = slice(j * rq, (j + 1) * rq)
        q2 = jnp.concatenate([q_ref[rs, (2 * g) * LANES:(2 * g + 1) * LANES],
                              q_ref[rs, (2 * g + 1) * LANES:(2 * g + 2) * LANES]], axis=0)
        return _dot_nt(q2, kcat)

    tiles = {}
    s_next = scores(0)
    for u, (j, g, _, vcat, keep) in enumerate(units):
        s = s_next
        if u + 1 < len(units):
            s_next = scores(u + 1)
        sink = jnp.where(rows < rq, sink_ref[layer, 2 * g], sink_ref[layer, 2 * g + 1]) * LOG2E
        if local:
            s = jnp.where(keep, s, NEG_INF)
        m = jnp.maximum(jnp.max(s, axis=-1, keepdims=True), sink)
        e = jnp.exp2(s - m)
        den = jnp.sum(e, axis=-1, keepdims=True) + jnp.exp2(sink - m)
        o2 = _dot(e.astype(BF), vcat) / den
        oa, ob = o2[:rq], o2[rq:]
        if g == 0:
            tiles[(j, g)] = jnp.where(low, oa, pltpu.roll(ob, HEAD_DIM, 1))
        else:
            tiles[(j, g)] = jnp.where(low, pltpu.roll(oa, HEAD_DIM, 1), ob)
            rs = slice(j * rq, (j + 1) * rq)
            o_ref[rs, :LANES] = tiles[(j, 0)].astype(BF)
            o_ref[rs, LANES:] = tiles[(j, 1)].astype(BF)


def _swa_call(sinks, layer, q, k, v, kx, vx, x_layer, batch, seq, local, tq=256):
    nt = seq // tq
    per = tq // SWA_BLOCK
    nb = seq // SWA_BLOCK
    cx = kx.shape[2]
    qspec = pl.BlockSpec((tq, 2 * MIX), lambda b, i: (b * nt + i, 0))
    in_specs = [pl.BlockSpec(memory_space=pltpu.SMEM), qspec]
    args = [sinks, q]
    if local:
        prv = pl.BlockSpec((SWA_BLOCK, LANES), lambda b, i: (b * nb + jnp.maximum(i * per - 1, 0), 0))
        cur = pl.BlockSpec((tq, LANES), lambda b, i: (b * nt + i, 0))
        nxt = pl.BlockSpec((SWA_BLOCK, LANES), lambda b, i: (b * nb + jnp.minimum((i + 1) * per, nb - 1), 0))
        in_specs += [prv, cur, nxt, prv, cur, nxt]
        args += [k, k, k, v, v, v]
    xspec = pl.BlockSpec((None, None, cx, LANES), lambda b, i: (b, x_layer, 0, 0))
    in_specs += [xspec, xspec]
    args += [kx, vx]
    return pl.pallas_call(
        functools.partial(_swa_kernel, local=local, layer=layer),
        grid=(batch, nt),
        in_specs=in_specs,
        out_specs=pl.BlockSpec((tq, MIX), lambda b, i: (b * nt + i, 0)),
        out_shape=jax.ShapeDtypeStruct((batch * seq, MIX), BF),
        compiler_params=_params("parallel", "parallel"),
        name="swa_lat" if local else "swa_ctx",
    )(*args)


def _fft_kernel(xc_ref, xs_ref, yc_ref, ys_ref, zc_ref, zs_ref, o_ref, *, nj, scale):
    yc, ys = yc_ref[...], ys_ref[...]
    xc, xs = xc_ref[...], xs_ref[...]
    acc = None
    for jp in range(nj // 2):
        cos_t, sin_t = [], []
        for j in (2 * jp, 2 * jp + 1):
            a, b = xc[:, j:j + 1], xs[:, j:j + 1]
            cos_t.append(a * yc - b * ys)
            sin_t.append(b * yc + a * ys)
        cos_t = jnp.concatenate(cos_t, axis=1).astype(BF)
        sin_t = jnp.concatenate(sin_t, axis=1).astype(BF)
        rows = slice(jp * MXU_DEPTH, (jp + 1) * MXU_DEPTH)
        d = _dot(cos_t, zc_ref[rows, :]) + _dot(sin_t, zs_ref[rows, :])
        acc = d if acc is None else acc + d
    o_ref[...] = (acc * scale).astype(BF)


def _fft_tables(seq):
    sp = jnp.arange(seq, dtype=jnp.int32)[:, None]
    w = 2.0 * jnp.pi / seq
    ax = ((sp * (jnp.arange(seq // LANES, dtype=jnp.int32) * LANES)[None, :]) % seq).astype(F32) * w
    ay = ((sp * jnp.arange(LANES, dtype=jnp.int32)[None, :]) % seq).astype(F32) * w
    return jnp.cos(ax), jnp.sin(ax), jnp.cos(ay), jnp.sin(ay)


def _fft_call(tables, zc, zs, seq, tm=256):
    xc, xs, yc, ys = tables
    nj = seq // LANES
    width = zc.shape[1]
    return pl.pallas_call(
        functools.partial(_fft_kernel, nj=nj, scale=(seq * FNET_GROUP_CH) ** -0.5),
        grid=(seq // tm,),
        in_specs=[
            pl.BlockSpec((tm, nj), lambda i: (i, 0)), pl.BlockSpec((tm, nj), lambda i: (i, 0)),
            pl.BlockSpec((tm, LANES), lambda i: (i, 0)), pl.BlockSpec((tm, LANES), lambda i: (i, 0)),
            _pick(zc), _pick(zs),
        ],
        out_specs=pl.BlockSpec((tm, width), lambda i: (i, 0)),
        out_shape=jax.ShapeDtypeStruct((seq, width), BF),
        compiler_params=_params("parallel"),
        name="fft",
    )(xc, xs, yc, ys, zc, zs)


def _fft2_kernel(zc_ref, zs_ref, m1_ref, m2_ref, tr_ref, ti_ref, o_ref, x_ref, *, nb1, scale):
    i = pl.program_id(1)
    n = m2_ref.shape[0]

    @pl.when(i < nb1)
    def _():
        m1 = m1_ref[...]
        for j in range(8):
            z = jnp.concatenate([zc_ref[:, j, :], zs_ref[:, j, :]], axis=0).astype(BF)
            x1 = _dot(m1, z)
            xr, xi = x1[:n], x1[n:]
            tr, ti = tr_ref[:, j:j + 1], ti_ref[:, j:j + 1]
            x2 = jnp.concatenate([xr * tr - xi * ti, xr * ti + xi * tr], axis=0)
            x_ref[i * 8 + j] = x2.reshape(2 * n // 8, 8, x2.shape[-1])

    @pl.when(i >= nb1)
    def _():
        m2 = m2_ref[...]
        blk = i - nb1
        for j in range(8):
            z = jnp.concatenate([x_ref[:, blk, j, :], x_ref[:, n // 8 + blk, j, :]], axis=0).astype(BF)
            o_ref[:, j, :] = _dot(m2, z) * scale


def _fft2_call(zc, zs, seq, wl=512):
    n = int(round(seq ** 0.5))
    assert seq == n * n and n % 8 == 0
    width = zc.shape[1]
    nb = n // 8
    k = jnp.arange(n, dtype=jnp.int32)
    ang = ((k[:, None] * k[None, :]) % n).astype(F32) * (2.0 * jnp.pi / n)
    c, sn = jnp.cos(ang), jnp.sin(ang)
    m1 = jnp.concatenate([jnp.concatenate([c, sn], axis=1), jnp.concatenate([-sn, c], axis=1)], axis=0).astype(BF)
    m2 = jnp.concatenate([c, sn], axis=1).astype(BF)
    tw = (k[:, None] * k[None, :]).astype(F32) * (2.0 * jnp.pi / seq)
    tr = jnp.cos(tw).reshape(n, nb, 8).transpose(1, 0, 2)
    ti = (-jnp.sin(tw)).reshape(n, nb, 8).transpose(1, 0, 2)
    zspec = pl.BlockSpec((n, 8, wl), lambda h, i: (0, jnp.minimum(i, nb - 1), h))
    tspec = pl.BlockSpec((None, n, 8), lambda h, i: (jnp.minimum(i, nb - 1), 0, 0))
    out = pl.pallas_call(
        functools.partial(_fft2_kernel, nb1=nb, scale=(seq * FNET_GROUP_CH) ** -0.5),
        grid=(width // wl, 2 * nb),
        in_specs=[zspec, zspec, _pick(m1), _pick(m2), tspec, tspec],
        out_specs=pl.BlockSpec((n, 8, wl), lambda h, i: (0, jnp.maximum(i - nb, 0), h)),
        out_shape=jax.ShapeDtypeStruct((n, n, width), F32),
        scratch_shapes=[pltpu.VMEM((n, 2 * n // 8, 8, wl), F32)],
        compiler_params=_params("parallel", "arbitrary"),
        name="fft2",
    )(zc.reshape(n, n, width), zs.reshape(n, n, width), m1, m2, tr, ti)
    return out.reshape(seq, width)


def _mla_kernel(*refs, has_ctx):
    qc_ref, qr_ref, ks_ref, vs_ref = refs[:4]
    pos = 4
    if has_ctx:
        kx_ref, vx_ref = refs[pos:pos + 2]
        pos += 2
    wv_ref, o_ref = refs[pos:pos + 2]
    qr = qr_ref[...]
    lane = lax.broadcasted_iota(jnp.int32, qr.shape, 1) // MLA_ROPE
    tq = qr.shape[0]
    qs = []
    for h in range(MLA_HEADS):
        qrh = jnp.where(lane == h, qr, jnp.zeros_like(qr))
        qs.append(jnp.concatenate([qc_ref[:, h * LANES:(h + 1) * LANES], qrh], axis=1))
    qall = jnp.concatenate(qs, axis=0)
    nk = ks_ref.shape[0]
    tk = min(nk, MLA_KEY_TILE)
    tiles = ([(kx_ref, vx_ref, 0, kx_ref.shape[0])] if has_ctx else [])
    tiles += [(ks_ref, vs_ref, j * tk, tk) for j in range(nk // tk)]

    def scores(t):
        kref, _, lo, n = tiles[t]
        return _dot_nt(kref[lo:lo + n, :], qall)

    m = den = acc = None
    st_next = scores(0)
    for t, (_, vref, lo, n) in enumerate(tiles):
        st = st_next
        if t + 1 < len(tiles):
            st_next = scores(t + 1)
        mt = jnp.max(st, axis=0, keepdims=True)
        m_new = mt if m is None else jnp.maximum(m, mt)
        ps = pv = None
        for c0 in range(0, n, MLA_SUB_TILE):
            c1 = min(n, c0 + MLA_SUB_TILE)
            p = jnp.exp2(st[c0:c1] - m_new)
            r = jnp.sum(p, axis=0, keepdims=True)
            d = _dot(vref[:, lo + c0:lo + c1], p.astype(BF))
            ps, pv = (r, d) if ps is None else (ps + r, pv + d)
        if m is None:
            den, acc = ps, pv
        else:
            alpha = jnp.exp2(m - m_new)
            den = alpha * den + ps
            acc = alpha * acc + pv
        m = m_new
    ot = acc / den
    ocat_t = jnp.concatenate([ot[:, h * tq:(h + 1) * tq] for h in range(MLA_HEADS)], axis=0).astype(BF)
    o_ref[...] = _dot_tn(ocat_t, wv_ref[...]).astype(BF)


def _mla_call(qc, qr, kcat, vt, kx, vxt, wvbd, l, batch, seq, tq=256):
    nq = seq // tq
    has_ctx = kx is not None
    in_specs = [pl.BlockSpec((tq, 2 * MIX), lambda b, i: (b * nq + i, 0)),
                pl.BlockSpec((tq, LANES), lambda b, i: (b * nq + i, 0)),
                pl.BlockSpec((None, seq, MIX), lambda b, i: (b, 0, 0)),
                pl.BlockSpec((None, MLA_KV_RANK, seq), lambda b, i: (b, 0, 0))]
    args = [qc, qr, kcat.reshape(batch, seq, MIX), vt]
    if has_ctx:
        in_specs += [pl.BlockSpec((None, None, kx.shape[2], MIX), lambda b, i: (b, l, 0, 0)),
                     pl.BlockSpec((None, None, MLA_KV_RANK, kx.shape[2]), lambda b, i: (b, l, 0, 0))]
        args += [kx, vxt]
    in_specs.append(_pick(wvbd, l))
    args.append(wvbd)
    return pl.pallas_call(
        functools.partial(_mla_kernel, has_ctx=has_ctx),
        grid=(batch, nq),
        in_specs=in_specs,
        out_specs=pl.BlockSpec((tq, MIX), lambda b, i: (b * nq + i, 0)),
        out_shape=jax.ShapeDtypeStruct((batch * seq, MIX), BF),
        compiler_params=_params("parallel", "parallel"),
        name="mla_lat" if has_ctx else "mla_ctx",
    )(*args)


def _rot_cols(w, width):
    half = width // 2
    parts = w.reshape(w.shape[:-1] + (w.shape[-1] // width, 2, half))
    return jnp.stack([-parts[..., 1, :], parts[..., 0, :]], axis=-2).reshape(w.shape)


def _block_diag(blocks):
    n, r, c = blocks.shape[-3:]
    eye = jnp.eye(n, dtype=blocks.dtype)
    out = jnp.einsum("...hrc,hg->...hrgc", blocks, eye)
    return out.reshape(blocks.shape[:-3] + (n * r, n * c))


def _layer_weights(w_in, gla_w_gate, gla_b_gate, mla_w_q_b, mla_w_kv_b):
    nl = w_in.shape[0]
    w_in = w_in.astype(BF)
    offs = [0]
    for n in (256, 256, 256, 256, 16, 16, 256, 128, 128, 256, 256, 128, 32):
        offs.append(offs[-1] + n)
    seg = lambda k: w_in[..., offs[k]:offs[k + 1]]
    zeros = lambda n: jnp.zeros((nl, D_MODEL, n), w_in.dtype)
    sq = seg(6).reshape(nl, D_MODEL, SWA_Q_HEADS, HEAD_DIM)
    sq_tiles = []
    for hq in range(SWA_Q_HEADS):
        parts = [zeros(HEAD_DIM), zeros(HEAD_DIM)]
        parts[hq // 2] = sq[:, :, hq]
        sq_tiles += parts
    sq_w = jnp.concatenate(sq_tiles, axis=-1)
    kr4 = jnp.tile(seg(12), (1, 1, MLA_HEADS))
    cols = [seg(0), seg(1), seg(2), seg(3), seg(4), seg(5), zeros(LANES - 2 * GLA_GATE_RANK), seg(7),
            seg(8), seg(11), sq_w, seg(9), seg(10), kr4,
            _rot_cols(seg(7), HEAD_DIM // 2), _rot_cols(sq_w, HEAD_DIM // 2), _rot_cols(kr4, MLA_ROPE // 2)]
    w = jnp.concatenate(cols, axis=-1)

    wgate = jnp.zeros((nl, LANES, 2 * MIX), F32)
    wgate = wgate.at[:, :GLA_GATE_RANK, :MIX].set(gla_w_gate[:, 0])
    wgate = wgate.at[:, GLA_GATE_RANK:2 * GLA_GATE_RANK, MIX:].set(gla_w_gate[:, 1])
    bgate = gla_b_gate.reshape(nl, 1, 2 * MIX)

    wq = mla_w_q_b.reshape(nl, MLA_Q_RANK, MLA_HEADS, MLA_NOPE + MLA_ROPE)
    q_nope = wq[..., :MLA_NOPE].reshape(nl, MLA_Q_RANK, MLA_HEADS * MLA_NOPE)
    q_rope = wq[..., MLA_NOPE:].reshape(nl, MLA_Q_RANK, MLA_HEADS * MLA_ROPE)
    wqb = jnp.concatenate([q_nope, q_rope, _rot_cols(q_rope, MLA_ROPE // 2)], axis=-1).astype(BF)
    wkv = mla_w_kv_b.reshape(nl, MLA_KV_RANK, MLA_HEADS, MLA_NOPE + MLA_V)
    wkbd = _block_diag(jnp.transpose(wkv[..., :MLA_NOPE], (0, 2, 3, 1))).astype(BF)
    wvbd = _block_diag(jnp.transpose(wkv[..., MLA_NOPE:], (0, 2, 1, 3))).astype(BF)
    return (w, wgate.astype(BF), bgate, wqb, wkbd), wvbd


def _rope_tables(seq):
    nrow = seq // GRID_W

    def table(width):
        half = width // 2
        lane = jnp.arange(LANES)
        inv = ROPE_BASE ** (-(2 * (lane % half)).astype(F32) / width)
        by_row = ((lane // width) % 2 == 0)[None, :]
        ang_r = jnp.arange(nrow, dtype=F32)[:, None] * inv[None, :]
        ang_c = jnp.arange(GRID_W, dtype=F32)[:, None] * inv[None, :]

        def expand(fn):
            rows = jnp.repeat(fn(ang_r), GRID_W, axis=0)
            cols = jnp.tile(fn(ang_c), (nrow, 1))
            return jnp.where(by_row, rows, cols)

        return expand(jnp.cos), expand(jnp.sin)

    cs, ss = table(HEAD_DIM // 2)
    cm, sm = table(MLA_ROPE // 2)
    return cs, ss, cm, sm


def _dft_channel():
    c = jnp.arange(FNET_GROUP_CH, dtype=jnp.int32)
    ang = ((c[:, None] * c[None, :]) % FNET_GROUP_CH).astype(F32) * (2.0 * jnp.pi / FNET_GROUP_CH)
    cos_b = jnp.broadcast_to(jnp.cos(ang)[None], (FNET_GROUPS, FNET_GROUP_CH, FNET_GROUP_CH))
    sin_b = jnp.broadcast_to(jnp.sin(ang)[None], (FNET_GROUPS, FNET_GROUP_CH, FNET_GROUP_CH))
    return _block_diag(cos_b).astype(BF), (-_block_diag(sin_b)).astype(BF)


def _state_in(s):
    return _block_diag(jnp.swapaxes(s, -1, -2))


def _state_out(st):
    blocks = [st[:, h * GLA_DV:(h + 1) * GLA_DV, h * GLA_DK:(h + 1) * GLA_DK] for h in range(GLA_HEADS)]
    return jnp.swapaxes(jnp.stack(blocks, axis=1), 2, 3)


def _token_mix(x, l, shared, st0, st0_idx, cache, batch, seq, rows_per_batch, first_row, latent):
    mod, gn, wts, wvbd, gq, gkv, sinks, cdft, sdft, rope_t, fft_t = shared
    outs = _inproj_call(x, mod, gn, wts, gq, gkv, cdft, sdft, rope_t, l, batch, seq, rows_per_batch,
                        first_row, latent, tm=512)
    gqv, gkv_, gv, gr, la, sq, sk, sv, zc, zs, qc, qr, kcat, vt = outs[:14]
    tb = min(seq, 512)
    o_f, st_f = _gla_call(gqv, gkv_, gv, la, st0, st0_idx[0], None, batch, seq, False, tb)
    o_gla, st_b = _gla_call(gqv, gkv_, gv, la, st0, st0_idx[1], o_f, batch, seq, True, tb)
    if latent:
        kx, vx, mx, mxt = cache
        o_swa = _swa_call(sinks, l, sq, sk, sv, kx, vx, l, batch, seq, True, tq=512)
        o_mla = _mla_call(qc, qr, kcat, vt, mx, mxt, wvbd, l, batch, seq)
        new_ctx = None
    else:
        k4 = sk.reshape(batch, 1, seq, LANES)
        v4 = sv.reshape(batch, 1, seq, LANES)
        o_swa = _swa_call(sinks, l, sq, None, None, k4, v4, 0, batch, seq, False)
        o_mla = _mla_call(qc, qr, kcat, vt, None, None, wvbd, l, batch, seq)
        new_ctx = (st_f, st_b, sk, sv, outs[14], outs[15])
    o_fft = _fft2_call(zc, zs, seq) if latent else _fft_call(fft_t, zc, zs, seq)
    return (o_gla, gr, o_swa, o_fft, o_mla), new_ctx


def kernel(x_prompt, x_sample, c, state_gla, cache_swa_k, cache_swa_v, cache_mla_ckv, cache_mla_krope,
           c_ctx, w_mod, b_mod, g_norm, w_ffn_gate, w_ffn_up, w_ffn_down, w_in, gla_w_gate, gla_b_gate,
           gla_g_out, swa_sink, mla_g_q, mla_g_kv, mla_w_q_b, mla_w_kv_b, w_out):
    nb, ns, _ = x_prompt.shape
    db, dsq, _ = x_sample.shape
    past = cache_swa_k.shape[2]

    cvec = jnp.zeros((MOD_ROWS, D_MODEL), F32).at[0].set(c_ctx).at[1:1 + db].set(c)
    mod = _mod_call(cvec, w_mod, b_mod).reshape(DEPTH, MOD_ROWS, N_MOD, 1, D_MODEL)
    gn = g_norm.reshape(DEPTH, 6, 1, D_MODEL)

    wg, wu, wd, wo = (a.astype(BF) for a in (w_ffn_gate, w_ffn_up, w_ffn_down, w_out))
    wts, wvbd = _layer_weights(w_in, gla_w_gate, gla_b_gate, mla_w_q_b, mla_w_kv_b)
    gq = mla_g_q.reshape(DEPTH, 1, MLA_Q_RANK)
    gkv = mla_g_kv.reshape(DEPTH, 1, MLA_KV_RANK)
    gout = jnp.tile(gla_g_out, (1, GLA_HEADS)).reshape(DEPTH, 1, MIX)
    cdft, sdft = _dft_channel()
    shared = (mod, gn, wts, wvbd, gq, gkv, swa_sink, cdft, sdft)
    head = jnp.arange(MIX) // GLA_DV
    avg = jnp.where(head[:, None] == head[None, :], 1.0 / GLA_DV, 0.0).astype(BF)
    shared_ctx = shared + (None, _fft_tables(ns))
    shared_lat = shared + (_rope_tables(dsq), None)

    st_lat = _state_in(state_gla)
    st_zero = jnp.zeros((nb, 1, 1, MIX, MIX), F32)
    kx = cache_swa_k.reshape(db, DEPTH, past, LANES)
    vx = cache_swa_v.reshape(db, DEPTH, past, LANES)
    mx = jnp.concatenate([cache_mla_ckv, jnp.tile(cache_mla_krope, (1, 1, 1, MLA_HEADS))], axis=-1).astype(BF)

    mxt = jnp.swapaxes(cache_mla_ckv, -1, -2).astype(BF)

    xc = x_prompt.reshape(nb * ns, D_MODEL)
    xl = x_sample.reshape(db * dsq, D_MODEL)
    st_gla, st_k, st_v, st_ckv, st_kr = [], [], [], [], []
    for l in range(DEPTH):
        xc = _ffn_call(xc, mod, gn, wg, wu, wd, l, rows_per_batch=nb * ns, first_row=0)
        mixed, new_ctx = _token_mix(xc, l, shared_ctx, st_zero, ((0, 0), (0, 0)), None, nb, ns, nb * ns, 0, False)
        xc = _outffn_call(*mixed, xc, mod, gn, gout, avg, wo, wg, wu, wd, l, seq=ns, rows_per_batch=nb * ns,
                          first_row=0, tm=512)
        st_f, st_b, k_c, v_c, ckv_c, kr_c = new_ctx
        st_gla.append(jnp.stack([_state_out(st_f), _state_out(st_b)], axis=1))
        st_k.append(k_c.reshape(nb, ns, SWA_KV_HEADS, HEAD_DIM))
        st_v.append(v_c.reshape(nb, ns, SWA_KV_HEADS, HEAD_DIM))
        st_ckv.append(ckv_c.reshape(nb, ns, MLA_KV_RANK))
        st_kr.append(kr_c.reshape(nb, ns, MLA_ROPE))

        xl = _ffn_call(xl, mod, gn, wg, wu, wd, l, rows_per_batch=dsq, first_row=1)
        mixed, _ = _token_mix(xl, l, shared_lat, st_lat, ((l, 0), (l, 1)), (kx, vx, mx, mxt), db, dsq, dsq, 1, True)
        xl = _outffn_call(*mixed, xl, mod, gn, gout, avg, wo, wg, wu, wd, l, seq=dsq, rows_per_batch=dsq,
                          first_row=1, tm=512)

    def stack(parts):
        return jnp.moveaxis(jnp.stack(parts, axis=0), 0, 1)

    return (xc.reshape(nb, ns, D_MODEL), xl.reshape(db, dsq, D_MODEL), stack(st_gla), stack(st_k), stack(st_v),
            stack(st_ckv), stack(st_kr))
```

```python
import functools

import jax
import jax.numpy as jnp
from jax import lax
from jax.experimental import pallas as pl
from jax.experimental.pallas import tpu as pltpu

D_MODEL = 1024
DEPTH = 4
GRID_W = 64
HEAD_DIM = 64
GLA_HEADS = 4
GLA_DK = 64
GLA_DV = 64
GLA_GATE_RANK = 16
GLA_TAU = 16.0
GLA_CHUNK = 64
SWA_Q_HEADS = 4
SWA_KV_HEADS = 2
SWA_BLOCK = 128
FNET_GROUPS = 4
FNET_GROUP_CH = 64
MLA_HEADS = 4
MLA_Q_RANK = 256
MLA_KV_RANK = 128
MLA_NOPE = 64
MLA_ROPE = 32
MLA_V = 64
D_FF = 2816
FFN_RES = 0.5
N_MOD = 9
ROPE_BASE = 10000.0
EPS = 1e-6
NEG_INF = -1e30

MIX = 256
LANES = 128
MXU_DEPTH = 256
FF_CHUNK = 256
MLA_KEY_TILE = 2048
FFT_BLOCK = 16
GLA_BLOCK = 1024
GLA_GROUP = 4
LOG2E = 1.4426950408889634
MOD_ROWS = 8
VMEM_LIMIT = 56 * 1024 * 1024

BF = jnp.bfloat16
F32 = jnp.float32

C_GQ, C_GK, C_GV, C_GR = 0, 256, 512, 768
C_ASK, C_SVKVA, C_SQ, C_ZF, C_QA, C_KR = 1024, 1280, 1536, 2048, 2304, 2560
C_SKR, C_SQR, C_KRR = 2688, 2816, 3328
NC_CTX, NC_LAT = 2688, 3456


def _dot(a, b):
    return jnp.dot(a, b, preferred_element_type=F32)


def _dot_nt(a, b):
    return lax.dot_general(a, b, (((1,), (1,)), ((), ())), preferred_element_type=F32)


def _dot_tn(a, b):
    return lax.dot_general(a, b, (((0,), (0,)), ((), ())), preferred_element_type=F32)


def _rms(x, g):
    return x * lax.rsqrt(jnp.mean(x * x, axis=-1, keepdims=True) + EPS) * g


def _silu(x):
    return x * jax.nn.sigmoid(x)


def _params(*sem):
    return pltpu.CompilerParams(dimension_semantics=sem, vmem_limit_bytes=VMEM_LIMIT)


def _pick(arr, *lead, block=None):
    tail = tuple(arr.shape[len(lead):]) if block is None else tuple(block)
    zeros = (0,) * len(tail)
    return pl.BlockSpec((None,) * len(lead) + tail, lambda *_: tuple(lead) + zeros,
                        pipeline_mode=pl.Buffered(1))


def _mod_kernel(c_ref, w_ref, b_ref, o_ref):
    s = _silu(c_ref[...]).astype(BF)
    o_ref[...] = _dot(s, w_ref[...].astype(BF)) + b_ref[...]


def _mod_call(cvec, w_mod, b_mod):
    nl = w_mod.shape[0]
    tn = D_MODEL
    return pl.pallas_call(
        _mod_kernel,
        grid=(nl, N_MOD * D_MODEL // tn),
        in_specs=[
            pl.BlockSpec((MOD_ROWS, D_MODEL), lambda l, j: (0, 0)),
            pl.BlockSpec((None, D_MODEL, tn), lambda l, j: (l, 0, j)),
            pl.BlockSpec((None, 1, tn), lambda l, j: (l, 0, j)),
        ],
        out_specs=pl.BlockSpec((None, MOD_ROWS, tn), lambda l, j: (l, 0, j)),
        out_shape=jax.ShapeDtypeStruct((nl, MOD_ROWS, N_MOD * D_MODEL), F32),
        compiler_params=_params("parallel", "parallel"),
        name="mod",
    )(cvec, w_mod, b_mod.reshape(nl, 1, N_MOD * D_MODEL))


def _mod_spec(l, sub, tm, rows_per_batch, first_row):
    per = rows_per_batch // tm
    return pl.BlockSpec((None, None, 3, 1, D_MODEL), lambda i: (l, first_row + i // per, sub, 0, 0))


def _fft_layout_spec(tm, seq):
    per = max(seq // tm, 1)
    nbat = max(tm // seq, 1)
    return pl.BlockSpec((tm // nbat, nbat * MIX), lambda i: (i % per, i // per))


def _gn_spec(l, first, n):
    return pl.BlockSpec((None, n, 1, D_MODEL), lambda i: (l, first // n, 0, 0))


def _ffn_body(x, mod_ref, g_pre, g_post, wg_ref, wu_ref, wd_ref):
    sh, sc, gt = mod_ref[0], mod_ref[1], mod_ref[2]
    hb = (_rms(x, g_pre) * (1.0 + sc) + sh).astype(BF)
    acc = None
    for j in range(D_FF // FF_CHUNK):
        sl = slice(j * FF_CHUNK, (j + 1) * FF_CHUNK)
        g = _dot(hb, wg_ref[:, sl])
        u = _dot(hb, wu_ref[:, sl])
        d = _dot((_silu(g) * u).astype(BF), wd_ref[sl, :])
        acc = d if acc is None else acc + d
    return x + FFN_RES * gt * _rms(acc, g_post)


def _ffn_kernel(x_ref, mod_ref, gn_ref, wg_ref, wu_ref, wd_ref, o_ref):
    o_ref[...] = _ffn_body(x_ref[...], mod_ref, gn_ref[0], gn_ref[1], wg_ref, wu_ref, wd_ref)


def _ffn_call(x, mod, gn, wg, wu, wd, l, rows_per_batch, first_row, tm=512):
    r = x.shape[0]
    return pl.pallas_call(
        _ffn_kernel,
        grid=(r // tm,),
        in_specs=[
            pl.BlockSpec((tm, D_MODEL), lambda i: (i, 0)),
            _mod_spec(l, 0, tm, rows_per_batch, first_row),
            _gn_spec(l, 0, 2),
            _pick(wg, l, 0), _pick(wu, l, 0), _pick(wd, l, 0),
        ],
        out_specs=pl.BlockSpec((tm, D_MODEL), lambda i: (i, 0)),
        out_shape=jax.ShapeDtypeStruct(x.shape, F32),
        compiler_params=_params("parallel"),
        name="ffn",
    )(x, mod, gn, wg, wu, wd)


def _outffn_kernel(og_ref, gr_ref, os_ref, of_ref, om_ref, x_ref, mod2_ref, mod3_ref, gn_ref, gout_ref,
                   avg_ref, wo_ref, wg_ref, wu_ref, wd_ref, o_ref):
    og = og_ref[...]
    ms = _dot((og * og).astype(BF), avg_ref[...])
    og = (og * lax.rsqrt(ms + EPS) * gout_ref[...] * gr_ref[...]).astype(BF)
    nbat = of_ref.shape[1] // MIX
    of = jnp.concatenate([of_ref[:, j * MIX:(j + 1) * MIX] for j in range(nbat)], axis=0).astype(BF)
    mix = jnp.concatenate([og, os_ref[...], of, om_ref[...]], axis=1)
    x = x_ref[...] + mod2_ref[2] * _rms(_dot(mix, wo_ref[...]), gn_ref[0])
    o_ref[...] = _ffn_body(x, mod3_ref, gn_ref[1], gn_ref[2], wg_ref, wu_ref, wd_ref)


def _outffn_call(og, gr, osw, of, om, x, mod, gn, gout, avg, wo, wg, wu, wd, l, seq, rows_per_batch, first_row, tm):
    r = x.shape[0]
    mspec = pl.BlockSpec((tm, MIX), lambda i: (i, 0))
    return pl.pallas_call(
        _outffn_kernel,
        grid=(r // tm,),
        in_specs=[
            mspec, mspec, mspec,
            _fft_layout_spec(tm, seq),
            mspec,
            pl.BlockSpec((tm, D_MODEL), lambda i: (i, 0)),
            _mod_spec(l, 1, tm, rows_per_batch, first_row),
            _mod_spec(l, 2, tm, rows_per_batch, first_row),
            _gn_spec(l, 3, 3),
            _pick(gout, l), _pick(avg),
            _pick(wo, l), _pick(wg, l, 1), _pick(wu, l, 1), _pick(wd, l, 1),
        ],
        out_specs=pl.BlockSpec((tm, D_MODEL), lambda i: (i, 0)),
        out_shape=jax.ShapeDtypeStruct(x.shape, F32),
        compiler_params=_params("parallel"),
        name="outffn",
    )(og, gr, osw, of, om, x, mod, mod, gn, gout, avg, wo, wg, wu, wd)


def _log_sigmoid(x):
    return -(jnp.maximum(-x, 0.0) + jnp.log(1.0 + jnp.exp(-jnp.abs(x))))


def _inproj_kernel(*refs, latent, nbat):
    (x_ref, mod_ref, gn_ref, w_ref, wgate_ref, bgate_ref, gq_ref, gkv_ref, wqb_ref, wkbd_ref,
     cdft_ref, sdft_ref) = refs[:12]
    pos = 12
    if latent:
        cs_ref, ss_ref, cm_ref, sm_ref = refs[pos:pos + 4]
        pos += 4
    (gq_o, gk_o, gv_o, gr_o, la_o, sq_o, sk_o, sv_o, zc_o, zs_o, qc_o, qr_o, kcat_o) = refs[pos:pos + 13]
    pos += 13
    if not latent:
        ckv_o, kr_o = refs[pos:pos + 2]

    x = x_ref[...]
    hb = (_rms(x, gn_ref[0]) * (1.0 + mod_ref[1]) + mod_ref[0]).astype(BF)

    def col(off, n):
        return _dot(hb, w_ref[:, off:off + n])

    qa = col(C_QA, MLA_Q_RANK)
    ask = col(C_ASK, 2 * LANES)
    a_in = ask[:, :LANES].astype(BF)
    zf = col(C_ZF, MIX).astype(BF)

    gq_o[...] = col(C_GQ, MIX) * (GLA_DK ** -0.5)
    gk_o[...] = col(C_GK, MIX)

    qm = _dot(_rms(qa, gq_ref[...]).astype(BF), wqb_ref[...])
    logit = _dot(a_in, wgate_ref[...]) + bgate_ref[...]
    la_o[...] = _log_sigmoid(logit) * (1.0 / GLA_TAU)

    zc = _dot(zf, cdft_ref[...]).astype(zc_o.dtype)
    zs = _dot(zf, sdft_ref[...]).astype(zs_o.dtype)
    rows_b = zf.shape[0] // nbat
    for j in range(nbat):
        zc_o[:, j * MIX:(j + 1) * MIX] = zc[j * rows_b:(j + 1) * rows_b]
        zs_o[:, j * MIX:(j + 1) * MIX] = zs[j * rows_b:(j + 1) * rows_b]

    gv_o[...] = col(C_GV, MIX).astype(BF)
    gr_o[...] = _silu(col(C_GR, MIX))

    scale = (MLA_NOPE + MLA_ROPE) ** -0.5 * LOG2E
    qc_o[...] = (_dot(qm[:, :MIX].astype(BF), wkbd_ref[...]) * scale).astype(BF)
    q_rope = qm[:, MIX:MIX + LANES]
    if latent:
        cm, sm = cm_ref[...], sm_ref[...]
        q_rope = q_rope * cm + qm[:, MIX + LANES:MIX + 2 * LANES] * sm
    qr_o[...] = (q_rope * scale).astype(BF)

    sq = col(C_SQ, 2 * MIX)
    sk = ask[:, LANES:]
    svk = col(C_SVKVA, 2 * LANES)
    sv = svk[:, :LANES]
    if latent:
        krp = col(C_KR, 2 * LANES)
        kr = krp[:, :LANES]
        cs, ss = cs_ref[...], ss_ref[...]
        cs4 = jnp.concatenate([cs] * 4, axis=1)
        ss4 = jnp.concatenate([ss] * 4, axis=1)
        sq = sq * cs4 + col(C_SQR, 2 * MIX) * ss4
        sk = sk * cs + krp[:, LANES:] * ss
    else:
        kr = col(C_KR, LANES)
    sq_o[...] = (sq * (HEAD_DIM ** -0.5 * LOG2E)).astype(BF)
    sk_o[...] = sk.astype(sk_o.dtype)
    sv_o[...] = sv.astype(sv_o.dtype)

    ckv = _rms(svk[:, LANES:], gkv_ref[...])
    if not latent:
        ckv_o[...] = ckv
        kr_o[...] = kr[:, :MLA_ROPE]
    else:
        kr = kr * cm + col(C_KRR, LANES) * sm
    kcat_o[:, :LANES] = ckv.astype(BF)
    kcat_o[:, LANES:] = kr.astype(BF)


def _inproj_call(x, mod, gn, wts, gq, gkv, cdft, sdft, tables, l, batch, seq, rows_per_batch, first_row,
                 latent, tm):
    w, wgate, bgate, wqb, wkbd = wts
    r = x.shape[0]
    per = max(seq // tm, 1)
    nbat = max(tm // seq, 1)
    row = lambda n: pl.BlockSpec((tm, n), lambda i: (i, 0))
    ncol = NC_LAT if latent else NC_CTX
    nq = wqb.shape[-1] if latent else MIX + LANES
    in_specs = [
        row(D_MODEL),
        _mod_spec(l, 1, tm, rows_per_batch, first_row),
        _gn_spec(l, 2, 1),
        _pick(w, l, block=(D_MODEL, ncol)), _pick(wgate, l), _pick(bgate, l), _pick(gq, l), _pick(gkv, l),
        _pick(wqb, l, block=(MLA_Q_RANK, nq)), _pick(wkbd, l), _pick(cdft), _pick(sdft),
    ]
    args = [x, mod, gn, w, wgate, bgate, gq, gkv, wqb, wkbd, cdft, sdft]
    if latent:
        in_specs += [pl.BlockSpec((tm, LANES), lambda i: (i % per, 0))] * 4
        args += list(tables)
    kvdt = BF if latent else F32
    zdt = F32 if latent else BF
    fft_spec = _fft_layout_spec(tm, seq)
    out_specs = [row(MIX), row(MIX), row(MIX), row(MIX), row(2 * MIX), row(2 * MIX), row(LANES), row(LANES),
                 fft_spec, fft_spec, row(2 * MIX), row(LANES), row(MIX)]
    sd = jax.ShapeDtypeStruct
    out_shape = [sd((r, MIX), F32), sd((r, MIX), F32), sd((r, MIX), BF), sd((r, MIX), F32),
                 sd((r, 2 * MIX), F32), sd((r, 2 * MIX), BF), sd((r, LANES), kvdt), sd((r, LANES), kvdt),
                 sd((seq, batch * MIX), zdt), sd((seq, batch * MIX), zdt),
                 sd((r, 2 * MIX), BF), sd((r, LANES), BF), sd((r, MIX), BF)]
    if not latent:
        out_specs += [row(MLA_KV_RANK), row(MLA_ROPE)]
        out_shape += [sd((r, MLA_KV_RANK), F32), sd((r, MLA_ROPE), F32)]
    return pl.pallas_call(
        functools.partial(_inproj_kernel, latent=latent, nbat=nbat),
        grid=(r // tm,),
        in_specs=in_specs,
        out_specs=out_specs,
        out_shape=out_shape,
        compiler_params=_params("parallel"),
        name="inproj_lat" if latent else "inproj_ctx",
    )(*args)


def _chunk_cumsum(x, row, reverse):
    n = x.shape[0]
    k = 1
    while k < n:
        if reverse:
            x = x + jnp.where(row < n - k, pltpu.roll(x, n - k, 0), 0.0)
        else:
            x = x + jnp.where(row >= k, pltpu.roll(x, k, 0), 0.0)
        k *= 2
    return x


def _gla_kernel(q_ref, k_ref, v_ref, la_ref, s0_ref, *rest, reverse, final, nchunk, group):
    if final:
        of_ref, o_ref, sfin_ref, st_ref = rest
    else:
        o_ref, sfin_ref, st_ref = rest
    i = pl.program_id(1)
    c = GLA_CHUNK

    @pl.when(i == 0)
    def _():
        st_ref[...] = s0_ref[...]

    row = lax.broadcasted_iota(jnp.int32, (c, MIX), 0)
    lane = lax.broadcasted_iota(jnp.int32, (c, MIX), 1) % c
    keep = (lane >= row) if reverse else (lane <= row)
    lane1 = lax.broadcasted_iota(jnp.int32, (1, MIX), 1) // c
    hmask = [(lane1 == h).astype(F32) for h in range(GLA_HEADS)]
    hmask_b = [m.astype(BF) for m in hmask]
    bdiag = (lax.broadcasted_iota(jnp.int32, (MIX, MIX), 0) // c
             == lax.broadcasted_iota(jnp.int32, (MIX, MIX), 1) // c)

    def chunk(t, carry):
        cc = (nchunk - 1 - t) if reverse else t
        rs = pl.ds(pl.multiple_of(cc * c, c), c)

        def front(g):
            cum = _chunk_cumsum(la_ref[g, rs, :], row, reverse)
            tot = cum[0:1] if reverse else cum[c - 1:c]
            q, k, v = q_ref[g, rs, :], k_ref[g, rs, :], v_ref[g, rs, :]
            qd = (q * jnp.exp(cum)).astype(BF)
            ki = k * jnp.exp(-cum)
            kd = (k * jnp.exp(tot - cum)).astype(BF)
            kstack = jnp.concatenate([(ki * m).astype(BF) for m in hmask], axis=0)
            vstack = jnp.concatenate([v * m for m in hmask_b], axis=0)
            return qd, vstack, _dot_nt(qd, kstack), _dot_tn(v, kd), jnp.exp(tot)

        def back(g, qd, vstack, scores, upd, decay):
            att = jnp.where(keep, scores, 0.0).astype(BF)
            st = st_ref[g]
            o = _dot(att, vstack) + _dot_nt(qd, st.astype(BF))
            st_ref[g] = st * decay + jnp.where(bdiag, upd, 0.0)
            if final:
                o = o + of_ref[g, rs, :]
            o_ref[g, rs, :] = o

        pending = front(0)
        for g in range(group):
            cur = pending
            if g + 1 < group:
                pending = front(g + 1)
            back(g, *cur)
        return carry

    lax.fori_loop(0, nchunk, chunk, 0, unroll=2)

    @pl.when(i == pl.num_programs(1) - 1)
    def _():
        sfin_ref[...] = st_ref[...]


def _gla_call(q, k, v, la, s0, s0_idx, extra, batch, seq, reverse, tb, group=GLA_GROUP):
    nblk = seq // tb
    final = extra is not None
    pos = (lambda i: nblk - 1 - i) if reverse else (lambda i: i)
    blk = pl.BlockSpec((group, tb, MIX), lambda b, i: (b, pos(i), 0))
    lblk = pl.BlockSpec((group, tb, MIX), lambda b, i: (b, pos(i), 1 if reverse else 0))
    sblk = pl.BlockSpec((group, MIX, MIX), lambda b, i: (b, 0, 0))
    s0blk = pl.BlockSpec((group, None, None, MIX, MIX), lambda b, i: (b,) + tuple(s0_idx) + (0, 0))
    r3 = lambda a: a.reshape(batch, seq, a.shape[-1])
    in_specs = [blk, blk, blk, lblk, s0blk]
    args = [r3(q), r3(k), r3(v), r3(la), s0]
    if final:
        in_specs.append(blk)
        args.append(r3(extra))
    o, st = pl.pallas_call(
        functools.partial(_gla_kernel, reverse=reverse, final=final, nchunk=tb // GLA_CHUNK, group=group),
        grid=(batch // group, nblk),
        in_specs=in_specs,
        out_specs=[blk, sblk],
        out_shape=[jax.ShapeDtypeStruct((batch, seq, MIX), F32),
                   jax.ShapeDtypeStruct((batch, MIX, MIX), F32)],
        scratch_shapes=[pltpu.VMEM((group, MIX, MIX), F32)],
        compiler_params=_params("parallel", "arbitrary"),
        name="gla_bwd" if reverse else "gla_fwd",
    )(*args)
    return o.reshape(batch * seq, MIX), st


def _swa_kernel(*refs, local, layer):
    sink_ref, q_ref = refs[:2]
    pos = 2
    if local:
        kp_ref, kc_ref, kn_ref, vp_ref, vc_ref, vn_ref = refs[pos:pos + 6]
        pos += 6
    kx_ref, vx_ref, o_ref = refs[pos:pos + 3]
    i = pl.program_id(1)
    last = pl.num_programs(1) - 1
    blk = SWA_BLOCK
    tq = q_ref.shape[0]
    cx = kx_ref.shape[0]
    kx = kx_ref[...].astype(BF)
    vx = vx_ref[...].astype(BF)
    if local:
        nsub = tq // blk
        kblocks = [kp_ref[...]] + [kc_ref[j * blk:(j + 1) * blk, :] for j in range(nsub)] + [kn_ref[...]]
        vblocks = [vp_ref[...]] + [vc_ref[j * blk:(j + 1) * blk, :] for j in range(nsub)] + [vn_ref[...]]
        row = lax.broadcasted_iota(jnp.int32, (blk, blk), 0)
        col = lax.broadcasted_iota(jnp.int32, (blk, blk), 1)
        full = jnp.full((blk, blk), True)
        fullx = jnp.full((blk, cx), True)
    else:
        nsub = 1
    rq = tq // nsub
    rows = lax.broadcasted_iota(jnp.int32, (2 * rq, 1), 0)
    low = lax.broadcasted_iota(jnp.int32, (rq, LANES), 1) < HEAD_DIM
    units = []
    for j in range(nsub):
        if local:
            kcat = jnp.concatenate([kx, kblocks[j], kblocks[j + 1], kblocks[j + 2]], axis=0)
            vcat = jnp.concatenate([vx, vblocks[j], vblocks[j + 1], vblocks[j + 2]], axis=0)
            prev_ok = (col >= row) if j > 0 else jnp.logical_and(col >= row, i > 0)
            next_ok = (col <= row) if j < nsub - 1 else jnp.logical_and(col <= row, i < last)
            keep = jnp.concatenate([fullx, prev_ok, full, next_ok], axis=1)
            keep = jnp.concatenate([keep, keep], axis=0)
        else:
            kcat, vcat, keep = kx, vx, None
        for g in range(SWA_KV_HEADS):
            units.append((j, g, kcat, vcat, keep))

    def scores(u):
        j, g, kcat, _, _ = units[u]
        rs = slice(j * rq, (j + 1) * rq)
        q2 = jnp.concatenate([q_ref[rs, (2 * g) * LANES:(2 * g + 1) * LANES],
                              q_ref[rs, (2 * g + 1) * LANES:(2 * g + 2) * LANES]], axis=0)
        return _dot_nt(q2, kcat)

    tiles = {}
    s_next = scores(0)
    for u, (j, g, _, vcat, keep) in enumerate(units):
        s = s_next
        if u + 1 < len(units):
            s_next = scores(u + 1)
        sink = jnp.where(rows < rq, sink_ref[layer, 2 * g], sink_ref[layer, 2 * g + 1]) * LOG2E
        if local:
            s = jnp.where(keep, s, NEG_INF)
        m = jnp.maximum(jnp.max(s, axis=-1, keepdims=True), sink)
        e = jnp.exp2(s - m)
        den = jnp.sum(e, axis=-1, keepdims=True) + jnp.exp2(sink - m)
        o2 = _dot(e.astype(BF), vcat) / den
        oa, ob = o2[:rq], o2[rq:]
        if g == 0:
            tiles[(j, g)] = jnp.where(low, oa, pltpu.roll(ob, HEAD_DIM, 1))
        else:
            tiles[(j, g)] = jnp.where(low, pltpu.roll(oa, HEAD_DIM, 1), ob)
            rs = slice(j * rq, (j + 1) * rq)
            o_ref[rs, :LANES] = tiles[(j, 0)].astype(BF)
            o_ref[rs, LANES:] = tiles[(j, 1)].astype(BF)


def _swa_call(sinks, layer, q, k, v, kx, vx, x_layer, batch, seq, local, tq=256):
    nt = seq // tq
    per = tq // SWA_BLOCK
    nb = seq // SWA_BLOCK
    cx = kx.shape[2]
    qspec = pl.BlockSpec((tq, 2 * MIX), lambda b, i: (b * nt + i, 0))
    in_specs = [pl.BlockSpec(memory_space=pltpu.SMEM), qspec]
    args = [sinks, q]
    if local:
        prv = pl.BlockSpec((SWA_BLOCK, LANES), lambda b, i: (b * nb + jnp.maximum(i * per - 1, 0), 0))
        cur = pl.BlockSpec((tq, LANES), lambda b, i: (b * nt + i, 0))
        nxt = pl.BlockSpec((SWA_BLOCK, LANES), lambda b, i: (b * nb + jnp.minimum((i + 1) * per, nb - 1), 0))
        in_specs += [prv, cur, nxt, prv, cur, nxt]
        args += [k, k, k, v, v, v]
    xspec = pl.BlockSpec((None, None, cx, LANES), lambda b, i: (b, x_layer, 0, 0))
    in_specs += [xspec, xspec]
    args += [kx, vx]
    return pl.pallas_call(
        functools.partial(_swa_kernel, local=local, layer=layer),
        grid=(batch, nt),
        in_specs=in_specs,
        out_specs=pl.BlockSpec((tq, MIX), lambda b, i: (b * nt + i, 0)),
        out_shape=jax.ShapeDtypeStruct((batch * seq, MIX), BF),
        compiler_params=_params("parallel", "parallel"),
        name="swa_lat" if local else "swa_ctx",
    )(*args)


def _fft_kernel(xc_ref, xs_ref, yc_ref, ys_ref, zc_ref, zs_ref, o_ref, *, nj, scale):
    yc, ys = yc_ref[...], ys_ref[...]
    xc, xs = xc_ref[...], xs_ref[...]
    acc = None
    for jp in range(nj // 2):
        cos_t, sin_t = [], []
        for j in (2 * jp, 2 * jp + 1):
            a, b = xc[:, j:j + 1], xs[:, j:j + 1]
            cos_t.append(a * yc - b * ys)
            sin_t.append(b * yc + a * ys)
        cos_t = jnp.concatenate(cos_t, axis=1).astype(BF)
        sin_t = jnp.concatenate(sin_t, axis=1).astype(BF)
        rows = slice(jp * MXU_DEPTH, (jp + 1) * MXU_DEPTH)
        d = _dot(cos_t, zc_ref[rows, :]) + _dot(sin_t, zs_ref[rows, :])
        acc = d if acc is None else acc + d
    o_ref[...] = (acc * scale).astype(BF)


def _fft_tables(seq):
    sp = jnp.arange(seq, dtype=jnp.int32)[:, None]
    w = 2.0 * jnp.pi / seq
    ax = ((sp * (jnp.arange(seq // LANES, dtype=jnp.int32) * LANES)[None, :]) % seq).astype(F32) * w
    ay = ((sp * jnp.arange(LANES, dtype=jnp.int32)[None, :]) % seq).astype(F32) * w
    return jnp.cos(ax), jnp.sin(ax), jnp.cos(ay), jnp.sin(ay)


def _fft_call(tables, zc, zs, seq, tm=256):
    xc, xs, yc, ys = tables
    nj = seq // LANES
    width = zc.shape[1]
    return pl.pallas_call(
        functools.partial(_fft_kernel, nj=nj, scale=(seq * FNET_GROUP_CH) ** -0.5),
        grid=(seq // tm,),
        in_specs=[
            pl.BlockSpec((tm, nj), lambda i: (i, 0)), pl.BlockSpec((tm, nj), lambda i: (i, 0)),
            pl.BlockSpec((tm, LANES), lambda i: (i, 0)), pl.BlockSpec((tm, LANES), lambda i: (i, 0)),
            _pick(zc), _pick(zs),
        ],
        out_specs=pl.BlockSpec((tm, width), lambda i: (i, 0)),
        out_shape=jax.ShapeDtypeStruct((seq, width), BF),
        compiler_params=_params("parallel"),
        name="fft",
    )(xc, xs, yc, ys, zc, zs)


def _fft2_kernel(zc_ref, zs_ref, m1_ref, m2_ref, tr_ref, ti_ref, o_ref, x_ref, *, nb1, scale):
    i = pl.program_id(1)
    n = m2_ref.shape[0]
    fb = FFT_BLOCK

    @pl.when(i < nb1)
    def _():
        m1 = m1_ref[...]
        for j in range(fb):
            z = jnp.concatenate([zc_ref[:, j, :], zs_ref[:, j, :]], axis=0).astype(BF)
            x1 = _dot(m1, z)
            xr, xi = x1[:n], x1[n:]
            tr, ti = tr_ref[:, j:j + 1], ti_ref[:, j:j + 1]
            x2 = jnp.concatenate([xr * tr - xi * ti, xr * ti + xi * tr], axis=0)
            x_ref[i * fb + j] = x2.reshape(2 * n // 8, 8, x2.shape[-1])

    @pl.when(i >= nb1)
    def _():
        m2 = m2_ref[...]
        blk = (i - nb1) * (fb // 8)
        for j in range(fb):
            z = jnp.concatenate([x_ref[:, blk + j // 8, j % 8, :],
                                 x_ref[:, n // 8 + blk + j // 8, j % 8, :]], axis=0).astype(BF)
            o_ref[:, j, :] = _dot(m2, z) * scale


def _fft2_call(zc, zs, seq, wl=512):
    n = int(round(seq ** 0.5))
    fb = FFT_BLOCK
    assert seq == n * n and n % fb == 0
    width = zc.shape[1]
    nb = n // fb
    k = jnp.arange(n, dtype=jnp.int32)
    ang = ((k[:, None] * k[None, :]) % n).astype(F32) * (2.0 * jnp.pi / n)
    c, sn = jnp.cos(ang), jnp.sin(ang)
    m1 = jnp.concatenate([jnp.concatenate([c, sn], axis=1), jnp.concatenate([-sn, c], axis=1)], axis=0).astype(BF)
    m2 = jnp.concatenate([c, sn], axis=1).astype(BF)
    tw = (k[:, None] * k[None, :]).astype(F32) * (2.0 * jnp.pi / seq)
    tr = jnp.cos(tw).reshape(n, nb, fb).transpose(1, 0, 2)
    ti = (-jnp.sin(tw)).reshape(n, nb, fb).transpose(1, 0, 2)
    zspec = pl.BlockSpec((n, fb, wl), lambda h, i: (0, jnp.minimum(i, nb - 1), h))
    tspec = pl.BlockSpec((None, n, fb), lambda h, i: (jnp.minimum(i, nb - 1), 0, 0))
    out = pl.pallas_call(
        functools.partial(_fft2_kernel, nb1=nb, scale=(seq * FNET_GROUP_CH) ** -0.5),
        grid=(width // wl, 2 * nb),
        in_specs=[zspec, zspec, _pick(m1), _pick(m2), tspec, tspec],
        out_specs=pl.BlockSpec((n, fb, wl), lambda h, i: (0, jnp.maximum(i - nb, 0), h)),
        out_shape=jax.ShapeDtypeStruct((n, n, width), F32),
        scratch_shapes=[pltpu.VMEM((n, 2 * n // 8, 8, wl), F32)],
        compiler_params=_params("parallel", "arbitrary"),
        name="fft2",
    )(zc.reshape(n, n, width), zs.reshape(n, n, width), m1, m2, tr, ti)
    return out.reshape(seq, width)


def _mla_kernel(*refs, has_ctx):
    qc_ref, qr_ref, ks_ref = refs[:3]
    pos = 3
    if has_ctx:
        kx_ref = refs[pos]
        pos += 1
    wv_ref, o_ref = refs[pos:pos + 2]
    qr = qr_ref[...]
    lane = lax.broadcasted_iota(jnp.int32, qr.shape, 1) // MLA_ROPE
    tq = qr.shape[0]
    qs = []
    for h in range(MLA_HEADS):
        qrh = jnp.where(lane == h, qr, jnp.zeros_like(qr))
        qs.append(jnp.concatenate([qc_ref[:, h * LANES:(h + 1) * LANES], qrh], axis=1))
    qall = jnp.concatenate(qs, axis=0)
    tiles = [(kx_ref, 0, kx_ref.shape[0])] if has_ctx else []
    nk = ks_ref.shape[0]
    tk = min(nk, MLA_KEY_TILE)
    tiles += [(ks_ref, j * tk, tk) for j in range(nk // tk)]
    m = den = acc = None
    for ref, lo, n in tiles:
        kt = ref[lo:lo + n, :]
        s = _dot_nt(qall, kt)
        mt = jnp.max(s, axis=-1, keepdims=True)
        if m is None:
            m_new = mt
        else:
            m_new = jnp.maximum(m, mt)
            alpha = jnp.exp2(m - m_new)
        p = jnp.exp2(s - m_new)
        pv = _dot(p.astype(BF), kt[:, :LANES])
        ps = jnp.sum(p, axis=-1, keepdims=True)
        if m is None:
            den, acc = ps, pv
        else:
            den = alpha * den + ps
            acc = alpha * acc + pv
        m = m_new
    ot = acc / den
    ocat = jnp.concatenate([ot[h * tq:(h + 1) * tq] for h in range(MLA_HEADS)], axis=1).astype(BF)
    o_ref[...] = _dot(ocat, wv_ref[...]).astype(BF)


def _mla_call(qc, qr, kcat, kx, wvbd, l, batch, seq, tq=256):
    nq = seq // tq
    has_ctx = kx is not None
    in_specs = [pl.BlockSpec((tq, 2 * MIX), lambda b, i: (b * nq + i, 0)),
                pl.BlockSpec((tq, LANES), lambda b, i: (b * nq + i, 0)),
                pl.BlockSpec((None, seq, MIX), lambda b, i: (b, 0, 0))]
    args = [qc, qr, kcat.reshape(batch, seq, MIX)]
    if has_ctx:
        in_specs.append(pl.BlockSpec((None, None, kx.shape[2], MIX), lambda b, i: (b, l, 0, 0)))
        args.append(kx)
    in_specs.append(_pick(wvbd, l))
    args.append(wvbd)
    return pl.pallas_call(
        functools.partial(_mla_kernel, has_ctx=has_ctx),
        grid=(batch, nq),
        in_specs=in_specs,
        out_specs=pl.BlockSpec((tq, MIX), lambda b, i: (b * nq + i, 0)),
        out_shape=jax.ShapeDtypeStruct((batch * seq, MIX), BF),
        compiler_params=_params("parallel", "parallel"),
        name="mla_lat" if has_ctx else "mla_ctx",
    )(*args)


def _rot_cols(w, width):
    half = width // 2
    parts = w.reshape(w.shape[:-1] + (w.shape[-1] // width, 2, half))
    return jnp.stack([-parts[..., 1, :], parts[..., 0, :]], axis=-2).reshape(w.shape)


def _block_diag(blocks):
    n, r, c = blocks.shape[-3:]
    eye = jnp.eye(n, dtype=blocks.dtype)
    out = jnp.einsum("...hrc,hg->...hrgc", blocks, eye)
    return out.reshape(blocks.shape[:-3] + (n * r, n * c))


def _layer_weights(w_in, gla_w_gate, gla_b_gate, mla_w_q_b, mla_w_kv_b):
    nl = w_in.shape[0]
    w_in = w_in.astype(BF)
    offs = [0]
    for n in (256, 256, 256, 256, 16, 16, 256, 128, 128, 256, 256, 128, 32):
        offs.append(offs[-1] + n)
    seg = lambda k: w_in[..., offs[k]:offs[k + 1]]
    zeros = lambda n: jnp.zeros((nl, D_MODEL, n), w_in.dtype)
    sq = seg(6).reshape(nl, D_MODEL, SWA_Q_HEADS, HEAD_DIM)
    sq_tiles = []
    for hq in range(SWA_Q_HEADS):
        parts = [zeros(HEAD_DIM), zeros(HEAD_DIM)]
        parts[hq // 2] = sq[:, :, hq]
        sq_tiles += parts
    sq_w = jnp.concatenate(sq_tiles, axis=-1)
    kr4 = jnp.tile(seg(12), (1, 1, MLA_HEADS))
    cols = [seg(0), seg(1), seg(2), seg(3), seg(4), seg(5), zeros(LANES - 2 * GLA_GATE_RANK), seg(7),
            seg(8), seg(11), sq_w, seg(9), seg(10), kr4,
            _rot_cols(seg(7), HEAD_DIM // 2), _rot_cols(sq_w, HEAD_DIM // 2), _rot_cols(kr4, MLA_ROPE // 2)]
    w = jnp.concatenate(cols, axis=-1)

    wgate = jnp.zeros((nl, LANES, 2 * MIX), F32)
    wgate = wgate.at[:, :GLA_GATE_RANK, :MIX].set(gla_w_gate[:, 0])
    wgate = wgate.at[:, GLA_GATE_RANK:2 * GLA_GATE_RANK, MIX:].set(gla_w_gate[:, 1])
    bgate = gla_b_gate.reshape(nl, 1, 2 * MIX)

    wq = mla_w_q_b.reshape(nl, MLA_Q_RANK, MLA_HEADS, MLA_NOPE + MLA_ROPE)
    q_nope = wq[..., :MLA_NOPE].reshape(nl, MLA_Q_RANK, MLA_HEADS * MLA_NOPE)
    q_rope = wq[..., MLA_NOPE:].reshape(nl, MLA_Q_RANK, MLA_HEADS * MLA_ROPE)
    wqb = jnp.concatenate([q_nope, q_rope, _rot_cols(q_rope, MLA_ROPE // 2)], axis=-1).astype(BF)
    wkv = mla_w_kv_b.reshape(nl, MLA_KV_RANK, MLA_HEADS, MLA_NOPE + MLA_V)
    wkbd = _block_diag(jnp.transpose(wkv[..., :MLA_NOPE], (0, 2, 3, 1))).astype(BF)
    wvbd = _block_diag(jnp.transpose(wkv[..., MLA_NOPE:], (0, 2, 1, 3))).astype(BF)
    return (w, wgate.astype(BF), bgate, wqb, wkbd), wvbd


def _rope_tables(seq):
    t = jnp.arange(seq)
    rows = (t // GRID_W).astype(F32)[:, None]
    cols = (t % GRID_W).astype(F32)[:, None]

    def table(width):
        half = width // 2
        lane = jnp.arange(LANES)
        inv = ROPE_BASE ** (-(2 * (lane % half)).astype(F32) / width)
        ang = jnp.where(((lane // width) % 2 == 0)[None, :], rows, cols) * inv[None, :]
        return jnp.cos(ang), jnp.sin(ang)

    cs, ss = table(HEAD_DIM // 2)
    cm, sm = table(MLA_ROPE // 2)
    return cs, ss, cm, sm


def _dft_channel():
    c = jnp.arange(FNET_GROUP_CH, dtype=jnp.int32)
    ang = ((c[:, None] * c[None, :]) % FNET_GROUP_CH).astype(F32) * (2.0 * jnp.pi / FNET_GROUP_CH)
    cos_b = jnp.broadcast_to(jnp.cos(ang)[None], (FNET_GROUPS, FNET_GROUP_CH, FNET_GROUP_CH))
    sin_b = jnp.broadcast_to(jnp.sin(ang)[None], (FNET_GROUPS, FNET_GROUP_CH, FNET_GROUP_CH))
    return _block_diag(cos_b).astype(BF), (-_block_diag(sin_b)).astype(BF)


def _state_in(s):
    return _block_diag(jnp.swapaxes(s, -1, -2))


def _state_out(st):
    blocks = [st[:, h * GLA_DV:(h + 1) * GLA_DV, h * GLA_DK:(h + 1) * GLA_DK] for h in range(GLA_HEADS)]
    return jnp.swapaxes(jnp.stack(blocks, axis=1), 2, 3)


def _token_mix(x, l, shared, st0, st0_idx, cache, batch, seq, rows_per_batch, first_row, latent):
    mod, gn, wts, wvbd, gq, gkv, sinks, cdft, sdft, rope_t, fft_t = shared
    outs = _inproj_call(x, mod, gn, wts, gq, gkv, cdft, sdft, rope_t, l, batch, seq, rows_per_batch,
                        first_row, latent, tm=512)
    gqv, gkv_, gv, gr, la, sq, sk, sv, zc, zs, qc, qr, kcat = outs[:13]
    tb = min(seq, GLA_BLOCK)
    o_f, st_f = _gla_call(gqv, gkv_, gv, la, st0, st0_idx[0], None, batch, seq, False, tb)
    o_gla, st_b = _gla_call(gqv, gkv_, gv, la, st0, st0_idx[1], o_f, batch, seq, True, tb)
    if latent:
        kx, vx, mx = cache
        o_swa = _swa_call(sinks, l, sq, sk, sv, kx, vx, l, batch, seq, True, tq=1024)
        o_mla = _mla_call(qc, qr, kcat, mx, wvbd, l, batch, seq)
        new_ctx = None
    else:
        k4 = sk.reshape(batch, 1, seq, LANES)
        v4 = sv.reshape(batch, 1, seq, LANES)
        o_swa = _swa_call(sinks, l, sq, None, None, k4, v4, 0, batch, seq, False)
        o_mla = _mla_call(qc, qr, kcat, None, wvbd, l, batch, seq)
        new_ctx = (st_f, st_b, sk, sv, outs[13], outs[14])
    o_fft = _fft2_call(zc, zs, seq) if latent else _fft_call(fft_t, zc, zs, seq)
    return (o_gla, gr, o_swa, o_fft, o_mla), new_ctx


def kernel(x_prompt, x_sample, c, state_gla, cache_swa_k, cache_swa_v, cache_mla_ckv, cache_mla_krope,
           c_ctx, w_mod, b_mod, g_norm, w_ffn_gate, w_ffn_up, w_ffn_down, w_in, gla_w_gate, gla_b_gate,
           gla_g_out, swa_sink, mla_g_q, mla_g_kv, mla_w_q_b, mla_w_kv_b, w_out):
    nb, ns, _ = x_prompt.shape
    db, dsq, _ = x_sample.shape
    past = cache_swa_k.shape[2]

    cvec = jnp.zeros((MOD_ROWS, D_MODEL), F32).at[0].set(c_ctx).at[1:1 + db].set(c)
    mod = _mod_call(cvec, w_mod, b_mod).reshape(DEPTH, MOD_ROWS, N_MOD, 1, D_MODEL)
    gn = g_norm.reshape(DEPTH, 6, 1, D_MODEL)

    wg, wu, wd, wo = (a.astype(BF) for a in (w_ffn_gate, w_ffn_up, w_ffn_down, w_out))
    wts, wvbd = _layer_weights(w_in, gla_w_gate, gla_b_gate, mla_w_q_b, mla_w_kv_b)
    gq = mla_g_q.reshape(DEPTH, 1, MLA_Q_RANK)
    gkv = mla_g_kv.reshape(DEPTH, 1, MLA_KV_RANK)
    gout = jnp.tile(gla_g_out, (1, GLA_HEADS)).reshape(DEPTH, 1, MIX)
    cdft, sdft = _dft_channel()
    shared = (mod, gn, wts, wvbd, gq, gkv, swa_sink, cdft, sdft)
    head = jnp.arange(MIX) // GLA_DV
    avg = jnp.where(head[:, None] == head[None, :], 1.0 / GLA_DV, 0.0).astype(BF)
    shared_ctx = shared + (None, _fft_tables(ns))
    shared_lat = shared + (_rope_tables(dsq), None)

    st_lat = _state_in(state_gla)
    st_zero = jnp.zeros((nb, 1, 1, MIX, MIX), F32)
    kx = cache_swa_k.reshape(db, DEPTH, past, LANES)
    vx = cache_swa_v.reshape(db, DEPTH, past, LANES)
    mx = jnp.concatenate([cache_mla_ckv, jnp.tile(cache_mla_krope, (1, 1, 1, MLA_HEADS))], axis=-1).astype(BF)

    xc = x_prompt.reshape(nb * ns, D_MODEL)
    xl = x_sample.reshape(db * dsq, D_MODEL)
    st_gla, st_k, st_v, st_ckv, st_kr = [], [], [], [], []
    for l in range(DEPTH):
        xc = _ffn_call(xc, mod, gn, wg, wu, wd, l, rows_per_batch=nb * ns, first_row=0)
        mixed, new_ctx = _token_mix(xc, l, shared_ctx, st_zero, ((0, 0), (0, 0)), None, nb, ns, nb * ns, 0, False)
        xc = _outffn_call(*mixed, xc, mod, gn, gout, avg, wo, wg, wu, wd, l, seq=ns, rows_per_batch=nb * ns,
                          first_row=0, tm=512)
        st_f, st_b, k_c, v_c, ckv_c, kr_c = new_ctx
        st_gla.append(jnp.stack([_state_out(st_f), _state_out(st_b)], axis=1))
        st_k.append(k_c.reshape(nb, ns, SWA_KV_HEADS, HEAD_DIM))
        st_v.append(v_c.reshape(nb, ns, SWA_KV_HEADS, HEAD_DIM))
        st_ckv.append(ckv_c.reshape(nb, ns, MLA_KV_RANK))
        st_kr.append(kr_c.reshape(nb, ns, MLA_ROPE))

        xl = _ffn_call(xl, mod, gn, wg, wu, wd, l, rows_per_batch=dsq, first_row=1)
        mixed, _ = _token_mix(xl, l, shared_lat, st_lat, ((l, 0), (l, 1)), (kx, vx, mx), db, dsq, dsq, 1, True)
        xl = _outffn_call(*mixed, xl, mod, gn, gout, avg, wo, wg, wu, wd, l, seq=dsq, rows_per_batch=dsq,
                          first_row=1, tm=512)

    return (xc.reshape(nb, ns, D_MODEL), xl.reshape(db, dsq, D_MODEL), jnp.stack(st_gla, axis=1),
            jnp.stack(st_k, axis=1), jnp.stack(st_v, axis=1), jnp.stack(st_ckv, axis=1),
            jnp.stack(st_kr, axis=1))
```

```python
import functools

import jax
import jax.numpy as jnp
from jax import lax
from jax.experimental import pallas as pl
from jax.experimental.pallas import tpu as pltpu

D_MODEL = 1024
DEPTH = 4
GRID_W = 64
HEAD_DIM = 64
GLA_HEADS = 4
GLA_DK = 64
GLA_DV = 64
GLA_GATE_RANK = 16
GLA_TAU = 16.0
GLA_CHUNK = 64
SWA_Q_HEADS = 4
SWA_KV_HEADS = 2
SWA_BLOCK = 128
FNET_GROUPS = 4
FNET_GROUP_CH = 64
MLA_HEADS = 4
MLA_Q_RANK = 256
MLA_KV_RANK = 128
MLA_NOPE = 64
MLA_ROPE = 32
MLA_V = 64
D_FF = 2816
FFN_RES = 0.5
N_MOD = 9
ROPE_BASE = 10000.0
EPS = 1e-6
NEG_INF = -1e30

MIX = 256
LANES = 128
MXU_DEPTH = 256
FF_CHUNK = 256
MLA_KEY_TILE = 2048
FFT_BLOCK = 16
GLA_BLOCK = 512
GLA_GROUP = 4
LOG2E = 1.4426950408889634
MOD_ROWS = 8
VMEM_LIMIT = 56 * 1024 * 1024

BF = jnp.bfloat16
F32 = jnp.float32

C_GQ, C_GK, C_GV, C_GR = 0, 256, 512, 768
C_ASK, C_SVKVA, C_SQ, C_ZF, C_QA, C_KR = 1024, 1280, 1536, 2048, 2304, 2560
C_SKR, C_SQR, C_KRR = 2688, 2816, 3328
NC_CTX, NC_LAT = 2688, 3456


def _dot(a, b):
    return jnp.dot(a, b, preferred_element_type=F32)


def _dot_nt(a, b):
    return lax.dot_general(a, b, (((1,), (1,)), ((), ())), preferred_element_type=F32)


def _dot_tn(a, b):
    return lax.dot_general(a, b, (((0,), (0,)), ((), ())), preferred_element_type=F32)


def _rms(x, g):
    return x * lax.rsqrt(jnp.mean(x * x, axis=-1, keepdims=True) + EPS) * g


def _silu(x):
    return x * jax.nn.sigmoid(x)


def _params(*sem):
    return pltpu.CompilerParams(dimension_semantics=sem, vmem_limit_bytes=VMEM_LIMIT)


def _pick(arr, *lead, block=None):
    tail = tuple(arr.shape[len(lead):]) if block is None else tuple(block)
    zeros = (0,) * len(tail)
    return pl.BlockSpec((None,) * len(lead) + tail, lambda *_: tuple(lead) + zeros,
                        pipeline_mode=pl.Buffered(1))


def _mod_kernel(c_ref, w_ref, b_ref, o_ref):
    s = _silu(c_ref[...]).astype(BF)
    o_ref[...] = _dot(s, w_ref[...].astype(BF)) + b_ref[...]


def _mod_call(cvec, w_mod, b_mod):
    nl = w_mod.shape[0]
    tn = D_MODEL
    return pl.pallas_call(
        _mod_kernel,
        grid=(nl, N_MOD * D_MODEL // tn),
        in_specs=[
            pl.BlockSpec((MOD_ROWS, D_MODEL), lambda l, j: (0, 0)),
            pl.BlockSpec((None, D_MODEL, tn), lambda l, j: (l, 0, j)),
            pl.BlockSpec((None, 1, tn), lambda l, j: (l, 0, j)),
        ],
        out_specs=pl.BlockSpec((None, MOD_ROWS, tn), lambda l, j: (l, 0, j)),
        out_shape=jax.ShapeDtypeStruct((nl, MOD_ROWS, N_MOD * D_MODEL), F32),
        compiler_params=_params("parallel", "parallel"),
        name="mod",
    )(cvec, w_mod, b_mod.reshape(nl, 1, N_MOD * D_MODEL))


def _mod_spec(l, sub, tm, rows_per_batch, first_row):
    per = rows_per_batch // tm
    return pl.BlockSpec((None, None, 3, 1, D_MODEL), lambda i: (l, first_row + i // per, sub, 0, 0))


def _fft_layout_spec(tm, seq):
    per = max(seq // tm, 1)
    nbat = max(tm // seq, 1)
    return pl.BlockSpec((tm // nbat, nbat * MIX), lambda i: (i % per, i // per))


def _gn_spec(l, first, n):
    return pl.BlockSpec((None, n, 1, D_MODEL), lambda i: (l, first // n, 0, 0))


def _ffn_body(x, mod_ref, g_pre, g_post, wg_ref, wu_ref, wd_ref):
    sh, sc, gt = mod_ref[0], mod_ref[1], mod_ref[2]
    hb = (_rms(x, g_pre) * (1.0 + sc) + sh).astype(BF)
    acc = None
    for j in range(D_FF // FF_CHUNK):
        sl = slice(j * FF_CHUNK, (j + 1) * FF_CHUNK)
        g = _dot(hb, wg_ref[:, sl])
        u = _dot(hb, wu_ref[:, sl])
        d = _dot((_silu(g) * u).astype(BF), wd_ref[sl, :])
        acc = d if acc is None else acc + d
    return x + FFN_RES * gt * _rms(acc, g_post)


def _retarget(spec, f):
    return pl.BlockSpec(spec.block_shape, lambda i: spec.index_map(f(i)))


def _two_groups(nc):
    return (lambda i: jnp.minimum(i, nc - 1)), (lambda i: jnp.maximum(i - nc, 0))


def _mod_spec2(l, sub, nc, per):
    return pl.BlockSpec((None, None, 3, 1, D_MODEL),
                        lambda i: (l, jnp.where(i < nc, 0, 1 + jnp.maximum(i - nc, 0) // per), sub, 0, 0))


def _ffn_kernel(xc_ref, xl_ref, mod_ref, gn_ref, wg_ref, wu_ref, wd_ref, oc_ref, ol_ref, *, nc):
    i = pl.program_id(0)

    @pl.when(i < nc)
    def _():
        oc_ref[...] = _ffn_body(xc_ref[...], mod_ref, gn_ref[0], gn_ref[1], wg_ref, wu_ref, wd_ref)

    @pl.when(i >= nc)
    def _():
        ol_ref[...] = _ffn_body(xl_ref[...], mod_ref, gn_ref[0], gn_ref[1], wg_ref, wu_ref, wd_ref)


def _ffn_call(xc, xl, mod, gn, wg, wu, wd, l, lat_seq, tm=512):
    nc, nl = xc.shape[0] // tm, xl.shape[0] // tm
    fc, fl = _two_groups(nc)
    row = pl.BlockSpec((tm, D_MODEL), lambda i: (i, 0))
    return pl.pallas_call(
        functools.partial(_ffn_kernel, nc=nc),
        grid=(nc + nl,),
        in_specs=[
            _retarget(row, fc), _retarget(row, fl),
            _mod_spec2(l, 0, nc, lat_seq // tm),
            _gn_spec(l, 0, 2),
            _pick(wg, l, 0), _pick(wu, l, 0), _pick(wd, l, 0),
        ],
        out_specs=[_retarget(row, fc), _retarget(row, fl)],
        out_shape=[jax.ShapeDtypeStruct(xc.shape, F32), jax.ShapeDtypeStruct(xl.shape, F32)],
        compiler_params=_params("arbitrary"),
        name="ffn",
    )(xc, xl, mod, gn, wg, wu, wd)


def _outffn_body(og_ref, gr_ref, os_ref, of_ref, om_ref, x_ref, mod2_ref, mod3_ref, gn_ref, gout_ref,
                 avg_ref, wo_ref, wg_ref, wu_ref, wd_ref):
    og = og_ref[...]
    ms = _dot((og * og).astype(BF), avg_ref[...])
    og = (og * lax.rsqrt(ms + EPS) * gout_ref[...] * gr_ref[...]).astype(BF)
    nbat = of_ref.shape[1] // MIX
    of = jnp.concatenate([of_ref[:, j * MIX:(j + 1) * MIX] for j in range(nbat)], axis=0).astype(BF)
    mix = jnp.concatenate([og, os_ref[...], of, om_ref[...]], axis=1)
    x = x_ref[...] + mod2_ref[2] * _rms(_dot(mix, wo_ref[...]), gn_ref[0])
    return _ffn_body(x, mod3_ref, gn_ref[1], gn_ref[2], wg_ref, wu_ref, wd_ref)


def _outffn_kernel(*refs, nc):
    ctx, lat, rest = refs[:6], refs[6:12], refs[12:]
    shared, (oc_ref, ol_ref) = rest[:-2], rest[-2:]
    i = pl.program_id(0)

    @pl.when(i < nc)
    def _():
        oc_ref[...] = _outffn_body(*ctx, *shared)

    @pl.when(i >= nc)
    def _():
        ol_ref[...] = _outffn_body(*lat, *shared)


def _outffn_call(mix_c, xc, mix_l, xl, mod, gn, gout, avg, wo, wg, wu, wd, l, ctx_seq, lat_seq, tm=512):
    nc, nl = xc.shape[0] // tm, xl.shape[0] // tm
    fc, fl = _two_groups(nc)
    mspec = pl.BlockSpec((tm, MIX), lambda i: (i, 0))
    row = pl.BlockSpec((tm, D_MODEL), lambda i: (i, 0))

    def group_specs(seq, f):
        return [_retarget(sp, f) for sp in (mspec, mspec, mspec, _fft_layout_spec(tm, seq), mspec, row)]

    per = lat_seq // tm
    return pl.pallas_call(
        functools.partial(_outffn_kernel, nc=nc),
        grid=(nc + nl,),
        in_specs=group_specs(ctx_seq, fc) + group_specs(lat_seq, fl) + [
            _mod_spec2(l, 1, nc, per), _mod_spec2(l, 2, nc, per),
            _gn_spec(l, 3, 3),
            _pick(gout, l), _pick(avg),
            _pick(wo, l), _pick(wg, l, 1), _pick(wu, l, 1), _pick(wd, l, 1),
        ],
        out_specs=[_retarget(row, fc), _retarget(row, fl)],
        out_shape=[jax.ShapeDtypeStruct(xc.shape, F32), jax.ShapeDtypeStruct(xl.shape, F32)],
        compiler_params=_params("arbitrary"),
        name="outffn",
    )(*mix_c, xc, *mix_l, xl, mod, mod, gn, gout, avg, wo, wg, wu, wd)


def _log_sigmoid(x):
    return -(jnp.maximum(-x, 0.0) + jnp.log(1.0 + jnp.exp(-jnp.abs(x))))


def _inproj_kernel(*refs, latent, nbat):
    (x_ref, mod_ref, gn_ref, w_ref, wgate_ref, bgate_ref, gq_ref, gkv_ref, wqb_ref, wkbd_ref,
     cdft_ref, sdft_ref) = refs[:12]
    pos = 12
    if latent:
        cs_ref, ss_ref, cm_ref, sm_ref = refs[pos:pos + 4]
        pos += 4
    (gq_o, gk_o, gv_o, gr_o, la_o, sq_o, sk_o, sv_o, zc_o, zs_o, qc_o, qr_o, kcat_o) = refs[pos:pos + 13]
    pos += 13
    if not latent:
        ckv_o, kr_o = refs[pos:pos + 2]

    x = x_ref[...]
    hb = (_rms(x, gn_ref[0]) * (1.0 + mod_ref[1]) + mod_ref[0]).astype(BF)

    def col(off, n):
        return _dot(hb, w_ref[:, off:off + n])

    qa = col(C_QA, MLA_Q_RANK)
    ask = col(C_ASK, 2 * LANES)
    a_in = ask[:, :LANES].astype(BF)
    zf = col(C_ZF, MIX).astype(BF)

    gq_o[...] = col(C_GQ, MIX) * (GLA_DK ** -0.5)
    gk_o[...] = col(C_GK, MIX)

    qm = _dot(_rms(qa, gq_ref[...]).astype(BF), wqb_ref[...])
    logit = _dot(a_in, wgate_ref[...]) + bgate_ref[...]
    la_o[...] = _log_sigmoid(logit) * (1.0 / GLA_TAU)

    zc = _dot(zf, cdft_ref[...]).astype(zc_o.dtype)
    zs = _dot(zf, sdft_ref[...]).astype(zs_o.dtype)
    rows_b = zf.shape[0] // nbat
    for j in range(nbat):
        zc_o[:, j * MIX:(j + 1) * MIX] = zc[j * rows_b:(j + 1) * rows_b]
        zs_o[:, j * MIX:(j + 1) * MIX] = zs[j * rows_b:(j + 1) * rows_b]

    gv_o[...] = col(C_GV, MIX).astype(BF)
    gr_o[...] = _silu(col(C_GR, MIX))

    scale = (MLA_NOPE + MLA_ROPE) ** -0.5 * LOG2E
    qc_o[...] = (_dot(qm[:, :MIX].astype(BF), wkbd_ref[...]) * scale).astype(BF)
    q_rope = qm[:, MIX:MIX + LANES]
    if latent:
        cm, sm = cm_ref[...], sm_ref[...]
        q_rope = q_rope * cm + qm[:, MIX + LANES:MIX + 2 * LANES] * sm
    qr_o[...] = (q_rope * scale).astype(BF)

    sq = col(C_SQ, 2 * MIX)
    sk = ask[:, LANES:]
    svk = col(C_SVKVA, 2 * LANES)
    sv = svk[:, :LANES]
    if latent:
        krp = col(C_KR, 2 * LANES)
        kr = krp[:, :LANES]
        cs, ss = cs_ref[...], ss_ref[...]
        cs4 = jnp.concatenate([cs] * 4, axis=1)
        ss4 = jnp.concatenate([ss] * 4, axis=1)
        sq = sq * cs4 + col(C_SQR, 2 * MIX) * ss4
        sk = sk * cs + krp[:, LANES:] * ss
    else:
        kr = col(C_KR, LANES)
    sq_o[...] = (sq * (HEAD_DIM ** -0.5 * LOG2E)).astype(BF)
    sk_o[...] = sk.astype(sk_o.dtype)
    sv_o[...] = sv.astype(sv_o.dtype)

    ckv = _rms(svk[:, LANES:], gkv_ref[...])
    if not latent:
        ckv_o[...] = ckv
        kr_o[...] = kr[:, :MLA_ROPE]
    else:
        kr = kr * cm + col(C_KRR, LANES) * sm
    kcat_o[:, :LANES] = ckv.astype(BF)
    kcat_o[:, LANES:] = kr.astype(BF)


def _inproj_call(x, mod, gn, wts, gq, gkv, cdft, sdft, tables, l, batch, seq, rows_per_batch, first_row,
                 latent, tm):
    w, wgate, bgate, wqb, wkbd = wts
    r = x.shape[0]
    per = max(seq // tm, 1)
    nbat = max(tm // seq, 1)
    row = lambda n: pl.BlockSpec((tm, n), lambda i: (i, 0))
    ncol = NC_LAT if latent else NC_CTX
    nq = wqb.shape[-1] if latent else MIX + LANES
    in_specs = [
        row(D_MODEL),
        _mod_spec(l, 1, tm, rows_per_batch, first_row),
        _gn_spec(l, 2, 1),
        _pick(w, l, block=(D_MODEL, ncol)), _pick(wgate, l), _pick(bgate, l), _pick(gq, l), _pick(gkv, l),
        _pick(wqb, l, block=(MLA_Q_RANK, nq)), _pick(wkbd, l), _pick(cdft), _pick(sdft),
    ]
    args = [x, mod, gn, w, wgate, bgate, gq, gkv, wqb, wkbd, cdft, sdft]
    if latent:
        in_specs += [pl.BlockSpec((tm, LANES), lambda i: (i % per, 0))] * 4
        args += list(tables)
    kvdt = BF if latent else F32
    zdt = F32 if latent else BF
    fft_spec = _fft_layout_spec(tm, seq)
    out_specs = [row(MIX), row(MIX), row(MIX), row(MIX), row(2 * MIX), row(2 * MIX), row(LANES), row(LANES),
                 fft_spec, fft_spec, row(2 * MIX), row(LANES), row(MIX)]
    sd = jax.ShapeDtypeStruct
    out_shape = [sd((r, MIX), F32), sd((r, MIX), F32), sd((r, MIX), BF), sd((r, MIX), F32),
                 sd((r, 2 * MIX), F32), sd((r, 2 * MIX), BF), sd((r, LANES), kvdt), sd((r, LANES), kvdt),
                 sd((seq, batch * MIX), zdt), sd((seq, batch * MIX), zdt),
                 sd((r, 2 * MIX), BF), sd((r, LANES), BF), sd((r, MIX), BF)]
    if not latent:
        out_specs += [row(MLA_KV_RANK), row(MLA_ROPE)]
        out_shape += [sd((r, MLA_KV_RANK), F32), sd((r, MLA_ROPE), F32)]
    return pl.pallas_call(
        functools.partial(_inproj_kernel, latent=latent, nbat=nbat),
        grid=(r // tm,),
        in_specs=in_specs,
        out_specs=out_specs,
        out_shape=out_shape,
        compiler_params=_params("parallel"),
        name="inproj_lat" if latent else "inproj_ctx",
    )(*args)


def _chunk_cumsum(x, row, reverse):
    n = x.shape[0]
    k = 1
    while k < n:
        if reverse:
            x = x + jnp.where(row < n - k, pltpu.roll(x, n - k, 0), 0.0)
        else:
            x = x + jnp.where(row >= k, pltpu.roll(x, k, 0), 0.0)
        k *= 2
    return x


def _gla_kernel(q_ref, k_ref, v_ref, la_ref, s0_ref, *rest, reverse, final, nchunk, group):
    if final:
        of_ref, o_ref, sfin_ref, st_ref = rest
    else:
        o_ref, sfin_ref, st_ref = rest
    i = pl.program_id(1)
    c = GLA_CHUNK

    @pl.when(i == 0)
    def _():
        st_ref[...] = s0_ref[...]

    row = lax.broadcasted_iota(jnp.int32, (c, MIX), 0)
    lane = lax.broadcasted_iota(jnp.int32, (c, MIX), 1) % c
    keep = (lane >= row) if reverse else (lane <= row)
    lane1 = lax.broadcasted_iota(jnp.int32, (1, MIX), 1) // c
    hmask = [(lane1 == h).astype(F32) for h in range(GLA_HEADS)]
    hmask_b = [m.astype(BF) for m in hmask]
    bdiag = (lax.broadcasted_iota(jnp.int32, (MIX, MIX), 0) // c
             == lax.broadcasted_iota(jnp.int32, (MIX, MIX), 1) // c)

    def chunk(t, carry):
        cc = (nchunk - 1 - t) if reverse else t
        rs = pl.ds(pl.multiple_of(cc * c, c), c)

        def front(g):
            cum = _chunk_cumsum(la_ref[g, rs, :], row, reverse)
            tot = cum[0:1] if reverse else cum[c - 1:c]
            q, k, v = q_ref[g, rs, :], k_ref[g, rs, :], v_ref[g, rs, :]
            qd = (q * jnp.exp(cum)).astype(BF)
            ki = k * jnp.exp(-cum)
            kd = (k * jnp.exp(tot - cum)).astype(BF)
            kstack = jnp.concatenate([(ki * m).astype(BF) for m in hmask], axis=0)
            vstack = jnp.concatenate([v * m for m in hmask_b], axis=0)
            return qd, vstack, _dot_nt(qd, kstack), _dot_tn(v, kd), jnp.exp(tot)

        def back(g, qd, vstack, scores, upd, decay):
            att = jnp.where(keep, scores, 0.0).astype(BF)
            st = st_ref[g]
            o = _dot(att, vstack) + _dot_nt(qd, st.astype(BF))
            st_ref[g] = st * decay + jnp.where(bdiag, upd, 0.0)
            if final:
                o = o + of_ref[g, rs, :]
            o_ref[g, rs, :] = o

        pending = front(0)
        for g in range(group):
            cur = pending
            if g + 1 < group:
                pending = front(g + 1)
            back(g, *cur)
        return carry

    lax.fori_loop(0, nchunk, chunk, 0, unroll=2)

    @pl.when(i == pl.num_programs(1) - 1)
    def _():
        sfin_ref[...] = st_ref[...]


def _gla_call(q, k, v, la, s0, s0_idx, extra, batch, seq, reverse, tb, group=GLA_GROUP):
    nblk = seq // tb
    final = extra is not None
    pos = (lambda i: nblk - 1 - i) if reverse else (lambda i: i)
    blk = pl.BlockSpec((group, tb, MIX), lambda b, i: (b, pos(i), 0))
    lblk = pl.BlockSpec((group, tb, MIX), lambda b, i: (b, pos(i), 1 if reverse else 0))
    sblk = pl.BlockSpec((group, MIX, MIX), lambda b, i: (b, 0, 0))
    s0blk = pl.BlockSpec((group, None, None, MIX, MIX), lambda b, i: (b,) + tuple(s0_idx) + (0, 0))
    r3 = lambda a: a.reshape(batch, seq, a.shape[-1])
    in_specs = [blk, blk, blk, lblk, s0blk]
    args = [r3(q), r3(k), r3(v), r3(la), s0]
    if final:
        in_specs.append(blk)
        args.append(r3(extra))
    o, st = pl.pallas_call(
        functools.partial(_gla_kernel, reverse=reverse, final=final, nchunk=tb // GLA_CHUNK, group=group),
        grid=(batch // group, nblk),
        in_specs=in_specs,
        out_specs=[blk, sblk],
        out_shape=[jax.ShapeDtypeStruct((batch, seq, MIX), F32),
                   jax.ShapeDtypeStruct((batch, MIX, MIX), F32)],
        scratch_shapes=[pltpu.VMEM((group, MIX, MIX), F32)],
        compiler_params=_params("parallel", "arbitrary"),
        name="gla_bwd" if reverse else "gla_fwd",
    )(*args)
    return o.reshape(batch * seq, MIX), st


def _swa_kernel(*refs, local, layer):
    sink_ref, q_ref = refs[:2]
    pos = 2
    if local:
        kp_ref, kc_ref, kn_ref, vp_ref, vc_ref, vn_ref = refs[pos:pos + 6]
        pos += 6
    kx_ref, vx_ref, o_ref = refs[pos:pos + 3]
    i = pl.program_id(1)
    last = pl.num_programs(1) - 1
    blk = SWA_BLOCK
    tq = q_ref.shape[0]
    cx = kx_ref.shape[0]
    kx = kx_ref[...].astype(BF)
    vx = vx_ref[...].astype(BF)
    if local:
        nsub = tq // blk
        kblocks = [kp_ref[...]] + [kc_ref[j * blk:(j + 1) * blk, :] for j in range(nsub)] + [kn_ref[...]]
        vblocks = [vp_ref[...]] + [vc_ref[j * blk:(j + 1) * blk, :] for j in range(nsub)] + [vn_ref[...]]
        row = lax.broadcasted_iota(jnp.int32, (blk, blk), 0)
        col = lax.broadcasted_iota(jnp.int32, (blk, blk), 1)
        full = jnp.full((blk, blk), True)
        fullx = jnp.full((blk, cx), True)
    else:
        nsub = 1
    rq = tq // nsub
    rows = lax.broadcasted_iota(jnp.int32, (2 * rq, 1), 0)
    low = lax.broadcasted_iota(jnp.int32, (rq, LANES), 1) < HEAD_DIM
    units = []
    for j in range(nsub):
        if local:
            kcat = jnp.concatenate([kx, kblocks[j], kblocks[j + 1], kblocks[j + 2]], axis=0)
            vcat = jnp.concatenate([vx, vblocks[j], vblocks[j + 1], vblocks[j + 2]], axis=0)
            prev_ok = (col >= row) if j > 0 else jnp.logical_and(col >= row, i > 0)
            next_ok = (col <= row) if j < nsub - 1 else jnp.logical_and(col <= row, i < last)
            keep = jnp.concatenate([fullx, prev_ok, full, next_ok], axis=1)
            keep = jnp.concatenate([keep, keep], axis=0)
        else:
            kcat, vcat, keep = kx, vx, None
        for g in range(SWA_KV_HEADS):
            units.append((j, g, kcat, vcat, keep))

    def scores(u):
        j, g, kcat, _, _ = units[u]
        rs = slice(j * rq, (j + 1) * rq)
        q2 = jnp.concatenate([q_ref[rs, (2 * g) * LANES:(2 * g + 1) * LANES],
                              q_ref[rs, (2 * g + 1) * LANES:(2 * g + 2) * LANES]], axis=0)
        return _dot_nt(q2, kcat)

    tiles = {}
    s_next = scores(0)
    for u, (j, g, _, vcat, keep) in enumerate(units):
        s = s_next
        if u + 1 < len(units):
            s_next = scores(u + 1)
        sink = jnp.where(rows < rq, sink_ref[layer, 2 * g], sink_ref[layer, 2 * g + 1]) * LOG2E
        if local:
            s = jnp.where(keep, s, NEG_INF)
        m = jnp.maximum(jnp.max(s, axis=-1, keepdims=True), sink)
        e = jnp.exp2(s - m)
        den = jnp.sum(e, axis=-1, keepdims=True) + jnp.exp2(sink - m)
        o2 = _dot(e.astype(BF), vcat) / den
        oa, ob = o2[:rq], o2[rq:]
        if g == 0:
            tiles[(j, g)] = jnp.where(low, oa, pltpu.roll(ob, HEAD_DIM, 1))
        else:
            tiles[(j, g)] = jnp.where(low, pltpu.roll(oa, HEAD_DIM, 1), ob)
            rs = slice(j * rq, (j + 1) * rq)
            o_ref[rs, :LANES] = tiles[(j, 0)].astype(BF)
            o_ref[rs, LANES:] = tiles[(j, 1)].astype(BF)


def _swa_call(sinks, layer, q, k, v, kx, vx, x_layer, batch, seq, local, tq=256):
    nt = seq // tq
    per = tq // SWA_BLOCK
    nb = seq // SWA_BLOCK
    cx = kx.shape[2]
    qspec = pl.BlockSpec((tq, 2 * MIX), lambda b, i: (b * nt + i, 0))
    in_specs = [pl.BlockSpec(memory_space=pltpu.SMEM), qspec]
    args = [sinks, q]
    if local:
        prv = pl.BlockSpec((SWA_BLOCK, LANES), lambda b, i: (b * nb + jnp.maximum(i * per - 1, 0), 0))
        cur = pl.BlockSpec((tq, LANES), lambda b, i: (b * nt + i, 0))
        nxt = pl.BlockSpec((SWA_BLOCK, LANES), lambda b, i: (b * nb + jnp.minimum((i + 1) * per, nb - 1), 0))
        in_specs += [prv, cur, nxt, prv, cur, nxt]
        args += [k, k, k, v, v, v]
    xspec = pl.BlockSpec((None, None, cx, LANES), lambda b, i: (b, x_layer, 0, 0))
    in_specs += [xspec, xspec]
    args += [kx, vx]
    return pl.pallas_call(
        functools.partial(_swa_kernel, local=local, layer=layer),
        grid=(batch, nt),
        in_specs=in_specs,
        out_specs=pl.BlockSpec((tq, MIX), lambda b, i: (b * nt + i, 0)),
        out_shape=jax.ShapeDtypeStruct((batch * seq, MIX), BF),
        compiler_params=_params("parallel", "parallel"),
        name="swa_lat" if local else "swa_ctx",
    )(*args)


def _fft_kernel(xc_ref, xs_ref, yc_ref, ys_ref, zc_ref, zs_ref, o_ref, *, nj, scale):
    yc, ys = yc_ref[...], ys_ref[...]
    xc, xs = xc_ref[...], xs_ref[...]
    acc = None
    for jp in range(nj // 2):
        cos_t, sin_t = [], []
        for j in (2 * jp, 2 * jp + 1):
            a, b = xc[:, j:j + 1], xs[:, j:j + 1]
            cos_t.append(a * yc - b * ys)
            sin_t.append(b * yc + a * ys)
        cos_t = jnp.concatenate(cos_t, axis=1).astype(BF)
        sin_t = jnp.concatenate(sin_t, axis=1).astype(BF)
        rows = slice(jp * MXU_DEPTH, (jp + 1) * MXU_DEPTH)
        d = _dot(cos_t, zc_ref[rows, :]) + _dot(sin_t, zs_ref[rows, :])
        acc = d if acc is None else acc + d
    o_ref[...] = (acc * scale).astype(BF)


def _fft_tables(seq):
    sp = jnp.arange(seq, dtype=jnp.int32)[:, None]
    w = 2.0 * jnp.pi / seq
    ax = ((sp * (jnp.arange(seq // LANES, dtype=jnp.int32) * LANES)[None, :]) % seq).astype(F32) * w
    ay = ((sp * jnp.arange(LANES, dtype=jnp.int32)[None, :]) % seq).astype(F32) * w
    return jnp.cos(ax), jnp.sin(ax), jnp.cos(ay), jnp.sin(ay)


def _fft_call(tables, zc, zs, seq, tm=256):
    xc, xs, yc, ys = tables
    nj = seq // LANES
    width = zc.shape[1]
    return pl.pallas_call(
        functools.partial(_fft_kernel, nj=nj, scale=(seq * FNET_GROUP_CH) ** -0.5),
        grid=(seq // tm,),
        in_specs=[
            pl.BlockSpec((tm, nj), lambda i: (i, 0)), pl.BlockSpec((tm, nj), lambda i: (i, 0)),
            pl.BlockSpec((tm, LANES), lambda i: (i, 0)), pl.BlockSpec((tm, LANES), lambda i: (i, 0)),
            _pick(zc), _pick(zs),
        ],
        out_specs=pl.BlockSpec((tm, width), lambda i: (i, 0)),
        out_shape=jax.ShapeDtypeStruct((seq, width), BF),
        compiler_params=_params("parallel"),
        name="fft",
    )(xc, xs, yc, ys, zc, zs)


def _fft2_kernel(zc_ref, zs_ref, m1_ref, m2_ref, tr_ref, ti_ref, o_ref, x_ref, *, nb1, scale):
    i = pl.program_id(1)
    n = m2_ref.shape[0]
    fb = FFT_BLOCK

    @pl.when(i < nb1)
    def _():
        m1 = m1_ref[...]
        for j in range(fb):
            z = jnp.concatenate([zc_ref[:, j, :], zs_ref[:, j, :]], axis=0).astype(BF)
            x1 = _dot(m1, z)
            xr, xi = x1[:n], x1[n:]
            tr, ti = tr_ref[:, j:j + 1], ti_ref[:, j:j + 1]
            x2 = jnp.concatenate([xr * tr - xi * ti, xr * ti + xi * tr], axis=0)
            x_ref[i * fb + j] = x2.reshape(2 * n // 8, 8, x2.shape[-1])

    @pl.when(i >= nb1)
    def _():
        m2 = m2_ref[...]
        blk = (i - nb1) * (fb // 8)
        for j in range(fb):
            z = jnp.concatenate([x_ref[:, blk + j // 8, j % 8, :],
                                 x_ref[:, n // 8 + blk + j // 8, j % 8, :]], axis=0).astype(BF)
            o_ref[:, j, :] = _dot(m2, z) * scale


def _fft2_call(zc, zs, seq, wl=512):
    n = int(round(seq ** 0.5))
    fb = FFT_BLOCK
    assert seq == n * n and n % fb == 0
    width = zc.shape[1]
    nb = n // fb
    k = jnp.arange(n, dtype=jnp.int32)
    ang = ((k[:, None] * k[None, :]) % n).astype(F32) * (2.0 * jnp.pi / n)
    c, sn = jnp.cos(ang), jnp.sin(ang)
    m1 = jnp.concatenate([jnp.concatenate([c, sn], axis=1), jnp.concatenate([-sn, c], axis=1)], axis=0).astype(BF)
    m2 = jnp.concatenate([c, sn], axis=1).astype(BF)
    tw = (k[:, None] * k[None, :]).astype(F32) * (2.0 * jnp.pi / seq)
    tr = jnp.cos(tw).reshape(n, nb, fb).transpose(1, 0, 2)
    ti = (-jnp.sin(tw)).reshape(n, nb, fb).transpose(1, 0, 2)
    zspec = pl.BlockSpec((n, fb, wl), lambda h, i: (0, jnp.minimum(i, nb - 1), h))
    tspec = pl.BlockSpec((None, n, fb), lambda h, i: (jnp.minimum(i, nb - 1), 0, 0))
    out = pl.pallas_call(
        functools.partial(_fft2_kernel, nb1=nb, scale=(seq * FNET_GROUP_CH) ** -0.5),
        grid=(width // wl, 2 * nb),
        in_specs=[zspec, zspec, _pick(m1), _pick(m2), tspec, tspec],
        out_specs=pl.BlockSpec((n, fb, wl), lambda h, i: (0, jnp.maximum(i - nb, 0), h)),
        out_shape=jax.ShapeDtypeStruct((n, n, width), F32),
        scratch_shapes=[pltpu.VMEM((n, 2 * n // 8, 8, wl), F32)],
        compiler_params=_params("parallel", "arbitrary"),
        name="fft2",
    )(zc.reshape(n, n, width), zs.reshape(n, n, width), m1, m2, tr, ti)
    return out.reshape(seq, width)


def _mla_kernel(*refs, has_ctx):
    qc_ref, qr_ref, ks_ref = refs[:3]
    pos = 3
    if has_ctx:
        kx_ref = refs[pos]
        pos += 1
    wv_ref, o_ref = refs[pos:pos + 2]
    qr = qr_ref[...]
    lane = lax.broadcasted_iota(jnp.int32, qr.shape, 1) // MLA_ROPE
    tq = qr.shape[0]
    qs = []
    for h in range(MLA_HEADS):
        qrh = jnp.where(lane == h, qr, jnp.zeros_like(qr))
        qs.append(jnp.concatenate([qc_ref[:, h * LANES:(h + 1) * LANES], qrh], axis=1))
    qall = jnp.concatenate(qs, axis=0)
    tiles = [(kx_ref, 0, kx_ref.shape[0])] if has_ctx else []
    nk = ks_ref.shape[0]
    tk = min(nk, MLA_KEY_TILE)
    tiles += [(ks_ref, j * tk, tk) for j in range(nk // tk)]
    m = den = acc = None
    for ref, lo, n in tiles:
        kt = ref[lo:lo + n, :]
        s = _dot_nt(qall, kt)
        mt = jnp.max(s, axis=-1, keepdims=True)
        if m is None:
            m_new = mt
        else:
            m_new = jnp.maximum(m, mt)
            alpha = jnp.exp2(m - m_new)
        p = jnp.exp2(s - m_new)
        pv = _dot(p.astype(BF), kt[:, :LANES])
        ps = jnp.sum(p, axis=-1, keepdims=True)
        if m is None:
            den, acc = ps, pv
        else:
            den = alpha * den + ps
            acc = alpha * acc + pv
        m = m_new
    ot = acc / den
    ocat = jnp.concatenate([ot[h * tq:(h + 1) * tq] for h in range(MLA_HEADS)], axis=1).astype(BF)
    o_ref[...] = _dot(ocat, wv_ref[...]).astype(BF)


def _mla_call(qc, qr, kcat, kx, wvbd, l, batch, seq, tq=256):
    nq = seq // tq
    has_ctx = kx is not None
    in_specs = [pl.BlockSpec((tq, 2 * MIX), lambda b, i: (b * nq + i, 0)),
                pl.BlockSpec((tq, LANES), lambda b, i: (b * nq + i, 0)),
                pl.BlockSpec((None, seq, MIX), lambda b, i: (b, 0, 0))]
    args = [qc, qr, kcat.reshape(batch, seq, MIX)]
    if has_ctx:
        in_specs.append(pl.BlockSpec((None, None, kx.shape[2], MIX), lambda b, i: (b, l, 0, 0)))
        args.append(kx)
    in_specs.append(_pick(wvbd, l))
    args.append(wvbd)
    return pl.pallas_call(
        functools.partial(_mla_kernel, has_ctx=has_ctx),
        grid=(batch, nq),
        in_specs=in_specs,
        out_specs=pl.BlockSpec((tq, MIX), lambda b, i: (b * nq + i, 0)),
        out_shape=jax.ShapeDtypeStruct((batch * seq, MIX), BF),
        compiler_params=_params("parallel", "parallel"),
        name="mla_lat" if has_ctx else "mla_ctx",
    )(*args)


def _rot_cols(w, width):
    half = width // 2
    parts = w.reshape(w.shape[:-1] + (w.shape[-1] // width, 2, half))
    return jnp.stack([-parts[..., 1, :], parts[..., 0, :]], axis=-2).reshape(w.shape)


def _block_diag(blocks):
    n, r, c = blocks.shape[-3:]
    eye = jnp.eye(n, dtype=blocks.dtype)
    out = jnp.einsum("...hrc,hg->...hrgc", blocks, eye)
    return out.reshape(blocks.shape[:-3] + (n * r, n * c))


def _layer_weights(w_in, gla_w_gate, gla_b_gate, mla_w_q_b, mla_w_kv_b):
    nl = w_in.shape[0]
    w_in = w_in.astype(BF)
    offs = [0]
    for n in (256, 256, 256, 256, 16, 16, 256, 128, 128, 256, 256, 128, 32):
        offs.append(offs[-1] + n)
    seg = lambda k: w_in[..., offs[k]:offs[k + 1]]
    zeros = lambda n: jnp.zeros((nl, D_MODEL, n), w_in.dtype)
    sq = seg(6).reshape(nl, D_MODEL, SWA_Q_HEADS, HEAD_DIM)
    sq_tiles = []
    for hq in range(SWA_Q_HEADS):
        parts = [zeros(HEAD_DIM), zeros(HEAD_DIM)]
        parts[hq // 2] = sq[:, :, hq]
        sq_tiles += parts
    sq_w = jnp.concatenate(sq_tiles, axis=-1)
    kr4 = jnp.tile(seg(12), (1, 1, MLA_HEADS))
    cols = [seg(0), seg(1), seg(2), seg(3), seg(4), seg(5), zeros(LANES - 2 * GLA_GATE_RANK), seg(7),
            seg(8), seg(11), sq_w, seg(9), seg(10), kr4,
            _rot_cols(seg(7), HEAD_DIM // 2), _rot_cols(sq_w, HEAD_DIM // 2), _rot_cols(kr4, MLA_ROPE // 2)]
    w = jnp.concatenate(cols, axis=-1)

    wgate = jnp.zeros((nl, LANES, 2 * MIX), F32)
    wgate = wgate.at[:, :GLA_GATE_RANK, :MIX].set(gla_w_gate[:, 0])
    wgate = wgate.at[:, GLA_GATE_RANK:2 * GLA_GATE_RANK, MIX:].set(gla_w_gate[:, 1])
    bgate = gla_b_gate.reshape(nl, 1, 2 * MIX)

    wq = mla_w_q_b.reshape(nl, MLA_Q_RANK, MLA_HEADS, MLA_NOPE + MLA_ROPE)
    q_nope = wq[..., :MLA_NOPE].reshape(nl, MLA_Q_RANK, MLA_HEADS * MLA_NOPE)
    q_rope = wq[..., MLA_NOPE:].reshape(nl, MLA_Q_RANK, MLA_HEADS * MLA_ROPE)
    wqb = jnp.concatenate([q_nope, q_rope, _rot_cols(q_rope, MLA_ROPE // 2)], axis=-1).astype(BF)
    wkv = mla_w_kv_b.reshape(nl, MLA_KV_RANK, MLA_HEADS, MLA_NOPE + MLA_V)
    wkbd = _block_diag(jnp.transpose(wkv[..., :MLA_NOPE], (0, 2, 3, 1))).astype(BF)
    wvbd = _block_diag(jnp.transpose(wkv[..., MLA_NOPE:], (0, 2, 1, 3))).astype(BF)
    return (w, wgate.astype(BF), bgate, wqb, wkbd), wvbd


def _rope_tables(seq):
    t = jnp.arange(seq)
    rows = (t // GRID_W).astype(F32)[:, None]
    cols = (t % GRID_W).astype(F32)[:, None]

    def table(width):
        half = width // 2
        lane = jnp.arange(LANES)
        inv = ROPE_BASE ** (-(2 * (lane % half)).astype(F32) / width)
        ang = jnp.where(((lane // width) % 2 == 0)[None, :], rows, cols) * inv[None, :]
        return jnp.cos(ang), jnp.sin(ang)

    cs, ss = table(HEAD_DIM // 2)
    cm, sm = table(MLA_ROPE // 2)
    return cs, ss, cm, sm


def _dft_channel():
    c = jnp.arange(FNET_GROUP_CH, dtype=jnp.int32)
    ang = ((c[:, None] * c[None, :]) % FNET_GROUP_CH).astype(F32) * (2.0 * jnp.pi / FNET_GROUP_CH)
    cos_b = jnp.broadcast_to(jnp.cos(ang)[None], (FNET_GROUPS, FNET_GROUP_CH, FNET_GROUP_CH))
    sin_b = jnp.broadcast_to(jnp.sin(ang)[None], (FNET_GROUPS, FNET_GROUP_CH, FNET_GROUP_CH))
    return _block_diag(cos_b).astype(BF), (-_block_diag(sin_b)).astype(BF)


def _state_in(s):
    return _block_diag(jnp.swapaxes(s, -1, -2))


def _state_out(st):
    blocks = [st[:, h * GLA_DV:(h + 1) * GLA_DV, h * GLA_DK:(h + 1) * GLA_DK] for h in range(GLA_HEADS)]
    return jnp.swapaxes(jnp.stack(blocks, axis=1), 2, 3)


def _token_mix(x, l, shared, st0, st0_idx, cache, batch, seq, rows_per_batch, first_row, latent):
    mod, gn, wts, wvbd, gq, gkv, sinks, cdft, sdft, rope_t, fft_t = shared
    outs = _inproj_call(x, mod, gn, wts, gq, gkv, cdft, sdft, rope_t, l, batch, seq, rows_per_batch,
                        first_row, latent, tm=512)
    gqv, gkv_, gv, gr, la, sq, sk, sv, zc, zs, qc, qr, kcat = outs[:13]
    tb = min(seq, GLA_BLOCK)
    o_f, st_f = _gla_call(gqv, gkv_, gv, la, st0, st0_idx[0], None, batch, seq, False, tb)
    o_gla, st_b = _gla_call(gqv, gkv_, gv, la, st0, st0_idx[1], o_f, batch, seq, True, tb)
    if latent:
        kx, vx, mx = cache
        o_swa = _swa_call(sinks, l, sq, sk, sv, kx, vx, l, batch, seq, True, tq=1024)
        o_mla = _mla_call(qc, qr, kcat, mx, wvbd, l, batch, seq)
        new_ctx = None
    else:
        k4 = sk.reshape(batch, 1, seq, LANES)
        v4 = sv.reshape(batch, 1, seq, LANES)
        o_swa = _swa_call(sinks, l, sq, None, None, k4, v4, 0, batch, seq, False)
        o_mla = _mla_call(qc, qr, kcat, None, wvbd, l, batch, seq)
        new_ctx = (st_f, st_b, sk, sv, outs[13], outs[14])
    o_fft = _fft2_call(zc, zs, seq) if latent else _fft_call(fft_t, zc, zs, seq)
    return (o_gla, gr, o_swa, o_fft, o_mla), new_ctx


def kernel(x_prompt, x_sample, c, state_gla, cache_swa_k, cache_swa_v, cache_mla_ckv, cache_mla_krope,
           c_ctx, w_mod, b_mod, g_norm, w_ffn_gate, w_ffn_up, w_ffn_down, w_in, gla_w_gate, gla_b_gate,
           gla_g_out, swa_sink, mla_g_q, mla_g_kv, mla_w_q_b, mla_w_kv_b, w_out):
    nb, ns, _ = x_prompt.shape
    db, dsq, _ = x_sample.shape
    past = cache_swa_k.shape[2]

    cvec = jnp.zeros((MOD_ROWS, D_MODEL), F32).at[0].set(c_ctx).at[1:1 + db].set(c)
    mod = _mod_call(cvec, w_mod, b_mod).reshape(DEPTH, MOD_ROWS, N_MOD, 1, D_MODEL)
    gn = g_norm.reshape(DEPTH, 6, 1, D_MODEL)

    wg, wu, wd, wo = (a.astype(BF) for a in (w_ffn_gate, w_ffn_up, w_ffn_down, w_out))
    wts, wvbd = _layer_weights(w_in, gla_w_gate, gla_b_gate, mla_w_q_b, mla_w_kv_b)
    gq = mla_g_q.reshape(DEPTH, 1, MLA_Q_RANK)
    gkv = mla_g_kv.reshape(DEPTH, 1, MLA_KV_RANK)
    gout = jnp.tile(gla_g_out, (1, GLA_HEADS)).reshape(DEPTH, 1, MIX)
    cdft, sdft = _dft_channel()
    shared = (mod, gn, wts, wvbd, gq, gkv, swa_sink, cdft, sdft)
    head = jnp.arange(MIX) // GLA_DV
    avg = jnp.where(head[:, None] == head[None, :], 1.0 / GLA_DV, 0.0).astype(BF)
    shared_ctx = shared + (None, _fft_tables(ns))
    shared_lat = shared + (_rope_tables(dsq), None)

    st_lat = _state_in(state_gla)
    st_zero = jnp.zeros((nb, 1, 1, MIX, MIX), F32)
    kx = cache_swa_k.reshape(db, DEPTH, past, LANES)
    vx = cache_swa_v.reshape(db, DEPTH, past, LANES)
    mx = jnp.concatenate([cache_mla_ckv, jnp.tile(cache_mla_krope, (1, 1, 1, MLA_HEADS))], axis=-1).astype(BF)

    xc = x_prompt.reshape(nb * ns, D_MODEL)
    xl = x_sample.reshape(db * dsq, D_MODEL)
    st_gla, st_k, st_v, st_ckv, st_kr = [], [], [], [], []
    for l in range(DEPTH):
        xc, xl = _ffn_call(xc, xl, mod, gn, wg, wu, wd, l, dsq)
        mixed_c, new_ctx = _token_mix(xc, l, shared_ctx, st_zero, ((0, 0), (0, 0)), None, nb, ns, nb * ns, 0, False)
        st_f, st_b, k_c, v_c, ckv_c, kr_c = new_ctx
        st_gla.append(jnp.stack([_state_out(st_f), _state_out(st_b)], axis=1))
        st_k.append(k_c.reshape(nb, ns, SWA_KV_HEADS, HEAD_DIM))
        st_v.append(v_c.reshape(nb, ns, SWA_KV_HEADS, HEAD_DIM))
        st_ckv.append(ckv_c.reshape(nb, ns, MLA_KV_RANK))
        st_kr.append(kr_c.reshape(nb, ns, MLA_ROPE))
        mixed_l, _ = _token_mix(xl, l, shared_lat, st_lat, ((l, 0), (l, 1)), (kx, vx, mx), db, dsq, dsq, 1, True)
        xc, xl = _outffn_call(mixed_c, xc, mixed_l, xl, mod, gn, gout, avg, wo, wg, wu, wd, l, ns, dsq)

    return (xc.reshape(nb, ns, D_MODEL), xl.reshape(db, dsq, D_MODEL), jnp.stack(st_gla, axis=1),
            jnp.stack(st_k, axis=1), jnp.stack(st_v, axis=1), jnp.stack(st_ckv, axis=1),
            jnp.stack(st_kr, axis=1))
```

```python
import functools

import jax
import jax.numpy as jnp
from jax import lax
from jax.experimental import pallas as pl
from jax.experimental.pallas import tpu as pltpu

D_MODEL = 1024
DEPTH = 4
GRID_W = 64
HEAD_DIM = 64
GLA_HEADS = 4
GLA_DK = 64
GLA_DV = 64
GLA_GATE_RANK = 16
GLA_TAU = 16.0
GLA_CHUNK = 64
SWA_Q_HEADS = 4
SWA_KV_HEADS = 2
SWA_BLOCK = 128
FNET_GROUPS = 4
FNET_GROUP_CH = 64
MLA_HEADS = 4
MLA_Q_RANK = 256
MLA_KV_RANK = 128
MLA_NOPE = 64
MLA_ROPE = 32
MLA_V = 64
D_FF = 2816
FFN_RES = 0.5
N_MOD = 9
ROPE_BASE = 10000.0
EPS = 1e-6
NEG_INF = -1e30

MIX = 256
LANES = 128
MXU_DEPTH = 256
FF_CHUNK = 256
MLA_KEY_TILE = 2048
FFT_BLOCK = 16
GLA_BLOCK = 512
GLA_GROUP = 4
LOG2E = 1.4426950408889634
MOD_ROWS = 8
VMEM_LIMIT = 56 * 1024 * 1024

BF = jnp.bfloat16
F32 = jnp.float32

C_GQ, C_GK, C_GV, C_GR = 0, 256, 512, 768
C_ASK, C_SVKVA, C_SQ, C_ZF, C_QA, C_KR = 1024, 1280, 1536, 2048, 2304, 2560
C_SKR, C_SQR, C_KRR = 2688, 2816, 3328
NC_CTX, NC_LAT = 2688, 3456


def _dot(a, b):
    return jnp.dot(a, b, preferred_element_type=F32)


def _dot_nt(a, b):
    return lax.dot_general(a, b, (((1,), (1,)), ((), ())), preferred_element_type=F32)


def _dot_tn(a, b):
    return lax.dot_general(a, b, (((0,), (0,)), ((), ())), preferred_element_type=F32)


def _rms(x, g):
    return x * lax.rsqrt(jnp.mean(x * x, axis=-1, keepdims=True) + EPS) * g


def _silu(x):
    return x * jax.nn.sigmoid(x)


def _params(*sem):
    return pltpu.CompilerParams(dimension_semantics=sem, vmem_limit_bytes=VMEM_LIMIT)


def _pick(arr, *lead, block=None):
    tail = tuple(arr.shape[len(lead):]) if block is None else tuple(block)
    zeros = (0,) * len(tail)
    return pl.BlockSpec((None,) * len(lead) + tail, lambda *_: tuple(lead) + zeros,
                        pipeline_mode=pl.Buffered(1))


def _mod_kernel(c_ref, w_ref, b_ref, o_ref):
    s = _silu(c_ref[...]).astype(BF)
    o_ref[...] = _dot(s, w_ref[...].astype(BF)) + b_ref[...]


def _mod_call(cvec, w_mod, b_mod):
    nl = w_mod.shape[0]
    tn = D_MODEL
    return pl.pallas_call(
        _mod_kernel,
        grid=(nl, N_MOD * D_MODEL // tn),
        in_specs=[
            pl.BlockSpec((MOD_ROWS, D_MODEL), lambda l, j: (0, 0)),
            pl.BlockSpec((None, D_MODEL, tn), lambda l, j: (l, 0, j)),
            pl.BlockSpec((None, 1, tn), lambda l, j: (l, 0, j)),
        ],
        out_specs=pl.BlockSpec((None, MOD_ROWS, tn), lambda l, j: (l, 0, j)),
        out_shape=jax.ShapeDtypeStruct((nl, MOD_ROWS, N_MOD * D_MODEL), F32),
        compiler_params=_params("parallel", "parallel"),
        name="mod",
    )(cvec, w_mod, b_mod.reshape(nl, 1, N_MOD * D_MODEL))


def _mod_spec(l, sub, tm, rows_per_batch, first_row):
    per = rows_per_batch // tm
    return pl.BlockSpec((None, None, 3, 1, D_MODEL), lambda i: (l, first_row + i // per, sub, 0, 0))


def _fft_layout_spec(tm, seq):
    per = max(seq // tm, 1)
    nbat = max(tm // seq, 1)
    return pl.BlockSpec((tm // nbat, nbat * MIX), lambda i: (i % per, i // per))


def _gn_spec(l, first, n):
    return pl.BlockSpec((None, n, 1, D_MODEL), lambda i: (l, first // n, 0, 0))


def _ffn_body(x, mod_ref, g_pre, g_post, wg_ref, wu_ref, wd_ref):
    sh, sc, gt = mod_ref[0], mod_ref[1], mod_ref[2]
    hb = (_rms(x, g_pre) * (1.0 + sc) + sh).astype(BF)
    acc = None
    for j in range(D_FF // FF_CHUNK):
        sl = slice(j * FF_CHUNK, (j + 1) * FF_CHUNK)
        g = _dot(hb, wg_ref[:, sl])
        u = _dot(hb, wu_ref[:, sl])
        d = _dot((_silu(g) * u).astype(BF), wd_ref[sl, :])
        acc = d if acc is None else acc + d
    return x + FFN_RES * gt * _rms(acc, g_post)


def _retarget(spec, f):
    return pl.BlockSpec(spec.block_shape, lambda i: spec.index_map(f(i)))


def _two_groups(nc):
    return (lambda i: jnp.minimum(i, nc - 1)), (lambda i: jnp.maximum(i - nc, 0))


def _mod_spec2(l, sub, nc, per):
    return pl.BlockSpec((None, None, 3, 1, D_MODEL),
                        lambda i: (l, jnp.where(i < nc, 0, 1 + jnp.maximum(i - nc, 0) // per), sub, 0, 0))


def _ffn_kernel(xc_ref, xl_ref, mod_ref, gn_ref, wg_ref, wu_ref, wd_ref, oc_ref, ol_ref, *, nc):
    i = pl.program_id(0)

    @pl.when(i < nc)
    def _():
        oc_ref[...] = _ffn_body(xc_ref[...], mod_ref, gn_ref[0], gn_ref[1], wg_ref, wu_ref, wd_ref)

    @pl.when(i >= nc)
    def _():
        ol_ref[...] = _ffn_body(xl_ref[...], mod_ref, gn_ref[0], gn_ref[1], wg_ref, wu_ref, wd_ref)


def _ffn_call(xc, xl, mod, gn, wg, wu, wd, l, lat_seq, tm=512):
    nc, nl = xc.shape[0] // tm, xl.shape[0] // tm
    fc, fl = _two_groups(nc)
    row = pl.BlockSpec((tm, D_MODEL), lambda i: (i, 0))
    return pl.pallas_call(
        functools.partial(_ffn_kernel, nc=nc),
        grid=(nc + nl,),
        in_specs=[
            _retarget(row, fc), _retarget(row, fl),
            _mod_spec2(l, 0, nc, lat_seq // tm),
            _gn_spec(l, 0, 2),
            _pick(wg, l, 0), _pick(wu, l, 0), _pick(wd, l, 0),
        ],
        out_specs=[_retarget(row, fc), _retarget(row, fl)],
        out_shape=[jax.ShapeDtypeStruct(xc.shape, F32), jax.ShapeDtypeStruct(xl.shape, F32)],
        compiler_params=_params("arbitrary"),
        name="ffn",
    )(xc, xl, mod, gn, wg, wu, wd)


def _outffn_body(og_ref, gr_ref, os_ref, of_ref, om_ref, x_ref, mod2_ref, mod3_ref, gn_ref, gout_ref,
                 avg_ref, wo_ref, wg_ref, wu_ref, wd_ref):
    og = og_ref[...]
    ms = _dot((og * og).astype(BF), avg_ref[...])
    og = (og * lax.rsqrt(ms + EPS) * gout_ref[...] * gr_ref[...]).astype(BF)
    nbat = of_ref.shape[1] // MIX
    of = jnp.concatenate([of_ref[:, j * MIX:(j + 1) * MIX] for j in range(nbat)], axis=0).astype(BF)
    mix = jnp.concatenate([og, os_ref[...], of, om_ref[...]], axis=1)
    x = x_ref[...] + mod2_ref[2] * _rms(_dot(mix, wo_ref[...]), gn_ref[0])
    return _ffn_body(x, mod3_ref, gn_ref[1], gn_ref[2], wg_ref, wu_ref, wd_ref)


def _outffn_kernel(*refs, nc):
    ctx, lat, rest = refs[:6], refs[6:12], refs[12:]
    shared, (oc_ref, ol_ref) = rest[:-2], rest[-2:]
    i = pl.program_id(0)

    @pl.when(i < nc)
    def _():
        oc_ref[...] = _outffn_body(*ctx, *shared)

    @pl.when(i >= nc)
    def _():
        ol_ref[...] = _outffn_body(*lat, *shared)


def _outffn_call(mix_c, xc, mix_l, xl, mod, gn, gout, avg, wo, wg, wu, wd, l, ctx_seq, lat_seq, tm=512):
    nc, nl = xc.shape[0] // tm, xl.shape[0] // tm
    fc, fl = _two_groups(nc)
    mspec = pl.BlockSpec((tm, MIX), lambda i: (i, 0))
    row = pl.BlockSpec((tm, D_MODEL), lambda i: (i, 0))

    def group_specs(seq, f):
        return [_retarget(sp, f) for sp in (mspec, mspec, mspec, _fft_layout_spec(tm, seq), mspec, row)]

    per = lat_seq // tm
    return pl.pallas_call(
        functools.partial(_outffn_kernel, nc=nc),
        grid=(nc + nl,),
        in_specs=group_specs(ctx_seq, fc) + group_specs(lat_seq, fl) + [
            _mod_spec2(l, 1, nc, per), _mod_spec2(l, 2, nc, per),
            _gn_spec(l, 3, 3),
            _pick(gout, l), _pick(avg),
            _pick(wo, l), _pick(wg, l, 1), _pick(wu, l, 1), _pick(wd, l, 1),
        ],
        out_specs=[_retarget(row, fc), _retarget(row, fl)],
        out_shape=[jax.ShapeDtypeStruct(xc.shape, F32), jax.ShapeDtypeStruct(xl.shape, F32)],
        compiler_params=_params("arbitrary"),
        name="outffn",
    )(*mix_c, xc, *mix_l, xl, mod, mod, gn, gout, avg, wo, wg, wu, wd)


def _log_sigmoid(x):
    return -(jnp.maximum(-x, 0.0) + jnp.log(1.0 + jnp.exp(-jnp.abs(x))))


def _inproj_kernel(*refs, latent, nbat):
    (x_ref, mod_ref, gn_ref, w_ref, wgate_ref, bgate_ref, gq_ref, gkv_ref, wqb_ref, wkbd_ref,
     cdft_ref, sdft_ref) = refs[:12]
    pos = 12
    if latent:
        cs_ref, ss_ref, cm_ref, sm_ref = refs[pos:pos + 4]
        pos += 4
    (gq_o, gk_o, gv_o, gr_o, la_o, sq_o, sk_o, sv_o, zc_o, zs_o, qc_o, qr_o, kcat_o) = refs[pos:pos + 13]
    pos += 13
    if not latent:
        ckv_o, kr_o = refs[pos:pos + 2]

    x = x_ref[...]
    hb = (_rms(x, gn_ref[0]) * (1.0 + mod_ref[1]) + mod_ref[0]).astype(BF)

    def col(off, n):
        return _dot(hb, w_ref[:, off:off + n])

    qa = col(C_QA, MLA_Q_RANK)
    ask = col(C_ASK, 2 * LANES)
    a_in = ask[:, :LANES].astype(BF)
    zf = col(C_ZF, MIX).astype(BF)

    gq_o[...] = col(C_GQ, MIX) * (GLA_DK ** -0.5)
    gk_o[...] = col(C_GK, MIX)

    qm = _dot(_rms(qa, gq_ref[...]).astype(BF), wqb_ref[...])
    logit = _dot(a_in, wgate_ref[...]) + bgate_ref[...]
    la_o[...] = _log_sigmoid(logit) * (1.0 / GLA_TAU)

    zc = _dot(zf, cdft_ref[...]).astype(zc_o.dtype)
    zs = _dot(zf, sdft_ref[...]).astype(zs_o.dtype)
    rows_b = zf.shape[0] // nbat
    for j in range(nbat):
        zc_o[:, j * MIX:(j + 1) * MIX] = zc[j * rows_b:(j + 1) * rows_b]
        zs_o[:, j * MIX:(j + 1) * MIX] = zs[j * rows_b:(j + 1) * rows_b]

    gv_o[...] = col(C_GV, MIX).astype(BF)
    gr_o[...] = _silu(col(C_GR, MIX))

    scale = (MLA_NOPE + MLA_ROPE) ** -0.5 * LOG2E
    qc_o[...] = (_dot(qm[:, :MIX].astype(BF), wkbd_ref[...]) * scale).astype(BF)
    q_rope = qm[:, MIX:MIX + LANES]
    if latent:
        cm, sm = cm_ref[...], sm_ref[...]
        q_rope = q_rope * cm + qm[:, MIX + LANES:MIX + 2 * LANES] * sm
    qr_o[...] = (q_rope * scale).astype(BF)

    sq = col(C_SQ, 2 * MIX)
    sk = ask[:, LANES:]
    svk = col(C_SVKVA, 2 * LANES)
    sv = svk[:, :LANES]
    if latent:
        krp = col(C_KR, 2 * LANES)
        kr = krp[:, :LANES]
        cs, ss = cs_ref[...], ss_ref[...]
        cs4 = jnp.concatenate([cs] * 4, axis=1)
        ss4 = jnp.concatenate([ss] * 4, axis=1)
        sq = sq * cs4 + col(C_SQR, 2 * MIX) * ss4
        sk = sk * cs + krp[:, LANES:] * ss
    else:
        kr = col(C_KR, LANES)
    sq_o[...] = (sq * (HEAD_DIM ** -0.5 * LOG2E)).astype(BF)
    sk_o[...] = sk.astype(sk_o.dtype)
    sv_o[...] = sv.astype(sv_o.dtype)

    ckv = _rms(svk[:, LANES:], gkv_ref[...])
    if not latent:
        ckv_o[...] = ckv
        kr_o[...] = kr[:, :MLA_ROPE]
    else:
        kr = kr * cm + col(C_KRR, LANES) * sm
    kcat_o[:, :LANES] = ckv.astype(BF)
    kcat_o[:, LANES:] = kr.astype(BF)


def _inproj_call(x, mod, gn, wts, gq, gkv, cdft, sdft, tables, l, batch, seq, rows_per_batch, first_row,
                 latent, tm):
    w, wgate, bgate, wqb, wkbd = wts
    r = x.shape[0]
    per = max(seq // tm, 1)
    nbat = max(tm // seq, 1)
    row = lambda n: pl.BlockSpec((tm, n), lambda i: (i, 0))
    ncol = NC_LAT if latent else NC_CTX
    nq = wqb.shape[-1] if latent else MIX + LANES
    in_specs = [
        row(D_MODEL),
        _mod_spec(l, 1, tm, rows_per_batch, first_row),
        _gn_spec(l, 2, 1),
        _pick(w, l, block=(D_MODEL, ncol)), _pick(wgate, l), _pick(bgate, l), _pick(gq, l), _pick(gkv, l),
        _pick(wqb, l, block=(MLA_Q_RANK, nq)), _pick(wkbd, l), _pick(cdft), _pick(sdft),
    ]
    args = [x, mod, gn, w, wgate, bgate, gq, gkv, wqb, wkbd, cdft, sdft]
    if latent:
        in_specs += [pl.BlockSpec((tm, LANES), lambda i: (i % per, 0))] * 4
        args += list(tables)
    kvdt = BF if latent else F32
    zdt = F32 if latent else BF
    fft_spec = _fft_layout_spec(tm, seq)
    out_specs = [row(MIX), row(MIX), row(MIX), row(MIX), row(2 * MIX), row(2 * MIX), row(LANES), row(LANES),
                 fft_spec, fft_spec, row(2 * MIX), row(LANES), row(MIX)]
    sd = jax.ShapeDtypeStruct
    out_shape = [sd((r, MIX), F32), sd((r, MIX), F32), sd((r, MIX), BF), sd((r, MIX), F32),
                 sd((r, 2 * MIX), F32), sd((r, 2 * MIX), BF), sd((r, LANES), kvdt), sd((r, LANES), kvdt),
                 sd((seq, batch * MIX), zdt), sd((seq, batch * MIX), zdt),
                 sd((r, 2 * MIX), BF), sd((r, LANES), BF), sd((r, MIX), BF)]
    if not latent:
        out_specs += [row(MLA_KV_RANK), row(MLA_ROPE)]
        out_shape += [sd((r, MLA_KV_RANK), F32), sd((r, MLA_ROPE), F32)]
    return pl.pallas_call(
        functools.partial(_inproj_kernel, latent=latent, nbat=nbat),
        grid=(r // tm,),
        in_specs=in_specs,
        out_specs=out_specs,
        out_shape=out_shape,
        compiler_params=_params("parallel"),
        name="inproj_lat" if latent else "inproj_ctx",
    )(*args)


def _chunk_cumsum(x, row, reverse):
    n = x.shape[0]
    k = 1
    while k < n:
        if reverse:
            x = x + jnp.where(row < n - k, pltpu.roll(x, n - k, 0), 0.0)
        else:
            x = x + jnp.where(row >= k, pltpu.roll(x, k, 0), 0.0)
        k *= 2
    return x


def _gla_kernel(q_ref, k_ref, v_ref, la_ref, s0_ref, *rest, reverse, final, nchunk, group):
    if final:
        of_ref, o_ref, sfin_ref, st_ref = rest
    else:
        o_ref, sfin_ref, st_ref = rest
    i = pl.program_id(1)
    c = GLA_CHUNK

    @pl.when(i == 0)
    def _():
        st_ref[...] = s0_ref[...]

    row = lax.broadcasted_iota(jnp.int32, (c, MIX), 0)
    lane = lax.broadcasted_iota(jnp.int32, (c, MIX), 1) % c
    keep = (lane >= row) if reverse else (lane <= row)
    lane1 = lax.broadcasted_iota(jnp.int32, (1, MIX), 1) // c
    hmask = [(lane1 == h).astype(F32) for h in range(GLA_HEADS)]
    hmask_b = [m.astype(BF) for m in hmask]
    bdiag = (lax.broadcasted_iota(jnp.int32, (MIX, MIX), 0) // c
             == lax.broadcasted_iota(jnp.int32, (MIX, MIX), 1) // c)

    def chunk(t, carry):
        cc = (nchunk - 1 - t) if reverse else t
        rs = pl.ds(pl.multiple_of(cc * c, c), c)

        def front(g):
            cum = _chunk_cumsum(la_ref[g, rs, :], row, reverse)
            tot = cum[0:1] if reverse else cum[c - 1:c]
            q, k, v = q_ref[g, rs, :], k_ref[g, rs, :], v_ref[g, rs, :]
            qd = (q * jnp.exp(cum)).astype(BF)
            ki = k * jnp.exp(-cum)
            kd = (k * jnp.exp(tot - cum)).astype(BF)
            kstack = jnp.concatenate([(ki * m).astype(BF) for m in hmask], axis=0)
            vstack = jnp.concatenate([v * m for m in hmask_b], axis=0)
            return qd, vstack, _dot_nt(qd, kstack), _dot_tn(v, kd), jnp.exp(tot)

        def back(g, qd, vstack, scores, upd, decay):
            att = jnp.where(keep, scores, 0.0).astype(BF)
            st = st_ref[g]
            o = _dot(att, vstack) + _dot_nt(qd, st.astype(BF))
            st_ref[g] = st * decay + jnp.where(bdiag, upd, 0.0)
            if final:
                o = o + of_ref[g, rs, :]
            o_ref[g, rs, :] = o

        pending = front(0)
        for g in range(group):
            cur = pending
            if g + 1 < group:
                pending = front(g + 1)
            back(g, *cur)
        return carry

    lax.fori_loop(0, nchunk, chunk, 0, unroll=2)

    @pl.when(i == pl.num_programs(1) - 1)
    def _():
        sfin_ref[...] = st_ref[...]


def _gla_call(q, k, v, la, s0, s0_idx, extra, batch, seq, reverse, tb, group=GLA_GROUP):
    nblk = seq // tb
    final = extra is not None
    pos = (lambda i: nblk - 1 - i) if reverse else (lambda i: i)
    blk = pl.BlockSpec((group, tb, MIX), lambda b, i: (b, pos(i), 0))
    lblk = pl.BlockSpec((group, tb, MIX), lambda b, i: (b, pos(i), 1 if reverse else 0))
    sblk = pl.BlockSpec((group, MIX, MIX), lambda b, i: (b, 0, 0))
    s0blk = pl.BlockSpec((group, None, None, MIX, MIX), lambda b, i: (b,) + tuple(s0_idx) + (0, 0))
    r3 = lambda a: a.reshape(batch, seq, a.shape[-1])
    in_specs = [blk, blk, blk, lblk, s0blk]
    args = [r3(q), r3(k), r3(v), r3(la), s0]
    if final:
        in_specs.append(blk)
        args.append(r3(extra))
    o, st = pl.pallas_call(
        functools.partial(_gla_kernel, reverse=reverse, final=final, nchunk=tb // GLA_CHUNK, group=group),
        grid=(batch // group, nblk),
        in_specs=in_specs,
        out_specs=[blk, sblk],
        out_shape=[jax.ShapeDtypeStruct((batch, seq, MIX), F32),
                   jax.ShapeDtypeStruct((batch, MIX, MIX), F32)],
        scratch_shapes=[pltpu.VMEM((group, MIX, MIX), F32)],
        compiler_params=_params("parallel", "arbitrary"),
        name="gla_bwd" if reverse else "gla_fwd",
    )(*args)
    return o.reshape(batch * seq, MIX), st


def _swa_kernel(*refs, local, layer):
    sink_ref, q_ref = refs[:2]
    pos = 2
    if local:
        kp_ref, kc_ref, kn_ref, vp_ref, vc_ref, vn_ref = refs[pos:pos + 6]
        pos += 6
    kx_ref, vx_ref, o_ref = refs[pos:pos + 3]
    i = pl.program_id(1)
    last = pl.num_programs(1) - 1
    blk = SWA_BLOCK
    tq = q_ref.shape[0]
    cx = kx_ref.shape[0]
    kx = kx_ref[...].astype(BF)
    vx = vx_ref[...].astype(BF)
    if local:
        nsub = tq // blk
        kblocks = [kp_ref[...]] + [kc_ref[j * blk:(j + 1) * blk, :] for j in range(nsub)] + [kn_ref[...]]
        vblocks = [vp_ref[...]] + [vc_ref[j * blk:(j + 1) * blk, :] for j in range(nsub)] + [vn_ref[...]]
        row = lax.broadcasted_iota(jnp.int32, (blk, blk), 0)
        col = lax.broadcasted_iota(jnp.int32, (blk, blk), 1)
        full = jnp.full((blk, blk), True)
        fullx = jnp.full((blk, cx), True)
    else:
        nsub = 1
    rq = tq // nsub
    rows = lax.broadcasted_iota(jnp.int32, (2 * rq, 1), 0)
    low = lax.broadcasted_iota(jnp.int32, (rq, LANES), 1) < HEAD_DIM
    units = []
    for j in range(nsub):
        if local:
            kcat = jnp.concatenate([kx, kblocks[j], kblocks[j + 1], kblocks[j + 2]], axis=0)
            vcat = jnp.concatenate([vx, vblocks[j], vblocks[j + 1], vblocks[j + 2]], axis=0)
            prev_ok = (col >= row) if j > 0 else jnp.logical_and(col >= row, i > 0)
            next_ok = (col <= row) if j < nsub - 1 else jnp.logical_and(col <= row, i < last)
            keep = jnp.concatenate([fullx, prev_ok, full, next_ok], axis=1)
            keep = jnp.concatenate([keep, keep], axis=0)
        else:
            kcat, vcat, keep = kx, vx, None
        for g in range(SWA_KV_HEADS):
            units.append((j, g, kcat, vcat, keep))

    def scores(u):
        j, g, kcat, _, _ = units[u]
        rs = slice(j * rq, (j + 1) * rq)
        q2 = jnp.concatenate([q_ref[rs, (2 * g) * LANES:(2 * g + 1) * LANES],
                              q_ref[rs, (2 * g + 1) * LANES:(2 * g + 2) * LANES]], axis=0)
        return _dot_nt(q2, kcat)

    tiles = {}
    s_next = scores(0)
    for u, (j, g, _, vcat, keep) in enumerate(units):
        s = s_next
        if u + 1 < len(units):
            s_next = scores(u + 1)
        sink = jnp.where(rows < rq, sink_ref[layer, 2 * g], sink_ref[layer, 2 * g + 1]) * LOG2E
        if local:
            s = jnp.where(keep, s, NEG_INF)
        m = jnp.maximum(jnp.max(s, axis=-1, keepdims=True), sink)
        e = jnp.exp2(s - m)
        den = jnp.sum(e, axis=-1, keepdims=True) + jnp.exp2(sink - m)
        o2 = _dot(e.astype(BF), vcat) / den
        oa, ob = o2[:rq], o2[rq:]
        if g == 0:
            tiles[(j, g)] = jnp.where(low, oa, pltpu.roll(ob, HEAD_DIM, 1))
        else:
            tiles[(j, g)] = jnp.where(low, pltpu.roll(oa, HEAD_DIM, 1), ob)
            rs = slice(j * rq, (j + 1) * rq)
            o_ref[rs, :LANES] = tiles[(j, 0)].astype(BF)
            o_ref[rs, LANES:] = tiles[(j, 1)].astype(BF)


def _swa_call(sinks, layer, q, k, v, kx, vx, x_layer, batch, seq, local, tq=256):
    nt = seq // tq
    per = tq // SWA_BLOCK
    nb = seq // SWA_BLOCK
    cx = kx.shape[2]
    qspec = pl.BlockSpec((tq, 2 * MIX), lambda b, i: (b * nt + i, 0))
    in_specs = [pl.BlockSpec(memory_space=pltpu.SMEM), qspec]
    args = [sinks, q]
    if local:
        prv = pl.BlockSpec((SWA_BLOCK, LANES), lambda b, i: (b * nb + jnp.maximum(i * per - 1, 0), 0))
        cur = pl.BlockSpec((tq, LANES), lambda b, i: (b * nt + i, 0))
        nxt = pl.BlockSpec((SWA_BLOCK, LANES), lambda b, i: (b * nb + jnp.minimum((i + 1) * per, nb - 1), 0))
        in_specs += [prv, cur, nxt, prv, cur, nxt]
        args += [k, k, k, v, v, v]
    xspec = pl.BlockSpec((None, None, cx, LANES), lambda b, i: (b, x_layer, 0, 0))
    in_specs += [xspec, xspec]
    args += [kx, vx]
    return pl.pallas_call(
        functools.partial(_swa_kernel, local=local, layer=layer),
        grid=(batch, nt),
        in_specs=in_specs,
        out_specs=pl.BlockSpec((tq, MIX), lambda b, i: (b * nt + i, 0)),
        out_shape=jax.ShapeDtypeStruct((batch * seq, MIX), BF),
        compiler_params=_params("parallel", "parallel"),
        name="swa_lat" if local else "swa_ctx",
    )(*args)


def _fft_kernel(xc_ref, xs_ref, yc_ref, ys_ref, zc_ref, zs_ref, o_ref, *, nj, scale):
    yc, ys = yc_ref[...], ys_ref[...]
    xc, xs = xc_ref[...], xs_ref[...]
    acc = None
    for jp in range(nj // 2):
        cos_t, sin_t = [], []
        for j in (2 * jp, 2 * jp + 1):
            a, b = xc[:, j:j + 1], xs[:, j:j + 1]
            cos_t.append(a * yc - b * ys)
            sin_t.append(b * yc + a * ys)
        cos_t = jnp.concatenate(cos_t, axis=1).astype(BF)
        sin_t = jnp.concatenate(sin_t, axis=1).astype(BF)
        rows = slice(jp * MXU_DEPTH, (jp + 1) * MXU_DEPTH)
        d = _dot(cos_t, zc_ref[rows, :]) + _dot(sin_t, zs_ref[rows, :])
        acc = d if acc is None else acc + d
    o_ref[...] = (acc * scale).astype(BF)


def _fft_tables(seq):
    sp = jnp.arange(seq, dtype=jnp.int32)[:, None]
    w = 2.0 * jnp.pi / seq
    ax = ((sp * (jnp.arange(seq // LANES, dtype=jnp.int32) * LANES)[None, :]) % seq).astype(F32) * w
    ay = ((sp * jnp.arange(LANES, dtype=jnp.int32)[None, :]) % seq).astype(F32) * w
    return jnp.cos(ax), jnp.sin(ax), jnp.cos(ay), jnp.sin(ay)


def _fft_call(tables, zc, zs, seq, tm=256):
    xc, xs, yc, ys = tables
    nj = seq // LANES
    width = zc.shape[1]
    return pl.pallas_call(
        functools.partial(_fft_kernel, nj=nj, scale=(seq * FNET_GROUP_CH) ** -0.5),
        grid=(seq // tm,),
        in_specs=[
            pl.BlockSpec((tm, nj), lambda i: (i, 0)), pl.BlockSpec((tm, nj), lambda i: (i, 0)),
            pl.BlockSpec((tm, LANES), lambda i: (i, 0)), pl.BlockSpec((tm, LANES), lambda i: (i, 0)),
            _pick(zc), _pick(zs),
        ],
        out_specs=pl.BlockSpec((tm, width), lambda i: (i, 0)),
        out_shape=jax.ShapeDtypeStruct((seq, width), BF),
        compiler_params=_params("parallel"),
        name="fft",
    )(xc, xs, yc, ys, zc, zs)


def _fft2_kernel(zc_ref, zs_ref, m1_ref, m2_ref, tr_ref, ti_ref, o_ref, x_ref, *, nb1, scale):
    i = pl.program_id(1)
    n = m2_ref.shape[0]
    fb = FFT_BLOCK

    @pl.when(i < nb1)
    def _():
        m1 = m1_ref[...]
        for j in range(fb):
            z = jnp.concatenate([zc_ref[:, j, :], zs_ref[:, j, :]], axis=0).astype(BF)
            x1 = _dot(m1, z)
            xr, xi = x1[:n], x1[n:]
            tr, ti = tr_ref[:, j:j + 1], ti_ref[:, j:j + 1]
            x2 = jnp.concatenate([xr * tr - xi * ti, xr * ti + xi * tr], axis=0)
            x_ref[i * fb + j] = x2.reshape(2 * n // 8, 8, x2.shape[-1])

    @pl.when(i >= nb1)
    def _():
        m2 = m2_ref[...]
        blk = (i - nb1) * (fb // 8)
        for j in range(fb):
            z = jnp.concatenate([x_ref[:, blk + j // 8, j % 8, :],
                                 x_ref[:, n // 8 + blk + j // 8, j % 8, :]], axis=0).astype(BF)
            o_ref[:, j, :] = _dot(m2, z) * scale


def _fft2_call(zc, zs, seq, wl=512):
    n = int(round(seq ** 0.5))
    fb = FFT_BLOCK
    assert seq == n * n and n % fb == 0
    width = zc.shape[1]
    nb = n // fb
    k = jnp.arange(n, dtype=jnp.int32)
    ang = ((k[:, None] * k[None, :]) % n).astype(F32) * (2.0 * jnp.pi / n)
    c, sn = jnp.cos(ang), jnp.sin(ang)
    m1 = jnp.concatenate([jnp.concatenate([c, sn], axis=1), jnp.concatenate([-sn, c], axis=1)], axis=0).astype(BF)
    m2 = jnp.concatenate([c, sn], axis=1).astype(BF)
    tw = (k[:, None] * k[None, :]).astype(F32) * (2.0 * jnp.pi / seq)
    tr = jnp.cos(tw).reshape(n, nb, fb).transpose(1, 0, 2)
    ti = (-jnp.sin(tw)).reshape(n, nb, fb).transpose(1, 0, 2)
    zspec = pl.BlockSpec((n, fb, wl), lambda h, i: (0, jnp.minimum(i, nb - 1), h))
    tspec = pl.BlockSpec((None, n, fb), lambda h, i: (jnp.minimum(i, nb - 1), 0, 0))
    out = pl.pallas_call(
        functools.partial(_fft2_kernel, nb1=nb, scale=(seq * FNET_GROUP_CH) ** -0.5),
        grid=(width // wl, 2 * nb),
        in_specs=[zspec, zspec, _pick(m1), _pick(m2), tspec, tspec],
        out_specs=pl.BlockSpec((n, fb, wl), lambda h, i: (0, jnp.maximum(i - nb, 0), h)),
        out_shape=jax.ShapeDtypeStruct((n, n, width), F32),
        scratch_shapes=[pltpu.VMEM((n, 2 * n // 8, 8, wl), F32)],
        compiler_params=_params("parallel", "arbitrary"),
        name="fft2",
    )(zc.reshape(n, n, width), zs.reshape(n, n, width), m1, m2, tr, ti)
    return out.reshape(seq, width)


def _mla_kernel(*refs, has_ctx):
    qc_ref, qr_ref, ks_ref = refs[:3]
    pos = 3
    if has_ctx:
        kx_ref = refs[pos]
        pos += 1
    wv_ref, o_ref = refs[pos:pos + 2]
    qr = qr_ref[...]
    lane = lax.broadcasted_iota(jnp.int32, qr.shape, 1) // MLA_ROPE
    tq = qr.shape[0]
    qs = []
    for h in range(MLA_HEADS):
        qrh = jnp.where(lane == h, qr, jnp.zeros_like(qr))
        qs.append(jnp.concatenate([qc_ref[:, h * LANES:(h + 1) * LANES], qrh], axis=1))
    qall = jnp.concatenate(qs, axis=0)
    tiles = [(kx_ref, 0, kx_ref.shape[0])] if has_ctx else []
    nk = ks_ref.shape[0]
    tk = min(nk, MLA_KEY_TILE)
    tiles += [(ks_ref, j * tk, tk) for j in range(nk // tk)]
    m = den = acc = None
    for ref, lo, n in tiles:
        kt = ref[lo:lo + n, :]
        s = _dot_nt(qall, kt)
        mt = jnp.max(s, axis=-1, keepdims=True)
        if m is None:
            m_new = mt
        else:
            m_new = jnp.maximum(m, mt)
            alpha = jnp.exp2(m - m_new)
        p = jnp.exp2(s - m_new)
        pv = _dot(p.astype(BF), kt[:, :LANES])
        ps = jnp.sum(p, axis=-1, keepdims=True)
        if m is None:
            den, acc = ps, pv
        else:
            den = alpha * den + ps
            acc = alpha * acc + pv
        m = m_new
    ot = acc / den
    ocat = jnp.concatenate([ot[h * tq:(h + 1) * tq] for h in range(MLA_HEADS)], axis=1).astype(BF)
    o_ref[...] = _dot(ocat, wv_ref[...]).astype(BF)


def _mla_call(qc, qr, kcat, kx, wvbd, l, batch, seq, tq=256):
    nq = seq // tq
    has_ctx = kx is not None
    in_specs = [pl.BlockSpec((tq, 2 * MIX), lambda b, i: (b * nq + i, 0)),
                pl.BlockSpec((tq, LANES), lambda b, i: (b * nq + i, 0)),
                pl.BlockSpec((None, seq, MIX), lambda b, i: (b, 0, 0))]
    args = [qc, qr, kcat.reshape(batch, seq, MIX)]
    if has_ctx:
        in_specs.append(pl.BlockSpec((None, None, kx.shape[2], MIX), lambda b, i: (b, l, 0, 0)))
        args.append(kx)
    in_specs.append(_pick(wvbd, l))
    args.append(wvbd)
    return pl.pallas_call(
        functools.partial(_mla_kernel, has_ctx=has_ctx),
        grid=(batch, nq),
        in_specs=in_specs,
        out_specs=pl.BlockSpec((tq, MIX), lambda b, i: (b * nq + i, 0)),
        out_shape=jax.ShapeDtypeStruct((batch * seq, MIX), BF),
        compiler_params=_params("parallel", "parallel"),
        name="mla_lat" if has_ctx else "mla_ctx",
    )(*args)


def _rot_cols(w, width):
    half = width // 2
    parts = w.reshape(w.shape[:-1] + (w.shape[-1] // width, 2, half))
    return jnp.stack([-parts[..., 1, :], parts[..., 0, :]], axis=-2).reshape(w.shape)


def _block_diag(blocks):
    n, r, c = blocks.shape[-3:]
    eye = jnp.eye(n, dtype=blocks.dtype)
    out = jnp.einsum("...hrc,hg->...hrgc", blocks, eye)
    return out.reshape(blocks.shape[:-3] + (n * r, n * c))


def _layer_weights(w_in, gla_w_gate, gla_b_gate, mla_w_q_b, mla_w_kv_b):
    nl = w_in.shape[0]
    w_in = w_in.astype(BF)
    offs = [0]
    for n in (256, 256, 256, 256, 16, 16, 256, 128, 128, 256, 256, 128, 32):
        offs.append(offs[-1] + n)
    seg = lambda k: w_in[..., offs[k]:offs[k + 1]]
    zeros = lambda n: jnp.zeros((nl, D_MODEL, n), w_in.dtype)
    sq = seg(6).reshape(nl, D_MODEL, SWA_Q_HEADS, HEAD_DIM)
    sq_tiles = []
    for hq in range(SWA_Q_HEADS):
        parts = [zeros(HEAD_DIM), zeros(HEAD_DIM)]
        parts[hq // 2] = sq[:, :, hq]
        sq_tiles += parts
    sq_w = jnp.concatenate(sq_tiles, axis=-1)
    kr4 = jnp.tile(seg(12), (1, 1, MLA_HEADS))
    cols = [seg(0), seg(1), seg(2), seg(3), seg(4), seg(5), zeros(LANES - 2 * GLA_GATE_RANK), seg(7),
            seg(8), seg(11), sq_w, seg(9), seg(10), kr4,
            _rot_cols(seg(7), HEAD_DIM // 2), _rot_cols(sq_w, HEAD_DIM // 2), _rot_cols(kr4, MLA_ROPE // 2)]
    w = jnp.concatenate(cols, axis=-1)

    wgate = jnp.zeros((nl, LANES, 2 * MIX), F32)
    wgate = wgate.at[:, :GLA_GATE_RANK, :MIX].set(gla_w_gate[:, 0])
    wgate = wgate.at[:, GLA_GATE_RANK:2 * GLA_GATE_RANK, MIX:].set(gla_w_gate[:, 1])
    bgate = gla_b_gate.reshape(nl, 1, 2 * MIX)

    wq = mla_w_q_b.reshape(nl, MLA_Q_RANK, MLA_HEADS, MLA_NOPE + MLA_ROPE)
    q_nope = wq[..., :MLA_NOPE].reshape(nl, MLA_Q_RANK, MLA_HEADS * MLA_NOPE)
    q_rope = wq[..., MLA_NOPE:].reshape(nl, MLA_Q_RANK, MLA_HEADS * MLA_ROPE)
    wqb = jnp.concatenate([q_nope, q_rope, _rot_cols(q_rope, MLA_ROPE // 2)], axis=-1).astype(BF)
    wkv = mla_w_kv_b.reshape(nl, MLA_KV_RANK, MLA_HEADS, MLA_NOPE + MLA_V)
    wkbd = _block_diag(jnp.transpose(wkv[..., :MLA_NOPE], (0, 2, 3, 1))).astype(BF)
    wvbd = _block_diag(jnp.transpose(wkv[..., MLA_NOPE:], (0, 2, 1, 3))).astype(BF)
    return (w, wgate.astype(BF), bgate, wqb, wkbd), wvbd


def _rope_tables(seq):
    t = jnp.arange(seq)
    rows = (t // GRID_W).astype(F32)[:, None]
    cols = (t % GRID_W).astype(F32)[:, None]

    def table(width):
        half = width // 2
        lane = jnp.arange(LANES)
        inv = ROPE_BASE ** (-(2 * (lane % half)).astype(F32) / width)
        ang = jnp.where(((lane // width) % 2 == 0)[None, :], rows, cols) * inv[None, :]
        return jnp.cos(ang), jnp.sin(ang)

    cs, ss = table(HEAD_DIM // 2)
    cm, sm = table(MLA_ROPE // 2)
    return cs, ss, cm, sm


def _dft_channel():
    c = jnp.arange(FNET_GROUP_CH, dtype=jnp.int32)
    ang = ((c[:, None] * c[None, :]) % FNET_GROUP_CH).astype(F32) * (2.0 * jnp.pi / FNET_GROUP_CH)
    cos_b = jnp.broadcast_to(jnp.cos(ang)[None], (FNET_GROUPS, FNET_GROUP_CH, FNET_GROUP_CH))
    sin_b = jnp.broadcast_to(jnp.sin(ang)[None], (FNET_GROUPS, FNET_GROUP_CH, FNET_GROUP_CH))
    return _block_diag(cos_b).astype(BF), (-_block_diag(sin_b)).astype(BF)


def _state_in(s):
    return _block_diag(jnp.swapaxes(s, -1, -2))


def _state_out(st):
    blocks = [st[:, h * GLA_DV:(h + 1) * GLA_DV, h * GLA_DK:(h + 1) * GLA_DK] for h in range(GLA_HEADS)]
    return jnp.swapaxes(jnp.stack(blocks, axis=1), 2, 3)


def _token_mix(x, l, shared, st0, st0_idx, cache, batch, seq, rows_per_batch, first_row, latent):
    mod, gn, wts, wvbd, gq, gkv, sinks, cdft, sdft, rope_t, fft_t = shared
    outs = _inproj_call(x, mod, gn, wts, gq, gkv, cdft, sdft, rope_t, l, batch, seq, rows_per_batch,
                        first_row, latent, tm=512)
    gqv, gkv_, gv, gr, la, sq, sk, sv, zc, zs, qc, qr, kcat = outs[:13]
    tb = min(seq, GLA_BLOCK)
    o_f, st_f = _gla_call(gqv, gkv_, gv, la, st0, st0_idx[0], None, batch, seq, False, tb)
    o_gla, st_b = _gla_call(gqv, gkv_, gv, la, st0, st0_idx[1], o_f, batch, seq, True, tb)
    if latent:
        kx, vx, mx = cache
        o_swa = _swa_call(sinks, l, sq, sk, sv, kx, vx, l, batch, seq, True, tq=1024)
        o_mla = _mla_call(qc, qr, kcat, mx, wvbd, l, batch, seq)
        new_ctx = None
    else:
        k4 = sk.reshape(batch, 1, seq, LANES)
        v4 = sv.reshape(batch, 1, seq, LANES)
        o_swa = _swa_call(sinks, l, sq, None, None, k4, v4, 0, batch, seq, False)
        o_mla = _mla_call(qc, qr, kcat, None, wvbd, l, batch, seq)
        new_ctx = (st_f, st_b, sk, sv, outs[13], outs[14])
    o_fft = _fft2_call(zc, zs, seq) if latent else _fft_call(fft_t, zc, zs, seq)
    return (o_gla, gr, o_swa, o_fft, o_mla), new_ctx


def kernel(x_prompt, x_sample, c, state_gla, cache_swa_k, cache_swa_v, cache_mla_ckv, cache_mla_krope,
           c_ctx, w_mod, b_mod, g_norm, w_ffn_gate, w_ffn_up, w_ffn_down, w_in, gla_w_gate, gla_b_gate,
           gla_g_out, swa_sink, mla_g_q, mla_g_kv, mla_w_q_b, mla_w_kv_b, w_out):
    nb, ns, _ = x_prompt.shape
    db, dsq, _ = x_sample.shape
    past = cache_swa_k.shape[2]

    cvec = jnp.zeros((MOD_ROWS, D_MODEL), F32).at[0].set(c_ctx).at[1:1 + db].set(c)
    mod = _mod_call(cvec, w_mod, b_mod).reshape(DEPTH, MOD_ROWS, N_MOD, 1, D_MODEL)
    gn = g_norm.reshape(DEPTH, 6, 1, D_MODEL)

    wg, wu, wd, wo = (a.astype(BF) for a in (w_ffn_gate, w_ffn_up, w_ffn_down, w_out))
    wts, wvbd = _layer_weights(w_in, gla_w_gate, gla_b_gate, mla_w_q_b, mla_w_kv_b)
    gq = mla_g_q.reshape(DEPTH, 1, MLA_Q_RANK)
    gkv = mla_g_kv.reshape(DEPTH, 1, MLA_KV_RANK)
    gout = jnp.tile(gla_g_out, (1, GLA_HEADS)).reshape(DEPTH, 1, MIX)
    cdft, sdft = _dft_channel()
    shared = (mod, gn, wts, wvbd, gq, gkv, swa_sink, cdft, sdft)
    head = jnp.arange(MIX) // GLA_DV
    avg = jnp.where(head[:, None] == head[None, :], 1.0 / GLA_DV, 0.0).astype(BF)
    shared_ctx = shared + (None, _fft_tables(ns))
    shared_lat = shared + (_rope_tables(dsq), None)

    st_lat = _state_in(state_gla)
    st_zero = jnp.zeros((nb, 1, 1, MIX, MIX), F32)
    kx = cache_swa_k.reshape(db, DEPTH, past, LANES)
    vx = cache_swa_v.reshape(db, DEPTH, past, LANES)
    mx = jnp.concatenate([cache_mla_ckv, jnp.tile(cache_mla_krope, (1, 1, 1, MLA_HEADS))], axis=-1).astype(BF)

    xc = x_prompt.reshape(nb * ns, D_MODEL)
    xl = x_sample.reshape(db * dsq, D_MODEL)
    st_gla, st_k, st_v, st_ckv, st_kr = [], [], [], [], []
    for l in range(DEPTH):
        xc, xl = _ffn_call(xc, xl, mod, gn, wg, wu, wd, l, dsq)
        mixed_c, new_ctx = _token_mix(xc, l, shared_ctx, st_zero, ((0, 0), (0, 0)), None, nb, ns, nb * ns, 0, False)
        st_f, st_b, k_c, v_c, ckv_c, kr_c = new_ctx
        st_gla.append(jnp.stack([_state_out(st_f), _state_out(st_b)], axis=1))
        st_k.append(k_c.reshape(nb, 1, ns, LANES))
        st_v.append(v_c.reshape(nb, 1, ns, LANES))
        st_ckv.append(ckv_c.reshape(nb, 1, ns, MLA_KV_RANK))
        st_kr.append(kr_c.reshape(nb, 1, ns, MLA_ROPE))
        mixed_l, _ = _token_mix(xl, l, shared_lat, st_lat, ((l, 0), (l, 1)), (kx, vx, mx), db, dsq, dsq, 1, True)
        xc, xl = _outffn_call(mixed_c, xc, mixed_l, xl, mod, gn, gout, avg, wo, wg, wu, wd, l, ns, dsq)

    cat = lambda parts: jnp.concatenate(parts, axis=1)
    kv_shape = (nb, DEPTH, ns, SWA_KV_HEADS, HEAD_DIM)
    return (xc.reshape(nb, ns, D_MODEL), xl.reshape(db, dsq, D_MODEL), jnp.stack(st_gla, axis=1),
            cat(st_k).reshape(kv_shape), cat(st_v).reshape(kv_shape), cat(st_ckv), cat(st_kr))
```

```python
import functools

import jax
import jax.numpy as jnp
from jax import lax
from jax.experimental import pallas as pl
from jax.experimental.pallas import tpu as pltpu

D_MODEL = 1024
DEPTH = 4
GRID_W = 64
HEAD_DIM = 64
GLA_HEADS = 4
GLA_DK = 64
GLA_DV = 64
GLA_GATE_RANK = 16
GLA_TAU = 16.0
GLA_CHUNK = 64
SWA_Q_HEADS = 4
SWA_KV_HEADS = 2
SWA_BLOCK = 128
FNET_GROUPS = 4
FNET_GROUP_CH = 64
MLA_HEADS = 4
MLA_Q_RANK = 256
MLA_KV_RANK = 128
MLA_NOPE = 64
MLA_ROPE = 32
MLA_V = 64
D_FF = 2816
FFN_RES = 0.5
N_MOD = 9
ROPE_BASE = 10000.0
EPS = 1e-6
NEG_INF = -1e30

MIX = 256
LANES = 128
MXU_DEPTH = 256
FF_CHUNK = 256
MLA_KEY_TILE = 2048
FFT_BLOCK = 16
GLA_BLOCK = 512
GLA_GROUP = 4
LOG2E = 1.4426950408889634
MOD_ROWS = 8
VMEM_LIMIT = 56 * 1024 * 1024

BF = jnp.bfloat16
F32 = jnp.float32

C_GQ, C_GK, C_GV, C_GR = 0, 256, 512, 768
C_ASK, C_SVKVA, C_SQ, C_ZF, C_QA, C_KR = 1024, 1280, 1536, 2048, 2304, 2560
C_SKR, C_SQR, C_KRR = 2688, 2816, 3328
NC_CTX, NC_LAT = 2688, 3456


def _dot(a, b):
    return jnp.dot(a, b, preferred_element_type=F32)


def _dot_nt(a, b):
    return lax.dot_general(a, b, (((1,), (1,)), ((), ())), preferred_element_type=F32)


def _dot_tn(a, b):
    return lax.dot_general(a, b, (((0,), (0,)), ((), ())), preferred_element_type=F32)


def _rms(x, g):
    return x * lax.rsqrt(jnp.mean(x * x, axis=-1, keepdims=True) + EPS) * g


def _silu(x):
    return x * jax.nn.sigmoid(x)


def _params(*sem):
    return pltpu.CompilerParams(dimension_semantics=sem, vmem_limit_bytes=VMEM_LIMIT)


def _pick(arr, *lead, block=None):
    tail = tuple(arr.shape[len(lead):]) if block is None else tuple(block)
    zeros = (0,) * len(tail)
    return pl.BlockSpec((None,) * len(lead) + tail, lambda *_: tuple(lead) + zeros,
                        pipeline_mode=pl.Buffered(1))


def _mod_kernel(c_ref, w_ref, b_ref, o_ref):
    s = _silu(c_ref[...]).astype(BF)
    o_ref[...] = _dot(s, w_ref[...].astype(BF)) + b_ref[...]


def _mod_call(cvec, w_mod, b_mod):
    nl = w_mod.shape[0]
    tn = D_MODEL
    return pl.pallas_call(
        _mod_kernel,
        grid=(nl, N_MOD * D_MODEL // tn),
        in_specs=[
            pl.BlockSpec((MOD_ROWS, D_MODEL), lambda l, j: (0, 0)),
            pl.BlockSpec((None, D_MODEL, tn), lambda l, j: (l, 0, j)),
            pl.BlockSpec((None, 1, tn), lambda l, j: (l, 0, j)),
        ],
        out_specs=pl.BlockSpec((None, MOD_ROWS, tn), lambda l, j: (l, 0, j)),
        out_shape=jax.ShapeDtypeStruct((nl, MOD_ROWS, N_MOD * D_MODEL), F32),
        compiler_params=_params("parallel", "parallel"),
        name="mod",
    )(cvec, w_mod, b_mod.reshape(nl, 1, N_MOD * D_MODEL))


def _mod_spec(l, sub, tm, rows_per_batch, first_row):
    per = rows_per_batch // tm
    return pl.BlockSpec((None, None, 3, 1, D_MODEL), lambda i: (l, first_row + i // per, sub, 0, 0))


def _fft_layout_spec(tm, seq):
    per = max(seq // tm, 1)
    nbat = max(tm // seq, 1)
    return pl.BlockSpec((tm // nbat, nbat * MIX), lambda i: (i % per, i // per))


def _gn_spec(l, first, n):
    return pl.BlockSpec((None, n, 1, D_MODEL), lambda i: (l, first // n, 0, 0))


def _ffn_body(x, mod_ref, g_pre, g_post, wg_ref, wu_ref, wd_ref):
    sh, sc, gt = mod_ref[0], mod_ref[1], mod_ref[2]
    hb = (_rms(x, g_pre) * (1.0 + sc) + sh).astype(BF)
    acc = None
    for j in range(D_FF // FF_CHUNK):
        sl = slice(j * FF_CHUNK, (j + 1) * FF_CHUNK)
        g = _dot(hb, wg_ref[:, sl])
        u = _dot(hb, wu_ref[:, sl])
        d = _dot((_silu(g) * u).astype(BF), wd_ref[sl, :])
        acc = d if acc is None else acc + d
    return x + FFN_RES * gt * _rms(acc, g_post)


def _retarget(spec, f):
    return pl.BlockSpec(spec.block_shape, lambda i: spec.index_map(f(i)))


def _two_groups(nc):
    return (lambda i: jnp.minimum(i, nc - 1)), (lambda i: jnp.maximum(i - nc, 0))


def _mod_spec2(l, sub, nc, per):
    return pl.BlockSpec((None, None, 3, 1, D_MODEL),
                        lambda i: (l, jnp.where(i < nc, 0, 1 + jnp.maximum(i - nc, 0) // per), sub, 0, 0))


def _ffn_kernel(xc_ref, xl_ref, mod_ref, gn_ref, wg_ref, wu_ref, wd_ref, oc_ref, ol_ref, *, nc):
    i = pl.program_id(0)

    @pl.when(i < nc)
    def _():
        oc_ref[...] = _ffn_body(xc_ref[...], mod_ref, gn_ref[0], gn_ref[1], wg_ref, wu_ref, wd_ref)

    @pl.when(i >= nc)
    def _():
        ol_ref[...] = _ffn_body(xl_ref[...], mod_ref, gn_ref[0], gn_ref[1], wg_ref, wu_ref, wd_ref)


def _ffn_call(xc, xl, mod, gn, wg, wu, wd, l, lat_seq, tm=512):
    nc, nl = xc.shape[0] // tm, xl.shape[0] // tm
    fc, fl = _two_groups(nc)
    row = pl.BlockSpec((tm, D_MODEL), lambda i: (i, 0))
    return pl.pallas_call(
        functools.partial(_ffn_kernel, nc=nc),
        grid=(nc + nl,),
        in_specs=[
            _retarget(row, fc), _retarget(row, fl),
            _mod_spec2(l, 0, nc, lat_seq // tm),
            _gn_spec(l, 0, 2),
            _pick(wg, l, 0), _pick(wu, l, 0), _pick(wd, l, 0),
        ],
        out_specs=[_retarget(row, fc), _retarget(row, fl)],
        out_shape=[jax.ShapeDtypeStruct(xc.shape, F32), jax.ShapeDtypeStruct(xl.shape, F32)],
        compiler_params=_params("arbitrary"),
        name="ffn",
    )(xc, xl, mod, gn, wg, wu, wd)


def _outffn_body(og_ref, gr_ref, os_ref, of_ref, om_ref, x_ref, mod2_ref, mod3_ref, gn_ref, gout_ref,
                 avg_ref, wo_ref, wg_ref, wu_ref, wd_ref):
    og = og_ref[...]
    ms = _dot((og * og).astype(BF), avg_ref[...])
    og = (og * lax.rsqrt(ms + EPS) * gout_ref[...] * gr_ref[...]).astype(BF)
    nbat = of_ref.shape[1] // MIX
    of = jnp.concatenate([of_ref[:, j * MIX:(j + 1) * MIX] for j in range(nbat)], axis=0).astype(BF)
    mix = jnp.concatenate([og, os_ref[...], of, om_ref[...]], axis=1)
    x = x_ref[...] + mod2_ref[2] * _rms(_dot(mix, wo_ref[...]), gn_ref[0])
    return _ffn_body(x, mod3_ref, gn_ref[1], gn_ref[2], wg_ref, wu_ref, wd_ref)


def _outffn_kernel(*refs, nc):
    ctx, lat, rest = refs[:6], refs[6:12], refs[12:]
    shared, (oc_ref, ol_ref) = rest[:-2], rest[-2:]
    i = pl.program_id(0)

    @pl.when(i < nc)
    def _():
        oc_ref[...] = _outffn_body(*ctx, *shared)

    @pl.when(i >= nc)
    def _():
        ol_ref[...] = _outffn_body(*lat, *shared)


def _outffn_call(mix_c, xc, mix_l, xl, mod, gn, gout, avg, wo, wg, wu, wd, l, ctx_seq, lat_seq, tm=512):
    nc, nl = xc.shape[0] // tm, xl.shape[0] // tm
    fc, fl = _two_groups(nc)
    mspec = pl.BlockSpec((tm, MIX), lambda i: (i, 0))
    row = pl.BlockSpec((tm, D_MODEL), lambda i: (i, 0))

    def group_specs(seq, f):
        return [_retarget(sp, f) for sp in (mspec, mspec, mspec, _fft_layout_spec(tm, seq), mspec, row)]

    per = lat_seq // tm
    return pl.pallas_call(
        functools.partial(_outffn_kernel, nc=nc),
        grid=(nc + nl,),
        in_specs=group_specs(ctx_seq, fc) + group_specs(lat_seq, fl) + [
            _mod_spec2(l, 1, nc, per), _mod_spec2(l, 2, nc, per),
            _gn_spec(l, 3, 3),
            _pick(gout, l), _pick(avg),
            _pick(wo, l), _pick(wg, l, 1), _pick(wu, l, 1), _pick(wd, l, 1),
        ],
        out_specs=[_retarget(row, fc), _retarget(row, fl)],
        out_shape=[jax.ShapeDtypeStruct(xc.shape, F32), jax.ShapeDtypeStruct(xl.shape, F32)],
        compiler_params=_params("arbitrary"),
        name="outffn",
    )(*mix_c, xc, *mix_l, xl, mod, mod, gn, gout, avg, wo, wg, wu, wd)


def _log_sigmoid(x):
    return -(jnp.maximum(-x, 0.0) + jnp.log(1.0 + jnp.exp(-jnp.abs(x))))


def _inproj_kernel(*refs, latent, nbat):
    (x_ref, mod_ref, gn_ref, w_ref, wgate_ref, bgate_ref, gq_ref, gkv_ref, wqb_ref, wkbd_ref,
     cdft_ref, sdft_ref) = refs[:12]
    pos = 12
    if latent:
        cs_ref, ss_ref, cm_ref, sm_ref = refs[pos:pos + 4]
        pos += 4
    (gq_o, gk_o, gv_o, gr_o, la_o, sq_o, sk_o, sv_o, zc_o, zs_o, qc_o, qr_o, kcat_o) = refs[pos:pos + 13]
    pos += 13
    if not latent:
        ckv_o, kr_o = refs[pos:pos + 2]

    x = x_ref[...]
    hb = (_rms(x, gn_ref[0]) * (1.0 + mod_ref[1]) + mod_ref[0]).astype(BF)

    def col(off, n):
        return _dot(hb, w_ref[:, off:off + n])

    qa = col(C_QA, MLA_Q_RANK)
    ask = col(C_ASK, 2 * LANES)
    a_in = ask[:, :LANES].astype(BF)
    zf = col(C_ZF, MIX).astype(BF)

    gq_o[...] = col(C_GQ, MIX) * (GLA_DK ** -0.5)
    gk_o[...] = col(C_GK, MIX)

    qm = _dot(_rms(qa, gq_ref[...]).astype(BF), wqb_ref[...])
    logit = _dot(a_in, wgate_ref[...]) + bgate_ref[...]
    la_o[...] = _log_sigmoid(logit) * (LOG2E / GLA_TAU)

    zc = _dot(zf, cdft_ref[...]).astype(zc_o.dtype)
    zs = _dot(zf, sdft_ref[...]).astype(zs_o.dtype)
    rows_b = zf.shape[0] // nbat
    for j in range(nbat):
        zc_o[:, j * MIX:(j + 1) * MIX] = zc[j * rows_b:(j + 1) * rows_b]
        zs_o[:, j * MIX:(j + 1) * MIX] = zs[j * rows_b:(j + 1) * rows_b]

    gv_o[...] = col(C_GV, MIX).astype(BF)
    gr_o[...] = _silu(col(C_GR, MIX))

    scale = (MLA_NOPE + MLA_ROPE) ** -0.5 * LOG2E
    qc_o[...] = (_dot(qm[:, :MIX].astype(BF), wkbd_ref[...]) * scale).astype(BF)
    q_rope = qm[:, MIX:MIX + LANES]
    if latent:
        cm, sm = cm_ref[...], sm_ref[...]
        q_rope = q_rope * cm + qm[:, MIX + LANES:MIX + 2 * LANES] * sm
    qr_o[...] = (q_rope * scale).astype(BF)

    sq = col(C_SQ, 2 * MIX)
    sk = ask[:, LANES:]
    svk = col(C_SVKVA, 2 * LANES)
    sv = svk[:, :LANES]
    if latent:
        krp = col(C_KR, 2 * LANES)
        kr = krp[:, :LANES]
        cs, ss = cs_ref[...], ss_ref[...]
        cs4 = jnp.concatenate([cs] * 4, axis=1)
        ss4 = jnp.concatenate([ss] * 4, axis=1)
        sq = sq * cs4 + col(C_SQR, 2 * MIX) * ss4
        sk = sk * cs + krp[:, LANES:] * ss
    else:
        kr = col(C_KR, LANES)
    sq_o[...] = (sq * (HEAD_DIM ** -0.5 * LOG2E)).astype(BF)
    sk_o[...] = sk.astype(sk_o.dtype)
    sv_o[...] = sv.astype(sv_o.dtype)

    ckv = _rms(svk[:, LANES:], gkv_ref[...])
    if not latent:
        ckv_o[...] = ckv
        kr_o[...] = kr[:, :MLA_ROPE]
    else:
        kr = kr * cm + col(C_KRR, LANES) * sm
    kcat_o[:, :LANES] = ckv.astype(BF)
    kcat_o[:, LANES:] = kr.astype(BF)


def _inproj_call(x, mod, gn, wts, gq, gkv, cdft, sdft, tables, l, batch, seq, rows_per_batch, first_row,
                 latent, tm):
    w, wgate, bgate, wqb, wkbd = wts
    r = x.shape[0]
    per = max(seq // tm, 1)
    nbat = max(tm // seq, 1)
    row = lambda n: pl.BlockSpec((tm, n), lambda i: (i, 0))
    ncol = NC_LAT if latent else NC_CTX
    nq = wqb.shape[-1] if latent else MIX + LANES
    in_specs = [
        row(D_MODEL),
        _mod_spec(l, 1, tm, rows_per_batch, first_row),
        _gn_spec(l, 2, 1),
        _pick(w, l, block=(D_MODEL, ncol)), _pick(wgate, l), _pick(bgate, l), _pick(gq, l), _pick(gkv, l),
        _pick(wqb, l, block=(MLA_Q_RANK, nq)), _pick(wkbd, l), _pick(cdft), _pick(sdft),
    ]
    args = [x, mod, gn, w, wgate, bgate, gq, gkv, wqb, wkbd, cdft, sdft]
    if latent:
        in_specs += [pl.BlockSpec((tm, LANES), lambda i: (i % per, 0))] * 4
        args += list(tables)
    kvdt = BF if latent else F32
    zdt = F32 if latent else BF
    fft_spec = _fft_layout_spec(tm, seq)
    out_specs = [row(MIX), row(MIX), row(MIX), row(MIX), row(2 * MIX), row(2 * MIX), row(LANES), row(LANES),
                 fft_spec, fft_spec, row(2 * MIX), row(LANES), row(MIX)]
    sd = jax.ShapeDtypeStruct
    out_shape = [sd((r, MIX), F32), sd((r, MIX), F32), sd((r, MIX), BF), sd((r, MIX), F32),
                 sd((r, 2 * MIX), F32), sd((r, 2 * MIX), BF), sd((r, LANES), kvdt), sd((r, LANES), kvdt),
                 sd((seq, batch * MIX), zdt), sd((seq, batch * MIX), zdt),
                 sd((r, 2 * MIX), BF), sd((r, LANES), BF), sd((r, MIX), BF)]
    if not latent:
        out_specs += [row(MLA_KV_RANK), row(MLA_ROPE)]
        out_shape += [sd((r, MLA_KV_RANK), F32), sd((r, MLA_ROPE), F32)]
    return pl.pallas_call(
        functools.partial(_inproj_kernel, latent=latent, nbat=nbat),
        grid=(r // tm,),
        in_specs=in_specs,
        out_specs=out_specs,
        out_shape=out_shape,
        compiler_params=_params("parallel"),
        name="inproj_lat" if latent else "inproj_ctx",
    )(*args)


def _chunk_cumsum(x, row, reverse):
    n = x.shape[0]
    k = 1
    while k < n:
        if k % 8 == 0:
            pad = jnp.zeros((k, x.shape[1]), x.dtype)
            shifted = jnp.concatenate([x[k:], pad] if reverse else [pad, x[:n - k]], axis=0)
        elif reverse:
            shifted = jnp.where(row < n - k, pltpu.roll(x, n - k, 0), 0.0)
        else:
            shifted = jnp.where(row >= k, pltpu.roll(x, k, 0), 0.0)
        x = x + shifted
        k *= 2
    return x


def _gla_kernel(q_ref, k_ref, v_ref, la_ref, s0_ref, *rest, reverse, final, nchunk, group):
    if final:
        of_ref, o_ref, sfin_ref, st_ref = rest
    else:
        o_ref, sfin_ref, st_ref = rest
    i = pl.program_id(1)
    c = GLA_CHUNK

    @pl.when(i == 0)
    def _():
        st_ref[...] = s0_ref[...]

    row = lax.broadcasted_iota(jnp.int32, (c, MIX), 0)
    lane = lax.broadcasted_iota(jnp.int32, (c, MIX), 1) % c
    keep = (lane >= row) if reverse else (lane <= row)
    lane1 = lax.broadcasted_iota(jnp.int32, (1, MIX), 1) // c
    hmask_b = [(lane1 == h).astype(BF) for h in range(GLA_HEADS)]
    bdiag = (lax.broadcasted_iota(jnp.int32, (MIX, MIX), 0) // c
             == lax.broadcasted_iota(jnp.int32, (MIX, MIX), 1) // c)

    def chunk(t, carry):
        cc = (nchunk - 1 - t) if reverse else t
        rs = pl.ds(pl.multiple_of(cc * c, c), c)

        def front(g):
            cum = _chunk_cumsum(la_ref[g, rs, :], row, reverse)
            tot = cum[0:1] if reverse else cum[c - 1:c]
            q, k, v = q_ref[g, rs, :], k_ref[g, rs, :], v_ref[g, rs, :]
            qd = (q * jnp.exp2(cum)).astype(BF)
            ki = (k * jnp.exp2(-cum)).astype(BF)
            kd = (k * jnp.exp2(tot - cum)).astype(BF)
            kstack = jnp.concatenate([ki * m for m in hmask_b], axis=0)
            vstack = jnp.concatenate([v * m for m in hmask_b], axis=0)
            return qd, vstack, _dot_nt(qd, kstack), _dot_tn(v, kd), jnp.exp2(tot)

        def back(g, qd, vstack, scores, upd, decay):
            att = jnp.where(keep, scores, 0.0).astype(BF)
            st = st_ref[g]
            o = _dot(att, vstack) + _dot_nt(qd, st.astype(BF))
            st_ref[g] = st * decay + jnp.where(bdiag, upd, 0.0)
            if final:
                o = o + of_ref[g, rs, :]
            o_ref[g, rs, :] = o

        pending = front(0)
        for g in range(group):
            cur = pending
            if g + 1 < group:
                pending = front(g + 1)
            back(g, *cur)
        return carry

    lax.fori_loop(0, nchunk, chunk, 0, unroll=2)

    @pl.when(i == pl.num_programs(1) - 1)
    def _():
        sfin_ref[...] = st_ref[...]


def _gla_call(q, k, v, la, s0, s0_idx, extra, batch, seq, reverse, tb, group=GLA_GROUP):
    nblk = seq // tb
    final = extra is not None
    pos = (lambda i: nblk - 1 - i) if reverse else (lambda i: i)
    blk = pl.BlockSpec((group, tb, MIX), lambda b, i: (b, pos(i), 0))
    lblk = pl.BlockSpec((group, tb, MIX), lambda b, i: (b, pos(i), 1 if reverse else 0))
    sblk = pl.BlockSpec((group, MIX, MIX), lambda b, i: (b, 0, 0))
    s0blk = pl.BlockSpec((group, None, None, MIX, MIX), lambda b, i: (b,) + tuple(s0_idx) + (0, 0))
    r3 = lambda a: a.reshape(batch, seq, a.shape[-1])
    in_specs = [blk, blk, blk, lblk, s0blk]
    args = [r3(q), r3(k), r3(v), r3(la), s0]
    if final:
        in_specs.append(blk)
        args.append(r3(extra))
    o, st = pl.pallas_call(
        functools.partial(_gla_kernel, reverse=reverse, final=final, nchunk=tb // GLA_CHUNK, group=group),
        grid=(batch // group, nblk),
        in_specs=in_specs,
        out_specs=[blk, sblk],
        out_shape=[jax.ShapeDtypeStruct((batch, seq, MIX), F32),
                   jax.ShapeDtypeStruct((batch, MIX, MIX), F32)],
        scratch_shapes=[pltpu.VMEM((group, MIX, MIX), F32)],
        compiler_params=_params("parallel", "arbitrary"),
        name="gla_bwd" if reverse else "gla_fwd",
    )(*args)
    return o.reshape(batch * seq, MIX), st


def _swa_kernel(*refs, local, layer):
    sink_ref, q_ref = refs[:2]
    pos = 2
    if local:
        kp_ref, kc_ref, kn_ref, vp_ref, vc_ref, vn_ref = refs[pos:pos + 6]
        pos += 6
    kx_ref, vx_ref, o_ref = refs[pos:pos + 3]
    i = pl.program_id(1)
    last = pl.num_programs(1) - 1
    blk = SWA_BLOCK
    tq = q_ref.shape[0]
    cx = kx_ref.shape[0]
    kx = kx_ref[...].astype(BF)
    vx = vx_ref[...].astype(BF)
    if local:
        nsub = tq // blk
        kblocks = [kp_ref[...]] + [kc_ref[j * blk:(j + 1) * blk, :] for j in range(nsub)] + [kn_ref[...]]
        vblocks = [vp_ref[...]] + [vc_ref[j * blk:(j + 1) * blk, :] for j in range(nsub)] + [vn_ref[...]]
        row = lax.broadcasted_iota(jnp.int32, (blk, blk), 0)
        col = lax.broadcasted_iota(jnp.int32, (blk, blk), 1)
        full = jnp.full((blk, blk), True)
        fullx = jnp.full((blk, cx), True)
    else:
        nsub = 1
    rq = tq // nsub
    rows = lax.broadcasted_iota(jnp.int32, (2 * rq, 1), 0)
    low = lax.broadcasted_iota(jnp.int32, (rq, LANES), 1) < HEAD_DIM
    units = []
    for j in range(nsub):
        if local:
            kcat = jnp.concatenate([kx, kblocks[j], kblocks[j + 1], kblocks[j + 2]], axis=0)
            vcat = jnp.concatenate([vx, vblocks[j], vblocks[j + 1], vblocks[j + 2]], axis=0)
            prev_ok = (col >= row) if j > 0 else jnp.logical_and(col >= row, i > 0)
            next_ok = (col <= row) if j < nsub - 1 else jnp.logical_and(col <= row, i < last)
            keep = jnp.concatenate([fullx, prev_ok, full, next_ok], axis=1)
            keep = jnp.concatenate([keep, keep], axis=0)
        else:
            kcat, vcat, keep = kx, vx, None
        for g in range(SWA_KV_HEADS):
            units.append((j, g, kcat, vcat, keep))

    def scores(u):
        j, g, kcat, _, _ = units[u]
        rs = slice(j * rq, (j + 1) * rq)
        q2 = jnp.concatenate([q_ref[rs, (2 * g) * LANES:(2 * g + 1) * LANES],
                              q_ref[rs, (2 * g + 1) * LANES:(2 * g + 2) * LANES]], axis=0)
        return _dot_nt(q2, kcat)

    tiles = {}
    s_next = scores(0)
    for u, (j, g, _, vcat, keep) in enumerate(units):
        s = s_next
        if u + 1 < len(units):
            s_next = scores(u + 1)
        sink = jnp.where(rows < rq, sink_ref[layer, 2 * g], sink_ref[layer, 2 * g + 1]) * LOG2E
        if local:
            s = jnp.where(keep, s, NEG_INF)
        m = jnp.maximum(jnp.max(s, axis=-1, keepdims=True), sink)
        e = jnp.exp2(s - m)
        den = jnp.sum(e, axis=-1, keepdims=True) + jnp.exp2(sink - m)
        o2 = _dot(e.astype(BF), vcat) / den
        oa, ob = o2[:rq], o2[rq:]
        if g == 0:
            tiles[(j, g)] = jnp.where(low, oa, pltpu.roll(ob, HEAD_DIM, 1))
        else:
            tiles[(j, g)] = jnp.where(low, pltpu.roll(oa, HEAD_DIM, 1), ob)
            rs = slice(j * rq, (j + 1) * rq)
            o_ref[rs, :LANES] = tiles[(j, 0)].astype(BF)
            o_ref[rs, LANES:] = tiles[(j, 1)].astype(BF)


def _swa_call(sinks, layer, q, k, v, kx, vx, x_layer, batch, seq, local, tq=256):
    nt = seq // tq
    per = tq // SWA_BLOCK
    nb = seq // SWA_BLOCK
    cx = kx.shape[2]
    qspec = pl.BlockSpec((tq, 2 * MIX), lambda b, i: (b * nt + i, 0))
    in_specs = [pl.BlockSpec(memory_space=pltpu.SMEM), qspec]
    args = [sinks, q]
    if local:
        prv = pl.BlockSpec((SWA_BLOCK, LANES), lambda b, i: (b * nb + jnp.maximum(i * per - 1, 0), 0))
        cur = pl.BlockSpec((tq, LANES), lambda b, i: (b * nt + i, 0))
        nxt = pl.BlockSpec((SWA_BLOCK, LANES), lambda b, i: (b * nb + jnp.minimum((i + 1) * per, nb - 1), 0))
        in_specs += [prv, cur, nxt, prv, cur, nxt]
        args += [k, k, k, v, v, v]
    xspec = pl.BlockSpec((None, None, cx, LANES), lambda b, i: (b, x_layer, 0, 0))
    in_specs += [xspec, xspec]
    args += [kx, vx]
    return pl.pallas_call(
        functools.partial(_swa_kernel, local=local, layer=layer),
        grid=(batch, nt),
        in_specs=in_specs,
        out_specs=pl.BlockSpec((tq, MIX), lambda b, i: (b * nt + i, 0)),
        out_shape=jax.ShapeDtypeStruct((batch * seq, MIX), BF),
        compiler_params=_params("parallel", "parallel"),
        name="swa_lat" if local else "swa_ctx",
    )(*args)


def _fft_kernel(xc_ref, xs_ref, yc_ref, ys_ref, zc_ref, zs_ref, o_ref, *, nj, scale):
    yc, ys = yc_ref[...], ys_ref[...]
    xc, xs = xc_ref[...], xs_ref[...]
    acc = None
    for jp in range(nj // 2):
        cos_t, sin_t = [], []
        for j in (2 * jp, 2 * jp + 1):
            a, b = xc[:, j:j + 1], xs[:, j:j + 1]
            cos_t.append(a * yc - b * ys)
            sin_t.append(b * yc + a * ys)
        cos_t = jnp.concatenate(cos_t, axis=1).astype(BF)
        sin_t = jnp.concatenate(sin_t, axis=1).astype(BF)
        rows = slice(jp * MXU_DEPTH, (jp + 1) * MXU_DEPTH)
        d = _dot(cos_t, zc_ref[rows, :]) + _dot(sin_t, zs_ref[rows, :])
        acc = d if acc is None else acc + d
    o_ref[...] = (acc * scale).astype(BF)


def _fft_tables(seq):
    sp = jnp.arange(seq, dtype=jnp.int32)[:, None]
    w = 2.0 * jnp.pi / seq
    ax = ((sp * (jnp.arange(seq // LANES, dtype=jnp.int32) * LANES)[None, :]) % seq).astype(F32) * w
    ay = ((sp * jnp.arange(LANES, dtype=jnp.int32)[None, :]) % seq).astype(F32) * w
    return jnp.cos(ax), jnp.sin(ax), jnp.cos(ay), jnp.sin(ay)


def _fft_call(tables, zc, zs, seq, tm=256):
    xc, xs, yc, ys = tables
    nj = seq // LANES
    width = zc.shape[1]
    return pl.pallas_call(
        functools.partial(_fft_kernel, nj=nj, scale=(seq * FNET_GROUP_CH) ** -0.5),
        grid=(seq // tm,),
        in_specs=[
            pl.BlockSpec((tm, nj), lambda i: (i, 0)), pl.BlockSpec((tm, nj), lambda i: (i, 0)),
            pl.BlockSpec((tm, LANES), lambda i: (i, 0)), pl.BlockSpec((tm, LANES), lambda i: (i, 0)),
            _pick(zc), _pick(zs),
        ],
        out_specs=pl.BlockSpec((tm, width), lambda i: (i, 0)),
        out_shape=jax.ShapeDtypeStruct((seq, width), BF),
        compiler_params=_params("parallel"),
        name="fft",
    )(xc, xs, yc, ys, zc, zs)


def _fft2_kernel(zc_ref, zs_ref, m1_ref, m2_ref, tr_ref, ti_ref, o_ref, x_ref, *, nb1, scale):
    i = pl.program_id(1)
    n = m2_ref.shape[0]
    fb = FFT_BLOCK

    @pl.when(i < nb1)
    def _():
        m1 = m1_ref[...]
        for j in range(fb):
            z = jnp.concatenate([zc_ref[:, j, :], zs_ref[:, j, :]], axis=0).astype(BF)
            x1 = _dot(m1, z)
            xr, xi = x1[:n], x1[n:]
            tr, ti = tr_ref[:, j:j + 1], ti_ref[:, j:j + 1]
            x2 = jnp.concatenate([xr * tr - xi * ti, xr * ti + xi * tr], axis=0)
            x_ref[i * fb + j] = x2.reshape(2 * n // 8, 8, x2.shape[-1])

    @pl.when(i >= nb1)
    def _():
        m2 = m2_ref[...]
        blk = (i - nb1) * (fb // 8)
        for j in range(fb):
            z = jnp.concatenate([x_ref[:, blk + j // 8, j % 8, :],
                                 x_ref[:, n // 8 + blk + j // 8, j % 8, :]], axis=0).astype(BF)
            o_ref[:, j, :] = _dot(m2, z) * scale


def _fft2_call(zc, zs, seq, wl=512):
    n = int(round(seq ** 0.5))
    fb = FFT_BLOCK
    assert seq == n * n and n % fb == 0
    width = zc.shape[1]
    nb = n // fb
    k = jnp.arange(n, dtype=jnp.int32)
    ang = ((k[:, None] * k[None, :]) % n).astype(F32) * (2.0 * jnp.pi / n)
    c, sn = jnp.cos(ang), jnp.sin(ang)
    m1 = jnp.concatenate([jnp.concatenate([c, sn], axis=1), jnp.concatenate([-sn, c], axis=1)], axis=0).astype(BF)
    m2 = jnp.concatenate([c, sn], axis=1).astype(BF)
    tw = (k[:, None] * k[None, :]).astype(F32) * (2.0 * jnp.pi / seq)
    tr = jnp.cos(tw).reshape(n, nb, fb).transpose(1, 0, 2)
    ti = (-jnp.sin(tw)).reshape(n, nb, fb).transpose(1, 0, 2)
    zspec = pl.BlockSpec((n, fb, wl), lambda h, i: (0, jnp.minimum(i, nb - 1), h))
    tspec = pl.BlockSpec((None, n, fb), lambda h, i: (jnp.minimum(i, nb - 1), 0, 0))
    out = pl.pallas_call(
        functools.partial(_fft2_kernel, nb1=nb, scale=(seq * FNET_GROUP_CH) ** -0.5),
        grid=(width // wl, 2 * nb),
        in_specs=[zspec, zspec, _pick(m1), _pick(m2), tspec, tspec],
        out_specs=pl.BlockSpec((n, fb, wl), lambda h, i: (0, jnp.maximum(i - nb, 0), h)),
        out_shape=jax.ShapeDtypeStruct((n, n, width), F32),
        scratch_shapes=[pltpu.VMEM((n, 2 * n // 8, 8, wl), F32)],
        compiler_params=_params("parallel", "arbitrary"),
        name="fft2",
    )(zc.reshape(n, n, width), zs.reshape(n, n, width), m1, m2, tr, ti)
    return out.reshape(seq, width)


def _mla_kernel(*refs, has_ctx):
    qc_ref, qr_ref, ks_ref = refs[:3]
    pos = 3
    if has_ctx:
        kx_ref = refs[pos]
        pos += 1
    wv_ref, o_ref = refs[pos:pos + 2]
    qr = qr_ref[...]
    lane = lax.broadcasted_iota(jnp.int32, qr.shape, 1) // MLA_ROPE
    tq = qr.shape[0]
    qs = []
    for h in range(MLA_HEADS):
        qrh = jnp.where(lane == h, qr, jnp.zeros_like(qr))
        qs.append(jnp.concatenate([qc_ref[:, h * LANES:(h + 1) * LANES], qrh], axis=1))
    qall = jnp.concatenate(qs, axis=0)
    tiles = [(kx_ref, 0, kx_ref.shape[0])] if has_ctx else []
    nk = ks_ref.shape[0]
    tk = min(nk, MLA_KEY_TILE)
    tiles += [(ks_ref, j * tk, tk) for j in range(nk // tk)]
    m = den = acc = None
    for ref, lo, n in tiles:
        kt = ref[lo:lo + n, :]
        s = _dot_nt(qall, kt)
        mt = jnp.max(s, axis=-1, keepdims=True)
        if m is None:
            m_new = mt
        else:
            m_new = jnp.maximum(m, mt)
            alpha = jnp.exp2(m - m_new)
        p = jnp.exp2(s - m_new)
        pv = _dot(p.astype(BF), kt[:, :LANES])
        ps = jnp.sum(p, axis=-1, keepdims=True)
        if m is None:
            den, acc = ps, pv
        else:
            den = alpha * den + ps
            acc = alpha * acc + pv
        m = m_new
    ot = acc / den
    ocat = jnp.concatenate([ot[h * tq:(h + 1) * tq] for h in range(MLA_HEADS)], axis=1).astype(BF)
    o_ref[...] = _dot(ocat, wv_ref[...]).astype(BF)


def _mla_call(qc, qr, kcat, kx, wvbd, l, batch, seq, tq=256):
    nq = seq // tq
    has_ctx = kx is not None
    in_specs = [pl.BlockSpec((tq, 2 * MIX), lambda b, i: (b * nq + i, 0)),
                pl.BlockSpec((tq, LANES), lambda b, i: (b * nq + i, 0)),
                pl.BlockSpec((None, seq, MIX), lambda b, i: (b, 0, 0))]
    args = [qc, qr, kcat.reshape(batch, seq, MIX)]
    if has_ctx:
        in_specs.append(pl.BlockSpec((None, None, kx.shape[2], MIX), lambda b, i: (b, l, 0, 0)))
        args.append(kx)
    in_specs.append(_pick(wvbd, l))
    args.append(wvbd)
    return pl.pallas_call(
        functools.partial(_mla_kernel, has_ctx=has_ctx),
        grid=(batch, nq),
        in_specs=in_specs,
        out_specs=pl.BlockSpec((tq, MIX), lambda b, i: (b * nq + i, 0)),
        out_shape=jax.ShapeDtypeStruct((batch * seq, MIX), BF),
        compiler_params=_params("parallel", "parallel"),
        name="mla_lat" if has_ctx else "mla_ctx",
    )(*args)


def _rot_cols(w, width):
    half = width // 2
    parts = w.reshape(w.shape[:-1] + (w.shape[-1] // width, 2, half))
    return jnp.stack([-parts[..., 1, :], parts[..., 0, :]], axis=-2).reshape(w.shape)


def _block_diag(blocks):
    n, r, c = blocks.shape[-3:]
    eye = jnp.eye(n, dtype=blocks.dtype)
    out = jnp.einsum("...hrc,hg->...hrgc", blocks, eye)
    return out.reshape(blocks.shape[:-3] + (n * r, n * c))


def _layer_weights(w_in, gla_w_gate, gla_b_gate, mla_w_q_b, mla_w_kv_b):
    nl = w_in.shape[0]
    w_in = w_in.astype(BF)
    offs = [0]
    for n in (256, 256, 256, 256, 16, 16, 256, 128, 128, 256, 256, 128, 32):
        offs.append(offs[-1] + n)
    seg = lambda k: w_in[..., offs[k]:offs[k + 1]]
    zeros = lambda n: jnp.zeros((nl, D_MODEL, n), w_in.dtype)
    sq = seg(6).reshape(nl, D_MODEL, SWA_Q_HEADS, HEAD_DIM)
    sq_tiles = []
    for hq in range(SWA_Q_HEADS):
        parts = [zeros(HEAD_DIM), zeros(HEAD_DIM)]
        parts[hq // 2] = sq[:, :, hq]
        sq_tiles += parts
    sq_w = jnp.concatenate(sq_tiles, axis=-1)
    kr4 = jnp.tile(seg(12), (1, 1, MLA_HEADS))
    cols = [seg(0), seg(1), seg(2), seg(3), seg(4), seg(5), zeros(LANES - 2 * GLA_GATE_RANK), seg(7),
            seg(8), seg(11), sq_w, seg(9), seg(10), kr4,
            _rot_cols(seg(7), HEAD_DIM // 2), _rot_cols(sq_w, HEAD_DIM // 2), _rot_cols(kr4, MLA_ROPE // 2)]
    w = jnp.concatenate(cols, axis=-1)

    wgate = jnp.zeros((nl, LANES, 2 * MIX), F32)
    wgate = wgate.at[:, :GLA_GATE_RANK, :MIX].set(gla_w_gate[:, 0])
    wgate = wgate.at[:, GLA_GATE_RANK:2 * GLA_GATE_RANK, MIX:].set(gla_w_gate[:, 1])
    bgate = gla_b_gate.reshape(nl, 1, 2 * MIX)

    wq = mla_w_q_b.reshape(nl, MLA_Q_RANK, MLA_HEADS, MLA_NOPE + MLA_ROPE)
    q_nope = wq[..., :MLA_NOPE].reshape(nl, MLA_Q_RANK, MLA_HEADS * MLA_NOPE)
    q_rope = wq[..., MLA_NOPE:].reshape(nl, MLA_Q_RANK, MLA_HEADS * MLA_ROPE)
    wqb = jnp.concatenate([q_nope, q_rope, _rot_cols(q_rope, MLA_ROPE // 2)], axis=-1).astype(BF)
    wkv = mla_w_kv_b.reshape(nl, MLA_KV_RANK, MLA_HEADS, MLA_NOPE + MLA_V)
    wkbd = _block_diag(jnp.transpose(wkv[..., :MLA_NOPE], (0, 2, 3, 1))).astype(BF)
    wvbd = _block_diag(jnp.transpose(wkv[..., MLA_NOPE:], (0, 2, 1, 3))).astype(BF)
    return (w, wgate.astype(BF), bgate, wqb, wkbd), wvbd


def _rope_tables(seq):
    t = jnp.arange(seq)
    rows = (t // GRID_W).astype(F32)[:, None]
    cols = (t % GRID_W).astype(F32)[:, None]

    def table(width):
        half = width // 2
        lane = jnp.arange(LANES)
        inv = ROPE_BASE ** (-(2 * (lane % half)).astype(F32) / width)
        ang = jnp.where(((lane // width) % 2 == 0)[None, :], rows, cols) * inv[None, :]
        return jnp.cos(ang), jnp.sin(ang)

    cs, ss = table(HEAD_DIM // 2)
    cm, sm = table(MLA_ROPE // 2)
    return cs, ss, cm, sm


def _dft_channel():
    c = jnp.arange(FNET_GROUP_CH, dtype=jnp.int32)
    ang = ((c[:, None] * c[None, :]) % FNET_GROUP_CH).astype(F32) * (2.0 * jnp.pi / FNET_GROUP_CH)
    cos_b = jnp.broadcast_to(jnp.cos(ang)[None], (FNET_GROUPS, FNET_GROUP_CH, FNET_GROUP_CH))
    sin_b = jnp.broadcast_to(jnp.sin(ang)[None], (FNET_GROUPS, FNET_GROUP_CH, FNET_GROUP_CH))
    return _block_diag(cos_b).astype(BF), (-_block_diag(sin_b)).astype(BF)


def _state_in(s):
    return _block_diag(jnp.swapaxes(s, -1, -2))


def _state_out(st):
    blocks = [st[:, h * GLA_DV:(h + 1) * GLA_DV, h * GLA_DK:(h + 1) * GLA_DK] for h in range(GLA_HEADS)]
    return jnp.swapaxes(jnp.stack(blocks, axis=1), 2, 3)


def _token_mix(x, l, shared, st0, st0_idx, cache, batch, seq, rows_per_batch, first_row, latent):
    mod, gn, wts, wvbd, gq, gkv, sinks, cdft, sdft, rope_t, fft_t = shared
    outs = _inproj_call(x, mod, gn, wts, gq, gkv, cdft, sdft, rope_t, l, batch, seq, rows_per_batch,
                        first_row, latent, tm=512)
    gqv, gkv_, gv, gr, la, sq, sk, sv, zc, zs, qc, qr, kcat = outs[:13]
    tb = min(seq, GLA_BLOCK)
    o_f, st_f = _gla_call(gqv, gkv_, gv, la, st0, st0_idx[0], None, batch, seq, False, tb)
    o_gla, st_b = _gla_call(gqv, gkv_, gv, la, st0, st0_idx[1], o_f, batch, seq, True, tb)
    if latent:
        kx, vx, mx = cache
        o_swa = _swa_call(sinks, l, sq, sk, sv, kx, vx, l, batch, seq, True, tq=1024)
        o_mla = _mla_call(qc, qr, kcat, mx, wvbd, l, batch, seq)
        new_ctx = None
    else:
        k4 = sk.reshape(batch, 1, seq, LANES)
        v4 = sv.reshape(batch, 1, seq, LANES)
        o_swa = _swa_call(sinks, l, sq, None, None, k4, v4, 0, batch, seq, False)
        o_mla = _mla_call(qc, qr, kcat, None, wvbd, l, batch, seq)
        new_ctx = (st_f, st_b, sk, sv, outs[13], outs[14])
    o_fft = _fft2_call(zc, zs, seq) if latent else _fft_call(fft_t, zc, zs, seq)
    return (o_gla, gr, o_swa, o_fft, o_mla), new_ctx


def kernel(x_prompt, x_sample, c, state_gla, cache_swa_k, cache_swa_v, cache_mla_ckv, cache_mla_krope,
           c_ctx, w_mod, b_mod, g_norm, w_ffn_gate, w_ffn_up, w_ffn_down, w_in, gla_w_gate, gla_b_gate,
           gla_g_out, swa_sink, mla_g_q, mla_g_kv, mla_w_q_b, mla_w_kv_b, w_out):
    nb, ns, _ = x_prompt.shape
    db, dsq, _ = x_sample.shape
    past = cache_swa_k.shape[2]

    cvec = jnp.zeros((MOD_ROWS, D_MODEL), F32).at[0].set(c_ctx).at[1:1 + db].set(c)
    mod = _mod_call(cvec, w_mod, b_mod).reshape(DEPTH, MOD_ROWS, N_MOD, 1, D_MODEL)
    gn = g_norm.reshape(DEPTH, 6, 1, D_MODEL)

    wg, wu, wd, wo = (a.astype(BF) for a in (w_ffn_gate, w_ffn_up, w_ffn_down, w_out))
    wts, wvbd = _layer_weights(w_in, gla_w_gate, gla_b_gate, mla_w_q_b, mla_w_kv_b)
    gq = mla_g_q.reshape(DEPTH, 1, MLA_Q_RANK)
    gkv = mla_g_kv.reshape(DEPTH, 1, MLA_KV_RANK)
    gout = jnp.tile(gla_g_out, (1, GLA_HEADS)).reshape(DEPTH, 1, MIX)
    cdft, sdft = _dft_channel()
    shared = (mod, gn, wts, wvbd, gq, gkv, swa_sink, cdft, sdft)
    head = jnp.arange(MIX) // GLA_DV
    avg = jnp.where(head[:, None] == head[None, :], 1.0 / GLA_DV, 0.0).astype(BF)
    shared_ctx = shared + (None, _fft_tables(ns))
    shared_lat = shared + (_rope_tables(dsq), None)

    st_lat = _state_in(state_gla)
    st_zero = jnp.zeros((nb, 1, 1, MIX, MIX), F32)
    kx = cache_swa_k.reshape(db, DEPTH, past, LANES)
    vx = cache_swa_v.reshape(db, DEPTH, past, LANES)
    mx = jnp.concatenate([cache_mla_ckv, jnp.tile(cache_mla_krope, (1, 1, 1, MLA_HEADS))], axis=-1).astype(BF)

    xc = x_prompt.reshape(nb * ns, D_MODEL)
    xl = x_sample.reshape(db * dsq, D_MODEL)
    st_gla, st_k, st_v, st_ckv, st_kr = [], [], [], [], []
    for l in range(DEPTH):
        xc, xl = _ffn_call(xc, xl, mod, gn, wg, wu, wd, l, dsq)
        mixed_c, new_ctx = _token_mix(xc, l, shared_ctx, st_zero, ((0, 0), (0, 0)), None, nb, ns, nb * ns, 0, False)
        st_f, st_b, k_c, v_c, ckv_c, kr_c = new_ctx
        st_gla.append(jnp.stack([_state_out(st_f), _state_out(st_b)], axis=1))
        st_k.append(k_c.reshape(nb, ns, SWA_KV_HEADS, HEAD_DIM))
        st_v.append(v_c.reshape(nb, ns, SWA_KV_HEADS, HEAD_DIM))
        st_ckv.append(ckv_c.reshape(nb, ns, MLA_KV_RANK))
        st_kr.append(kr_c.reshape(nb, ns, MLA_ROPE))
        mixed_l, _ = _token_mix(xl, l, shared_lat, st_lat, ((l, 0), (l, 1)), (kx, vx, mx), db, dsq, dsq, 1, True)
        xc, xl = _outffn_call(mixed_c, xc, mixed_l, xl, mod, gn, gout, avg, wo, wg, wu, wd, l, ns, dsq)

    return (xc.reshape(nb, ns, D_MODEL), xl.reshape(db, dsq, D_MODEL), jnp.stack(st_gla, axis=1),
            jnp.stack(st_k, axis=1), jnp.stack(st_v, axis=1), jnp.stack(st_ckv, axis=1),
            jnp.stack(st_kr, axis=1))
```

```python
import functools

import jax
import jax.numpy as jnp
from jax import lax
from jax.experimental import pallas as pl
from jax.experimental.pallas import tpu as pltpu

D_MODEL = 1024
DEPTH = 4
GRID_W = 64
HEAD_DIM = 64
GLA_HEADS = 4
GLA_DK = 64
GLA_DV = 64
GLA_GATE_RANK = 16
GLA_TAU = 16.0
GLA_CHUNK = 64
SWA_Q_HEADS = 4
SWA_KV_HEADS = 2
SWA_BLOCK = 128
FNET_GROUPS = 4
FNET_GROUP_CH = 64
MLA_HEADS = 4
MLA_Q_RANK = 256
MLA_KV_RANK = 128
MLA_NOPE = 64
MLA_ROPE = 32
MLA_V = 64
D_FF = 2816
FFN_RES = 0.5
N_MOD = 9
ROPE_BASE = 10000.0
EPS = 1e-6
NEG_INF = -1e30

MIX = 256
LANES = 128
MXU_DEPTH = 256
FF_CHUNK = 256
MLA_KEY_TILE = 2048
FFT_BLOCK = 16
GLA_BLOCK = 512
GLA_GROUP = 4
LOG2E = 1.4426950408889634
MOD_ROWS = 8
VMEM_LIMIT = 56 * 1024 * 1024

BF = jnp.bfloat16
F32 = jnp.float32

C_GQ, C_GK, C_GV, C_GR = 0, 256, 512, 768
C_ASK, C_SVKVA, C_SQ, C_ZF, C_QA, C_KR = 1024, 1280, 1536, 2048, 2304, 2560
C_SKR, C_SQR, C_KRR = 2688, 2816, 3328
NC_CTX, NC_LAT = 2688, 3456


def _dot(a, b):
    return jnp.dot(a, b, preferred_element_type=F32)


def _dot_nt(a, b):
    return lax.dot_general(a, b, (((1,), (1,)), ((), ())), preferred_element_type=F32)


def _dot_tn(a, b):
    return lax.dot_general(a, b, (((0,), (0,)), ((), ())), preferred_element_type=F32)


def _rms(x, g):
    return x * lax.rsqrt(jnp.mean(x * x, axis=-1, keepdims=True) + EPS) * g


def _silu(x):
    return x * jax.nn.sigmoid(x)


def _params(*sem):
    return pltpu.CompilerParams(dimension_semantics=sem, vmem_limit_bytes=VMEM_LIMIT)


def _pick(arr, *lead, block=None):
    tail = tuple(arr.shape[len(lead):]) if block is None else tuple(block)
    zeros = (0,) * len(tail)
    return pl.BlockSpec((None,) * len(lead) + tail, lambda *_: tuple(lead) + zeros,
                        pipeline_mode=pl.Buffered(1))


def _mod_kernel(c_ref, w_ref, b_ref, o_ref):
    s = _silu(c_ref[...]).astype(BF)
    o_ref[...] = _dot(s, w_ref[...].astype(BF)) + b_ref[...]


def _mod_call(cvec, w_mod, b_mod):
    nl = w_mod.shape[0]
    tn = D_MODEL
    return pl.pallas_call(
        _mod_kernel,
        grid=(nl, N_MOD * D_MODEL // tn),
        in_specs=[
            pl.BlockSpec((MOD_ROWS, D_MODEL), lambda l, j: (0, 0)),
            pl.BlockSpec((None, D_MODEL, tn), lambda l, j: (l, 0, j)),
            pl.BlockSpec((None, 1, tn), lambda l, j: (l, 0, j)),
        ],
        out_specs=pl.BlockSpec((None, MOD_ROWS, tn), lambda l, j: (l, 0, j)),
        out_shape=jax.ShapeDtypeStruct((nl, MOD_ROWS, N_MOD * D_MODEL), F32),
        compiler_params=_params("parallel", "parallel"),
        name="mod",
    )(cvec, w_mod, b_mod.reshape(nl, 1, N_MOD * D_MODEL))


def _mod_spec(l, sub, tm, rows_per_batch, first_row):
    per = rows_per_batch // tm
    return pl.BlockSpec((None, None, 3, 1, D_MODEL), lambda i: (l, first_row + i // per, sub, 0, 0))


def _fft_layout_spec(tm, seq):
    per = max(seq // tm, 1)
    nbat = max(tm // seq, 1)
    return pl.BlockSpec((tm // nbat, nbat * MIX), lambda i: (i % per, i // per))


def _gn_spec(l, first, n):
    return pl.BlockSpec((None, n, 1, D_MODEL), lambda i: (l, first // n, 0, 0))


def _ffn_body(x, mod_ref, g_pre, g_post, wg_ref, wu_ref, wd_ref):
    sh, sc, gt = mod_ref[0], mod_ref[1], mod_ref[2]
    hb = (_rms(x, g_pre) * (1.0 + sc) + sh).astype(BF)
    acc = None
    for j in range(D_FF // FF_CHUNK):
        sl = slice(j * FF_CHUNK, (j + 1) * FF_CHUNK)
        g = _dot(hb, wg_ref[:, sl])
        u = _dot(hb, wu_ref[:, sl])
        d = _dot((_silu(g) * u).astype(BF), wd_ref[sl, :])
        acc = d if acc is None else acc + d
    return x + FFN_RES * gt * _rms(acc, g_post)


def _retarget(spec, f):
    return pl.BlockSpec(spec.block_shape, lambda i: spec.index_map(f(i)))


def _two_groups(nc):
    return (lambda i: jnp.minimum(i, nc - 1)), (lambda i: jnp.maximum(i - nc, 0))


def _mod_spec2(l, sub, nc, per):
    return pl.BlockSpec((None, None, 3, 1, D_MODEL),
                        lambda i: (l, jnp.where(i < nc, 0, 1 + jnp.maximum(i - nc, 0) // per), sub, 0, 0))


def _ffn_kernel(xc_ref, xl_ref, mod_ref, gn_ref, wg_ref, wu_ref, wd_ref, oc_ref, ol_ref, *, nc):
    i = pl.program_id(0)

    @pl.when(i < nc)
    def _():
        oc_ref[...] = _ffn_body(xc_ref[...], mod_ref, gn_ref[0], gn_ref[1], wg_ref, wu_ref, wd_ref)

    @pl.when(i >= nc)
    def _():
        ol_ref[...] = _ffn_body(xl_ref[...], mod_ref, gn_ref[0], gn_ref[1], wg_ref, wu_ref, wd_ref)


def _ffn_call(xc, xl, mod, gn, wg, wu, wd, l, lat_seq, tm=512):
    nc, nl = xc.shape[0] // tm, xl.shape[0] // tm
    fc, fl = _two_groups(nc)
    row = pl.BlockSpec((tm, D_MODEL), lambda i: (i, 0))
    return pl.pallas_call(
        functools.partial(_ffn_kernel, nc=nc),
        grid=(nc + nl,),
        in_specs=[
            _retarget(row, fc), _retarget(row, fl),
            _mod_spec2(l, 0, nc, lat_seq // tm),
            _gn_spec(l, 0, 2),
            _pick(wg, l, 0), _pick(wu, l, 0), _pick(wd, l, 0),
        ],
        out_specs=[_retarget(row, fc), _retarget(row, fl)],
        out_shape=[jax.ShapeDtypeStruct(xc.shape, F32), jax.ShapeDtypeStruct(xl.shape, F32)],
        compiler_params=_params("arbitrary"),
        name="ffn",
    )(xc, xl, mod, gn, wg, wu, wd)


def _outffn_body(og_ref, gr_ref, os_ref, of_ref, om_ref, x_ref, mod2_ref, mod3_ref, gn_ref, gout_ref,
                 avg_ref, wo_ref, wg_ref, wu_ref, wd_ref):
    og = og_ref[...]
    ms = _dot((og * og).astype(BF), avg_ref[...])
    og = (og * lax.rsqrt(ms + EPS) * gout_ref[...] * gr_ref[...]).astype(BF)
    nbat = of_ref.shape[1] // MIX
    of = jnp.concatenate([of_ref[:, j * MIX:(j + 1) * MIX] for j in range(nbat)], axis=0).astype(BF)
    mix = jnp.concatenate([og, os_ref[...], of, om_ref[...]], axis=1)
    x = x_ref[...] + mod2_ref[2] * _rms(_dot(mix, wo_ref[...]), gn_ref[0])
    return _ffn_body(x, mod3_ref, gn_ref[1], gn_ref[2], wg_ref, wu_ref, wd_ref)


def _outffn_kernel(*refs, nc):
    ctx, lat, rest = refs[:6], refs[6:12], refs[12:]
    shared, (oc_ref, ol_ref) = rest[:-2], rest[-2:]
    i = pl.program_id(0)

    @pl.when(i < nc)
    def _():
        oc_ref[...] = _outffn_body(*ctx, *shared)

    @pl.when(i >= nc)
    def _():
        ol_ref[...] = _outffn_body(*lat, *shared)


def _outffn_call(mix_c, xc, mix_l, xl, mod, gn, gout, avg, wo, wg, wu, wd, l, ctx_seq, lat_seq, tm=512):
    nc, nl = xc.shape[0] // tm, xl.shape[0] // tm
    fc, fl = _two_groups(nc)
    mspec = pl.BlockSpec((tm, MIX), lambda i: (i, 0))
    row = pl.BlockSpec((tm, D_MODEL), lambda i: (i, 0))

    def group_specs(seq, f):
        return [_retarget(sp, f) for sp in (mspec, mspec, mspec, _fft_layout_spec(tm, seq), mspec, row)]

    per = lat_seq // tm
    return pl.pallas_call(
        functools.partial(_outffn_kernel, nc=nc),
        grid=(nc + nl,),
        in_specs=group_specs(ctx_seq, fc) + group_specs(lat_seq, fl) + [
            _mod_spec2(l, 1, nc, per), _mod_spec2(l, 2, nc, per),
            _gn_spec(l, 3, 3),
            _pick(gout, l), _pick(avg),
            _pick(wo, l), _pick(wg, l, 1), _pick(wu, l, 1), _pick(wd, l, 1),
        ],
        out_specs=[_retarget(row, fc), _retarget(row, fl)],
        out_shape=[jax.ShapeDtypeStruct(xc.shape, F32), jax.ShapeDtypeStruct(xl.shape, F32)],
        compiler_params=_params("arbitrary"),
        name="outffn",
    )(*mix_c, xc, *mix_l, xl, mod, mod, gn, gout, avg, wo, wg, wu, wd)


def _log_sigmoid(x):
    return -(jnp.maximum(-x, 0.0) + jnp.log(1.0 + jnp.exp(-jnp.abs(x))))


def _inproj_kernel(*refs, latent, nbat):
    (x_ref, mod_ref, gn_ref, w_ref, wgate_ref, bgate_ref, gq_ref, gkv_ref, wqb_ref, wkbd_ref,
     cdft_ref, sdft_ref) = refs[:12]
    pos = 12
    if latent:
        cs_ref, ss_ref, cm_ref, sm_ref = refs[pos:pos + 4]
        pos += 4
    (gq_o, gk_o, gv_o, gr_o, la_o, sq_o, sk_o, sv_o, zc_o, zs_o, qc_o, qr_o, kcat_o) = refs[pos:pos + 13]
    pos += 13
    if not latent:
        ckv_o, kr_o = refs[pos:pos + 2]

    x = x_ref[...]
    hb = (_rms(x, gn_ref[0]) * (1.0 + mod_ref[1]) + mod_ref[0]).astype(BF)

    def col(off, n):
        return _dot(hb, w_ref[:, off:off + n])

    qa = col(C_QA, MLA_Q_RANK)
    ask = col(C_ASK, 2 * LANES)
    a_in = ask[:, :LANES].astype(BF)
    zf = col(C_ZF, MIX).astype(BF)

    gq_o[...] = col(C_GQ, MIX) * (GLA_DK ** -0.5)
    gk_o[...] = col(C_GK, MIX)

    qm = _dot(_rms(qa, gq_ref[...]).astype(BF), wqb_ref[...])
    logit = _dot(a_in, wgate_ref[...]) + bgate_ref[...]
    la_o[...] = _log_sigmoid(logit) * (LOG2E / GLA_TAU)

    zc = _dot(zf, cdft_ref[...]).astype(zc_o.dtype)
    zs = _dot(zf, sdft_ref[...]).astype(zs_o.dtype)
    rows_b = zf.shape[0] // nbat
    for j in range(nbat):
        zc_o[:, j * MIX:(j + 1) * MIX] = zc[j * rows_b:(j + 1) * rows_b]
        zs_o[:, j * MIX:(j + 1) * MIX] = zs[j * rows_b:(j + 1) * rows_b]

    gv_o[...] = col(C_GV, MIX).astype(BF)
    gr_o[...] = _silu(col(C_GR, MIX))

    scale = (MLA_NOPE + MLA_ROPE) ** -0.5 * LOG2E
    qc_o[...] = (_dot(qm[:, :MIX].astype(BF), wkbd_ref[...]) * scale).astype(BF)
    q_rope = qm[:, MIX:MIX + LANES]
    if latent:
        cm, sm = cm_ref[...], sm_ref[...]
        q_rope = q_rope * cm + qm[:, MIX + LANES:MIX + 2 * LANES] * sm
    qr_o[...] = (q_rope * scale).astype(BF)

    sq = col(C_SQ, 2 * MIX)
    sk = ask[:, LANES:]
    svk = col(C_SVKVA, 2 * LANES)
    sv = svk[:, :LANES]
    if latent:
        krp = col(C_KR, 2 * LANES)
        kr = krp[:, :LANES]
        cs, ss = cs_ref[...], ss_ref[...]
        cs4 = jnp.concatenate([cs] * 4, axis=1)
        ss4 = jnp.concatenate([ss] * 4, axis=1)
        sq = sq * cs4 + col(C_SQR, 2 * MIX) * ss4
        sk = sk * cs + krp[:, LANES:] * ss
    else:
        kr = col(C_KR, LANES)
    sq_o[...] = (sq * (HEAD_DIM ** -0.5 * LOG2E)).astype(BF)
    sk_o[...] = sk.astype(sk_o.dtype)
    sv_o[...] = sv.astype(sv_o.dtype)

    ckv = _rms(svk[:, LANES:], gkv_ref[...])
    if not latent:
        ckv_o[...] = ckv
        kr_o[...] = kr[:, :MLA_ROPE]
    else:
        kr = kr * cm + col(C_KRR, LANES) * sm
    kcat_o[:, :LANES] = ckv.astype(BF)
    kcat_o[:, LANES:] = kr.astype(BF)


def _inproj_call(x, mod, gn, wts, gq, gkv, cdft, sdft, tables, l, batch, seq, rows_per_batch, first_row,
                 latent, tm):
    w, wgate, bgate, wqb, wkbd = wts
    r = x.shape[0]
    per = max(seq // tm, 1)
    nbat = max(tm // seq, 1)
    row = lambda n: pl.BlockSpec((tm, n), lambda i: (i, 0))
    ncol = NC_LAT if latent else NC_CTX
    nq = wqb.shape[-1] if latent else MIX + LANES
    in_specs = [
        row(D_MODEL),
        _mod_spec(l, 1, tm, rows_per_batch, first_row),
        _gn_spec(l, 2, 1),
        _pick(w, l, block=(D_MODEL, ncol)), _pick(wgate, l), _pick(bgate, l), _pick(gq, l), _pick(gkv, l),
        _pick(wqb, l, block=(MLA_Q_RANK, nq)), _pick(wkbd, l), _pick(cdft), _pick(sdft),
    ]
    args = [x, mod, gn, w, wgate, bgate, gq, gkv, wqb, wkbd, cdft, sdft]
    if latent:
        in_specs += [pl.BlockSpec((tm, LANES), lambda i: (i % per, 0))] * 4
        args += list(tables)
    kvdt = BF if latent else F32
    zdt = F32 if latent else BF
    fft_spec = _fft_layout_spec(tm, seq)
    out_specs = [row(MIX), row(MIX), row(MIX), row(MIX), row(2 * MIX), row(2 * MIX), row(LANES), row(LANES),
                 fft_spec, fft_spec, row(2 * MIX), row(LANES), row(MIX)]
    sd = jax.ShapeDtypeStruct
    out_shape = [sd((r, MIX), F32), sd((r, MIX), F32), sd((r, MIX), BF), sd((r, MIX), F32),
                 sd((r, 2 * MIX), F32), sd((r, 2 * MIX), BF), sd((r, LANES), kvdt), sd((r, LANES), kvdt),
                 sd((seq, batch * MIX), zdt), sd((seq, batch * MIX), zdt),
                 sd((r, 2 * MIX), BF), sd((r, LANES), BF), sd((r, MIX), BF)]
    if not latent:
        out_specs += [row(MLA_KV_RANK), row(MLA_ROPE)]
        out_shape += [sd((r, MLA_KV_RANK), F32), sd((r, MLA_ROPE), F32)]
    return pl.pallas_call(
        functools.partial(_inproj_kernel, latent=latent, nbat=nbat),
        grid=(r // tm,),
        in_specs=in_specs,
        out_specs=out_specs,
        out_shape=out_shape,
        compiler_params=_params("parallel"),
        name="inproj_lat" if latent else "inproj_ctx",
    )(*args)


def _chunk_cumsum(x, row, reverse):
    n = x.shape[0]
    k = 1
    while k < n:
        if k % 8 == 0:
            pad = jnp.zeros((k, x.shape[1]), x.dtype)
            shifted = jnp.concatenate([x[k:], pad] if reverse else [pad, x[:n - k]], axis=0)
        elif reverse:
            shifted = jnp.where(row < n - k, pltpu.roll(x, n - k, 0), 0.0)
        else:
            shifted = jnp.where(row >= k, pltpu.roll(x, k, 0), 0.0)
        x = x + shifted
        k *= 2
    return x


def _gla_kernel(q_ref, k_ref, v_ref, la_ref, s0_ref, *rest, reverse, final, nchunk, group):
    if final:
        of_ref, o_ref, sfin_ref, st_ref = rest
    else:
        o_ref, sfin_ref, st_ref = rest
    i = pl.program_id(1)
    c = GLA_CHUNK

    @pl.when(i == 0)
    def _():
        st_ref[...] = s0_ref[...]

    row = lax.broadcasted_iota(jnp.int32, (c, MIX), 0)
    lane = lax.broadcasted_iota(jnp.int32, (c, MIX), 1) % c
    keep = (lane >= row) if reverse else (lane <= row)
    lane1 = lax.broadcasted_iota(jnp.int32, (1, MIX), 1) // c
    hmask_b = [(lane1 == h).astype(BF) for h in range(GLA_HEADS)]
    bdiag = (lax.broadcasted_iota(jnp.int32, (MIX, MIX), 0) // c
             == lax.broadcasted_iota(jnp.int32, (MIX, MIX), 1) // c)

    def chunk(t, carry):
        cc = (nchunk - 1 - t) if reverse else t
        rs = pl.ds(pl.multiple_of(cc * c, c), c)

        def front(g):
            cum = _chunk_cumsum(la_ref[g, rs, :], row, reverse)
            tot = cum[0:1] if reverse else cum[c - 1:c]
            q, k, v = q_ref[g, rs, :], k_ref[g, rs, :], v_ref[g, rs, :]
            qd = (q * jnp.exp2(cum)).astype(BF)
            ki = (k * jnp.exp2(-cum)).astype(BF)
            kd = (k * jnp.exp2(tot - cum)).astype(BF)
            kstack = jnp.concatenate([ki * m for m in hmask_b], axis=0)
            vstack = jnp.concatenate([v * m for m in hmask_b], axis=0)
            return qd, vstack, _dot_nt(qd, kstack), _dot_tn(v, kd), jnp.exp2(tot)

        def back(g, qd, vstack, scores, upd, decay):
            att = jnp.where(keep, scores, 0.0).astype(BF)
            st = st_ref[g]
            o = _dot(att, vstack) + _dot_nt(qd, st.astype(BF))
            st_ref[g] = st * decay + jnp.where(bdiag, upd, 0.0)
            if final:
                o = o + of_ref[g, rs, :]
            o_ref[g, rs, :] = o

        pending = front(0)
        for g in range(group):
            cur = pending
            if g + 1 < group:
                pending = front(g + 1)
            back(g, *cur)
        return carry

    lax.fori_loop(0, nchunk, chunk, 0, unroll=2)

    @pl.when(i == pl.num_programs(1) - 1)
    def _():
        sfin_ref[...] = st_ref[...]


def _gla_call(q, k, v, la, s0, s0_idx, extra, batch, seq, reverse, tb, group=GLA_GROUP):
    nblk = seq // tb
    final = extra is not None
    pos = (lambda i: nblk - 1 - i) if reverse else (lambda i: i)
    blk = pl.BlockSpec((group, tb, MIX), lambda b, i: (b, pos(i), 0))
    lblk = pl.BlockSpec((group, tb, MIX), lambda b, i: (b, pos(i), 1 if reverse else 0))
    sblk = pl.BlockSpec((group, MIX, MIX), lambda b, i: (b, 0, 0))
    s0blk = pl.BlockSpec((group, None, None, MIX, MIX), lambda b, i: (b,) + tuple(s0_idx) + (0, 0))
    r3 = lambda a: a.reshape(batch, seq, a.shape[-1])
    in_specs = [blk, blk, blk, lblk, s0blk]
    args = [r3(q), r3(k), r3(v), r3(la), s0]
    if final:
        in_specs.append(blk)
        args.append(r3(extra))
    o, st = pl.pallas_call(
        functools.partial(_gla_kernel, reverse=reverse, final=final, nchunk=tb // GLA_CHUNK, group=group),
        grid=(batch // group, nblk),
        in_specs=in_specs,
        out_specs=[blk, sblk],
        out_shape=[jax.ShapeDtypeStruct((batch, seq, MIX), F32),
                   jax.ShapeDtypeStruct((batch, MIX, MIX), F32)],
        scratch_shapes=[pltpu.VMEM((group, MIX, MIX), F32)],
        compiler_params=_params("parallel", "arbitrary"),
        name="gla_bwd" if reverse else "gla_fwd",
    )(*args)
    return o.reshape(batch * seq, MIX), st


def _swa_kernel(*refs, local, layer):
    sink_ref, q_ref = refs[:2]
    pos = 2
    if local:
        kp_ref, kc_ref, kn_ref, vp_ref, vc_ref, vn_ref = refs[pos:pos + 6]
        pos += 6
    kx_ref, vx_ref, o_ref = refs[pos:pos + 3]
    i = pl.program_id(1)
    last = pl.num_programs(1) - 1
    blk = SWA_BLOCK
    tq = q_ref.shape[0]
    cx = kx_ref.shape[0]
    kx = kx_ref[...].astype(BF)
    vx = vx_ref[...].astype(BF)
    if local:
        nsub = tq // blk
        kblocks = [kp_ref[...]] + [kc_ref[j * blk:(j + 1) * blk, :] for j in range(nsub)] + [kn_ref[...]]
        vblocks = [vp_ref[...]] + [vc_ref[j * blk:(j + 1) * blk, :] for j in range(nsub)] + [vn_ref[...]]
        row = lax.broadcasted_iota(jnp.int32, (blk, blk), 0)
        col = lax.broadcasted_iota(jnp.int32, (blk, blk), 1)
        full = jnp.full((blk, blk), True)
        fullx = jnp.full((blk, cx), True)
    else:
        nsub = 1
    rq = tq // nsub
    rows = lax.broadcasted_iota(jnp.int32, (2 * rq, 1), 0)
    low = lax.broadcasted_iota(jnp.int32, (rq, LANES), 1) < HEAD_DIM
    units = []
    for j in range(nsub):
        if local:
            kcat = jnp.concatenate([kx, kblocks[j], kblocks[j + 1], kblocks[j + 2]], axis=0)
            vcat = jnp.concatenate([vx, vblocks[j], vblocks[j + 1], vblocks[j + 2]], axis=0)
            prev_ok = (col >= row) if j > 0 else jnp.logical_and(col >= row, i > 0)
            next_ok = (col <= row) if j < nsub - 1 else jnp.logical_and(col <= row, i < last)
            keep = jnp.concatenate([fullx, prev_ok, full, next_ok], axis=1)
            keep = jnp.concatenate([keep, keep], axis=0)
        else:
            kcat, vcat, keep = kx, vx, None
        for g in range(SWA_KV_HEADS):
            units.append((j, g, kcat, vcat, keep))

    def scores(u):
        j, g, kcat, _, _ = units[u]
        rs = slice(j * rq, (j + 1) * rq)
        q2 = jnp.concatenate([q_ref[rs, (2 * g) * LANES:(2 * g + 1) * LANES],
                              q_ref[rs, (2 * g + 1) * LANES:(2 * g + 2) * LANES]], axis=0)
        return _dot_nt(q2, kcat)

    tiles = {}
    s_next = scores(0)
    for u, (j, g, _, vcat, keep) in enumerate(units):
        s = s_next
        if u + 1 < len(units):
            s_next = scores(u + 1)
        sink = jnp.where(rows < rq, sink_ref[layer, 2 * g], sink_ref[layer, 2 * g + 1]) * LOG2E
        if local:
            s = jnp.where(keep, s, NEG_INF)
        m = jnp.maximum(jnp.max(s, axis=-1, keepdims=True), sink)
        e = jnp.exp2(s - m)
        den = jnp.sum(e, axis=-1, keepdims=True) + jnp.exp2(sink - m)
        o2 = _dot(e.astype(BF), vcat) / den
        oa, ob = o2[:rq], o2[rq:]
        if g == 0:
            tiles[(j, g)] = jnp.where(low, oa, pltpu.roll(ob, HEAD_DIM, 1))
        else:
            tiles[(j, g)] = jnp.where(low, pltpu.roll(oa, HEAD_DIM, 1), ob)
            rs = slice(j * rq, (j + 1) * rq)
            o_ref[rs, :LANES] = tiles[(j, 0)].astype(BF)
            o_ref[rs, LANES:] = tiles[(j, 1)].astype(BF)


def _swa_call(sinks, layer, q, k, v, kx, vx, x_layer, batch, seq, local, tq=256):
    nt = seq // tq
    per = tq // SWA_BLOCK
    nb = seq // SWA_BLOCK
    cx = kx.shape[2]
    qspec = pl.BlockSpec((tq, 2 * MIX), lambda b, i: (b * nt + i, 0))
    in_specs = [pl.BlockSpec(memory_space=pltpu.SMEM), qspec]
    args = [sinks, q]
    if local:
        prv = pl.BlockSpec((SWA_BLOCK, LANES), lambda b, i: (b * nb + jnp.maximum(i * per - 1, 0), 0))
        cur = pl.BlockSpec((tq, LANES), lambda b, i: (b * nt + i, 0))
        nxt = pl.BlockSpec((SWA_BLOCK, LANES), lambda b, i: (b * nb + jnp.minimum((i + 1) * per, nb - 1), 0))
        in_specs += [prv, cur, nxt, prv, cur, nxt]
        args += [k, k, k, v, v, v]
    xspec = pl.BlockSpec((None, None, cx, LANES), lambda b, i: (b, x_layer, 0, 0))
    in_specs += [xspec, xspec]
    args += [kx, vx]
    return pl.pallas_call(
        functools.partial(_swa_kernel, local=local, layer=layer),
        grid=(batch, nt),
        in_specs=in_specs,
        out_specs=pl.BlockSpec((tq, MIX), lambda b, i: (b * nt + i, 0)),
        out_shape=jax.ShapeDtypeStruct((batch * seq, MIX), BF),
        compiler_params=_params("parallel", "parallel"),
        name="swa_lat" if local else "swa_ctx",
    )(*args)


def _fft_kernel(xc_ref, xs_ref, yc_ref, ys_ref, zc_ref, zs_ref, o_ref, *, nj, scale):
    yc, ys = yc_ref[...], ys_ref[...]
    xc, xs = xc_ref[...], xs_ref[...]
    acc = None
    for jp in range(nj // 2):
        cos_t, sin_t = [], []
        for j in (2 * jp, 2 * jp + 1):
            a, b = xc[:, j:j + 1], xs[:, j:j + 1]
            cos_t.append(a * yc - b * ys)
            sin_t.append(b * yc + a * ys)
        cos_t = jnp.concatenate(cos_t, axis=1).astype(BF)
        sin_t = jnp.concatenate(sin_t, axis=1).astype(BF)
        rows = slice(jp * MXU_DEPTH, (jp + 1) * MXU_DEPTH)
        d = _dot(cos_t, zc_ref[rows, :]) + _dot(sin_t, zs_ref[rows, :])
        acc = d if acc is None else acc + d
    o_ref[...] = (acc * scale).astype(BF)


def _fft_tables(seq):
    sp = jnp.arange(seq, dtype=jnp.int32)[:, None]
    w = 2.0 * jnp.pi / seq
    ax = ((sp * (jnp.arange(seq // LANES, dtype=jnp.int32) * LANES)[None, :]) % seq).astype(F32) * w
    ay = ((sp * jnp.arange(LANES, dtype=jnp.int32)[None, :]) % seq).astype(F32) * w
    return jnp.cos(ax), jnp.sin(ax), jnp.cos(ay), jnp.sin(ay)


def _fft_call(tables, zc, zs, seq, tm=256):
    xc, xs, yc, ys = tables
    nj = seq // LANES
    width = zc.shape[1]
    return pl.pallas_call(
        functools.partial(_fft_kernel, nj=nj, scale=(seq * FNET_GROUP_CH) ** -0.5),
        grid=(seq // tm,),
        in_specs=[
            pl.BlockSpec((tm, nj), lambda i: (i, 0)), pl.BlockSpec((tm, nj), lambda i: (i, 0)),
            pl.BlockSpec((tm, LANES), lambda i: (i, 0)), pl.BlockSpec((tm, LANES), lambda i: (i, 0)),
            _pick(zc), _pick(zs),
        ],
        out_specs=pl.BlockSpec((tm, width), lambda i: (i, 0)),
        out_shape=jax.ShapeDtypeStruct((seq, width), BF),
        compiler_params=_params("parallel"),
        name="fft",
    )(xc, xs, yc, ys, zc, zs)


def _fft2_kernel(zc_ref, zs_ref, m1_ref, m2_ref, tr_ref, ti_ref, o_ref, x_ref, *, nb1, scale):
    i = pl.program_id(1)
    n = m2_ref.shape[0]
    fb = FFT_BLOCK

    @pl.when(i < nb1)
    def _():
        m1 = m1_ref[...]
        for j in range(fb):
            z = jnp.concatenate([zc_ref[:, j, :], zs_ref[:, j, :]], axis=0).astype(BF)
            x1 = _dot(m1, z)
            xr, xi = x1[:n], x1[n:]
            tr, ti = tr_ref[:, j:j + 1], ti_ref[:, j:j + 1]
            x2 = jnp.concatenate([xr * tr - xi * ti, xr * ti + xi * tr], axis=0)
            x_ref[i * fb + j] = x2.reshape(2 * n // 8, 8, x2.shape[-1])

    @pl.when(i >= nb1)
    def _():
        m2 = m2_ref[...]
        blk = (i - nb1) * (fb // 8)
        for j in range(fb):
            z = jnp.concatenate([x_ref[:, blk + j // 8, j % 8, :],
                                 x_ref[:, n // 8 + blk + j // 8, j % 8, :]], axis=0).astype(BF)
            o_ref[:, j, :] = _dot(m2, z) * scale


def _fft2_call(zc, zs, seq, wl=512):
    n = int(round(seq ** 0.5))
    fb = FFT_BLOCK
    assert seq == n * n and n % fb == 0
    width = zc.shape[1]
    nb = n // fb
    k = jnp.arange(n, dtype=jnp.int32)
    ang = ((k[:, None] * k[None, :]) % n).astype(F32) * (2.0 * jnp.pi / n)
    c, sn = jnp.cos(ang), jnp.sin(ang)
    m1 = jnp.concatenate([jnp.concatenate([c, sn], axis=1), jnp.concatenate([-sn, c], axis=1)], axis=0).astype(BF)
    m2 = jnp.concatenate([c, sn], axis=1).astype(BF)
    tw = (k[:, None] * k[None, :]).astype(F32) * (2.0 * jnp.pi / seq)
    tr = jnp.cos(tw).reshape(n, nb, fb).transpose(1, 0, 2)
    ti = (-jnp.sin(tw)).reshape(n, nb, fb).transpose(1, 0, 2)
    zspec = pl.BlockSpec((n, fb, wl), lambda h, i: (0, jnp.minimum(i, nb - 1), h))
    tspec = pl.BlockSpec((None, n, fb), lambda h, i: (jnp.minimum(i, nb - 1), 0, 0))
    out = pl.pallas_call(
        functools.partial(_fft2_kernel, nb1=nb, scale=(seq * FNET_GROUP_CH) ** -0.5),
        grid=(width // wl, 2 * nb),
        in_specs=[zspec, zspec, _pick(m1), _pick(m2), tspec, tspec],
        out_specs=pl.BlockSpec((n, fb, wl), lambda h, i: (0, jnp.maximum(i - nb, 0), h)),
        out_shape=jax.ShapeDtypeStruct((n, n, width), F32),
        scratch_shapes=[pltpu.VMEM((n, 2 * n // 8, 8, wl), F32)],
        compiler_params=_params("parallel", "arbitrary"),
        name="fft2",
    )(zc.reshape(n, n, width), zs.reshape(n, n, width), m1, m2, tr, ti)
    return out.reshape(seq, width)


def _mla_kernel(*refs, has_ctx):
    qc_ref, qr_ref, ks_ref = refs[:3]
    pos = 3
    if has_ctx:
        kx_ref = refs[pos]
        pos += 1
    wv_ref, o_ref = refs[pos:pos + 2]
    qr = qr_ref[...]
    lane = lax.broadcasted_iota(jnp.int32, qr.shape, 1) // MLA_ROPE
    tq = qr.shape[0]
    qs = []
    for h in range(MLA_HEADS):
        qrh = jnp.where(lane == h, qr, jnp.zeros_like(qr))
        qs.append(jnp.concatenate([qc_ref[:, h * LANES:(h + 1) * LANES], qrh], axis=1))
    qall = jnp.concatenate(qs, axis=0)
    tiles = [(kx_ref, 0, kx_ref.shape[0])] if has_ctx else []
    nk = ks_ref.shape[0]
    tk = min(nk, MLA_KEY_TILE)
    tiles += [(ks_ref, j * tk, tk) for j in range(nk // tk)]
    m = den = acc = None
    for ref, lo, n in tiles:
        kt = ref[lo:lo + n, :]
        s = _dot_nt(qall, kt)
        mt = jnp.max(s, axis=-1, keepdims=True)
        if m is None:
            m_new = mt
        else:
            m_new = jnp.maximum(m, mt)
            alpha = jnp.exp2(m - m_new)
        p = jnp.exp2(s - m_new)
        pv = _dot(p.astype(BF), kt[:, :LANES])
        ps = jnp.sum(p, axis=-1, keepdims=True)
        if m is None:
            den, acc = ps, pv
        else:
            den = alpha * den + ps
            acc = alpha * acc + pv
        m = m_new
    ot = acc / den
    ocat = jnp.concatenate([ot[h * tq:(h + 1) * tq] for h in range(MLA_HEADS)], axis=1).astype(BF)
    o_ref[...] = _dot(ocat, wv_ref[...]).astype(BF)


def _mla_call(qc, qr, kcat, kx, wvbd, l, batch, seq, tq=256):
    nq = seq // tq
    has_ctx = kx is not None
    in_specs = [pl.BlockSpec((tq, 2 * MIX), lambda b, i: (b * nq + i, 0)),
                pl.BlockSpec((tq, LANES), lambda b, i: (b * nq + i, 0)),
                pl.BlockSpec((None, seq, MIX), lambda b, i: (b, 0, 0))]
    args = [qc, qr, kcat.reshape(batch, seq, MIX)]
    if has_ctx:
        in_specs.append(pl.BlockSpec((None, None, kx.shape[2], MIX), lambda b, i: (b, l, 0, 0)))
        args.append(kx)
    in_specs.append(_pick(wvbd, l))
    args.append(wvbd)
    return pl.pallas_call(
        functools.partial(_mla_kernel, has_ctx=has_ctx),
        grid=(batch, nq),
        in_specs=in_specs,
        out_specs=pl.BlockSpec((tq, MIX), lambda b, i: (b * nq + i, 0)),
        out_shape=jax.ShapeDtypeStruct((batch * seq, MIX), BF),
        compiler_params=_params("parallel", "parallel"),
        name="mla_lat" if has_ctx else "mla_ctx",
    )(*args)


def _rot_cols(w, width):
    half = width // 2
    parts = w.reshape(w.shape[:-1] + (w.shape[-1] // width, 2, half))
    return jnp.stack([-parts[..., 1, :], parts[..., 0, :]], axis=-2).reshape(w.shape)


def _block_diag(blocks):
    n, r, c = blocks.shape[-3:]
    eye = jnp.eye(n, dtype=blocks.dtype)
    out = jnp.einsum("...hrc,hg->...hrgc", blocks, eye)
    return out.reshape(blocks.shape[:-3] + (n * r, n * c))


def _layer_weights(w_in, gla_w_gate, gla_b_gate, mla_w_q_b, mla_w_kv_b):
    nl = w_in.shape[0]
    w_in = w_in.astype(BF)
    offs = [0]
    for n in (256, 256, 256, 256, 16, 16, 256, 128, 128, 256, 256, 128, 32):
        offs.append(offs[-1] + n)
    seg = lambda k: w_in[..., offs[k]:offs[k + 1]]
    zeros = lambda n: jnp.zeros((nl, D_MODEL, n), w_in.dtype)
    sq = seg(6).reshape(nl, D_MODEL, SWA_Q_HEADS, HEAD_DIM)
    sq_tiles = []
    for hq in range(SWA_Q_HEADS):
        parts = [zeros(HEAD_DIM), zeros(HEAD_DIM)]
        parts[hq // 2] = sq[:, :, hq]
        sq_tiles += parts
    sq_w = jnp.concatenate(sq_tiles, axis=-1)
    kr4 = jnp.tile(seg(12), (1, 1, MLA_HEADS))
    cols = [seg(0), seg(1), seg(2), seg(3), seg(4), seg(5), zeros(LANES - 2 * GLA_GATE_RANK), seg(7),
            seg(8), seg(11), sq_w, seg(9), seg(10), kr4,
            _rot_cols(seg(7), HEAD_DIM // 2), _rot_cols(sq_w, HEAD_DIM // 2), _rot_cols(kr4, MLA_ROPE // 2)]
    w = jnp.concatenate(cols, axis=-1)

    wgate = jnp.zeros((nl, LANES, 2 * MIX), F32)
    wgate = wgate.at[:, :GLA_GATE_RANK, :MIX].set(gla_w_gate[:, 0])
    wgate = wgate.at[:, GLA_GATE_RANK:2 * GLA_GATE_RANK, MIX:].set(gla_w_gate[:, 1])
    bgate = gla_b_gate.reshape(nl, 1, 2 * MIX)

    wq = mla_w_q_b.reshape(nl, MLA_Q_RANK, MLA_HEADS, MLA_NOPE + MLA_ROPE)
    q_nope = wq[..., :MLA_NOPE].reshape(nl, MLA_Q_RANK, MLA_HEADS * MLA_NOPE)
    q_rope = wq[..., MLA_NOPE:].reshape(nl, MLA_Q_RANK, MLA_HEADS * MLA_ROPE)
    wqb = jnp.concatenate([q_nope, q_rope, _rot_cols(q_rope, MLA_ROPE // 2)], axis=-1).astype(BF)
    wkv = mla_w_kv_b.reshape(nl, MLA_KV_RANK, MLA_HEADS, MLA_NOPE + MLA_V)
    wkbd = _block_diag(jnp.transpose(wkv[..., :MLA_NOPE], (0, 2, 3, 1))).astype(BF)
    wvbd = _block_diag(jnp.transpose(wkv[..., MLA_NOPE:], (0, 2, 1, 3))).astype(BF)
    return (w, wgate.astype(BF), bgate, wqb, wkbd), wvbd


def _rope_tables(seq):
    t = jnp.arange(seq)
    rows = (t // GRID_W).astype(F32)[:, None]
    cols = (t % GRID_W).astype(F32)[:, None]

    def table(width):
        half = width // 2
        lane = jnp.arange(LANES)
        inv = ROPE_BASE ** (-(2 * (lane % half)).astype(F32) / width)
        ang = jnp.where(((lane // width) % 2 == 0)[None, :], rows, cols) * inv[None, :]
        return jnp.cos(ang), jnp.sin(ang)

    cs, ss = table(HEAD_DIM // 2)
    cm, sm = table(MLA_ROPE // 2)
    return cs, ss, cm, sm


def _dft_channel():
    c = jnp.arange(FNET_GROUP_CH, dtype=jnp.int32)
    ang = ((c[:, None] * c[None, :]) % FNET_GROUP_CH).astype(F32) * (2.0 * jnp.pi / FNET_GROUP_CH)
    cos_b = jnp.broadcast_to(jnp.cos(ang)[None], (FNET_GROUPS, FNET_GROUP_CH, FNET_GROUP_CH))
    sin_b = jnp.broadcast_to(jnp.sin(ang)[None], (FNET_GROUPS, FNET_GROUP_CH, FNET_GROUP_CH))
    return _block_diag(cos_b).astype(BF), (-_block_diag(sin_b)).astype(BF)


def _state_in(s):
    return _block_diag(jnp.swapaxes(s, -1, -2))


def _state_out(st):
    blocks = [st[:, h * GLA_DV:(h + 1) * GLA_DV, h * GLA_DK:(h + 1) * GLA_DK] for h in range(GLA_HEADS)]
    return jnp.swapaxes(jnp.stack(blocks, axis=1), 2, 3)


def _token_mix(x, l, shared, st0, st0_idx, cache, batch, seq, rows_per_batch, first_row, latent):
    mod, gn, wts, wvbd, gq, gkv, sinks, cdft, sdft, rope_t, fft_t = shared
    outs = _inproj_call(x, mod, gn, wts, gq, gkv, cdft, sdft, rope_t, l, batch, seq, rows_per_batch,
                        first_row, latent, tm=1024 if latent else 512)
    gqv, gkv_, gv, gr, la, sq, sk, sv, zc, zs, qc, qr, kcat = outs[:13]
    tb = min(seq, GLA_BLOCK)
    o_f, st_f = _gla_call(gqv, gkv_, gv, la, st0, st0_idx[0], None, batch, seq, False, tb)
    o_gla, st_b = _gla_call(gqv, gkv_, gv, la, st0, st0_idx[1], o_f, batch, seq, True, tb)
    if latent:
        kx, vx, mx = cache
        o_swa = _swa_call(sinks, l, sq, sk, sv, kx, vx, l, batch, seq, True, tq=1024)
        o_mla = _mla_call(qc, qr, kcat, mx, wvbd, l, batch, seq)
        new_ctx = None
    else:
        k4 = sk.reshape(batch, 1, seq, LANES)
        v4 = sv.reshape(batch, 1, seq, LANES)
        o_swa = _swa_call(sinks, l, sq, None, None, k4, v4, 0, batch, seq, False)
        o_mla = _mla_call(qc, qr, kcat, None, wvbd, l, batch, seq)
        new_ctx = (st_f, st_b, sk, sv, outs[13], outs[14])
    o_fft = _fft2_call(zc, zs, seq) if latent else _fft_call(fft_t, zc, zs, seq)
    return (o_gla, gr, o_swa, o_fft, o_mla), new_ctx


def kernel(x_prompt, x_sample, c, state_gla, cache_swa_k, cache_swa_v, cache_mla_ckv, cache_mla_krope,
           c_ctx, w_mod, b_mod, g_norm, w_ffn_gate, w_ffn_up, w_ffn_down, w_in, gla_w_gate, gla_b_gate,
           gla_g_out, swa_sink, mla_g_q, mla_g_kv, mla_w_q_b, mla_w_kv_b, w_out):
    nb, ns, _ = x_prompt.shape
    db, dsq, _ = x_sample.shape
    past = cache_swa_k.shape[2]

    cvec = jnp.zeros((MOD_ROWS, D_MODEL), F32).at[0].set(c_ctx).at[1:1 + db].set(c)
    mod = _mod_call(cvec, w_mod, b_mod).reshape(DEPTH, MOD_ROWS, N_MOD, 1, D_MODEL)
    gn = g_norm.reshape(DEPTH, 6, 1, D_MODEL)

    wg, wu, wd, wo = (a.astype(BF) for a in (w_ffn_gate, w_ffn_up, w_ffn_down, w_out))
    wts, wvbd = _layer_weights(w_in, gla_w_gate, gla_b_gate, mla_w_q_b, mla_w_kv_b)
    gq = mla_g_q.reshape(DEPTH, 1, MLA_Q_RANK)
    gkv = mla_g_kv.reshape(DEPTH, 1, MLA_KV_RANK)
    gout = jnp.tile(gla_g_out, (1, GLA_HEADS)).reshape(DEPTH, 1, MIX)
    cdft, sdft = _dft_channel()
    shared = (mod, gn, wts, wvbd, gq, gkv, swa_sink, cdft, sdft)
    head = jnp.arange(MIX) // GLA_DV
    avg = jnp.where(head[:, None] == head[None, :], 1.0 / GLA_DV, 0.0).astype(BF)
    shared_ctx = shared + (None, _fft_tables(ns))
    shared_lat = shared + (_rope_tables(dsq), None)

    st_lat = _state_in(state_gla)
    st_zero = jnp.zeros((nb, 1, 1, MIX, MIX), F32)
    kx = cache_swa_k.reshape(db, DEPTH, past, LANES)
    vx = cache_swa_v.reshape(db, DEPTH, past, LANES)
    mx = jnp.concatenate([cache_mla_ckv, jnp.tile(cache_mla_krope, (1, 1, 1, MLA_HEADS))], axis=-1).astype(BF)

    xc = x_prompt.reshape(nb * ns, D_MODEL)
    xl = x_sample.reshape(db * dsq, D_MODEL)
    st_gla, st_k, st_v, st_ckv, st_kr = [], [], [], [], []
    for l in range(DEPTH):
        xc, xl = _ffn_call(xc, xl, mod, gn, wg, wu, wd, l, dsq)
        mixed_c, new_ctx = _token_mix(xc, l, shared_ctx, st_zero, ((0, 0), (0, 0)), None, nb, ns, nb * ns, 0, False)
        st_f, st_b, k_c, v_c, ckv_c, kr_c = new_ctx
        st_gla.append(jnp.stack([_state_out(st_f), _state_out(st_b)], axis=1))
        st_k.append(k_c.reshape(nb, ns, SWA_KV_HEADS, HEAD_DIM))
        st_v.append(v_c.reshape(nb, ns, SWA_KV_HEADS, HEAD_DIM))
        st_ckv.append(ckv_c.reshape(nb, ns, MLA_KV_RANK))
        st_kr.append(kr_c.reshape(nb, ns, MLA_ROPE))
        mixed_l, _ = _token_mix(xl, l, shared_lat, st_lat, ((l, 0), (l, 1)), (kx, vx, mx), db, dsq, dsq, 1, True)
        xc, xl = _outffn_call(mixed_c, xc, mixed_l, xl, mod, gn, gout, avg, wo, wg, wu, wd, l, ns, dsq)

    return (xc.reshape(nb, ns, D_MODEL), xl.reshape(db, dsq, D_MODEL), jnp.stack(st_gla, axis=1),
            jnp.stack(st_k, axis=1), jnp.stack(st_v, axis=1), jnp.stack(st_ckv, axis=1),
            jnp.stack(st_kr, axis=1))
```

```python
import functools

import jax
import jax.numpy as jnp
from jax import lax
from jax.experimental import pallas as pl
from jax.experimental.pallas import tpu as pltpu

D_MODEL = 1024
DEPTH = 4
GRID_W = 64
HEAD_DIM = 64
GLA_HEADS = 4
GLA_DK = 64
GLA_DV = 64
GLA_GATE_RANK = 16
GLA_TAU = 16.0
GLA_CHUNK = 64
SWA_Q_HEADS = 4
SWA_KV_HEADS = 2
SWA_BLOCK = 128
FNET_GROUPS = 4
FNET_GROUP_CH = 64
MLA_HEADS = 4
MLA_Q_RANK = 256
MLA_KV_RANK = 128
MLA_NOPE = 64
MLA_ROPE = 32
MLA_V = 64
D_FF = 2816
FFN_RES = 0.5
N_MOD = 9
ROPE_BASE = 10000.0
EPS = 1e-6
NEG_INF = -1e30

MIX = 256
LANES = 128
MXU_DEPTH = 256
FF_CHUNK = 256
MLA_KEY_TILE = 2048
FFT_BLOCK = 16
GLA_BLOCK = 512
CTX_GROUP = 4
GLA_GROUP = 4
LOG2E = 1.4426950408889634
MOD_ROWS = 8
VMEM_LIMIT = 56 * 1024 * 1024

BF = jnp.bfloat16
F32 = jnp.float32

C_GQ, C_GK, C_GV, C_GR = 0, 256, 512, 768
C_ASK, C_SVKVA, C_SQ, C_ZF, C_QA, C_KR = 1024, 1280, 1536, 2048, 2304, 2560
C_SKR, C_SQR, C_KRR = 2688, 2816, 3328
NC_CTX, NC_LAT = 2688, 3456


def _dot(a, b):
    return jnp.dot(a, b, preferred_element_type=F32)


def _dot_nt(a, b):
    return lax.dot_general(a, b, (((1,), (1,)), ((), ())), preferred_element_type=F32)


def _dot_tn(a, b):
    return lax.dot_general(a, b, (((0,), (0,)), ((), ())), preferred_element_type=F32)


def _rms(x, g):
    return x * lax.rsqrt(jnp.mean(x * x, axis=-1, keepdims=True) + EPS) * g


def _silu(x):
    return x * jax.nn.sigmoid(x)


def _params(*sem):
    return pltpu.CompilerParams(dimension_semantics=sem, vmem_limit_bytes=VMEM_LIMIT)


def _pick(arr, *lead, block=None):
    tail = tuple(arr.shape[len(lead):]) if block is None else tuple(block)
    zeros = (0,) * len(tail)
    return pl.BlockSpec((None,) * len(lead) + tail, lambda *_: tuple(lead) + zeros,
                        pipeline_mode=pl.Buffered(1))


def _mod_kernel(c_ref, w_ref, b_ref, o_ref):
    s = _silu(c_ref[...]).astype(BF)
    o_ref[...] = _dot(s, w_ref[...].astype(BF)) + b_ref[...]


def _mod_call(cvec, w_mod, b_mod):
    nl = w_mod.shape[0]
    tn = D_MODEL
    return pl.pallas_call(
        _mod_kernel,
        grid=(nl, N_MOD * D_MODEL // tn),
        in_specs=[
            pl.BlockSpec((MOD_ROWS, D_MODEL), lambda l, j: (0, 0)),
            pl.BlockSpec((None, D_MODEL, tn), lambda l, j: (l, 0, j)),
            pl.BlockSpec((None, 1, tn), lambda l, j: (l, 0, j)),
        ],
        out_specs=pl.BlockSpec((None, MOD_ROWS, tn), lambda l, j: (l, 0, j)),
        out_shape=jax.ShapeDtypeStruct((nl, MOD_ROWS, N_MOD * D_MODEL), F32),
        compiler_params=_params("parallel", "parallel"),
        name="mod",
    )(cvec, w_mod, b_mod.reshape(nl, 1, N_MOD * D_MODEL))


def _mod_spec(l, sub, tm, rows_per_batch, first_row):
    per = rows_per_batch // tm
    return pl.BlockSpec((None, None, 3, 1, D_MODEL), lambda i: (l, first_row + i // per, sub, 0, 0))


def _fft_layout_spec(tm, seq):
    per = max(seq // tm, 1)
    nbat = max(tm // seq, 1)
    return pl.BlockSpec((tm // nbat, nbat * MIX), lambda i: (i % per, i // per))


def _gn_spec(l, first, n):
    return pl.BlockSpec((None, n, 1, D_MODEL), lambda i: (l, first // n, 0, 0))


def _ffn_body(x, mod_ref, g_pre, g_post, wg_ref, wu_ref, wd_ref):
    sh, sc, gt = mod_ref[0], mod_ref[1], mod_ref[2]
    hb = (_rms(x, g_pre) * (1.0 + sc) + sh).astype(BF)
    acc = None
    for j in range(D_FF // FF_CHUNK):
        sl = slice(j * FF_CHUNK, (j + 1) * FF_CHUNK)
        g = _dot(hb, wg_ref[:, sl])
        u = _dot(hb, wu_ref[:, sl])
        d = _dot((_silu(g) * u).astype(BF), wd_ref[sl, :])
        acc = d if acc is None else acc + d
    return x + FFN_RES * gt * _rms(acc, g_post)


def _retarget(spec, f):
    return pl.BlockSpec(spec.block_shape, lambda i: spec.index_map(f(i)))


def _two_groups(nc):
    return (lambda i: jnp.minimum(i, nc - 1)), (lambda i: jnp.maximum(i - nc, 0))


def _mod_spec2(l, sub, nc, per):
    return pl.BlockSpec((None, None, 3, 1, D_MODEL),
                        lambda i: (l, jnp.where(i < nc, 0, 1 + jnp.maximum(i - nc, 0) // per), sub, 0, 0))


def _ffn_kernel(xc_ref, xl_ref, mod_ref, gn_ref, wg_ref, wu_ref, wd_ref, oc_ref, ol_ref, *, nc):
    i = pl.program_id(0)

    @pl.when(i < nc)
    def _():
        oc_ref[...] = _ffn_body(xc_ref[...], mod_ref, gn_ref[0], gn_ref[1], wg_ref, wu_ref, wd_ref)

    @pl.when(i >= nc)
    def _():
        ol_ref[...] = _ffn_body(xl_ref[...], mod_ref, gn_ref[0], gn_ref[1], wg_ref, wu_ref, wd_ref)


def _ffn_call(xc, xl, mod, gn, wg, wu, wd, l, lat_seq, tm=512):
    nc, nl = xc.shape[0] // tm, xl.shape[0] // tm
    fc, fl = _two_groups(nc)
    row = pl.BlockSpec((tm, D_MODEL), lambda i: (i, 0))
    return pl.pallas_call(
        functools.partial(_ffn_kernel, nc=nc),
        grid=(nc + nl,),
        in_specs=[
            _retarget(row, fc), _retarget(row, fl),
            _mod_spec2(l, 0, nc, lat_seq // tm),
            _gn_spec(l, 0, 2),
            _pick(wg, l, 0), _pick(wu, l, 0), _pick(wd, l, 0),
        ],
        out_specs=[_retarget(row, fc), _retarget(row, fl)],
        out_shape=[jax.ShapeDtypeStruct(xc.shape, F32), jax.ShapeDtypeStruct(xl.shape, F32)],
        compiler_params=_params("arbitrary"),
        name="ffn",
    )(xc, xl, mod, gn, wg, wu, wd)


def _outffn_body(og_ref, gr_ref, os_ref, of_ref, om_ref, x_ref, mod2_ref, mod3_ref, gn_ref, gout_ref,
                 avg_ref, wo_ref, wg_ref, wu_ref, wd_ref):
    og = og_ref[...]
    ms = _dot((og * og).astype(BF), avg_ref[...])
    og = (og * lax.rsqrt(ms + EPS) * gout_ref[...] * gr_ref[...]).astype(BF)
    nbat = of_ref.shape[1] // MIX
    of = jnp.concatenate([of_ref[:, j * MIX:(j + 1) * MIX] for j in range(nbat)], axis=0).astype(BF)
    mix = jnp.concatenate([og, os_ref[...], of, om_ref[...]], axis=1)
    x = x_ref[...] + mod2_ref[2] * _rms(_dot(mix, wo_ref[...]), gn_ref[0])
    return _ffn_body(x, mod3_ref, gn_ref[1], gn_ref[2], wg_ref, wu_ref, wd_ref)


def _outffn_kernel(*refs, nc):
    ctx, lat, rest = refs[:6], refs[6:12], refs[12:]
    shared, (oc_ref, ol_ref) = rest[:-2], rest[-2:]
    i = pl.program_id(0)

    @pl.when(i < nc)
    def _():
        oc_ref[...] = _outffn_body(*ctx, *shared)

    @pl.when(i >= nc)
    def _():
        ol_ref[...] = _outffn_body(*lat, *shared)


def _outffn_call(mix_c, xc, mix_l, xl, mod, gn, gout, avg, wo, wg, wu, wd, l, ctx_seq, lat_seq, tm=512):
    nc, nl = xc.shape[0] // tm, xl.shape[0] // tm
    fc, fl = _two_groups(nc)
    mspec = pl.BlockSpec((tm, MIX), lambda i: (i, 0))
    row = pl.BlockSpec((tm, D_MODEL), lambda i: (i, 0))

    def group_specs(seq, f):
        return [_retarget(sp, f) for sp in (mspec, mspec, mspec, _fft_layout_spec(tm, seq), mspec, row)]

    per = lat_seq // tm
    return pl.pallas_call(
        functools.partial(_outffn_kernel, nc=nc),
        grid=(nc + nl,),
        in_specs=group_specs(ctx_seq, fc) + group_specs(lat_seq, fl) + [
            _mod_spec2(l, 1, nc, per), _mod_spec2(l, 2, nc, per),
            _gn_spec(l, 3, 3),
            _pick(gout, l), _pick(avg),
            _pick(wo, l), _pick(wg, l, 1), _pick(wu, l, 1), _pick(wd, l, 1),
        ],
        out_specs=[_retarget(row, fc), _retarget(row, fl)],
        out_shape=[jax.ShapeDtypeStruct(xc.shape, F32), jax.ShapeDtypeStruct(xl.shape, F32)],
        compiler_params=_params("arbitrary"),
        name="outffn",
    )(*mix_c, xc, *mix_l, xl, mod, mod, gn, gout, avg, wo, wg, wu, wd)


def _log_sigmoid(x):
    return -(jnp.maximum(-x, 0.0) + jnp.log(1.0 + jnp.exp(-jnp.abs(x))))


def _inproj_kernel(*refs, latent, nbat):
    (x_ref, mod_ref, gn_ref, w_ref, wgate_ref, bgate_ref, gq_ref, gkv_ref, wqb_ref, wkbd_ref,
     cdft_ref, sdft_ref) = refs[:12]
    pos = 12
    if latent:
        cs_ref, ss_ref, cm_ref, sm_ref = refs[pos:pos + 4]
        pos += 4
    (gq_o, gk_o, gv_o, gr_o, la_o, sq_o, sk_o, sv_o, zc_o, zs_o, qc_o, qr_o, kcat_o) = refs[pos:pos + 13]
    pos += 13
    if not latent:
        ckv_o, kr_o = refs[pos:pos + 2]

    x = x_ref[...]
    hb = (_rms(x, gn_ref[0]) * (1.0 + mod_ref[1]) + mod_ref[0]).astype(BF)

    def col(off, n):
        return _dot(hb, w_ref[:, off:off + n])

    qa = col(C_QA, MLA_Q_RANK)
    ask = col(C_ASK, 2 * LANES)
    a_in = ask[:, :LANES].astype(BF)
    zf = col(C_ZF, MIX).astype(BF)

    gq_o[...] = col(C_GQ, MIX) * (GLA_DK ** -0.5)
    gk_o[...] = col(C_GK, MIX)

    qm = _dot(_rms(qa, gq_ref[...]).astype(BF), wqb_ref[...])
    logit = _dot(a_in, wgate_ref[...]) + bgate_ref[...]
    la_o[...] = _log_sigmoid(logit) * (LOG2E / GLA_TAU)

    zc = _dot(zf, cdft_ref[...]).astype(zc_o.dtype)
    zs = _dot(zf, sdft_ref[...]).astype(zs_o.dtype)
    rows_b = zf.shape[0] // nbat
    for j in range(nbat):
        zc_o[:, j * MIX:(j + 1) * MIX] = zc[j * rows_b:(j + 1) * rows_b]
        zs_o[:, j * MIX:(j + 1) * MIX] = zs[j * rows_b:(j + 1) * rows_b]

    gv_o[...] = col(C_GV, MIX).astype(BF)
    gr_o[...] = _silu(col(C_GR, MIX))

    scale = (MLA_NOPE + MLA_ROPE) ** -0.5 * LOG2E
    qc_o[...] = (_dot(qm[:, :MIX].astype(BF), wkbd_ref[...]) * scale).astype(BF)
    q_rope = qm[:, MIX:MIX + LANES]
    if latent:
        cm, sm = cm_ref[...], sm_ref[...]
        q_rope = q_rope * cm + qm[:, MIX + LANES:MIX + 2 * LANES] * sm
    qr_o[...] = (q_rope * scale).astype(BF)

    sq = col(C_SQ, 2 * MIX)
    sk = ask[:, LANES:]
    svk = col(C_SVKVA, 2 * LANES)
    sv = svk[:, :LANES]
    if latent:
        krp = col(C_KR, 2 * LANES)
        kr = krp[:, :LANES]
        cs, ss = cs_ref[...], ss_ref[...]
        cs4 = jnp.concatenate([cs] * 4, axis=1)
        ss4 = jnp.concatenate([ss] * 4, axis=1)
        sq = sq * cs4 + col(C_SQR, 2 * MIX) * ss4
        sk = sk * cs + krp[:, LANES:] * ss
    else:
        kr = col(C_KR, LANES)
    sq_o[...] = (sq * (HEAD_DIM ** -0.5 * LOG2E)).astype(BF)
    sk_o[...] = sk.astype(sk_o.dtype)
    sv_o[...] = sv.astype(sv_o.dtype)

    ckv = _rms(svk[:, LANES:], gkv_ref[...])
    if not latent:
        ckv_o[...] = ckv
        kr_o[...] = kr[:, :MLA_ROPE]
    else:
        kr = kr * cm + col(C_KRR, LANES) * sm
    kcat_o[:, :LANES] = ckv.astype(BF)
    kcat_o[:, LANES:] = kr.astype(BF)


def _inproj_call(x, mod, gn, wts, gq, gkv, cdft, sdft, tables, l, batch, seq, rows_per_batch, first_row,
                 latent, tm):
    w, wgate, bgate, wqb, wkbd = wts
    r = x.shape[0]
    per = max(seq // tm, 1)
    nbat = max(tm // seq, 1)
    row = lambda n: pl.BlockSpec((tm, n), lambda i: (i, 0))
    ncol = NC_LAT if latent else NC_CTX
    nq = wqb.shape[-1] if latent else MIX + LANES
    in_specs = [
        row(D_MODEL),
        _mod_spec(l, 1, tm, rows_per_batch, first_row),
        _gn_spec(l, 2, 1),
        _pick(w, l, block=(D_MODEL, ncol)), _pick(wgate, l), _pick(bgate, l), _pick(gq, l), _pick(gkv, l),
        _pick(wqb, l, block=(MLA_Q_RANK, nq)), _pick(wkbd, l), _pick(cdft), _pick(sdft),
    ]
    args = [x, mod, gn, w, wgate, bgate, gq, gkv, wqb, wkbd, cdft, sdft]
    if latent:
        in_specs += [pl.BlockSpec((tm, LANES), lambda i: (i % per, 0))] * 4
        args += list(tables)
    kvdt = BF if latent else F32
    zdt = F32 if latent else BF
    fft_spec = _fft_layout_spec(tm, seq)
    out_specs = [row(MIX), row(MIX), row(MIX), row(MIX), row(2 * MIX), row(2 * MIX), row(LANES), row(LANES),
                 fft_spec, fft_spec, row(2 * MIX), row(LANES), row(MIX)]
    sd = jax.ShapeDtypeStruct
    out_shape = [sd((r, MIX), F32), sd((r, MIX), F32), sd((r, MIX), BF), sd((r, MIX), F32),
                 sd((r, 2 * MIX), F32), sd((r, 2 * MIX), BF), sd((r, LANES), kvdt), sd((r, LANES), kvdt),
                 sd((seq, batch * MIX), zdt), sd((seq, batch * MIX), zdt),
                 sd((r, 2 * MIX), BF), sd((r, LANES), BF), sd((r, MIX), BF)]
    if not latent:
        out_specs += [row(MLA_KV_RANK), row(MLA_ROPE)]
        out_shape += [sd((r, MLA_KV_RANK), F32), sd((r, MLA_ROPE), F32)]
    return pl.pallas_call(
        functools.partial(_inproj_kernel, latent=latent, nbat=nbat),
        grid=(r // tm,),
        in_specs=in_specs,
        out_specs=out_specs,
        out_shape=out_shape,
        compiler_params=_params("parallel"),
        name="inproj_lat" if latent else "inproj_ctx",
    )(*args)


def _chunk_cumsum(x, row, reverse):
    n = x.shape[0]
    k = 1
    while k < n:
        if k % 8 == 0:
            pad = jnp.zeros((k, x.shape[1]), x.dtype)
            shifted = jnp.concatenate([x[k:], pad] if reverse else [pad, x[:n - k]], axis=0)
        elif reverse:
            shifted = jnp.where(row < n - k, pltpu.roll(x, n - k, 0), 0.0)
        else:
            shifted = jnp.where(row >= k, pltpu.roll(x, k, 0), 0.0)
        x = x + shifted
        k *= 2
    return x


def _gla_kernel(q_ref, k_ref, v_ref, la_ref, s0_ref, *rest, reverse, final, nchunk, group):
    if final:
        of_ref, o_ref, sfin_ref, st_ref = rest
    else:
        o_ref, sfin_ref, st_ref = rest
    i = pl.program_id(1)
    c = GLA_CHUNK

    @pl.when(i == 0)
    def _():
        st_ref[...] = s0_ref[...]

    row = lax.broadcasted_iota(jnp.int32, (c, MIX), 0)
    lane = lax.broadcasted_iota(jnp.int32, (c, MIX), 1) % c
    keep = (lane >= row) if reverse else (lane <= row)
    lane1 = lax.broadcasted_iota(jnp.int32, (1, MIX), 1) // c
    hmask_b = [(lane1 == h).astype(BF) for h in range(GLA_HEADS)]
    bdiag = (lax.broadcasted_iota(jnp.int32, (MIX, MIX), 0) // c
             == lax.broadcasted_iota(jnp.int32, (MIX, MIX), 1) // c)

    def chunk(t, carry):
        cc = (nchunk - 1 - t) if reverse else t
        rs = pl.ds(pl.multiple_of(cc * c, c), c)

        def front(g):
            cum = _chunk_cumsum(la_ref[g, rs, :], row, reverse)
            tot = cum[0:1] if reverse else cum[c - 1:c]
            q, k, v = q_ref[g, rs, :], k_ref[g, rs, :], v_ref[g, rs, :]
            qd = (q * jnp.exp2(cum)).astype(BF)
            ki = (k * jnp.exp2(-cum)).astype(BF)
            kd = (k * jnp.exp2(tot - cum)).astype(BF)
            kstack = jnp.concatenate([ki * m for m in hmask_b], axis=0)
            vstack = jnp.concatenate([v * m for m in hmask_b], axis=0)
            return qd, vstack, _dot_nt(qd, kstack), _dot_tn(v, kd), jnp.exp2(tot)

        def back(g, qd, vstack, scores, upd, decay):
            att = jnp.where(keep, scores, 0.0).astype(BF)
            st = st_ref[g]
            o = _dot(att, vstack) + _dot_nt(qd, st.astype(BF))
            st_ref[g] = st * decay + jnp.where(bdiag, upd, 0.0)
            if final:
                o = o + of_ref[g, rs, :]
            o_ref[g, rs, :] = o

        pending = front(0)
        for g in range(group):
            cur = pending
            if g + 1 < group:
                pending = front(g + 1)
            back(g, *cur)
        return carry

    lax.fori_loop(0, nchunk, chunk, 0, unroll=2)

    @pl.when(i == pl.num_programs(1) - 1)
    def _():
        sfin_ref[...] = st_ref[...]


def _gla_call(q, k, v, la, s0, s0_idx, extra, batch, seq, reverse, tb, group=GLA_GROUP):
    nblk = seq // tb
    final = extra is not None
    pos = (lambda i: nblk - 1 - i) if reverse else (lambda i: i)
    blk = pl.BlockSpec((group, tb, MIX), lambda b, i: (b, pos(i), 0))
    lblk = pl.BlockSpec((group, tb, MIX), lambda b, i: (b, pos(i), 1 if reverse else 0))
    sblk = pl.BlockSpec((group, MIX, MIX), lambda b, i: (b, 0, 0))
    s0blk = pl.BlockSpec((group, None, None, MIX, MIX), lambda b, i: (b,) + tuple(s0_idx) + (0, 0))
    r3 = lambda a: a.reshape(batch, seq, a.shape[-1])
    in_specs = [blk, blk, blk, lblk, s0blk]
    args = [r3(q), r3(k), r3(v), r3(la), s0]
    if final:
        in_specs.append(blk)
        args.append(r3(extra))
    o, st = pl.pallas_call(
        functools.partial(_gla_kernel, reverse=reverse, final=final, nchunk=tb // GLA_CHUNK, group=group),
        grid=(batch // group, nblk),
        in_specs=in_specs,
        out_specs=[blk, sblk],
        out_shape=[jax.ShapeDtypeStruct((batch, seq, MIX), F32),
                   jax.ShapeDtypeStruct((batch, MIX, MIX), F32)],
        scratch_shapes=[pltpu.VMEM((group, MIX, MIX), F32)],
        compiler_params=_params("parallel", "arbitrary"),
        name="gla_bwd" if reverse else "gla_fwd",
    )(*args)
    return o.reshape(batch * seq, MIX), st


def _swa_kernel(*refs, local, layer):
    sink_ref, q_ref = refs[:2]
    pos = 2
    if local:
        kp_ref, kc_ref, kn_ref, vp_ref, vc_ref, vn_ref = refs[pos:pos + 6]
        pos += 6
    kx_ref, vx_ref, o_ref = refs[pos:pos + 3]
    i = pl.program_id(1)
    last = pl.num_programs(1) - 1
    blk = SWA_BLOCK
    tq = q_ref.shape[0]
    cx = kx_ref.shape[-2]
    if local:
        kx = kx_ref[...].astype(BF)
        vx = vx_ref[...].astype(BF)
        nsub = tq // blk
        kblocks = [kp_ref[...]] + [kc_ref[j * blk:(j + 1) * blk, :] for j in range(nsub)] + [kn_ref[...]]
        vblocks = [vp_ref[...]] + [vc_ref[j * blk:(j + 1) * blk, :] for j in range(nsub)] + [vn_ref[...]]
        row = lax.broadcasted_iota(jnp.int32, (blk, blk), 0)
        col = lax.broadcasted_iota(jnp.int32, (blk, blk), 1)
        full = jnp.full((blk, blk), True)
        fullx = jnp.full((blk, cx), True)
    else:
        nsub = kx_ref.shape[0]
    rq = tq // nsub
    rows = lax.broadcasted_iota(jnp.int32, (2 * rq, 1), 0)
    low = lax.broadcasted_iota(jnp.int32, (rq, LANES), 1) < HEAD_DIM
    units = []
    for j in range(nsub):
        if local:
            kcat = jnp.concatenate([kx, kblocks[j], kblocks[j + 1], kblocks[j + 2]], axis=0)
            vcat = jnp.concatenate([vx, vblocks[j], vblocks[j + 1], vblocks[j + 2]], axis=0)
            prev_ok = (col >= row) if j > 0 else jnp.logical_and(col >= row, i > 0)
            next_ok = (col <= row) if j < nsub - 1 else jnp.logical_and(col <= row, i < last)
            keep = jnp.concatenate([fullx, prev_ok, full, next_ok], axis=1)
            keep = jnp.concatenate([keep, keep], axis=0)
        else:
            kcat, vcat, keep = kx_ref[j].astype(BF), vx_ref[j].astype(BF), None
        for g in range(SWA_KV_HEADS):
            units.append((j, g, kcat, vcat, keep))

    def scores(u):
        j, g, kcat, _, _ = units[u]
        rs = slice(j * rq, (j + 1) * rq)
        q2 = jnp.concatenate([q_ref[rs, (2 * g) * LANES:(2 * g + 1) * LANES],
                              q_ref[rs, (2 * g + 1) * LANES:(2 * g + 2) * LANES]], axis=0)
        return _dot_nt(q2, kcat)

    tiles = {}
    s_next = scores(0)
    for u, (j, g, _, vcat, keep) in enumerate(units):
        s = s_next
        if u + 1 < len(units):
            s_next = scores(u + 1)
        sink = jnp.where(rows < rq, sink_ref[layer, 2 * g], sink_ref[layer, 2 * g + 1]) * LOG2E
        if local:
            s = jnp.where(keep, s, NEG_INF)
        m = jnp.maximum(jnp.max(s, axis=-1, keepdims=True), sink)
        e = jnp.exp2(s - m)
        den = jnp.sum(e, axis=-1, keepdims=True) + jnp.exp2(sink - m)
        o2 = _dot(e.astype(BF), vcat) / den
        oa, ob = o2[:rq], o2[rq:]
        if g == 0:
            tiles[(j, g)] = jnp.where(low, oa, pltpu.roll(ob, HEAD_DIM, 1))
        else:
            tiles[(j, g)] = jnp.where(low, pltpu.roll(oa, HEAD_DIM, 1), ob)
            rs = slice(j * rq, (j + 1) * rq)
            o_ref[rs, :LANES] = tiles[(j, 0)].astype(BF)
            o_ref[rs, LANES:] = tiles[(j, 1)].astype(BF)


def _swa_call(sinks, layer, q, k, v, kx, vx, x_layer, batch, seq, local, tq=256):
    nt = seq // tq
    per = tq // SWA_BLOCK
    nb = seq // SWA_BLOCK
    cx = kx.shape[2]
    qspec = pl.BlockSpec((tq, 2 * MIX), lambda b, i: (b * nt + i, 0))
    in_specs = [pl.BlockSpec(memory_space=pltpu.SMEM), qspec]
    args = [sinks, q]
    if local:
        prv = pl.BlockSpec((SWA_BLOCK, LANES), lambda b, i: (b * nb + jnp.maximum(i * per - 1, 0), 0))
        cur = pl.BlockSpec((tq, LANES), lambda b, i: (b * nt + i, 0))
        nxt = pl.BlockSpec((SWA_BLOCK, LANES), lambda b, i: (b * nb + jnp.minimum((i + 1) * per, nb - 1), 0))
        in_specs += [prv, cur, nxt, prv, cur, nxt]
        args += [k, k, k, v, v, v]
    gb = 1 if local else min(batch, CTX_GROUP)
    if local:
        xspec = pl.BlockSpec((None, None, cx, LANES), lambda b, i: (b, x_layer, 0, 0))
    else:
        assert nt == 1
        tq = gb * seq
        qspec = pl.BlockSpec((tq, 2 * MIX), lambda b, i: (b, 0))
        xspec = pl.BlockSpec((gb, None, cx, LANES), lambda b, i: (b, x_layer, 0, 0))
    in_specs[1] = qspec
    in_specs += [xspec, xspec]
    args += [kx, vx]
    return pl.pallas_call(
        functools.partial(_swa_kernel, local=local, layer=layer),
        grid=(batch // gb, nt),
        in_specs=in_specs,
        out_specs=pl.BlockSpec((tq, MIX), lambda b, i: (b * nt + i, 0)),
        out_shape=jax.ShapeDtypeStruct((batch * seq, MIX), BF),
        compiler_params=_params("parallel", "parallel"),
        name="swa_lat" if local else "swa_ctx",
    )(*args)


def _fft_kernel(xc_ref, xs_ref, yc_ref, ys_ref, zc_ref, zs_ref, o_ref, *, nj, scale):
    yc, ys = yc_ref[...], ys_ref[...]
    xc, xs = xc_ref[...], xs_ref[...]
    acc = None
    for jp in range(nj // 2):
        cos_t, sin_t = [], []
        for j in (2 * jp, 2 * jp + 1):
            a, b = xc[:, j:j + 1], xs[:, j:j + 1]
            cos_t.append(a * yc - b * ys)
            sin_t.append(b * yc + a * ys)
        cos_t = jnp.concatenate(cos_t, axis=1).astype(BF)
        sin_t = jnp.concatenate(sin_t, axis=1).astype(BF)
        rows = slice(jp * MXU_DEPTH, (jp + 1) * MXU_DEPTH)
        d = _dot(cos_t, zc_ref[rows, :]) + _dot(sin_t, zs_ref[rows, :])
        acc = d if acc is None else acc + d
    o_ref[...] = (acc * scale).astype(BF)


def _fft_tables(seq):
    sp = jnp.arange(seq, dtype=jnp.int32)[:, None]
    w = 2.0 * jnp.pi / seq
    ax = ((sp * (jnp.arange(seq // LANES, dtype=jnp.int32) * LANES)[None, :]) % seq).astype(F32) * w
    ay = ((sp * jnp.arange(LANES, dtype=jnp.int32)[None, :]) % seq).astype(F32) * w
    return jnp.cos(ax), jnp.sin(ax), jnp.cos(ay), jnp.sin(ay)


def _fft_call(tables, zc, zs, seq, tm=256):
    xc, xs, yc, ys = tables
    nj = seq // LANES
    width = zc.shape[1]
    return pl.pallas_call(
        functools.partial(_fft_kernel, nj=nj, scale=(seq * FNET_GROUP_CH) ** -0.5),
        grid=(seq // tm,),
        in_specs=[
            pl.BlockSpec((tm, nj), lambda i: (i, 0)), pl.BlockSpec((tm, nj), lambda i: (i, 0)),
            pl.BlockSpec((tm, LANES), lambda i: (i, 0)), pl.BlockSpec((tm, LANES), lambda i: (i, 0)),
            _pick(zc), _pick(zs),
        ],
        out_specs=pl.BlockSpec((tm, width), lambda i: (i, 0)),
        out_shape=jax.ShapeDtypeStruct((seq, width), BF),
        compiler_params=_params("parallel"),
        name="fft",
    )(xc, xs, yc, ys, zc, zs)


def _fft2_kernel(zc_ref, zs_ref, m1_ref, m2_ref, tr_ref, ti_ref, o_ref, x_ref, *, nb1, scale):
    i = pl.program_id(1)
    n = m2_ref.shape[0]
    fb = FFT_BLOCK

    @pl.when(i < nb1)
    def _():
        m1 = m1_ref[...]
        for j in range(fb):
            z = jnp.concatenate([zc_ref[:, j, :], zs_ref[:, j, :]], axis=0).astype(BF)
            x1 = _dot(m1, z)
            xr, xi = x1[:n], x1[n:]
            tr, ti = tr_ref[:, j:j + 1], ti_ref[:, j:j + 1]
            x2 = jnp.concatenate([xr * tr - xi * ti, xr * ti + xi * tr], axis=0)
            x_ref[i * fb + j] = x2.reshape(2 * n // 8, 8, x2.shape[-1])

    @pl.when(i >= nb1)
    def _():
        m2 = m2_ref[...]
        blk = (i - nb1) * (fb // 8)
        for j in range(fb):
            z = jnp.concatenate([x_ref[:, blk + j // 8, j % 8, :],
                                 x_ref[:, n // 8 + blk + j // 8, j % 8, :]], axis=0).astype(BF)
            o_ref[:, j, :] = _dot(m2, z) * scale


def _fft2_call(zc, zs, seq, wl=512):
    n = int(round(seq ** 0.5))
    fb = FFT_BLOCK
    assert seq == n * n and n % fb == 0
    width = zc.shape[1]
    nb = n // fb
    k = jnp.arange(n, dtype=jnp.int32)
    ang = ((k[:, None] * k[None, :]) % n).astype(F32) * (2.0 * jnp.pi / n)
    c, sn = jnp.cos(ang), jnp.sin(ang)
    m1 = jnp.concatenate([jnp.concatenate([c, sn], axis=1), jnp.concatenate([-sn, c], axis=1)], axis=0).astype(BF)
    m2 = jnp.concatenate([c, sn], axis=1).astype(BF)
    tw = (k[:, None] * k[None, :]).astype(F32) * (2.0 * jnp.pi / seq)
    tr = jnp.cos(tw).reshape(n, nb, fb).transpose(1, 0, 2)
    ti = (-jnp.sin(tw)).reshape(n, nb, fb).transpose(1, 0, 2)
    zspec = pl.BlockSpec((n, fb, wl), lambda h, i: (0, jnp.minimum(i, nb - 1), h))
    tspec = pl.BlockSpec((None, n, fb), lambda h, i: (jnp.minimum(i, nb - 1), 0, 0))
    out = pl.pallas_call(
        functools.partial(_fft2_kernel, nb1=nb, scale=(seq * FNET_GROUP_CH) ** -0.5),
        grid=(width // wl, 2 * nb),
        in_specs=[zspec, zspec, _pick(m1), _pick(m2), tspec, tspec],
        out_specs=pl.BlockSpec((n, fb, wl), lambda h, i: (0, jnp.maximum(i - nb, 0), h)),
        out_shape=jax.ShapeDtypeStruct((n, n, width), F32),
        scratch_shapes=[pltpu.VMEM((n, 2 * n // 8, 8, wl), F32)],
        compiler_params=_params("parallel", "arbitrary"),
        name="fft2",
    )(zc.reshape(n, n, width), zs.reshape(n, n, width), m1, m2, tr, ti)
    return out.reshape(seq, width)


def _mla_kernel(*refs, has_ctx):
    qc_ref, qr_ref, ks_ref = refs[:3]
    pos = 3
    if has_ctx:
        kx_ref = refs[pos]
        pos += 1
    wv_ref, o_ref = refs[pos:pos + 2]
    nseq = 1 if has_ctx else ks_ref.shape[0]
    tq = qr_ref.shape[0] // nseq
    for b in range(nseq):
        rs = slice(b * tq, (b + 1) * tq)
        keys = ks_ref if has_ctx else ks_ref.at[b]
        qr = qr_ref[rs, :]
        lane = lax.broadcasted_iota(jnp.int32, qr.shape, 1) // MLA_ROPE
        qs = []
        for h in range(MLA_HEADS):
            qrh = jnp.where(lane == h, qr, jnp.zeros_like(qr))
            qs.append(jnp.concatenate([qc_ref[rs, h * LANES:(h + 1) * LANES], qrh], axis=1))
        qall = jnp.concatenate(qs, axis=0)
        tiles = [(kx_ref, 0, kx_ref.shape[0])] if has_ctx else []
        nk = keys.shape[0]
        tk = min(nk, MLA_KEY_TILE)
        tiles += [(keys, j * tk, tk) for j in range(nk // tk)]
        m = den = acc = None
        for ref, lo, n in tiles:
            kt = ref[lo:lo + n, :]
            s = _dot_nt(qall, kt)
            mt = jnp.max(s, axis=-1, keepdims=True)
            if m is None:
                m_new = mt
            else:
                m_new = jnp.maximum(m, mt)
                alpha = jnp.exp2(m - m_new)
            p = jnp.exp2(s - m_new)
            pv = _dot(p.astype(BF), kt[:, :LANES])
            ps = jnp.sum(p, axis=-1, keepdims=True)
            if m is None:
                den, acc = ps, pv
            else:
                den = alpha * den + ps
                acc = alpha * acc + pv
            m = m_new
        ot = acc / den
        ocat = jnp.concatenate([ot[h * tq:(h + 1) * tq] for h in range(MLA_HEADS)], axis=1).astype(BF)
        o_ref[rs, :] = _dot(ocat, wv_ref[...]).astype(BF)


def _mla_call(qc, qr, kcat, kx, wvbd, l, batch, seq, tq=256):
    nq = seq // tq
    has_ctx = kx is not None
    gb = 1
    key_block = (None, seq, MIX)
    if not has_ctx:
        assert nq == 1
        gb = min(batch, CTX_GROUP)
        tq, key_block = gb * seq, (gb, seq, MIX)
    in_specs = [pl.BlockSpec((tq, 2 * MIX), lambda b, i: (b * nq + i, 0)),
                pl.BlockSpec((tq, LANES), lambda b, i: (b * nq + i, 0)),
                pl.BlockSpec(key_block, lambda b, i: (b, 0, 0))]
    args = [qc, qr, kcat.reshape(batch, seq, MIX)]
    if has_ctx:
        in_specs.append(pl.BlockSpec((None, None, kx.shape[2], MIX), lambda b, i: (b, l, 0, 0)))
        args.append(kx)
    in_specs.append(_pick(wvbd, l))
    args.append(wvbd)
    return pl.pallas_call(
        functools.partial(_mla_kernel, has_ctx=has_ctx),
        grid=(batch // gb, nq),
        in_specs=in_specs,
        out_specs=pl.BlockSpec((tq, MIX), lambda b, i: (b * nq + i, 0)),
        out_shape=jax.ShapeDtypeStruct((batch * seq, MIX), BF),
        compiler_params=_params("parallel", "parallel"),
        name="mla_lat" if has_ctx else "mla_ctx",
    )(*args)


def _rot_cols(w, width):
    half = width // 2
    parts = w.reshape(w.shape[:-1] + (w.shape[-1] // width, 2, half))
    return jnp.stack([-parts[..., 1, :], parts[..., 0, :]], axis=-2).reshape(w.shape)


def _block_diag(blocks):
    n, r, c = blocks.shape[-3:]
    eye = jnp.eye(n, dtype=blocks.dtype)
    out = jnp.einsum("...hrc,hg->...hrgc", blocks, eye)
    return out.reshape(blocks.shape[:-3] + (n * r, n * c))


def _layer_weights(w_in, gla_w_gate, gla_b_gate, mla_w_q_b, mla_w_kv_b):
    nl = w_in.shape[0]
    w_in = w_in.astype(BF)
    offs = [0]
    for n in (256, 256, 256, 256, 16, 16, 256, 128, 128, 256, 256, 128, 32):
        offs.append(offs[-1] + n)
    seg = lambda k: w_in[..., offs[k]:offs[k + 1]]
    zeros = lambda n: jnp.zeros((nl, D_MODEL, n), w_in.dtype)
    sq = seg(6).reshape(nl, D_MODEL, SWA_Q_HEADS, HEAD_DIM)
    sq_tiles = []
    for hq in range(SWA_Q_HEADS):
        parts = [zeros(HEAD_DIM), zeros(HEAD_DIM)]
        parts[hq // 2] = sq[:, :, hq]
        sq_tiles += parts
    sq_w = jnp.concatenate(sq_tiles, axis=-1)
    kr4 = jnp.tile(seg(12), (1, 1, MLA_HEADS))
    cols = [seg(0), seg(1), seg(2), seg(3), seg(4), seg(5), zeros(LANES - 2 * GLA_GATE_RANK), seg(7),
            seg(8), seg(11), sq_w, seg(9), seg(10), kr4,
            _rot_cols(seg(7), HEAD_DIM // 2), _rot_cols(sq_w, HEAD_DIM // 2), _rot_cols(kr4, MLA_ROPE // 2)]
    w = jnp.concatenate(cols, axis=-1)

    wgate = jnp.zeros((nl, LANES, 2 * MIX), F32)
    wgate = wgate.at[:, :GLA_GATE_RANK, :MIX].set(gla_w_gate[:, 0])
    wgate = wgate.at[:, GLA_GATE_RANK:2 * GLA_GATE_RANK, MIX:].set(gla_w_gate[:, 1])
    bgate = gla_b_gate.reshape(nl, 1, 2 * MIX)

    wq = mla_w_q_b.reshape(nl, MLA_Q_RANK, MLA_HEADS, MLA_NOPE + MLA_ROPE)
    q_nope = wq[..., :MLA_NOPE].reshape(nl, MLA_Q_RANK, MLA_HEADS * MLA_NOPE)
    q_rope = wq[..., MLA_NOPE:].reshape(nl, MLA_Q_RANK, MLA_HEADS * MLA_ROPE)
    wqb = jnp.concatenate([q_nope, q_rope, _rot_cols(q_rope, MLA_ROPE // 2)], axis=-1).astype(BF)
    wkv = mla_w_kv_b.reshape(nl, MLA_KV_RANK, MLA_HEADS, MLA_NOPE + MLA_V)
    wkbd = _block_diag(jnp.transpose(wkv[..., :MLA_NOPE], (0, 2, 3, 1))).astype(BF)
    wvbd = _block_diag(jnp.transpose(wkv[..., MLA_NOPE:], (0, 2, 1, 3))).astype(BF)
    return (w, wgate.astype(BF), bgate, wqb, wkbd), wvbd


def _rope_tables(seq):
    t = jnp.arange(seq)
    rows = (t // GRID_W).astype(F32)[:, None]
    cols = (t % GRID_W).astype(F32)[:, None]

    def table(width):
        half = width // 2
        lane = jnp.arange(LANES)
        inv = ROPE_BASE ** (-(2 * (lane % half)).astype(F32) / width)
        ang = jnp.where(((lane // width) % 2 == 0)[None, :], rows, cols) * inv[None, :]
        return jnp.cos(ang), jnp.sin(ang)

    cs, ss = table(HEAD_DIM // 2)
    cm, sm = table(MLA_ROPE // 2)
    return cs, ss, cm, sm


def _dft_channel():
    c = jnp.arange(FNET_GROUP_CH, dtype=jnp.int32)
    ang = ((c[:, None] * c[None, :]) % FNET_GROUP_CH).astype(F32) * (2.0 * jnp.pi / FNET_GROUP_CH)
    cos_b = jnp.broadcast_to(jnp.cos(ang)[None], (FNET_GROUPS, FNET_GROUP_CH, FNET_GROUP_CH))
    sin_b = jnp.broadcast_to(jnp.sin(ang)[None], (FNET_GROUPS, FNET_GROUP_CH, FNET_GROUP_CH))
    return _block_diag(cos_b).astype(BF), (-_block_diag(sin_b)).astype(BF)


def _state_in(s):
    return _block_diag(jnp.swapaxes(s, -1, -2))


def _state_out(st):
    blocks = [st[:, h * GLA_DV:(h + 1) * GLA_DV, h * GLA_DK:(h + 1) * GLA_DK] for h in range(GLA_HEADS)]
    return jnp.swapaxes(jnp.stack(blocks, axis=1), 2, 3)


def _token_mix(x, l, shared, st0, st0_idx, cache, batch, seq, rows_per_batch, first_row, latent):
    mod, gn, wts, wvbd, gq, gkv, sinks, cdft, sdft, rope_t, fft_t = shared
    outs = _inproj_call(x, mod, gn, wts, gq, gkv, cdft, sdft, rope_t, l, batch, seq, rows_per_batch,
                        first_row, latent, tm=1024 if latent else 512)
    gqv, gkv_, gv, gr, la, sq, sk, sv, zc, zs, qc, qr, kcat = outs[:13]
    tb = min(seq, GLA_BLOCK)
    o_f, st_f = _gla_call(gqv, gkv_, gv, la, st0, st0_idx[0], None, batch, seq, False, tb)
    o_gla, st_b = _gla_call(gqv, gkv_, gv, la, st0, st0_idx[1], o_f, batch, seq, True, tb)
    if latent:
        kx, vx, mx = cache
        o_swa = _swa_call(sinks, l, sq, sk, sv, kx, vx, l, batch, seq, True, tq=1024)
        o_mla = _mla_call(qc, qr, kcat, mx, wvbd, l, batch, seq)
        new_ctx = None
    else:
        k4 = sk.reshape(batch, 1, seq, LANES)
        v4 = sv.reshape(batch, 1, seq, LANES)
        o_swa = _swa_call(sinks, l, sq, None, None, k4, v4, 0, batch, seq, False)
        o_mla = _mla_call(qc, qr, kcat, None, wvbd, l, batch, seq)
        new_ctx = (st_f, st_b, sk, sv, outs[13], outs[14])
    o_fft = _fft2_call(zc, zs, seq) if latent else _fft_call(fft_t, zc, zs, seq)
    return (o_gla, gr, o_swa, o_fft, o_mla), new_ctx


def kernel(x_prompt, x_sample, c, state_gla, cache_swa_k, cache_swa_v, cache_mla_ckv, cache_mla_krope,
           c_ctx, w_mod, b_mod, g_norm, w_ffn_gate, w_ffn_up, w_ffn_down, w_in, gla_w_gate, gla_b_gate,
           gla_g_out, swa_sink, mla_g_q, mla_g_kv, mla_w_q_b, mla_w_kv_b, w_out):
    nb, ns, _ = x_prompt.shape
    db, dsq, _ = x_sample.shape
    past = cache_swa_k.shape[2]

    cvec = jnp.zeros((MOD_ROWS, D_MODEL), F32).at[0].set(c_ctx).at[1:1 + db].set(c)
    mod = _mod_call(cvec, w_mod, b_mod).reshape(DEPTH, MOD_ROWS, N_MOD, 1, D_MODEL)
    gn = g_norm.reshape(DEPTH, 6, 1, D_MODEL)

    wg, wu, wd, wo = (a.astype(BF) for a in (w_ffn_gate, w_ffn_up, w_ffn_down, w_out))
    wts, wvbd = _layer_weights(w_in, gla_w_gate, gla_b_gate, mla_w_q_b, mla_w_kv_b)
    gq = mla_g_q.reshape(DEPTH, 1, MLA_Q_RANK)
    gkv = mla_g_kv.reshape(DEPTH, 1, MLA_KV_RANK)
    gout = jnp.tile(gla_g_out, (1, GLA_HEADS)).reshape(DEPTH, 1, MIX)
    cdft, sdft = _dft_channel()
    shared = (mod, gn, wts, wvbd, gq, gkv, swa_sink, cdft, sdft)
    head = jnp.arange(MIX) // GLA_DV
    avg = jnp.where(head[:, None] == head[None, :], 1.0 / GLA_DV, 0.0).astype(BF)
    shared_ctx = shared + (None, _fft_tables(ns))
    shared_lat = shared + (_rope_tables(dsq), None)

    st_lat = _state_in(state_gla)
    st_zero = jnp.zeros((nb, 1, 1, MIX, MIX), F32)
    kx = cache_swa_k.reshape(db, DEPTH, past, LANES)
    vx = cache_swa_v.reshape(db, DEPTH, past, LANES)
    mx = jnp.concatenate([cache_mla_ckv, jnp.tile(cache_mla_krope, (1, 1, 1, MLA_HEADS))], axis=-1).astype(BF)

    xc = x_prompt.reshape(nb * ns, D_MODEL)
    xl = x_sample.reshape(db * dsq, D_MODEL)
    st_gla, st_k, st_v, st_ckv, st_kr = [], [], [], [], []
    for l in range(DEPTH):
        xc, xl = _ffn_call(xc, xl, mod, gn, wg, wu, wd, l, dsq)
        mixed_c, new_ctx = _token_mix(xc, l, shared_ctx, st_zero, ((0, 0), (0, 0)), None, nb, ns, nb * ns, 0, False)
        st_f, st_b, k_c, v_c, ckv_c, kr_c = new_ctx
        st_gla.append(jnp.stack([_state_out(st_f), _state_out(st_b)], axis=1))
        st_k.append(k_c.reshape(nb, ns, SWA_KV_HEADS, HEAD_DIM))
        st_v.append(v_c.reshape(nb, ns, SWA_KV_HEADS, HEAD_DIM))
        st_ckv.append(ckv_c.reshape(nb, ns, MLA_KV_RANK))
        st_kr.append(kr_c.reshape(nb, ns, MLA_ROPE))
        mixed_l, _ = _token_mix(xl, l, shared_lat, st_lat, ((l, 0), (l, 1)), (kx, vx, mx), db, dsq, dsq, 1, True)
        xc, xl = _outffn_call(mixed_c, xc, mixed_l, xl, mod, gn, gout, avg, wo, wg, wu, wd, l, ns, dsq)

    return (xc.reshape(nb, ns, D_MODEL), xl.reshape(db, dsq, D_MODEL), jnp.stack(st_gla, axis=1),
            jnp.stack(st_k, axis=1), jnp.stack(st_v, axis=1), jnp.stack(st_ckv, axis=1),
            jnp.stack(st_kr, axis=1))
```

```python
import functools

import jax
import jax.numpy as jnp
from jax import lax
from jax.experimental import pallas as pl
from jax.experimental.pallas import tpu as pltpu

D_MODEL = 1024
DEPTH = 4
GRID_W = 64
HEAD_DIM = 64
GLA_HEADS = 4
GLA_DK = 64
GLA_DV = 64
GLA_GATE_RANK = 16
GLA_TAU = 16.0
GLA_CHUNK = 64
SWA_Q_HEADS = 4
SWA_KV_HEADS = 2
SWA_BLOCK = 128
FNET_GROUPS = 4
FNET_GROUP_CH = 64
MLA_HEADS = 4
MLA_Q_RANK = 256
MLA_KV_RANK = 128
MLA_NOPE = 64
MLA_ROPE = 32
MLA_V = 64
D_FF = 2816
FFN_RES = 0.5
N_MOD = 9
ROPE_BASE = 10000.0
EPS = 1e-6
NEG_INF = -1e30

MIX = 256
LANES = 128
MXU_DEPTH = 256
FF_CHUNK = 256
MLA_KEY_TILE = 2048
FFT_BLOCK = 32
GLA_BLOCK = 512
CTX_GROUP = 8
GLA_GROUP = 4
LOG2E = 1.4426950408889634
MOD_ROWS = 8
VMEM_LIMIT = 56 * 1024 * 1024

BF = jnp.bfloat16
F32 = jnp.float32

C_GQ, C_GK, C_GV, C_GR = 0, 256, 512, 768
C_ASK, C_SVKVA, C_SQ, C_ZF, C_QA, C_KR = 1024, 1280, 1536, 2048, 2304, 2560
C_SKR, C_SQR, C_KRR = 2688, 2816, 3328
NC_CTX, NC_LAT = 2688, 3456


def _dot(a, b):
    return jnp.dot(a, b, preferred_element_type=F32)


def _dot_nt(a, b):
    return lax.dot_general(a, b, (((1,), (1,)), ((), ())), preferred_element_type=F32)


def _dot_tn(a, b):
    return lax.dot_general(a, b, (((0,), (0,)), ((), ())), preferred_element_type=F32)


def _rms(x, g):
    return x * lax.rsqrt(jnp.mean(x * x, axis=-1, keepdims=True) + EPS) * g


def _silu(x):
    return x * jax.nn.sigmoid(x)


def _params(*sem):
    return pltpu.CompilerParams(dimension_semantics=sem, vmem_limit_bytes=VMEM_LIMIT)


def _pick(arr, *lead, block=None):
    tail = tuple(arr.shape[len(lead):]) if block is None else tuple(block)
    zeros = (0,) * len(tail)
    return pl.BlockSpec((None,) * len(lead) + tail, lambda *_: tuple(lead) + zeros,
                        pipeline_mode=pl.Buffered(1))


def _mod_kernel(c_ref, w_ref, b_ref, o_ref):
    s = _silu(c_ref[...]).astype(BF)
    o_ref[...] = _dot(s, w_ref[...].astype(BF)) + b_ref[...]


def _mod_call(cvec, w_mod, b_mod):
    nl = w_mod.shape[0]
    tn = D_MODEL
    return pl.pallas_call(
        _mod_kernel,
        grid=(nl, N_MOD * D_MODEL // tn),
        in_specs=[
            pl.BlockSpec((MOD_ROWS, D_MODEL), lambda l, j: (0, 0)),
            pl.BlockSpec((None, D_MODEL, tn), lambda l, j: (l, 0, j)),
            pl.BlockSpec((None, 1, tn), lambda l, j: (l, 0, j)),
        ],
        out_specs=pl.BlockSpec((None, MOD_ROWS, tn), lambda l, j: (l, 0, j)),
        out_shape=jax.ShapeDtypeStruct((nl, MOD_ROWS, N_MOD * D_MODEL), F32),
        compiler_params=_params("parallel", "parallel"),
        name="mod",
    )(cvec, w_mod, b_mod.reshape(nl, 1, N_MOD * D_MODEL))


def _mod_spec(l, sub, tm, rows_per_batch, first_row):
    per = rows_per_batch // tm
    return pl.BlockSpec((None, None, 3, 1, D_MODEL), lambda i: (l, first_row + i // per, sub, 0, 0))


def _fft_layout_spec(tm, seq):
    per = max(seq // tm, 1)
    nbat = max(tm // seq, 1)
    return pl.BlockSpec((tm // nbat, nbat * MIX), lambda i: (i % per, i // per))


def _gn_spec(l, first, n):
    return pl.BlockSpec((None, n, 1, D_MODEL), lambda i: (l, first // n, 0, 0))


def _ffn_body(x, mod_ref, g_pre, g_post, wg_ref, wu_ref, wd_ref):
    sh, sc, gt = mod_ref[0], mod_ref[1], mod_ref[2]
    hb = (_rms(x, g_pre) * (1.0 + sc) + sh).astype(BF)
    acc = None
    for j in range(D_FF // FF_CHUNK):
        sl = slice(j * FF_CHUNK, (j + 1) * FF_CHUNK)
        g = _dot(hb, wg_ref[:, sl])
        u = _dot(hb, wu_ref[:, sl])
        d = _dot((_silu(g) * u).astype(BF), wd_ref[sl, :])
        acc = d if acc is None else acc + d
    return x + FFN_RES * gt * _rms(acc, g_post)


def _retarget(spec, f):
    return pl.BlockSpec(spec.block_shape, lambda i: spec.index_map(f(i)))


def _two_groups(nc):
    return (lambda i: jnp.minimum(i, nc - 1)), (lambda i: jnp.maximum(i - nc, 0))


def _mod_spec2(l, sub, nc, per):
    return pl.BlockSpec((None, None, 3, 1, D_MODEL),
                        lambda i: (l, jnp.where(i < nc, 0, 1 + jnp.maximum(i - nc, 0) // per), sub, 0, 0))


def _ffn_kernel(xc_ref, xl_ref, mod_ref, gn_ref, wg_ref, wu_ref, wd_ref, oc_ref, ol_ref, *, nc):
    i = pl.program_id(0)

    @pl.when(i < nc)
    def _():
        oc_ref[...] = _ffn_body(xc_ref[...], mod_ref, gn_ref[0], gn_ref[1], wg_ref, wu_ref, wd_ref)

    @pl.when(i >= nc)
    def _():
        ol_ref[...] = _ffn_body(xl_ref[...], mod_ref, gn_ref[0], gn_ref[1], wg_ref, wu_ref, wd_ref)


def _ffn_call(xc, xl, mod, gn, wg, wu, wd, l, lat_seq, tm=512):
    nc, nl = xc.shape[0] // tm, xl.shape[0] // tm
    fc, fl = _two_groups(nc)
    row = pl.BlockSpec((tm, D_MODEL), lambda i: (i, 0))
    return pl.pallas_call(
        functools.partial(_ffn_kernel, nc=nc),
        grid=(nc + nl,),
        in_specs=[
            _retarget(row, fc), _retarget(row, fl),
            _mod_spec2(l, 0, nc, lat_seq // tm),
            _gn_spec(l, 0, 2),
            _pick(wg, l, 0), _pick(wu, l, 0), _pick(wd, l, 0),
        ],
        out_specs=[_retarget(row, fc), _retarget(row, fl)],
        out_shape=[jax.ShapeDtypeStruct(xc.shape, F32), jax.ShapeDtypeStruct(xl.shape, F32)],
        compiler_params=_params("arbitrary"),
        name="ffn",
    )(xc, xl, mod, gn, wg, wu, wd)


def _outffn_body(og_ref, gr_ref, os_ref, of_ref, om_ref, x_ref, mod2_ref, mod3_ref, gn_ref, gout_ref,
                 avg_ref, wo_ref, wg_ref, wu_ref, wd_ref):
    og = og_ref[...]
    ms = _dot((og * og).astype(BF), avg_ref[...])
    og = (og * lax.rsqrt(ms + EPS) * gout_ref[...] * gr_ref[...]).astype(BF)
    nbat = of_ref.shape[1] // MIX
    of = jnp.concatenate([of_ref[:, j * MIX:(j + 1) * MIX] for j in range(nbat)], axis=0).astype(BF)
    mix = jnp.concatenate([og, os_ref[...], of, om_ref[...]], axis=1)
    x = x_ref[...] + mod2_ref[2] * _rms(_dot(mix, wo_ref[...]), gn_ref[0])
    return _ffn_body(x, mod3_ref, gn_ref[1], gn_ref[2], wg_ref, wu_ref, wd_ref)


def _outffn_kernel(*refs, nc):
    ctx, lat, rest = refs[:6], refs[6:12], refs[12:]
    shared, (oc_ref, ol_ref) = rest[:-2], rest[-2:]
    i = pl.program_id(0)

    @pl.when(i < nc)
    def _():
        oc_ref[...] = _outffn_body(*ctx, *shared)

    @pl.when(i >= nc)
    def _():
        ol_ref[...] = _outffn_body(*lat, *shared)


def _outffn_call(mix_c, xc, mix_l, xl, mod, gn, gout, avg, wo, wg, wu, wd, l, ctx_seq, lat_seq, tm=512):
    nc, nl = xc.shape[0] // tm, xl.shape[0] // tm
    fc, fl = _two_groups(nc)
    mspec = pl.BlockSpec((tm, MIX), lambda i: (i, 0))
    row = pl.BlockSpec((tm, D_MODEL), lambda i: (i, 0))

    def group_specs(seq, f):
        return [_retarget(sp, f) for sp in (mspec, mspec, mspec, _fft_layout_spec(tm, seq), mspec, row)]

    per = lat_seq // tm
    return pl.pallas_call(
        functools.partial(_outffn_kernel, nc=nc),
        grid=(nc + nl,),
        in_specs=group_specs(ctx_seq, fc) + group_specs(lat_seq, fl) + [
            _mod_spec2(l, 1, nc, per), _mod_spec2(l, 2, nc, per),
            _gn_spec(l, 3, 3),
            _pick(gout, l), _pick(avg),
            _pick(wo, l), _pick(wg, l, 1), _pick(wu, l, 1), _pick(wd, l, 1),
        ],
        out_specs=[_retarget(row, fc), _retarget(row, fl)],
        out_shape=[jax.ShapeDtypeStruct(xc.shape, F32), jax.ShapeDtypeStruct(xl.shape, F32)],
        compiler_params=_params("arbitrary"),
        name="outffn",
    )(*mix_c, xc, *mix_l, xl, mod, mod, gn, gout, avg, wo, wg, wu, wd)


def _log_sigmoid(x):
    return -(jnp.maximum(-x, 0.0) + jnp.log(1.0 + jnp.exp(-jnp.abs(x))))


def _inproj_kernel(*refs, latent, nbat):
    (x_ref, mod_ref, gn_ref, w_ref, wgate_ref, bgate_ref, gq_ref, gkv_ref, wqb_ref, wkbd_ref,
     cdft_ref, sdft_ref) = refs[:12]
    pos = 12
    if latent:
        cs_ref, ss_ref, cm_ref, sm_ref = refs[pos:pos + 4]
        pos += 4
    (gq_o, gk_o, gv_o, gr_o, la_o, sq_o, sk_o, sv_o, zc_o, zs_o, qc_o, qr_o, kcat_o) = refs[pos:pos + 13]
    pos += 13
    if not latent:
        ckv_o, kr_o = refs[pos:pos + 2]

    x = x_ref[...]
    hb = (_rms(x, gn_ref[0]) * (1.0 + mod_ref[1]) + mod_ref[0]).astype(BF)

    def col(off, n):
        return _dot(hb, w_ref[:, off:off + n])

    qa = col(C_QA, MLA_Q_RANK)
    ask = col(C_ASK, 2 * LANES)
    a_in = ask[:, :LANES].astype(BF)
    zf = col(C_ZF, MIX).astype(BF)

    gq_o[...] = col(C_GQ, MIX) * (GLA_DK ** -0.5)
    gk_o[...] = col(C_GK, MIX)

    qm = _dot(_rms(qa, gq_ref[...]).astype(BF), wqb_ref[...])
    logit = _dot(a_in, wgate_ref[...]) + bgate_ref[...]
    la_o[...] = _log_sigmoid(logit) * (LOG2E / GLA_TAU)

    zc = _dot(zf, cdft_ref[...]).astype(zc_o.dtype)
    zs = _dot(zf, sdft_ref[...]).astype(zs_o.dtype)
    rows_b = zf.shape[0] // nbat
    for j in range(nbat):
        zc_o[:, j * MIX:(j + 1) * MIX] = zc[j * rows_b:(j + 1) * rows_b]
        zs_o[:, j * MIX:(j + 1) * MIX] = zs[j * rows_b:(j + 1) * rows_b]

    gv_o[...] = col(C_GV, MIX).astype(BF)
    gr_o[...] = _silu(col(C_GR, MIX))

    scale = (MLA_NOPE + MLA_ROPE) ** -0.5 * LOG2E
    qc_o[...] = (_dot(qm[:, :MIX].astype(BF), wkbd_ref[...]) * scale).astype(BF)
    q_rope = qm[:, MIX:MIX + LANES]
    if latent:
        cm, sm = cm_ref[...], sm_ref[...]
        q_rope = q_rope * cm + qm[:, MIX + LANES:MIX + 2 * LANES] * sm
    qr_o[...] = (q_rope * scale).astype(BF)

    sq = col(C_SQ, 2 * MIX)
    sk = ask[:, LANES:]
    svk = col(C_SVKVA, 2 * LANES)
    sv = svk[:, :LANES]
    if latent:
        krp = col(C_KR, 2 * LANES)
        kr = krp[:, :LANES]
        cs, ss = cs_ref[...], ss_ref[...]
        cs4 = jnp.concatenate([cs] * 4, axis=1)
        ss4 = jnp.concatenate([ss] * 4, axis=1)
        sq = sq * cs4 + col(C_SQR, 2 * MIX) * ss4
        sk = sk * cs + krp[:, LANES:] * ss
    else:
        kr = col(C_KR, LANES)
    sq_o[...] = (sq * (HEAD_DIM ** -0.5 * LOG2E)).astype(BF)
    sk_o[...] = sk.astype(sk_o.dtype)
    sv_o[...] = sv.astype(sv_o.dtype)

    ckv = _rms(svk[:, LANES:], gkv_ref[...])
    if not latent:
        ckv_o[...] = ckv
        kr_o[...] = kr[:, :MLA_ROPE]
    else:
        kr = kr * cm + col(C_KRR, LANES) * sm
    kcat_o[:, :LANES] = ckv.astype(BF)
    kcat_o[:, LANES:] = kr.astype(BF)


def _inproj_call(x, mod, gn, wts, gq, gkv, cdft, sdft, tables, l, batch, seq, rows_per_batch, first_row,
                 latent, tm):
    w, wgate, bgate, wqb, wkbd = wts
    r = x.shape[0]
    per = max(seq // tm, 1)
    nbat = max(tm // seq, 1)
    row = lambda n: pl.BlockSpec((tm, n), lambda i: (i, 0))
    ncol = NC_LAT if latent else NC_CTX
    nq = wqb.shape[-1] if latent else MIX + LANES
    in_specs = [
        row(D_MODEL),
        _mod_spec(l, 1, tm, rows_per_batch, first_row),
        _gn_spec(l, 2, 1),
        _pick(w, l, block=(D_MODEL, ncol)), _pick(wgate, l), _pick(bgate, l), _pick(gq, l), _pick(gkv, l),
        _pick(wqb, l, block=(MLA_Q_RANK, nq)), _pick(wkbd, l), _pick(cdft), _pick(sdft),
    ]
    args = [x, mod, gn, w, wgate, bgate, gq, gkv, wqb, wkbd, cdft, sdft]
    if latent:
        in_specs += [pl.BlockSpec((tm, LANES), lambda i: (i % per, 0))] * 4
        args += list(tables)
    kvdt = BF if latent else F32
    zdt = F32 if latent else BF
    fft_spec = _fft_layout_spec(tm, seq)
    out_specs = [row(MIX), row(MIX), row(MIX), row(MIX), row(2 * MIX), row(2 * MIX), row(LANES), row(LANES),
                 fft_spec, fft_spec, row(2 * MIX), row(LANES), row(MIX)]
    sd = jax.ShapeDtypeStruct
    out_shape = [sd((r, MIX), F32), sd((r, MIX), F32), sd((r, MIX), BF), sd((r, MIX), F32),
                 sd((r, 2 * MIX), F32), sd((r, 2 * MIX), BF), sd((r, LANES), kvdt), sd((r, LANES), kvdt),
                 sd((seq, batch * MIX), zdt), sd((seq, batch * MIX), zdt),
                 sd((r, 2 * MIX), BF), sd((r, LANES), BF), sd((r, MIX), BF)]
    if not latent:
        out_specs += [row(MLA_KV_RANK), row(MLA_ROPE)]
        out_shape += [sd((r, MLA_KV_RANK), F32), sd((r, MLA_ROPE), F32)]
    return pl.pallas_call(
        functools.partial(_inproj_kernel, latent=latent, nbat=nbat),
        grid=(r // tm,),
        in_specs=in_specs,
        out_specs=out_specs,
        out_shape=out_shape,
        compiler_params=_params("parallel"),
        name="inproj_lat" if latent else "inproj_ctx",
    )(*args)


def _chunk_cumsum(x, row, reverse):
    n = x.shape[0]
    k = 1
    while k < n:
        if k % 8 == 0:
            pad = jnp.zeros((k, x.shape[1]), x.dtype)
            shifted = jnp.concatenate([x[k:], pad] if reverse else [pad, x[:n - k]], axis=0)
        elif reverse:
            shifted = jnp.where(row < n - k, pltpu.roll(x, n - k, 0), 0.0)
        else:
            shifted = jnp.where(row >= k, pltpu.roll(x, k, 0), 0.0)
        x = x + shifted
        k *= 2
    return x


def _gla_kernel(q_ref, k_ref, v_ref, la_ref, s0_ref, *rest, reverse, final, nchunk, group):
    if final:
        of_ref, o_ref, sfin_ref, st_ref = rest
    else:
        o_ref, sfin_ref, st_ref = rest
    i = pl.program_id(1)
    c = GLA_CHUNK

    @pl.when(i == 0)
    def _():
        st_ref[...] = s0_ref[...]

    row = lax.broadcasted_iota(jnp.int32, (c, MIX), 0)
    lane = lax.broadcasted_iota(jnp.int32, (c, MIX), 1) % c
    keep = (lane >= row) if reverse else (lane <= row)
    lane1 = lax.broadcasted_iota(jnp.int32, (1, MIX), 1) // c
    hmask_b = [(lane1 == h).astype(BF) for h in range(GLA_HEADS)]
    bdiag = (lax.broadcasted_iota(jnp.int32, (MIX, MIX), 0) // c
             == lax.broadcasted_iota(jnp.int32, (MIX, MIX), 1) // c)

    def chunk(t, carry):
        cc = (nchunk - 1 - t) if reverse else t
        rs = pl.ds(pl.multiple_of(cc * c, c), c)

        def front(g):
            cum = _chunk_cumsum(la_ref[g, rs, :], row, reverse)
            tot = cum[0:1] if reverse else cum[c - 1:c]
            q, k, v = q_ref[g, rs, :], k_ref[g, rs, :], v_ref[g, rs, :]
            qd = (q * jnp.exp2(cum)).astype(BF)
            ki = (k * jnp.exp2(-cum)).astype(BF)
            kd = (k * jnp.exp2(tot - cum)).astype(BF)
            kstack = jnp.concatenate([ki * m for m in hmask_b], axis=0)
            vstack = jnp.concatenate([v * m for m in hmask_b], axis=0)
            return qd, vstack, _dot_nt(qd, kstack), _dot_tn(v, kd), jnp.exp2(tot)

        def back(g, qd, vstack, scores, upd, decay):
            att = jnp.where(keep, scores, 0.0).astype(BF)
            st = st_ref[g]
            o = _dot(att, vstack) + _dot_nt(qd, st.astype(BF))
            st_ref[g] = st * decay + jnp.where(bdiag, upd, 0.0)
            if final:
                o = o + of_ref[g, rs, :]
            o_ref[g, rs, :] = o

        pending = front(0)
        for g in range(group):
            cur = pending
            if g + 1 < group:
                pending = front(g + 1)
            back(g, *cur)
        return carry

    lax.fori_loop(0, nchunk, chunk, 0, unroll=2)

    @pl.when(i == pl.num_programs(1) - 1)
    def _():
        sfin_ref[...] = st_ref[...]


def _gla_call(q, k, v, la, s0, s0_idx, extra, batch, seq, reverse, tb, group=GLA_GROUP):
    nblk = seq // tb
    final = extra is not None
    pos = (lambda i: nblk - 1 - i) if reverse else (lambda i: i)
    blk = pl.BlockSpec((group, tb, MIX), lambda b, i: (b, pos(i), 0))
    lblk = pl.BlockSpec((group, tb, MIX), lambda b, i: (b, pos(i), 1 if reverse else 0))
    sblk = pl.BlockSpec((group, MIX, MIX), lambda b, i: (b, 0, 0))
    s0blk = pl.BlockSpec((group, None, None, MIX, MIX), lambda b, i: (b,) + tuple(s0_idx) + (0, 0))
    r3 = lambda a: a.reshape(batch, seq, a.shape[-1])
    in_specs = [blk, blk, blk, lblk, s0blk]
    args = [r3(q), r3(k), r3(v), r3(la), s0]
    if final:
        in_specs.append(blk)
        args.append(r3(extra))
    o, st = pl.pallas_call(
        functools.partial(_gla_kernel, reverse=reverse, final=final, nchunk=tb // GLA_CHUNK, group=group),
        grid=(batch // group, nblk),
        in_specs=in_specs,
        out_specs=[blk, sblk],
        out_shape=[jax.ShapeDtypeStruct((batch, seq, MIX), F32),
                   jax.ShapeDtypeStruct((batch, MIX, MIX), F32)],
        scratch_shapes=[pltpu.VMEM((group, MIX, MIX), F32)],
        compiler_params=_params("parallel", "arbitrary"),
        name="gla_bwd" if reverse else "gla_fwd",
    )(*args)
    return o.reshape(batch * seq, MIX), st


def _swa_kernel(*refs, local, layer):
    sink_ref, q_ref = refs[:2]
    pos = 2
    if local:
        kp_ref, kc_ref, kn_ref, vp_ref, vc_ref, vn_ref = refs[pos:pos + 6]
        pos += 6
    kx_ref, vx_ref, o_ref = refs[pos:pos + 3]
    i = pl.program_id(1)
    last = pl.num_programs(1) - 1
    blk = SWA_BLOCK
    tq = q_ref.shape[0]
    cx = kx_ref.shape[-2]
    if local:
        kx = kx_ref[...].astype(BF)
        vx = vx_ref[...].astype(BF)
        nsub = tq // blk
        kblocks = [kp_ref[...]] + [kc_ref[j * blk:(j + 1) * blk, :] for j in range(nsub)] + [kn_ref[...]]
        vblocks = [vp_ref[...]] + [vc_ref[j * blk:(j + 1) * blk, :] for j in range(nsub)] + [vn_ref[...]]
        row = lax.broadcasted_iota(jnp.int32, (blk, blk), 0)
        col = lax.broadcasted_iota(jnp.int32, (blk, blk), 1)
        full = jnp.full((blk, blk), True)
        fullx = jnp.full((blk, cx), True)
    else:
        nsub = kx_ref.shape[0]
    rq = tq // nsub
    rows = lax.broadcasted_iota(jnp.int32, (2 * rq, 1), 0)
    low = lax.broadcasted_iota(jnp.int32, (rq, LANES), 1) < HEAD_DIM
    units = []
    for j in range(nsub):
        if local:
            kcat = jnp.concatenate([kx, kblocks[j], kblocks[j + 1], kblocks[j + 2]], axis=0)
            vcat = jnp.concatenate([vx, vblocks[j], vblocks[j + 1], vblocks[j + 2]], axis=0)
            prev_ok = (col >= row) if j > 0 else jnp.logical_and(col >= row, i > 0)
            next_ok = (col <= row) if j < nsub - 1 else jnp.logical_and(col <= row, i < last)
            keep = jnp.concatenate([fullx, prev_ok, full, next_ok], axis=1)
            keep = jnp.concatenate([keep, keep], axis=0)
        else:
            kcat, vcat, keep = kx_ref[j].astype(BF), vx_ref[j].astype(BF), None
        for g in range(SWA_KV_HEADS):
            units.append((j, g, kcat, vcat, keep))

    def scores(u):
        j, g, kcat, _, _ = units[u]
        rs = slice(j * rq, (j + 1) * rq)
        q2 = jnp.concatenate([q_ref[rs, (2 * g) * LANES:(2 * g + 1) * LANES],
                              q_ref[rs, (2 * g + 1) * LANES:(2 * g + 2) * LANES]], axis=0)
        return _dot_nt(q2, kcat)

    tiles = {}
    s_next = scores(0)
    for u, (j, g, _, vcat, keep) in enumerate(units):
        s = s_next
        if u + 1 < len(units):
            s_next = scores(u + 1)
        sink = jnp.where(rows < rq, sink_ref[layer, 2 * g], sink_ref[layer, 2 * g + 1]) * LOG2E
        if local:
            s = jnp.where(keep, s, NEG_INF)
        m = jnp.maximum(jnp.max(s, axis=-1, keepdims=True), sink)
        e = jnp.exp2(s - m)
        den = jnp.sum(e, axis=-1, keepdims=True) + jnp.exp2(sink - m)
        o2 = _dot(e.astype(BF), vcat) / den
        oa, ob = o2[:rq], o2[rq:]
        if g == 0:
            tiles[(j, g)] = jnp.where(low, oa, pltpu.roll(ob, HEAD_DIM, 1))
        else:
            tiles[(j, g)] = jnp.where(low, pltpu.roll(oa, HEAD_DIM, 1), ob)
            rs = slice(j * rq, (j + 1) * rq)
            o_ref[rs, :LANES] = tiles[(j, 0)].astype(BF)
            o_ref[rs, LANES:] = tiles[(j, 1)].astype(BF)


def _swa_call(sinks, layer, q, k, v, kx, vx, x_layer, batch, seq, local, tq=256):
    nt = seq // tq
    per = tq // SWA_BLOCK
    nb = seq // SWA_BLOCK
    cx = kx.shape[2]
    qspec = pl.BlockSpec((tq, 2 * MIX), lambda b, i: (b * nt + i, 0))
    in_specs = [pl.BlockSpec(memory_space=pltpu.SMEM), qspec]
    args = [sinks, q]
    if local:
        prv = pl.BlockSpec((SWA_BLOCK, LANES), lambda b, i: (b * nb + jnp.maximum(i * per - 1, 0), 0))
        cur = pl.BlockSpec((tq, LANES), lambda b, i: (b * nt + i, 0))
        nxt = pl.BlockSpec((SWA_BLOCK, LANES), lambda b, i: (b * nb + jnp.minimum((i + 1) * per, nb - 1), 0))
        in_specs += [prv, cur, nxt, prv, cur, nxt]
        args += [k, k, k, v, v, v]
    gb = 1 if local else min(batch, CTX_GROUP)
    if local:
        xspec = pl.BlockSpec((None, None, cx, LANES), lambda b, i: (b, x_layer, 0, 0))
    else:
        assert nt == 1
        tq = gb * seq
        qspec = pl.BlockSpec((tq, 2 * MIX), lambda b, i: (b, 0))
        xspec = pl.BlockSpec((gb, None, cx, LANES), lambda b, i: (b, x_layer, 0, 0))
    in_specs[1] = qspec
    in_specs += [xspec, xspec]
    args += [kx, vx]
    return pl.pallas_call(
        functools.partial(_swa_kernel, local=local, layer=layer),
        grid=(batch // gb, nt),
        in_specs=in_specs,
        out_specs=pl.BlockSpec((tq, MIX), lambda b, i: (b * nt + i, 0)),
        out_shape=jax.ShapeDtypeStruct((batch * seq, MIX), BF),
        compiler_params=_params("parallel", "parallel"),
        name="swa_lat" if local else "swa_ctx",
    )(*args)


def _fft_kernel(xc_ref, xs_ref, yc_ref, ys_ref, zc_ref, zs_ref, o_ref, *, nj, scale):
    yc, ys = yc_ref[...], ys_ref[...]
    xc, xs = xc_ref[...], xs_ref[...]
    acc = None
    for jp in range(nj // 2):
        cos_t, sin_t = [], []
        for j in (2 * jp, 2 * jp + 1):
            a, b = xc[:, j:j + 1], xs[:, j:j + 1]
            cos_t.append(a * yc - b * ys)
            sin_t.append(b * yc + a * ys)
        cos_t = jnp.concatenate(cos_t, axis=1).astype(BF)
        sin_t = jnp.concatenate(sin_t, axis=1).astype(BF)
        rows = slice(jp * MXU_DEPTH, (jp + 1) * MXU_DEPTH)
        d = _dot(cos_t, zc_ref[rows, :]) + _dot(sin_t, zs_ref[rows, :])
        acc = d if acc is None else acc + d
    o_ref[...] = (acc * scale).astype(BF)


def _fft_tables(seq):
    sp = jnp.arange(seq, dtype=jnp.int32)[:, None]
    w = 2.0 * jnp.pi / seq
    ax = ((sp * (jnp.arange(seq // LANES, dtype=jnp.int32) * LANES)[None, :]) % seq).astype(F32) * w
    ay = ((sp * jnp.arange(LANES, dtype=jnp.int32)[None, :]) % seq).astype(F32) * w
    return jnp.cos(ax), jnp.sin(ax), jnp.cos(ay), jnp.sin(ay)


def _fft_call(tables, zc, zs, seq, tm=256):
    xc, xs, yc, ys = tables
    nj = seq // LANES
    width = zc.shape[1]
    return pl.pallas_call(
        functools.partial(_fft_kernel, nj=nj, scale=(seq * FNET_GROUP_CH) ** -0.5),
        grid=(seq // tm,),
        in_specs=[
            pl.BlockSpec((tm, nj), lambda i: (i, 0)), pl.BlockSpec((tm, nj), lambda i: (i, 0)),
            pl.BlockSpec((tm, LANES), lambda i: (i, 0)), pl.BlockSpec((tm, LANES), lambda i: (i, 0)),
            _pick(zc), _pick(zs),
        ],
        out_specs=pl.BlockSpec((tm, width), lambda i: (i, 0)),
        out_shape=jax.ShapeDtypeStruct((seq, width), BF),
        compiler_params=_params("parallel"),
        name="fft",
    )(xc, xs, yc, ys, zc, zs)


def _fft2_kernel(zc_ref, zs_ref, m1_ref, m2_ref, tr_ref, ti_ref, o_ref, x_ref, *, nb1, scale):
    i = pl.program_id(1)
    n = m2_ref.shape[0]
    fb = FFT_BLOCK

    @pl.when(i < nb1)
    def _():
        m1 = m1_ref[...]
        for j in range(fb):
            z = jnp.concatenate([zc_ref[:, j, :], zs_ref[:, j, :]], axis=0).astype(BF)
            x1 = _dot(m1, z)
            xr, xi = x1[:n], x1[n:]
            tr, ti = tr_ref[:, j:j + 1], ti_ref[:, j:j + 1]
            x2 = jnp.concatenate([xr * tr - xi * ti, xr * ti + xi * tr], axis=0)
            x_ref[i * fb + j] = x2.reshape(2 * n // 8, 8, x2.shape[-1])

    @pl.when(i >= nb1)
    def _():
        m2 = m2_ref[...]
        blk = (i - nb1) * (fb // 8)
        for j in range(fb):
            z = jnp.concatenate([x_ref[:, blk + j // 8, j % 8, :],
                                 x_ref[:, n // 8 + blk + j // 8, j % 8, :]], axis=0).astype(BF)
            o_ref[:, j, :] = _dot(m2, z) * scale


def _fft2_call(zc, zs, seq, wl=512):
    n = int(round(seq ** 0.5))
    fb = FFT_BLOCK
    assert seq == n * n and n % fb == 0
    width = zc.shape[1]
    nb = n // fb
    k = jnp.arange(n, dtype=jnp.int32)
    ang = ((k[:, None] * k[None, :]) % n).astype(F32) * (2.0 * jnp.pi / n)
    c, sn = jnp.cos(ang), jnp.sin(ang)
    m1 = jnp.concatenate([jnp.concatenate([c, sn], axis=1), jnp.concatenate([-sn, c], axis=1)], axis=0).astype(BF)
    m2 = jnp.concatenate([c, sn], axis=1).astype(BF)
    tw = (k[:, None] * k[None, :]).astype(F32) * (2.0 * jnp.pi / seq)
    tr = jnp.cos(tw).reshape(n, nb, fb).transpose(1, 0, 2)
    ti = (-jnp.sin(tw)).reshape(n, nb, fb).transpose(1, 0, 2)
    zspec = pl.BlockSpec((n, fb, wl), lambda h, i: (0, jnp.minimum(i, nb - 1), h))
    tspec = pl.BlockSpec((None, n, fb), lambda h, i: (jnp.minimum(i, nb - 1), 0, 0))
    out = pl.pallas_call(
        functools.partial(_fft2_kernel, nb1=nb, scale=(seq * FNET_GROUP_CH) ** -0.5),
        grid=(width // wl, 2 * nb),
        in_specs=[zspec, zspec, _pick(m1), _pick(m2), tspec, tspec],
        out_specs=pl.BlockSpec((n, fb, wl), lambda h, i: (0, jnp.maximum(i - nb, 0), h)),
        out_shape=jax.ShapeDtypeStruct((n, n, width), F32),
        scratch_shapes=[pltpu.VMEM((n, 2 * n // 8, 8, wl), F32)],
        compiler_params=_params("parallel", "arbitrary"),
        name="fft2",
    )(zc.reshape(n, n, width), zs.reshape(n, n, width), m1, m2, tr, ti)
    return out.reshape(seq, width)


def _mla_kernel(*refs, has_ctx):
    qc_ref, qr_ref, ks_ref = refs[:3]
    pos = 3
    if has_ctx:
        kx_ref = refs[pos]
        pos += 1
    wv_ref, o_ref = refs[pos:pos + 2]
    nseq = 1 if has_ctx else ks_ref.shape[0]
    tq = qr_ref.shape[0] // nseq
    for b in range(nseq):
        rs = slice(b * tq, (b + 1) * tq)
        keys = ks_ref if has_ctx else ks_ref.at[b]
        qr = qr_ref[rs, :]
        lane = lax.broadcasted_iota(jnp.int32, qr.shape, 1) // MLA_ROPE
        qs = []
        for h in range(MLA_HEADS):
            qrh = jnp.where(lane == h, qr, jnp.zeros_like(qr))
            qs.append(jnp.concatenate([qc_ref[rs, h * LANES:(h + 1) * LANES], qrh], axis=1))
        qall = jnp.concatenate(qs, axis=0)
        tiles = [(kx_ref, 0, kx_ref.shape[0])] if has_ctx else []
        nk = keys.shape[0]
        tk = min(nk, MLA_KEY_TILE)
        tiles += [(keys, j * tk, tk) for j in range(nk // tk)]
        m = den = acc = None
        for ref, lo, n in tiles:
            kt = ref[lo:lo + n, :]
            s = _dot_nt(qall, kt)
            mt = jnp.max(s, axis=-1, keepdims=True)
            if m is None:
                m_new = mt
            else:
                m_new = jnp.maximum(m, mt)
                alpha = jnp.exp2(m - m_new)
            p = jnp.exp2(s - m_new)
            pv = _dot(p.astype(BF), kt[:, :LANES])
            ps = jnp.sum(p, axis=-1, keepdims=True)
            if m is None:
                den, acc = ps, pv
            else:
                den = alpha * den + ps
                acc = alpha * acc + pv
            m = m_new
        ot = acc / den
        ocat = jnp.concatenate([ot[h * tq:(h + 1) * tq] for h in range(MLA_HEADS)], axis=1).astype(BF)
        o_ref[rs, :] = _dot(ocat, wv_ref[...]).astype(BF)


def _mla_call(qc, qr, kcat, kx, wvbd, l, batch, seq, tq=256):
    nq = seq // tq
    has_ctx = kx is not None
    gb = 1
    key_block = (None, seq, MIX)
    if not has_ctx:
        assert nq == 1
        gb = min(batch, CTX_GROUP)
        tq, key_block = gb * seq, (gb, seq, MIX)
    in_specs = [pl.BlockSpec((tq, 2 * MIX), lambda b, i: (b * nq + i, 0)),
                pl.BlockSpec((tq, LANES), lambda b, i: (b * nq + i, 0)),
                pl.BlockSpec(key_block, lambda b, i: (b, 0, 0))]
    args = [qc, qr, kcat.reshape(batch, seq, MIX)]
    if has_ctx:
        in_specs.append(pl.BlockSpec((None, None, kx.shape[2], MIX), lambda b, i: (b, l, 0, 0)))
        args.append(kx)
    in_specs.append(_pick(wvbd, l))
    args.append(wvbd)
    return pl.pallas_call(
        functools.partial(_mla_kernel, has_ctx=has_ctx),
        grid=(batch // gb, nq),
        in_specs=in_specs,
        out_specs=pl.BlockSpec((tq, MIX), lambda b, i: (b * nq + i, 0)),
        out_shape=jax.ShapeDtypeStruct((batch * seq, MIX), BF),
        compiler_params=_params("parallel", "parallel"),
        name="mla_lat" if has_ctx else "mla_ctx",
    )(*args)


def _rot_cols(w, width):
    half = width // 2
    parts = w.reshape(w.shape[:-1] + (w.shape[-1] // width, 2, half))
    return jnp.stack([-parts[..., 1, :], parts[..., 0, :]], axis=-2).reshape(w.shape)


def _block_diag(blocks):
    n, r, c = blocks.shape[-3:]
    eye = jnp.eye(n, dtype=blocks.dtype)
    out = jnp.einsum("...hrc,hg->...hrgc", blocks, eye)
    return out.reshape(blocks.shape[:-3] + (n * r, n * c))


def _layer_weights(w_in, gla_w_gate, gla_b_gate, mla_w_q_b, mla_w_kv_b):
    nl = w_in.shape[0]
    w_in = w_in.astype(BF)
    offs = [0]
    for n in (256, 256, 256, 256, 16, 16, 256, 128, 128, 256, 256, 128, 32):
        offs.append(offs[-1] + n)
    seg = lambda k: w_in[..., offs[k]:offs[k + 1]]
    zeros = lambda n: jnp.zeros((nl, D_MODEL, n), w_in.dtype)
    sq = seg(6).reshape(nl, D_MODEL, SWA_Q_HEADS, HEAD_DIM)
    sq_tiles = []
    for hq in range(SWA_Q_HEADS):
        parts = [zeros(HEAD_DIM), zeros(HEAD_DIM)]
        parts[hq // 2] = sq[:, :, hq]
        sq_tiles += parts
    sq_w = jnp.concatenate(sq_tiles, axis=-1)
    kr4 = jnp.tile(seg(12), (1, 1, MLA_HEADS))
    cols = [seg(0), seg(1), seg(2), seg(3), seg(4), seg(5), zeros(LANES - 2 * GLA_GATE_RANK), seg(7),
            seg(8), seg(11), sq_w, seg(9), seg(10), kr4,
            _rot_cols(seg(7), HEAD_DIM // 2), _rot_cols(sq_w, HEAD_DIM // 2), _rot_cols(kr4, MLA_ROPE // 2)]
    w = jnp.concatenate(cols, axis=-1)

    wgate = jnp.zeros((nl, LANES, 2 * MIX), F32)
    wgate = wgate.at[:, :GLA_GATE_RANK, :MIX].set(gla_w_gate[:, 0])
    wgate = wgate.at[:, GLA_GATE_RANK:2 * GLA_GATE_RANK, MIX:].set(gla_w_gate[:, 1])
    bgate = gla_b_gate.reshape(nl, 1, 2 * MIX)

    wq = mla_w_q_b.reshape(nl, MLA_Q_RANK, MLA_HEADS, MLA_NOPE + MLA_ROPE)
    q_nope = wq[..., :MLA_NOPE].reshape(nl, MLA_Q_RANK, MLA_HEADS * MLA_NOPE)
    q_rope = wq[..., MLA_NOPE:].reshape(nl, MLA_Q_RANK, MLA_HEADS * MLA_ROPE)
    wqb = jnp.concatenate([q_nope, q_rope, _rot_cols(q_rope, MLA_ROPE // 2)], axis=-1).astype(BF)
    wkv = mla_w_kv_b.reshape(nl, MLA_KV_RANK, MLA_HEADS, MLA_NOPE + MLA_V)
    wkbd = _block_diag(jnp.transpose(wkv[..., :MLA_NOPE], (0, 2, 3, 1))).astype(BF)
    wvbd = _block_diag(jnp.transpose(wkv[..., MLA_NOPE:], (0, 2, 1, 3))).astype(BF)
    return (w, wgate.astype(BF), bgate, wqb, wkbd), wvbd


def _rope_tables(seq):
    t = jnp.arange(seq)
    rows = (t // GRID_W).astype(F32)[:, None]
    cols = (t % GRID_W).astype(F32)[:, None]

    def table(width):
        half = width // 2
        lane = jnp.arange(LANES)
        inv = ROPE_BASE ** (-(2 * (lane % half)).astype(F32) / width)
        ang = jnp.where(((lane // width) % 2 == 0)[None, :], rows, cols) * inv[None, :]
        return jnp.cos(ang), jnp.sin(ang)

    cs, ss = table(HEAD_DIM // 2)
    cm, sm = table(MLA_ROPE // 2)
    return cs, ss, cm, sm


def _dft_channel():
    c = jnp.arange(FNET_GROUP_CH, dtype=jnp.int32)
    ang = ((c[:, None] * c[None, :]) % FNET_GROUP_CH).astype(F32) * (2.0 * jnp.pi / FNET_GROUP_CH)
    cos_b = jnp.broadcast_to(jnp.cos(ang)[None], (FNET_GROUPS, FNET_GROUP_CH, FNET_GROUP_CH))
    sin_b = jnp.broadcast_to(jnp.sin(ang)[None], (FNET_GROUPS, FNET_GROUP_CH, FNET_GROUP_CH))
    return _block_diag(cos_b).astype(BF), (-_block_diag(sin_b)).astype(BF)


def _state_in(s):
    return _block_diag(jnp.swapaxes(s, -1, -2))


def _state_out(st):
    blocks = [st[:, h * GLA_DV:(h + 1) * GLA_DV, h * GLA_DK:(h + 1) * GLA_DK] for h in range(GLA_HEADS)]
    return jnp.swapaxes(jnp.stack(blocks, axis=1), 2, 3)


def _token_mix(x, l, shared, st0, st0_idx, cache, batch, seq, rows_per_batch, first_row, latent):
    mod, gn, wts, wvbd, gq, gkv, sinks, cdft, sdft, rope_t, fft_t = shared
    outs = _inproj_call(x, mod, gn, wts, gq, gkv, cdft, sdft, rope_t, l, batch, seq, rows_per_batch,
                        first_row, latent, tm=1024 if latent else 512)
    gqv, gkv_, gv, gr, la, sq, sk, sv, zc, zs, qc, qr, kcat = outs[:13]
    tb = min(seq, GLA_BLOCK)
    o_f, st_f = _gla_call(gqv, gkv_, gv, la, st0, st0_idx[0], None, batch, seq, False, tb)
    o_gla, st_b = _gla_call(gqv, gkv_, gv, la, st0, st0_idx[1], o_f, batch, seq, True, tb)
    if latent:
        kx, vx, mx = cache
        o_swa = _swa_call(sinks, l, sq, sk, sv, kx, vx, l, batch, seq, True, tq=1024)
        o_mla = _mla_call(qc, qr, kcat, mx, wvbd, l, batch, seq)
        new_ctx = None
    else:
        k4 = sk.reshape(batch, 1, seq, LANES)
        v4 = sv.reshape(batch, 1, seq, LANES)
        o_swa = _swa_call(sinks, l, sq, None, None, k4, v4, 0, batch, seq, False)
        o_mla = _mla_call(qc, qr, kcat, None, wvbd, l, batch, seq)
        new_ctx = (st_f, st_b, sk, sv, outs[13], outs[14])
    o_fft = _fft2_call(zc, zs, seq) if latent else _fft_call(fft_t, zc, zs, seq)
    return (o_gla, gr, o_swa, o_fft, o_mla), new_ctx


def kernel(x_prompt, x_sample, c, state_gla, cache_swa_k, cache_swa_v, cache_mla_ckv, cache_mla_krope,
           c_ctx, w_mod, b_mod, g_norm, w_ffn_gate, w_ffn_up, w_ffn_down, w_in, gla_w_gate, gla_b_gate,
           gla_g_out, swa_sink, mla_g_q, mla_g_kv, mla_w_q_b, mla_w_kv_b, w_out):
    nb, ns, _ = x_prompt.shape
    db, dsq, _ = x_sample.shape
    past = cache_swa_k.shape[2]

    cvec = jnp.zeros((MOD_ROWS, D_MODEL), F32).at[0].set(c_ctx).at[1:1 + db].set(c)
    mod = _mod_call(cvec, w_mod, b_mod).reshape(DEPTH, MOD_ROWS, N_MOD, 1, D_MODEL)
    gn = g_norm.reshape(DEPTH, 6, 1, D_MODEL)

    wg, wu, wd, wo = (a.astype(BF) for a in (w_ffn_gate, w_ffn_up, w_ffn_down, w_out))
    wts, wvbd = _layer_weights(w_in, gla_w_gate, gla_b_gate, mla_w_q_b, mla_w_kv_b)
    gq = mla_g_q.reshape(DEPTH, 1, MLA_Q_RANK)
    gkv = mla_g_kv.reshape(DEPTH, 1, MLA_KV_RANK)
    gout = jnp.tile(gla_g_out, (1, GLA_HEADS)).reshape(DEPTH, 1, MIX)
    cdft, sdft = _dft_channel()
    shared = (mod, gn, wts, wvbd, gq, gkv, swa_sink, cdft, sdft)
    head = jnp.arange(MIX) // GLA_DV
    avg = jnp.where(head[:, None] == head[None, :], 1.0 / GLA_DV, 0.0).astype(BF)
    shared_ctx = shared + (None, _fft_tables(ns))
    shared_lat = shared + (_rope_tables(dsq), None)

    st_lat = _state_in(state_gla)
    st_zero = jnp.zeros((nb, 1, 1, MIX, MIX), F32)
    kx = cache_swa_k.reshape(db, DEPTH, past, LANES)
    vx = cache_swa_v.reshape(db, DEPTH, past, LANES)
    mx = jnp.concatenate([cache_mla_ckv, jnp.tile(cache_mla_krope, (1, 1, 1, MLA_HEADS))], axis=-1).astype(BF)

    xc = x_prompt.reshape(nb * ns, D_MODEL)
    xl = x_sample.reshape(db * dsq, D_MODEL)
    st_gla, st_k, st_v, st_ckv, st_kr = [], [], [], [], []
    for l in range(DEPTH):
        xc, xl = _ffn_call(xc, xl, mod, gn, wg, wu, wd, l, dsq)
        mixed_c, new_ctx = _token_mix(xc, l, shared_ctx, st_zero, ((0, 0), (0, 0)), None, nb, ns, nb * ns, 0, False)
        st_f, st_b, k_c, v_c, ckv_c, kr_c = new_ctx
        st_gla.append(jnp.stack([_state_out(st_f), _state_out(st_b)], axis=1))
        st_k.append(k_c.reshape(nb, ns, SWA_KV_HEADS, HEAD_DIM))
        st_v.append(v_c.reshape(nb, ns, SWA_KV_HEADS, HEAD_DIM))
        st_ckv.append(ckv_c.reshape(nb, ns, MLA_KV_RANK))
        st_kr.append(kr_c.reshape(nb, ns, MLA_ROPE))
        mixed_l, _ = _token_mix(xl, l, shared_lat, st_lat, ((l, 0), (l, 1)), (kx, vx, mx), db, dsq, dsq, 1, True)
        xc, xl = _outffn_call(mixed_c, xc, mixed_l, xl, mod, gn, gout, avg, wo, wg, wu, wd, l, ns, dsq)

    return (xc.reshape(nb, ns, D_MODEL), xl.reshape(db, dsq, D_MODEL), jnp.stack(st_gla, axis=1),
            jnp.stack(st_k, axis=1), jnp.stack(st_v, axis=1), jnp.stack(st_ckv, axis=1),
            jnp.stack(st_kr, axis=1))
```
